```python
import jax
import jax.numpy as jnp
from jax import lax
import numpy as np

D_MODEL = 1024
BATCH = 4
SEQ = 4096
DEPTH = 4

GRID_W = 64
CTX_LEN = 256
N_MOD = 6
NORM_EPS = 1e-6
CHUNK = 64
RET_HEADS = 8
RET_DK = 64
RET_DV = 64
RET_W = RET_HEADS * RET_DV
ROPE_BASE = 10000.0
HG_HEADS = 4
HG_DK = 128
HG_DV = 128
HG_W = HG_HEADS * HG_DK
LRU_W = 512
LRU_BLOCKS = 8
LRU_BD = LRU_W // LRU_BLOCKS
LRU_C = 8.0
CONV_W = 4
CONV_PAD = (2, 1)
N_EXPERTS = 32
TOP_K = 4
D_EXPERT = D_MODEL
SWIGLU_ALPHA = 1.702
SWIGLU_LIMIT = 7.0
MOE_BLOCK = 256
SPLIT_SIZES = (RET_W, RET_W, RET_W, RET_W, HG_W, HG_W, HG_W, HG_W, HG_W, LRU_W, LRU_W, D_MODEL, D_MODEL, D_MODEL)
IN_COLS = sum(SPLIT_SIZES)

kernel_name = 'hybrid_retention_hgrn2_rglru_moe_dit'


def rmsnorm(x, g):
    xf = x.astype(jnp.float32)
    y = xf * lax.rsqrt(jnp.mean(xf * xf, axis=-1, keepdims=True) + NORM_EPS)
    return (y * g.astype(jnp.float32)).astype(x.dtype)


def modulate(h, shift, scale):
    return h * (1 + scale) + shift


def heads(t, n):
    B, L, W = t.shape
    return t.reshape(B, L, n, W // n).transpose(0, 2, 1, 3)


def head_rmsnorm(o, g):
    B, H, L, d = o.shape
    of = o.astype(jnp.float32)
    of = of * lax.rsqrt(jnp.mean(of * of, axis=-1, keepdims=True) + NORM_EPS)
    return of.transpose(0, 2, 1, 3).reshape(B, L, H * d) * g.astype(jnp.float32)


def axial_rotary_tables(n_tokens):
    rows = n_tokens // GRID_W
    row = jnp.repeat(jnp.arange(rows, dtype=jnp.float32), GRID_W)
    col = jnp.tile(jnp.arange(GRID_W, dtype=jnp.float32), rows)
    n_freq = RET_DK // 4
    inv = ROPE_BASE ** (-jnp.arange(n_freq, dtype=jnp.float32) / n_freq)
    ang = jnp.concatenate([row[:, None] * inv, col[:, None] * inv], axis=-1)
    return jnp.cos(ang), jnp.sin(ang)


def apply_rotary(t, cos, sin):
    half = t.shape[-1] // 2
    t1, t2 = t[..., :half], t[..., half:]
    return jnp.concatenate([t1 * cos - t2 * sin, t1 * sin + t2 * cos], axis=-1).astype(t.dtype)


def chunk_scan(q, k, v, log_f, s0, include_diag):
    f32 = jnp.float32
    B, H, L, _ = q.shape
    n = L // CHUNK

    def blocks(t):
        return t.astype(f32).reshape(B, H, n, CHUNK, t.shape[-1]).transpose(2, 0, 1, 3, 4)

    mask = jnp.tril(jnp.ones((CHUNK, CHUNK), dtype=bool), 0 if include_diag else -1)
    scalar_decay = log_f.shape[-1] == 1

    def step(S, xs):
        qb, kb, vb, lfb = xs
        g = jnp.cumsum(lfb, axis=2)
        g_last = g[:, :, -1:]
        rel = jnp.exp(jnp.where(mask[:, :, None], g[:, :, :, None] - g[:, :, None], -jnp.inf))
        if scalar_decay:
            scores = jnp.einsum('bhid,bhjd->bhij', qb, kb) * rel[..., 0]
        else:
            scores = jnp.einsum('bhid,bhjd,bhijd->bhij', qb, kb, rel)
        o = jnp.einsum('bhij,bhjv->bhiv', scores, vb) + jnp.einsum('bhid,bhdv->bhiv', qb * jnp.exp(g), S)
        S_new = jnp.exp(g_last)[:, :, 0, :, None] * S + jnp.einsum('bhjd,bhjv->bhdv', kb * jnp.exp(g_last - g), vb)
        return S_new, o

    s_fin, o = lax.scan(step, s0.astype(f32), (blocks(q), blocks(k), blocks(v), blocks(log_f)))
    return o.transpose(1, 2, 0, 3, 4).reshape(B, H, L, v.shape[-1]), s_fin


def bidir_linear_recurrence(q, v, k_f, lf_f, k_b, lf_b, n_ctx, bwd_diag):
    c_ = lambda t: t[:, :, :n_ctx]
    l_ = lambda t: t[:, :, n_ctx:]
    r_ = lambda t: jnp.flip(t, axis=2)
    B, H, _, dk = q.shape
    s0 = jnp.zeros((B, H, dk, v.shape[-1]), jnp.float32)
    oc_f, sc_f = chunk_scan(c_(q), c_(k_f), c_(v), c_(lf_f), s0, True)
    ol_f, _ = chunk_scan(l_(q), l_(k_f), l_(v), l_(lf_f), sc_f, True)
    oc_b, sc_b = chunk_scan(r_(c_(q)), r_(c_(k_b)), r_(c_(v)), r_(c_(lf_b)), s0, bwd_diag)
    ol_b, _ = chunk_scan(r_(l_(q)), r_(l_(k_b)), r_(l_(v)), r_(l_(lf_b)), sc_b, bwd_diag)
    return oc_f + r_(oc_b), ol_f + r_(ol_b)


def hgrn_forget(z, lb):
    z = z.astype(jnp.float32)
    log_f = jnp.logaddexp(jnp.log(lb), jnp.log1p(-lb) + jax.nn.log_sigmoid(z))
    key_in = (1.0 - lb) * jax.nn.sigmoid(-z)
    return heads(key_in, HG_HEADS), heads(log_f, HG_HEADS)


def depthwise_conv(x, w, b):
    y = lax.conv_general_dilated(x, w[:, None, :].astype(x.dtype), window_strides=(1,), padding=[CONV_PAD],
                                 dimension_numbers=('NWC', 'WIO', 'NWC'), feature_group_count=x.shape[-1])
    return y + b.astype(x.dtype)


def rglru_coeffs(xs, w_a, b_a, w_x, b_x, lam):
    f32 = jnp.float32
    B, L, C = xs.shape
    xf = xs.astype(f32)
    xb = xf.reshape(B, L, LRU_BLOCKS, LRU_BD)
    r = jax.nn.sigmoid(jnp.einsum('blhi,hij->blhj', xb, w_a.astype(f32)).reshape(B, L, C) + b_a.astype(f32))
    i = jax.nn.sigmoid(jnp.einsum('blhi,hij->blhj', xb, w_x.astype(f32)).reshape(B, L, C) + b_x.astype(f32))
    log_a = -LRU_C * r * jax.nn.softplus(-lam.astype(f32))
    a = jnp.exp(log_a)
    b = jnp.sqrt(-jnp.expm1(2.0 * log_a)) * (i * xf)
    return a, b


def linear_scan(a, b, h0):
    b = b.at[:, 0].add(a[:, 0] * h0)
    _, h = lax.associative_scan(lambda l, r: (l[0] * r[0], r[0] * l[1] + r[1]), (a, b), axis=1)
    return h


def bidir_rglru(a_f, b_f, a_b, b_b, n_ctx):
    c_ = lambda t: t[:, :n_ctx]
    l_ = lambda t: t[:, n_ctx:]
    r_ = lambda t: jnp.flip(t, axis=1)
    h0 = jnp.zeros((a_f.shape[0], a_f.shape[2]), a_f.dtype)
    hc_f = linear_scan(c_(a_f), c_(b_f), h0)
    hl_f = linear_scan(l_(a_f), l_(b_f), hc_f[:, -1])
    hc_b = linear_scan(r_(c_(a_b)), r_(c_(b_b)), h0)
    hl_b = linear_scan(r_(l_(a_b)), r_(l_(b_b)), hc_b[:, -1])
    return hc_f + r_(hc_b), hl_f + r_(hl_b)


def merge_branches(o_ret, o_hg, y_lru, cols, p):
    rg, hgt, lgt, gr, gh, gl = cols
    ret = (head_rmsnorm(o_ret, p['ret_gn_g']) * jax.nn.silu(rg)) @ p['w_ret_o']
    hgr = (head_rmsnorm(o_hg, p['hgrn_gn_g']) * jax.nn.sigmoid(hgt)) @ p['w_hgrn_o']
    lru = (y_lru * jax.nn.gelu(lgt)) @ p['w_lru_o']
    merged = jax.nn.sigmoid(gr) * ret + jax.nn.sigmoid(gh) * hgr + jax.nn.sigmoid(gl) * lru
    return merged @ p['w_out']


def moe_ffn(h, router_w, router_b, w_gu, b_gu, w_down, b_down):
    T, D = h.shape
    logits = h.astype(jnp.float32) @ router_w.astype(jnp.float32) + router_b.astype(jnp.float32)
    top_val, top_idx = lax.top_k(logits, TOP_K)
    top_w = jax.nn.softmax(top_val, axis=-1).astype(h.dtype)
    n_assign = T * TOP_K
    flat_e = top_idx.reshape(-1)
    flat_tok = jnp.repeat(jnp.arange(T, dtype=jnp.int32), TOP_K)
    order = jnp.argsort(flat_e)
    se, stok, sw = flat_e[order], flat_tok[order], top_w.reshape(-1)[order]
    counts = jnp.bincount(flat_e, length=N_EXPERTS)
    padded = (counts + MOE_BLOCK - 1) // MOE_BLOCK * MOE_BLOCK
    pad_end = jnp.cumsum(padded)
    pad_start = pad_end - padded
    grp_start = jnp.cumsum(counts) - counts
    dest = pad_start[se] + jnp.arange(n_assign, dtype=jnp.int32) - grp_start[se]
    n_blocks = -(-n_assign // MOE_BLOCK) + N_EXPERTS
    slot_tok = jnp.zeros((n_blocks * MOE_BLOCK,), jnp.int32).at[dest].set(stok)
    block_exp = jnp.minimum(jnp.searchsorted(pad_end, jnp.arange(n_blocks, dtype=jnp.int32) * MOE_BLOCK, side='right'),
                            N_EXPERTS - 1)

    def run_block(args):
        tok, e = args
        xb = h[tok]
        gu = xb @ w_gu[e] + b_gu[e]
        gate = jnp.minimum(gu[:, :D_EXPERT], SWIGLU_LIMIT)
        up = jnp.clip(gu[:, D_EXPERT:], -SWIGLU_LIMIT, SWIGLU_LIMIT)
        glu = gate * jax.nn.sigmoid(SWIGLU_ALPHA * gate)
        return ((up + 1) * glu) @ w_down[e] + b_down[e]

    y_slots = lax.map(run_block, (slot_tok.reshape(n_blocks, MOE_BLOCK), block_exp)).reshape(-1, D)
    y = y_slots[dest] * sw[:, None]
    return jnp.zeros((T, D), y.dtype).at[stok].add(y)


def trunk_layer(xc, xl, mod_c, mod_l, p, cos, sin, last):
    f32 = jnp.float32
    B, Lc, D = xc.shape
    L = xl.shape[1]
    Lt = Lc + L
    sh1c, sc1c, g1c, sh2c, sc2c, g2c = [mod_c[i] for i in range(N_MOD)]
    sh1l, sc1l, g1l, sh2l, sc2l, g2l = [mod_l[:, None, i] for i in range(N_MOD)]

    hc = modulate(rmsnorm(xc, p['norm1_g']), sh1c, sc1c)
    hl = modulate(rmsnorm(xl, p['norm1_g']), sh1l, sc1l)
    u = jnp.concatenate([hc, hl], axis=1) @ p['w_in']
    points = np.cumsum(SPLIT_SIZES)[:-1].tolist()
    rq, rk, rv, rg, hq, hff, hfb, hi, hgt, lx, lgt, gr, gh, gl = jnp.split(u, points, axis=-1)

    q = heads(rq, RET_HEADS)
    k = heads(rk, RET_HEADS) * RET_DK ** -0.5
    v = heads(rv, RET_HEADS)
    q = jnp.concatenate([q[:, :, :Lc], apply_rotary(q[:, :, Lc:], cos, sin)], axis=2)
    k = jnp.concatenate([k[:, :, :Lc], apply_rotary(k[:, :, Lc:], cos, sin)], axis=2)
    log_gamma = jax.nn.log_sigmoid(p['ret_decay'].astype(f32))
    lg_f = jnp.broadcast_to(log_gamma[0][None, :, None, None], (B, RET_HEADS, Lt, 1))
    lg_b = jnp.broadcast_to(log_gamma[1][None, :, None, None], (B, RET_HEADS, Lt, 1))
    ret_c, ret_l = bidir_linear_recurrence(q, v, k, lg_f, k, lg_b, Lc, bwd_diag=False)

    hq_h = heads(jax.nn.silu(hq), HG_HEADS)
    hv_h = heads(hi, HG_HEADS)
    kf, lff = hgrn_forget(hff, p['hgrn_lb'][0])
    kb, lfb = hgrn_forget(hfb, p['hgrn_lb'][1])
    hg_c, hg_l = bidir_linear_recurrence(hq_h, hv_h, kf, lff, kb, lfb, Lc, bwd_diag=True)

    xconv = jnp.concatenate([depthwise_conv(lx[:, :Lc], p['lru_conv_w'], p['lru_conv_b']),
                             depthwise_conv(lx[:, Lc:], p['lru_conv_w'], p['lru_conv_b'])], axis=1)
    a_f, b_f = rglru_coeffs(xconv, p['lru_wa'][0], p['lru_ba'][0], p['lru_wx'][0], p['lru_bx'][0], p['lru_lambda'][0])
    a_b, b_b = rglru_coeffs(xconv, p['lru_wa'][1], p['lru_ba'][1], p['lru_wx'][1], p['lru_bx'][1], p['lru_lambda'][1])
    lru_c, lru_l = bidir_rglru(a_f, b_f, a_b, b_b, Lc)

    side_l = lambda t: t[:, Lc:]
    side_c = lambda t: t[:, :Lc]
    mix_l = merge_branches(ret_l, hg_l, lru_l, [side_l(t) for t in (rg, hgt, lgt, gr, gh, gl)], p)
    xl = xl + (g1l * mix_l).astype(xl.dtype)
    h2l = modulate(rmsnorm(xl, p['norm2_g']), sh2l, sc2l)

    moe_args = (p['router_w'], p['router_b'], p['exp_w_gu'], p['exp_b_gu'], p['exp_w_down'], p['exp_b_down'])
    if last:
        ff_l = moe_ffn(h2l.reshape(-1, D), *moe_args).reshape(B, L, D)
        return None, xl + (g2l * ff_l).astype(xl.dtype)

    mix_c = merge_branches(ret_c, hg_c, lru_c, [side_c(t) for t in (rg, hgt, lgt, gr, gh, gl)], p)
    xc = xc + (g1c * mix_c).astype(xc.dtype)
    h2c = modulate(rmsnorm(xc, p['norm2_g']), sh2c, sc2c)

    ff = moe_ffn(jnp.concatenate([h2c, h2l], axis=1).reshape(-1, D), *moe_args).reshape(B, Lt, D)
    xc = xc + (g2c * ff[:, :Lc]).astype(xc.dtype)
    xl = xl + (g2l * ff[:, Lc:]).astype(xl.dtype)
    return xc, xl


def setup_inputs(seed: int = 0) -> dict:
    key = jax.random.key(seed)
    ks = list(jax.random.split(key, 40))
    f32 = jnp.float32

    def nrm(shape, scale):
        return jax.random.normal(ks.pop(), shape, f32) * scale

    D, E, F = D_MODEL, N_EXPERTS, D_EXPERT
    gamma0 = 1.0 - 2.0 ** (-5.0 - jnp.arange(RET_HEADS, dtype=f32))
    ret_logit0 = jnp.log(gamma0) - jnp.log1p(-gamma0)
    a0 = jax.random.uniform(ks.pop(), (DEPTH, 2, LRU_W), f32, minval=0.9, maxval=0.999)
    s_lam = a0 ** (1.0 / LRU_C)
    return {
        'x': nrm((BATCH, SEQ, D), 1.0),
        'c': nrm((BATCH, D), 1.0),
        'ctx': nrm((BATCH, CTX_LEN, D), 1.0),
        'c_ctx': nrm((D,), 1.0),
        'mod_w': nrm((DEPTH, D, N_MOD * D), 0.5 * D ** -0.5),
        'mod_b': nrm((DEPTH, N_MOD * D), 0.02),
        'norm1_g': 1.0 + nrm((DEPTH, D), 0.02),
        'norm2_g': 1.0 + nrm((DEPTH, D), 0.02),
        'final_g': 1.0 + nrm((D,), 0.02),
        'w_in': nrm((DEPTH, D, IN_COLS), D ** -0.5),
        'ret_decay': ret_logit0[None, None, :] + nrm((DEPTH, 2, RET_HEADS), 0.05),
        'ret_gn_g': 1.0 + nrm((DEPTH, RET_W), 0.02),
        'w_ret_o': nrm((DEPTH, RET_W, D), RET_W ** -0.5),
        'hgrn_lb_logits': nrm((DEPTH, 2, HG_W), 0.5),
        'hgrn_gn_g': 1.0 + nrm((DEPTH, HG_W), 0.02),
        'w_hgrn_o': nrm((DEPTH, HG_W, D), HG_W ** -0.5),
        'lru_conv_w': nrm((DEPTH, CONV_W, LRU_W), CONV_W ** -0.5),
        'lru_conv_b': nrm((DEPTH, LRU_W), 0.02),
        'lru_wa': nrm((DEPTH, 2, LRU_BLOCKS, LRU_BD, LRU_BD), LRU_BD ** -0.5),
        'lru_ba': nrm((DEPTH, 2, LRU_W), 0.02),
        'lru_wx': nrm((DEPTH, 2, LRU_BLOCKS, LRU_BD, LRU_BD), LRU_BD ** -0.5),
        'lru_bx': nrm((DEPTH, 2, LRU_W), 0.02),
        'lru_lambda': jnp.log(s_lam) - jnp.log1p(-s_lam),
        'w_lru_o': nrm((DEPTH, LRU_W, D), LRU_W ** -0.5),
        'w_out': nrm((DEPTH, D, D), D ** -0.5),
        'router_w': nrm((DEPTH, D, E), D ** -0.5),
        'router_b': nrm((DEPTH, E), 0.01),
        'exp_w_gu': nrm((DEPTH, E, D, 2 * F), D ** -0.5),
        'exp_b_gu': nrm((DEPTH, E, 2 * F), 0.02),
        'exp_w_down': nrm((DEPTH, E, F, D), F ** -0.5),
        'exp_b_down': nrm((DEPTH, E, D), 0.02),
    }


def reference(x, c, ctx, c_ctx, mod_w, mod_b, norm1_g, norm2_g, final_g, w_in, ret_decay, ret_gn_g, w_ret_o,
              hgrn_lb_logits, hgrn_gn_g, w_hgrn_o, lru_conv_w, lru_conv_b, lru_wa, lru_ba, lru_wx, lru_bx,
              lru_lambda, w_lru_o, w_out, router_w, router_b, exp_w_gu, exp_b_gu, exp_w_down, exp_b_down):
    B, L, D = x.shape
    cos, sin = axial_rotary_tables(L)
    lb_cum = jnp.cumsum(jax.nn.softmax(hgrn_lb_logits.astype(jnp.float32), axis=0), axis=0)
    lower_bounds = lb_cum - lb_cum[0]
    s_c = jax.nn.silu(c)
    s_ctx = jax.nn.silu(c_ctx)
    xc, xl = ctx, x
    for l in range(DEPTH):
        mod_l = (s_c @ mod_w[l] + mod_b[l]).reshape(B, N_MOD, D)
        mod_c = (s_ctx @ mod_w[l] + mod_b[l]).reshape(N_MOD, D)
        p = dict(norm1_g=norm1_g[l], norm2_g=norm2_g[l], w_in=w_in[l], ret_decay=ret_decay[l],
                 ret_gn_g=ret_gn_g[l], w_ret_o=w_ret_o[l], hgrn_lb=lower_bounds[l], hgrn_gn_g=hgrn_gn_g[l],
                 w_hgrn_o=w_hgrn_o[l], lru_conv_w=lru_conv_w[l], lru_conv_b=lru_conv_b[l], lru_wa=lru_wa[l],
                 lru_ba=lru_ba[l], lru_wx=lru_wx[l], lru_bx=lru_bx[l], lru_lambda=lru_lambda[l],
                 w_lru_o=w_lru_o[l], w_out=w_out[l], router_w=router_w[l], router_b=router_b[l],
                 exp_w_gu=exp_w_gu[l], exp_b_gu=exp_b_gu[l], exp_w_down=exp_w_down[l], exp_b_down=exp_b_down[l])
        xc, xl = trunk_layer(xc, xl, mod_c, mod_l, p, cos, sin, last=(l == DEPTH - 1))
    return rmsnorm(xl, final_g)
```

```python
import functools

import jax
import jax.numpy as jnp
from jax import lax
from jax.experimental import pallas as pl
from jax.experimental.pallas import tpu as pltpu

F32 = jnp.float32
BF16 = jnp.bfloat16

D_MODEL = 1024
BATCH = 4
SEQ = 4096
DEPTH = 4
GRID_W = 64
CTX_LEN = 256
N_MOD = 6
NORM_EPS = 1e-6
RET_HEADS = 8
RET_DK = 64
RET_W = 512
ROPE_BASE = 10000.0
HG_HEADS = 4
HG_DK = 128
HG_W = 512
LRU_W = 512
LRU_BLOCKS = 8
LRU_BD = 64
LRU_C = 8.0
CONV_W = 4
N_EXPERTS = 32
TOP_K = 4
SWIGLU_ALPHA = 1.702
SWIGLU_LIMIT = 7.0

LANES = 128
SUBLANES = 8
RET_CHUNK = 256
HG_CHUNK = 64
HG_SUB = 16
LRU_CHUNK = 256
MOE_BLOCK = 256
ROW_TILE = 256
ROUTE_TILE = 512
MOD_TN = 1536
VMEM_LIMIT = 56 * 1024 * 1024

_C_RQ, _C_RK, _C_RV, _C_RG = 0, 4, 8, 12
_C_HQ, _C_HFF, _C_HFB, _C_HI, _C_HGT = 16, 20, 24, 28, 32
_C_LX, _C_LGT = 36, 40
_C_GATES = 44


def _seg():
    return SEQ + CTX_LEN


def _in_cols():
    return _C_GATES * LANES + 3 * D_MODEL


def _dot(a, b):
    return jnp.dot(a, b, preferred_element_type=F32)


def _dot_nt(a, b):
    return lax.dot_general(a, b, (((1,), (1,)), ((), ())), preferred_element_type=F32)


def _dot_tn(a, b):
    return lax.dot_general(a, b, (((0,), (0,)), ((), ())), preferred_element_type=F32)


def _split3(x):
    hi = x.astype(BF16)
    r = x - hi.astype(F32)
    mid = r.astype(BF16)
    lo = (r - mid.astype(F32)).astype(BF16)
    return hi, mid, lo


def _dot_exact_lhs(a, x):
    hi, mid, lo = _split3(x)
    return _dot(a, hi) + _dot(a, mid) + _dot(a, lo)


def _dot_exact_rhs(x, a):
    hi, mid, lo = _split3(x)
    return _dot(hi, a) + _dot(mid, a) + _dot(lo, a)


def _dot3(a, b):
    ah, am, _ = _split3(a)
    bh, bm, _ = _split3(b)
    return _dot(ah, bh) + (_dot(ah, bm) + _dot(am, bh))


def _sigmoid(x):
    return 1.0 / (1.0 + jnp.exp(-x))


def _log_sigmoid(x):
    return jnp.minimum(x, 0.0) - jnp.log1p(jnp.exp(-jnp.abs(x)))


def _params(sem):
    return pltpu.CompilerParams(dimension_semantics=sem, vmem_limit_bytes=VMEM_LIMIT)


def _mod_kernel(c_ref, w_ref, b_ref, o_ref):
    c = c_ref[...]
    s = c * _sigmoid(c)
    o_ref[0] = _dot3(s, w_ref[0]) + b_ref[0]


def _mod_call(cvec, mod_w, mod_b):
    n6 = N_MOD * D_MODEL
    tn = MOD_TN
    return pl.pallas_call(
        _mod_kernel,
        grid=(DEPTH, n6 // tn),
        in_specs=[pl.BlockSpec((SUBLANES, D_MODEL), lambda l, j: (0, 0)),
                  pl.BlockSpec((1, D_MODEL, tn), lambda l, j: (l, 0, j)),
                  pl.BlockSpec((1, 1, tn), lambda l, j: (l, 0, j))],
        out_specs=pl.BlockSpec((1, SUBLANES, tn), lambda l, j: (l, 0, j)),
        out_shape=jax.ShapeDtypeStruct((DEPTH, SUBLANES, n6), F32),
        compiler_params=_params(("arbitrary", "arbitrary")),
        name="modulation",
    )(cvec, mod_w, mod_b.reshape(DEPTH, 1, n6))


def _norm_modulate(x, g, mod_l, mod_c, is_ctx, first):
    ms = jnp.mean(x * x, axis=-1, keepdims=True)
    y = x * lax.rsqrt(ms + NORM_EPS) * g
    sh = jnp.where(is_ctx, mod_c[first:first + 1, :], mod_l[first:first + 1, :])
    sc = jnp.where(is_ctx, mod_c[first + 1:first + 2, :], mod_l[first + 1:first + 2, :])
    return y * (1.0 + sc) + sh


def _inproj_kernel(x_ref, ml_ref, mc_ref, g_ref, w_ref, o_ref, h_scr, *, tm):
    @pl.when(pl.program_id(1) == 0)
    def _():
        tiles_per_b = _seg() // tm
        row0 = (pl.program_id(0) % tiles_per_b) * tm
        rows = row0 + lax.broadcasted_iota(jnp.int32, (tm, 1), 0)
        h = _norm_modulate(x_ref[...], g_ref[...], ml_ref[0], mc_ref[0], rows >= SEQ, 0)
        h_scr[...] = h.astype(BF16)

    o_ref[...] = _dot(h_scr[...], w_ref[...])


def _inproj_call(xs, mod, g, w):
    t = xs.shape[0]
    tm = _seg() // 2
    ncol = _in_cols()
    tn = 512
    tiles_per_b = _seg() // tm
    return pl.pallas_call(
        functools.partial(_inproj_kernel, tm=tm),
        grid=(t // tm, ncol // tn),
        in_specs=[pl.BlockSpec((tm, D_MODEL), lambda i, j: (i, 0)),
                  pl.BlockSpec((1, N_MOD, D_MODEL), lambda i, j: (i // tiles_per_b, 0, 0)),
                  pl.BlockSpec((1, N_MOD, D_MODEL), lambda i, j: (BATCH, 0, 0)),
                  pl.BlockSpec((1, D_MODEL), lambda i, j: (0, 0)),
                  pl.BlockSpec((D_MODEL, tn), lambda i, j: (0, j))],
        out_specs=pl.BlockSpec((tm, tn), lambda i, j: (i, j)),
        out_shape=jax.ShapeDtypeStruct((t, ncol), F32),
        scratch_shapes=[pltpu.VMEM((tm, D_MODEL), BF16)],
        compiler_params=_params(("arbitrary", "arbitrary")),
        name="in_projection",
    )(xs, mod, mod, g, w)


def _ret_kernel(dec_ref, cos_ref, sin_ref, q_ref, k_ref, v_ref, o_ref, mask_scr, pw_scr):
    c_len = RET_CHUNK
    n_lat = SEQ // c_len
    n_all = _seg() // c_len
    lane = lax.broadcasted_iota(jnp.int32, (1, LANES), 1)
    head0 = lane < RET_DK
    first_half = (lane % RET_DK) < (RET_DK // 2)

    lg = _log_sigmoid(dec_ref[0])
    dvf = jnp.where(head0, lg[0:1], lg[1:2])
    dvb = jnp.where(head0, lg[2:3], lg[3:4])
    r = lax.broadcasted_iota(jnp.int32, (c_len, 1), 0).astype(F32)
    pw_scr[0] = jnp.exp((r + 1.0) * dvf)
    pw_scr[1] = jnp.exp((c_len - 1.0 - r) * dvf)
    pw_scr[2] = jnp.exp((c_len - r) * dvb)
    pw_scr[3] = jnp.exp(r * dvb)
    dcf = jnp.exp(c_len * dvf)
    dcb = jnp.exp(c_len * dvb)
    ri = lax.broadcasted_iota(jnp.int32, (c_len, c_len), 0)
    ci = lax.broadcasted_iota(jnp.int32, (c_len, c_len), 1)
    dm = (ri - ci).astype(F32)
    for h in range(2):
        lgf = jnp.concatenate([lg[h:h + 1]] * (c_len // LANES), axis=1)
        lgb = jnp.concatenate([lg[2 + h:3 + h]] * (c_len // LANES), axis=1)
        mask_scr[h] = jnp.exp(jnp.where(dm >= 0, dm * lgf, -dm * lgb))
    rb = lax.broadcasted_iota(jnp.int32, (LANES, LANES), 0) < RET_DK
    cb = lax.broadcasted_iota(jnp.int32, (LANES, LANES), 1) < RET_DK
    same_head = (rb == cb).astype(F32)

    def load(c):
        rows = pl.ds(pl.multiple_of(c * c_len, c_len), c_len)
        q = q_ref[rows, :]
        k = k_ref[rows, :] * (RET_DK ** -0.5)
        v = v_ref[rows, :]
        return rows, q, k, v

    def rotary(t, c):
        rows = pl.ds(pl.multiple_of(c * c_len, c_len), c_len)
        partner = jnp.where(first_half, pltpu.roll(t, LANES - RET_DK // 2, 1), pltpu.roll(t, RET_DK // 2, 1))
        return t * cos_ref[rows, :] + partner * sin_ref[rows, :]

    def load_rot(c):
        rows, q, k, v = load(c)
        is_lat = c < n_lat
        cc = jnp.minimum(c, n_lat - 1)
        q = jnp.where(is_lat, rotary(q, cc), q)
        k = jnp.where(is_lat, rotary(k, cc), k)
        return rows, q, k, v

    def fwd_step(s, st):
        c = jnp.where(s < n_all - n_lat, n_lat + s, s - (n_all - n_lat))
        rows, q, k, v = load_rot(c)
        kb = k.astype(BF16)
        vb = v.astype(BF16)
        o0 = _dot((_dot_nt(jnp.where(head0, q, 0.0).astype(BF16), kb) * mask_scr[0]).astype(BF16), vb)
        o1 = _dot((_dot_nt(jnp.where(head0, 0.0, q).astype(BF16), kb) * mask_scr[1]).astype(BF16), vb)
        o = jnp.where(head0, o0, o1)
        o = o + _dot_nt((q * pw_scr[0]).astype(BF16), st.astype(BF16))
        o_ref[rows, :] = o
        return st * dcf + _dot_tn(vb, (k * pw_scr[1]).astype(BF16)) * same_head

    def bwd_step(s, st):
        c = n_all - 1 - s
        rows, q, k, v = load_rot(c)
        o_ref[rows, :] += _dot_nt((q * pw_scr[2]).astype(BF16), st.astype(BF16))
        return st * dcb + _dot_tn(v.astype(BF16), (k * pw_scr[3]).astype(BF16)) * same_head

    zero = jnp.zeros((LANES, LANES), F32)
    lax.fori_loop(0, n_all, fwd_step, zero)
    lax.fori_loop(0, n_all, bwd_step, zero)


def _ret_call(u, dec, cos_t, sin_t):
    t = u.shape[0]
    sb = _seg()
    col = lambda off: pl.BlockSpec((sb, LANES), lambda b, p: (b, off + p))
    return pl.pallas_call(
        _ret_kernel,
        grid=(BATCH, RET_HEADS // 2),
        in_specs=[pl.BlockSpec((1, 4, LANES), lambda b, p: (p, 0, 0)),
                  pl.BlockSpec((SEQ, LANES), lambda b, p: (0, 0)),
                  pl.BlockSpec((SEQ, LANES), lambda b, p: (0, 0)),
                  col(_C_RQ), col(_C_RK), col(_C_RV)],
        out_specs=pl.BlockSpec((sb, LANES), lambda b, p: (b, p)),
        out_shape=jax.ShapeDtypeStruct((t, RET_W), F32),
        scratch_shapes=[pltpu.VMEM((2, RET_CHUNK, RET_CHUNK), F32),
                        pltpu.VMEM((4, RET_CHUNK, LANES), F32)],
        compiler_params=_params(("arbitrary", "arbitrary")),
        name="retention",
    )(dec, cos_t, sin_t, u, u, u)


def _hg_chunk(fwd, q, z, v, lb, st):
    c_len = HG_CHUNK
    n_sub = c_len // HG_SUB
    row = lax.broadcasted_iota(jnp.int32, (c_len, 1), 0)
    ri = lax.broadcasted_iota(jnp.int32, (c_len, c_len), 0)
    ci = lax.broadcasted_iota(jnp.int32, (c_len, c_len), 1)
    tri = (ri >= ci) if fwd else (ri <= ci)

    a = jnp.log(lb)
    b = jnp.log1p(-lb) + _log_sigmoid(z)
    lf = jnp.maximum(a, b) + jnp.log1p(jnp.exp(-jnp.abs(a - b)))
    kin = (1.0 - lb) * _sigmoid(-z)

    g = _dot_exact_lhs(tri.astype(BF16), lf)
    g_end = g[c_len - 1:c_len, :] if fwd else g[0:1, :]
    vb = v.astype(BF16)

    o = _dot_nt((q * jnp.exp(g)).astype(BF16), st.astype(BF16))

    rb = row // HG_SUB
    blocks = range(1, n_sub) if fwd else range(0, n_sub - 1)
    bound = {}
    for i in blocks:
        e = i * HG_SUB - 1 if fwd else (i + 1) * HG_SUB
        bound[i] = g[e:e + 1, :]
    bsel = g
    for i in blocks:
        bsel = jnp.where(rb == i, bound[i], bsel)
    qe = q * jnp.exp(g - bsel)
    qcat = jnp.concatenate([jnp.where(rb == i, qe, 0.0) for i in blocks], axis=1).astype(BF16)
    kparts = []
    for i in blocks:
        other = (rb < i) if fwd else (rb > i)
        kparts.append(jnp.where(other, kin * jnp.exp(jnp.minimum(bound[i] - g, 0.0)), 0.0))
    kcat = jnp.concatenate(kparts, axis=1).astype(BF16)
    o = o + _dot(_dot_nt(qcat, kcat).astype(BF16), vb)

    rm = row % HG_SUB
    for dl in range(HG_SUB):
        if dl == 0:
            w = q * kin
            s = jnp.sum(w, axis=-1, keepdims=True)
            o = o + s * v
            continue
        sh = dl if fwd else c_len - dl
        ksh = pltpu.roll(kin, sh, 0)
        gsh = pltpu.roll(g, sh, 0)
        vsh = pltpu.roll(v, sh, 0)
        valid = (rm >= dl) if fwd else (rm + dl < HG_SUB)
        w = jnp.where(valid, q * ksh * jnp.exp(jnp.minimum(g - gsh, 0.0)), 0.0)
        o = o + jnp.sum(w, axis=-1, keepdims=True) * vsh

    st_new = st * jnp.exp(g_end) + _dot_tn(vb, (kin * jnp.exp(g_end - g)).astype(BF16))
    return o, st_new


def _hg_kernel(lbl_ref, q_ref, zf_ref, zb_ref, v_ref, o_ref, *, layer):
    c_len = HG_CHUNK
    n_lat = SEQ // c_len
    n_all = _seg() // c_len

    lbs = []
    for d in range(2):
        x = lbl_ref[d]
        e = jnp.exp(x - jnp.max(x, axis=0, keepdims=True))
        tot = jnp.sum(e, axis=0, keepdims=True)
        part = jnp.zeros_like(tot)
        for i in range(1, layer + 1):
            part = part + e[i:i + 1, :]
        lbs.append(part / tot)

    o_ref[...] = jnp.zeros(o_ref.shape, F32)

    def step(s, carry):
        st_f, st_b = carry
        cf = jnp.where(s < n_all - n_lat, n_lat + s, s - (n_all - n_lat))
        cbk = n_all - 1 - s
        rf = pl.ds(pl.multiple_of(cf * c_len, c_len), c_len)
        rk = pl.ds(pl.multiple_of(cbk * c_len, c_len), c_len)
        qf = q_ref[rf, :]
        of, st_f = _hg_chunk(True, qf * _sigmoid(qf), zf_ref[rf, :], v_ref[rf, :], lbs[0], st_f)
        o_ref[rf, :] += of
        qb = q_ref[rk, :]
        ob, st_b = _hg_chunk(False, qb * _sigmoid(qb), zb_ref[rk, :], v_ref[rk, :], lbs[1], st_b)
        o_ref[rk, :] += ob
        return st_f, st_b

    zero = jnp.zeros((LANES, LANES), F32)
    lax.fori_loop(0, n_all, step, (zero, zero))


def _hg_call(u, lb_logits, layer):
    t = u.shape[0]
    sb = _seg()
    col = lambda off: pl.BlockSpec((sb, LANES), lambda b, h: (b, off + h))
    return pl.pallas_call(
        functools.partial(_hg_kernel, layer=layer),
        grid=(BATCH, HG_HEADS),
        in_specs=[pl.BlockSpec((2, DEPTH, LANES), lambda b, h: (0, 0, h)),
                  col(_C_HQ), col(_C_HFF), col(_C_HFB), col(_C_HI)],
        out_specs=pl.BlockSpec((sb, LANES), lambda b, h: (b, h)),
        out_shape=jax.ShapeDtypeStruct((t, HG_W), F32),
        compiler_params=_params(("arbitrary", "arbitrary")),
        name="hgrn2",
    )(lb_logits, u, u, u, u)


def _lru_kernel(x_ref, wg_ref, bg_ref, lam_ref, cw_ref, cb_ref, y_ref,
                xp_scr, af_scr, bf_scr, ab_scr, bb_scr, hb_scr):
    sb = _seg()
    c_len = LRU_CHUNK
    halo = SUBLANES
    n_tiles = sb // SUBLANES
    n_lat_t = SEQ // SUBLANES

    xp_scr[pl.ds(0, halo), :] = jnp.zeros((halo, LANES), F32)
    xp_scr[pl.ds(halo + sb, halo), :] = jnp.zeros((halo, LANES), F32)
    xp_scr[pl.ds(halo, sb), :] = x_ref[...]

    lam = lam_ref[0]
    sp = jnp.maximum(-lam, 0.0) + jnp.log1p(jnp.exp(-jnp.abs(lam)))
    cw = cw_ref[...]
    wg = wg_ref[0]
    bg = bg_ref[0]
    r8 = lax.broadcasted_iota(jnp.int32, (c_len, 1), 0)
    rm8 = r8 % SUBLANES

    def coeff_chunk(c, _):
        base = pl.multiple_of(c * c_len, c_len)
        w = xp_scr[pl.ds(base, c_len + 2 * halo), :]
        grow = base + r8
        in_lat = grow < SEQ
        pos = jnp.where(in_lat, grow, grow - SEQ)
        seglen = jnp.where(in_lat, SEQ, CTX_LEN)
        wlen = c_len + 2 * halo
        xm2 = jnp.where(pos >= 2, pltpu.roll(w, 2, 0)[halo:halo + c_len], 0.0)
        xm1 = jnp.where(pos >= 1, pltpu.roll(w, 1, 0)[halo:halo + c_len], 0.0)
        x0 = w[halo:halo + c_len]
        xp1 = jnp.where(pos + 1 < seglen, pltpu.roll(w, wlen - 1, 0)[halo:halo + c_len], 0.0)
        xc = xm2 * cw[0:1] + xm1 * cw[1:2] + x0 * cw[2:3] + xp1 * cw[3:4] + cb_ref[...]
        gates = _sigmoid(_dot(xc.astype(BF16), wg) + bg)
        rows = pl.ds(base, c_len)
        for d, (a_scr, b_scr) in enumerate(((af_scr, bf_scr), (ab_scr, bb_scr))):
            rg = gates[:, (2 * d) * LANES:(2 * d + 1) * LANES]
            ig = gates[:, (2 * d + 1) * LANES:(2 * d + 2) * LANES]
            log_a = -LRU_C * rg * sp[d:d + 1]
            a = jnp.exp(log_a)
            th = jnp.tanh(log_a)
            bv = jnp.sqrt(-2.0 * th / (1.0 - th)) * (ig * xc)
            for sft in (1, 2, 4):
                if d == 0:
                    a_s = pltpu.roll(a, sft, 0)
                    b_s = pltpu.roll(bv, sft, 0)
                    ok = rm8 >= sft
                else:
                    a_s = pltpu.roll(a, c_len - sft, 0)
                    b_s = pltpu.roll(bv, c_len - sft, 0)
                    ok = rm8 < SUBLANES - sft
                bv = jnp.where(ok, a * b_s + bv, bv)
                a = jnp.where(ok, a * a_s, a)
            a_scr[rows, :] = a
            b_scr[rows, :] = bv
        return 0

    lax.fori_loop(0, sb // c_len, coeff_chunk, 0)

    def carry_step(s, carry):
        hf, hb = carry
        tf = jnp.where(s < n_tiles - n_lat_t, n_lat_t + s, s - (n_tiles - n_lat_t))
        tb = n_tiles - 1 - s
        rf = pl.ds(pl.multiple_of(tf * SUBLANES, SUBLANES), SUBLANES)
        rk = pl.ds(pl.multiple_of(tb * SUBLANES, SUBLANES), SUBLANES)
        h1 = af_scr[rf, :] * hf + bf_scr[rf, :]
        y_ref[rf, :] = h1
        h2 = ab_scr[rk, :] * hb + bb_scr[rk, :]
        hb_scr[rk, :] = h2
        return (jnp.broadcast_to(h1[SUBLANES - 1:SUBLANES, :], (SUBLANES, LANES)),
                jnp.broadcast_to(h2[0:1, :], (SUBLANES, LANES)))

    zero = jnp.zeros((SUBLANES, LANES), F32)
    lax.fori_loop(0, n_tiles, carry_step, (zero, zero))
    y_ref[...] += hb_scr[...]


def _lru_call(u, wg, bg, lam, cw, cb):
    t = u.shape[0]
    sb = _seg()
    nb = LRU_W // LANES
    return pl.pallas_call(
        _lru_kernel,
        grid=(BATCH, nb),
        in_specs=[pl.BlockSpec((sb, LANES), lambda b, c: (b, _C_LX + c)),
                  pl.BlockSpec((1, LANES, 4 * LANES), lambda b, c: (c, 0, 0)),
                  pl.BlockSpec((1, 1, 4 * LANES), lambda b, c: (c, 0, 0)),
                  pl.BlockSpec((1, 2, LANES), lambda b, c: (c, 0, 0)),
                  pl.BlockSpec((CONV_W, LANES), lambda b, c: (0, c)),
                  pl.BlockSpec((1, LANES), lambda b, c: (0, c))],
        out_specs=pl.BlockSpec((sb, LANES), lambda b, c: (b, c)),
        out_shape=jax.ShapeDtypeStruct((t, LRU_W), F32),
        scratch_shapes=[pltpu.VMEM((sb + 2 * SUBLANES, LANES), F32)] + [pltpu.VMEM((sb, LANES), F32)] * 5,
        compiler_params=_params(("arbitrary", "arbitrary")),
        name="rglru",
    )(u, wg, bg, lam, cw, cb)


def _group_rms(o, group):
    ri = lax.broadcasted_iota(jnp.int32, (LANES, LANES), 0) // group
    ci = lax.broadcasted_iota(jnp.int32, (LANES, LANES), 1) // group
    ones_bd = (ri == ci).astype(BF16)
    outs = []
    for tix in range(o.shape[1] // LANES):
        x = o[:, tix * LANES:(tix + 1) * LANES]
        sq = x * x
        hi = sq.astype(BF16)
        lo = (sq - hi.astype(F32)).astype(BF16)
        ss = _dot(hi, ones_bd) + _dot(lo, ones_bd)
        outs.append(x * lax.rsqrt(ss * (1.0 / group) + NORM_EPS))
    return jnp.concatenate(outs, axis=1)


def _merge_kernel(*refs, n_gate):
    (x_ref, ml_ref, mc_ref, oret_ref, ohg_ref, ylru_ref, rg_ref, hgt_ref, lgt_ref) = refs[:9]
    gate_refs = refs[9:9 + 3 * n_gate]
    (gng_ref, ghg_ref, wr_ref, wh_ref, wl_ref, wo_ref, n2g_ref, rw_ref, rb_ref,
     xo_ref, h2_ref, lg_ref) = refs[9 + 3 * n_gate:]
    tiles_per_b = _seg() // ROW_TILE
    is_ctx = (pl.program_id(0) % tiles_per_b) >= SEQ // ROW_TILE
    ml = ml_ref[0]
    mc = mc_ref[0]

    def gate(i):
        parts = [gate_refs[i * n_gate + j][...] for j in range(n_gate)]
        return _sigmoid(parts[0] if n_gate == 1 else jnp.concatenate(parts, axis=1))

    rg = rg_ref[...]
    ret_in = _group_rms(oret_ref[...], RET_DK) * gng_ref[...] * (rg * _sigmoid(rg))
    ret = _dot(ret_in.astype(BF16), wr_ref[...])
    hg_in = _group_rms(ohg_ref[...], HG_DK) * ghg_ref[...] * _sigmoid(hgt_ref[...])
    hgr = _dot(hg_in.astype(BF16), wh_ref[...])
    lgt = lgt_ref[...]
    gelu = 0.5 * lgt * (1.0 + jnp.tanh(0.7978845608028654 * (lgt + 0.044715 * lgt * lgt * lgt)))
    lru = _dot((ylru_ref[...] * gelu).astype(BF16), wl_ref[...])
    merged = gate(0) * ret + gate(1) * hgr + gate(2) * lru
    mix = _dot(merged.astype(BF16), wo_ref[...])
    g1 = jnp.where(is_ctx, mc[2:3, :], ml[2:3, :])
    xn = x_ref[...] + g1 * mix
    xo_ref[...] = xn
    h2 = _norm_modulate(xn, n2g_ref[...], ml, mc, is_ctx, 3)
    h2_ref[...] = h2
    lg_ref[...] = _dot3(h2, rw_ref[...]) + rb_ref[...]


def _merge_call(xs, mod, o_ret, o_hg, y_lru, u, gn_g, hgn_g, wr, wh, wl, wo, n2g, rw, rb):
    t = xs.shape[0]
    r = ROW_TILE
    tiles_per_b = _seg() // r
    n_gate = D_MODEL // 512
    row = lambda w: pl.BlockSpec((r, w), lambda i: (i, 0))
    ucol = lambda blk: pl.BlockSpec((r, 512), lambda i: (i, blk))
    full = lambda a: pl.BlockSpec(a.shape, lambda i: (0,) * a.ndim)
    gate_base = _C_GATES * LANES // 512
    in_specs = [row(D_MODEL),
                pl.BlockSpec((1, N_MOD, D_MODEL), lambda i: (i // tiles_per_b, 0, 0)),
                pl.BlockSpec((1, N_MOD, D_MODEL), lambda i: (BATCH, 0, 0)),
                row(RET_W), row(HG_W), row(LRU_W),
                ucol(_C_RG * LANES // 512), ucol(_C_HGT * LANES // 512), ucol(_C_LGT * LANES // 512)]
    in_specs += [ucol(gate_base + j) for j in range(3 * n_gate)]
    consts = [gn_g, hgn_g, wr, wh, wl, wo, n2g, rw, rb]
    in_specs += [full(a) for a in consts]
    return pl.pallas_call(
        functools.partial(_merge_kernel, n_gate=n_gate),
        grid=(t // r,),
        in_specs=in_specs,
        out_specs=[row(D_MODEL), row(D_MODEL), row(LANES)],
        out_shape=[jax.ShapeDtypeStruct((t, D_MODEL), F32), jax.ShapeDtypeStruct((t, D_MODEL), F32),
                   jax.ShapeDtypeStruct((t, LANES), F32)],
        compiler_params=_params(("arbitrary",)),
        name="merge",
    )(xs, mod, mod, o_ret, o_hg, y_lru, *([u] * (3 + 3 * n_gate)), *consts)


def _route_kernel(lg_ref, oi_ref, ow_ref, cnt_ref, cnt_scr):
    @pl.when(pl.program_id(0) == 0)
    def _():
        cnt_scr[...] = jnp.zeros(cnt_scr.shape, F32)

    r = lg_ref.shape[0]
    lane = lax.broadcasted_iota(jnp.int32, (r, LANES), 1)
    neg = -jnp.inf
    l = jnp.where(lane < N_EXPERTS, lg_ref[...], neg)
    vals, idxs, hots = [], [], []
    lanef = lane.astype(F32)
    for _ in range(TOP_K):
        m = jnp.max(l, axis=-1, keepdims=True)
        idxf = jnp.min(jnp.where(l == m, lanef, float(LANES)), axis=-1, keepdims=True)
        hot = lanef == idxf
        l = jnp.where(hot, neg, l)
        vals.append(m)
        idxs.append(idxf.astype(jnp.int32))
        hots.append(hot)
    es = [jnp.exp(v - vals[0]) for v in vals]
    den = es[0]
    for e in es[1:]:
        den = den + e
    chosen = hots[0].astype(F32)
    for hot in hots[1:]:
        chosen = chosen + hot.astype(F32)
    ri = lax.broadcasted_iota(jnp.int32, (r, r), 0)
    ci = lax.broadcasted_iota(jnp.int32, (r, r), 1)
    before = _dot((ri > ci).astype(BF16), chosen.astype(BF16)) + cnt_scr[0:1, :]
    oi = jnp.zeros((r, LANES), jnp.int32)
    ow = jnp.zeros((r, LANES), F32)
    for k in range(TOP_K):
        rank = jnp.sum(jnp.where(hots[k], before, 0.0), axis=-1, keepdims=True).astype(jnp.int32)
        oi = jnp.where(lane == k, idxs[k], oi)
        oi = jnp.where(lane == TOP_K + k, rank, oi)
        ow = jnp.where(lane == k, es[k] / den, ow)
    oi_ref[...] = oi
    ow_ref[...] = ow
    cnt_scr[...] = cnt_scr[...] + jnp.sum(chosen, axis=0, keepdims=True)
    cnt_ref[...] = cnt_scr[...]


def _route_call(logits):
    t = logits.shape[0]
    r = ROUTE_TILE
    row = pl.BlockSpec((r, LANES), lambda i: (i, 0))
    return pl.pallas_call(
        _route_kernel,
        grid=(t // r,),
        in_specs=[row],
        out_specs=[row, row, pl.BlockSpec((SUBLANES, LANES), lambda i: (0, 0))],
        out_shape=[jax.ShapeDtypeStruct((t, LANES), jnp.int32), jax.ShapeDtypeStruct((t, LANES), F32),
                   jax.ShapeDtypeStruct((SUBLANES, LANES), F32)],
        scratch_shapes=[pltpu.VMEM((SUBLANES, LANES), F32)],
        compiler_params=_params(("arbitrary",)),
        name="route",
    )(logits)


def _dispatch_kernel(dest_ref, h_ref, xs_in_ref, xs_ref, sem):
    del xs_in_ref
    r = h_ref.shape[0]

    def copy(i, k):
        d = dest_ref[0, 0, i * TOP_K + k]
        return pltpu.make_async_copy(h_ref.at[pl.ds(i, 1)], xs_ref.at[pl.ds(d, 1)], sem)

    def start(i, _):
        for k in range(TOP_K):
            copy(i, k).start()
        return 0

    def wait(i, _):
        for k in range(TOP_K):
            copy(i, k).wait()
        return 0

    lax.fori_loop(0, r, start, 0)
    lax.fori_loop(0, r, wait, 0)


def _dispatch_call(h2, dest, slots):
    t = h2.shape[0]
    r = ROW_TILE
    return pl.pallas_call(
        _dispatch_kernel,
        grid=(t // r,),
        in_specs=[pl.BlockSpec((1, 1, r * TOP_K), lambda i: (i, 0, 0), memory_space=pltpu.SMEM),
                  pl.BlockSpec((r, D_MODEL), lambda i: (i, 0)),
                  pl.BlockSpec(memory_space=pl.ANY)],
        out_specs=pl.BlockSpec(memory_space=pl.ANY),
        out_shape=jax.ShapeDtypeStruct(slots.shape, F32),
        scratch_shapes=[pltpu.SemaphoreType.DMA(())],
        input_output_aliases={2: 0},
        compiler_params=_params(("arbitrary",)),
        name="dispatch",
    )(dest.reshape(t // r, 1, r * TOP_K), h2, slots)


def _expert_kernel(bexp_ref, nv_ref, x_ref, wgu_ref, bgu_ref, wd_ref, bd_ref, y_ref, wgu_scr, wd_scr):
    i = pl.program_id(0)
    f = D_MODEL
    prev = bexp_ref[jnp.maximum(i - 1, 0)]
    new_expert = jnp.logical_or(i == 0, bexp_ref[i] != prev)

    @pl.when(jnp.logical_and(i < nv_ref[0], new_expert))
    def _():
        slab = 256
        for s in range(D_MODEL // slab):
            wgu_scr[pl.ds(s * slab, slab), :] = wgu_ref[0, pl.ds(s * slab, slab), :].astype(BF16)
            wd_scr[pl.ds(s * slab, slab), :] = wd_ref[0, pl.ds(s * slab, slab), :].astype(BF16)

    @pl.when(i < nv_ref[0])
    def _():
        gu = _dot(x_ref[...].astype(BF16), wgu_scr[...]) + bgu_ref[0]
        gate = jnp.minimum(gu[:, :f], SWIGLU_LIMIT)
        up = jnp.clip(gu[:, f:], -SWIGLU_LIMIT, SWIGLU_LIMIT)
        glu = gate * _sigmoid(SWIGLU_ALPHA * gate)
        y_ref[...] = _dot(((up + 1.0) * glu).astype(BF16), wd_scr[...]) + bd_ref[0]

    @pl.when(i >= nv_ref[0])
    def _():
        y_ref[...] = jnp.zeros(y_ref.shape, F32)


def _expert_call(slots, bexp, nvalid, w_gu, b_gu, w_down, b_down):
    s = slots.shape[0]
    nb = s // MOE_BLOCK
    f = D_MODEL
    blk = lambda i, be, nv: (jnp.minimum(i, nv[0] - 1), 0)
    exp3 = lambda i, be, nv: (be[jnp.minimum(i, nv[0] - 1)], 0, 0)
    grid_spec = pltpu.PrefetchScalarGridSpec(
        num_scalar_prefetch=2,
        grid=(nb,),
        in_specs=[pl.BlockSpec((MOE_BLOCK, D_MODEL), blk),
                  pl.BlockSpec((1, D_MODEL, 2 * f), exp3),
                  pl.BlockSpec((1, 1, 2 * f), exp3),
                  pl.BlockSpec((1, f, D_MODEL), exp3),
                  pl.BlockSpec((1, 1, D_MODEL), exp3)],
        out_specs=pl.BlockSpec((MOE_BLOCK, D_MODEL), lambda i, be, nv: (i, 0)),
        scratch_shapes=[pltpu.VMEM((D_MODEL, 2 * f), BF16), pltpu.VMEM((f, D_MODEL), BF16)],
    )
    return pl.pallas_call(
        _expert_kernel,
        grid_spec=grid_spec,
        out_shape=jax.ShapeDtypeStruct((s, D_MODEL), F32),
        compiler_params=_params(("arbitrary",)),
        name="experts",
    )(bexp, nvalid, slots, w_gu, b_gu.reshape(N_EXPERTS, 1, 2 * f), w_down, b_down.reshape(N_EXPERTS, 1, D_MODEL))


def _combine_kernel(dest_ref, x_ref, ml_ref, mc_ref, w_ref, fg_ref, y_ref, o_ref, buf, sem, *, last):
    r = x_ref.shape[0]

    def copy(i, k):
        d = dest_ref[0, 0, i * TOP_K + k]
        return pltpu.make_async_copy(y_ref.at[pl.ds(d, 1)], buf.at[k, pl.ds(i, 1)], sem)

    def start(i, _):
        for k in range(TOP_K):
            copy(i, k).start()
        return 0

    def wait(i, _):
        for k in range(TOP_K):
            copy(i, k).wait()
        return 0

    lax.fori_loop(0, r, start, 0)
    lax.fori_loop(0, r, wait, 0)

    if last:
        is_ctx = False
    else:
        tiles_per_b = _seg() // ROW_TILE
        is_ctx = (pl.program_id(1)) >= SEQ // ROW_TILE
    w = w_ref[...]
    ff = w[:, 0:1] * buf[0]
    for k in range(1, TOP_K):
        ff = ff + w[:, k:k + 1] * buf[k]
    g2 = jnp.where(is_ctx, mc_ref[0][5:6, :], ml_ref[0][5:6, :])
    xn = x_ref[...] + g2 * ff
    if last:
        ms = jnp.mean(xn * xn, axis=-1, keepdims=True)
        xn = xn * lax.rsqrt(ms + NORM_EPS) * fg_ref[...]
    o_ref[...] = xn


def _combine_call(xs, mod, dest, wts, y_slots, final_g, last):
    t = xs.shape[0]
    r = ROW_TILE
    tiles_per_b = _seg() // r
    n_t = SEQ // r if last else tiles_per_b
    tile = lambda b, j: b * tiles_per_b + j
    out_rows = BATCH * SEQ if last else t
    return pl.pallas_call(
        functools.partial(_combine_kernel, last=last),
        grid=(BATCH, n_t),
        in_specs=[pl.BlockSpec((1, 1, r * TOP_K), lambda b, j: (tile(b, j), 0, 0), memory_space=pltpu.SMEM),
                  pl.BlockSpec((r, D_MODEL), lambda b, j: (tile(b, j), 0)),
                  pl.BlockSpec((1, N_MOD, D_MODEL), lambda b, j: (b, 0, 0)),
                  pl.BlockSpec((1, N_MOD, D_MODEL), lambda b, j: (BATCH, 0, 0)),
                  pl.BlockSpec((r, LANES), lambda b, j: (tile(b, j), 0)),
                  pl.BlockSpec((1, D_MODEL), lambda b, j: (0, 0)),
                  pl.BlockSpec(memory_space=pl.ANY)],
        out_specs=pl.BlockSpec((r, D_MODEL), lambda b, j: (b * n_t + j, 0)),
        out_shape=jax.ShapeDtypeStruct((out_rows, D_MODEL), F32),
        scratch_shapes=[pltpu.VMEM((TOP_K, r, D_MODEL), F32), pltpu.SemaphoreType.DMA(())],
        compiler_params=_params(("arbitrary", "arbitrary")),
        name="combine",
    )(dest.reshape(t // r, 1, r * TOP_K), xs, mod, mod, wts, final_g, y_slots)


def _rotary_tables():
    rows = SEQ // GRID_W
    row = jnp.repeat(jnp.arange(rows, dtype=F32), GRID_W)
    colp = jnp.tile(jnp.arange(GRID_W, dtype=F32), rows)
    n_freq = RET_DK // 4
    inv = ROPE_BASE ** (-jnp.arange(n_freq, dtype=F32) / n_freq)
    ang = jnp.concatenate([row[:, None] * inv, colp[:, None] * inv], axis=-1)
    cos, sin = jnp.cos(ang), jnp.sin(ang)
    reps = LANES // (RET_DK // 2)
    cos_t = jnp.tile(cos, (1, reps))
    sign = jnp.where((jnp.arange(LANES) % RET_DK) < RET_DK // 2, -1.0, 1.0).astype(F32)
    sin_t = jnp.tile(sin, (1, reps)) * sign
    return cos_t, sin_t


def _lru_gate_weights(wa, ba, wx, bx):
    nb = LRU_W // LANES
    per = LANES // LRU_BD
    eye = jnp.eye(per, dtype=F32)

    def embed(w):
        w = w.reshape(nb, per, LRU_BD, LRU_BD)
        return jnp.einsum('cpij,pq->cpiqj', w, eye).reshape(nb, LANES, LANES)

    wg = jnp.concatenate([embed(wa[0]), embed(wx[0]), embed(wa[1]), embed(wx[1])], axis=-1)
    bias = jnp.concatenate([ba[0].reshape(nb, 1, LANES), bx[0].reshape(nb, 1, LANES),
                            ba[1].reshape(nb, 1, LANES), bx[1].reshape(nb, 1, LANES)], axis=-1)
    return wg.astype(BF16), bias


def kernel(x, c, ctx, c_ctx, mod_w, mod_b, norm1_g, norm2_g, final_g, w_in, ret_decay, ret_gn_g, w_ret_o,
           hgrn_lb_logits, hgrn_gn_g, w_hgrn_o, lru_conv_w, lru_conv_b, lru_wa, lru_ba, lru_wx, lru_bx,
           lru_lambda, w_lru_o, w_out, router_w, router_b, exp_w_gu, exp_b_gu, exp_w_down, exp_b_down):
    assert CTX_LEN == RET_CHUNK and SEQ % RET_CHUNK == 0
    sb = _seg()
    t = BATCH * sb
    xs = jnp.concatenate([x, ctx], axis=1).reshape(t, D_MODEL)

    cvec = jnp.concatenate([c, c_ctx[None, :], jnp.zeros((SUBLANES - BATCH - 1, D_MODEL), F32)], axis=0)
    mod_all = _mod_call(cvec, mod_w, mod_b).reshape(DEPTH, SUBLANES, N_MOD, D_MODEL)

    cos_t, sin_t = _rotary_tables()
    lb_logits = jnp.transpose(hgrn_lb_logits, (1, 0, 2))
    n_slots = t * TOP_K + N_EXPERTS * MOE_BLOCK
    n_blocks = n_slots // MOE_BLOCK
    slots = jnp.zeros((n_slots, D_MODEL), F32)
    rw_pad = jnp.pad(router_w, ((0, 0), (0, 0), (0, LANES - N_EXPERTS)))
    rb_pad = jnp.pad(router_b, ((0, 0), (0, LANES - N_EXPERTS)))

    out = None
    for l in range(DEPTH):
        last = l == DEPTH - 1
        mod = mod_all[l]
        u = _inproj_call(xs, mod, norm1_g[l][None, :], w_in[l].astype(BF16))

        dec = jnp.broadcast_to(ret_decay[l].reshape(2, RET_HEADS // 2, 2, 1).transpose(1, 0, 2, 3),
                               (RET_HEADS // 2, 2, 2, LANES)).reshape(RET_HEADS // 2, 4, LANES)
        o_ret = _ret_call(u, dec, cos_t, sin_t)
        o_hg = _hg_call(u, lb_logits, l)
        wg, bg = _lru_gate_weights(lru_wa[l], lru_ba[l], lru_wx[l], lru_bx[l])
        lam = lru_lambda[l].reshape(2, LRU_W // LANES, LANES).transpose(1, 0, 2)
        y_lru = _lru_call(u, wg, bg, lam, lru_conv_w[l], lru_conv_b[l][None, :])

        xs, h2, logits = _merge_call(
            xs, mod, o_ret, o_hg, y_lru, u, ret_gn_g[l][None, :], hgrn_gn_g[l][None, :],
            w_ret_o[l].astype(BF16), w_hgrn_o[l].astype(BF16), w_lru_o[l].astype(BF16), w_out[l].astype(BF16),
            norm2_g[l][None, :], rw_pad[l], rb_pad[l][None, :])

        oi, wts, cnt = _route_call(logits)
        idx = oi[:, :TOP_K]
        rank = oi[:, TOP_K:2 * TOP_K]
        counts = cnt[0, :N_EXPERTS].astype(jnp.int32)
        padded = (counts + MOE_BLOCK - 1) // MOE_BLOCK * MOE_BLOCK
        pad_end = jnp.cumsum(padded)
        pad_start = pad_end - padded
        dest = (pad_start[idx] + rank).astype(jnp.int32)
        bexp = jnp.minimum(jnp.searchsorted(pad_end, jnp.arange(n_blocks, dtype=jnp.int32) * MOE_BLOCK,
                                            side='right'), N_EXPERTS - 1).astype(jnp.int32)
        nvalid = (pad_end[-1:] // MOE_BLOCK).astype(jnp.int32)

        slots = _dispatch_call(h2, dest, slots)
        y_slots = _expert_call(slots, bexp, nvalid, exp_w_gu[l], exp_b_gu[l], exp_w_down[l], exp_b_down[l])
        res = _combine_call(xs, mod, dest, wts, y_slots, final_g[None, :], last)
        if last:
            out = res.reshape(BATCH, SEQ, D_MODEL)
        else:
            xs = res
    return out
```

```python
import functools

import jax
import jax.numpy as jnp
from jax import lax
from jax.experimental import pallas as pl
from jax.experimental.pallas import tpu as pltpu

F32 = jnp.float32
BF16 = jnp.bfloat16

D_MODEL = 1024
BATCH = 4
SEQ = 4096
DEPTH = 4
GRID_W = 64
CTX_LEN = 256
N_MOD = 6
NORM_EPS = 1e-6
RET_HEADS = 8
RET_DK = 64
RET_W = 512
ROPE_BASE = 10000.0
HG_HEADS = 4
HG_DK = 128
HG_W = 512
LRU_W = 512
LRU_BLOCKS = 8
LRU_BD = 64
LRU_C = 8.0
CONV_W = 4
N_EXPERTS = 32
TOP_K = 4
SWIGLU_ALPHA = 1.702
SWIGLU_LIMIT = 7.0

LANES = 128
SUBLANES = 8
RET_CHUNK = 256
HG_CHUNK = 256
HG_BLOCK = 128
HG_SUB = 4
LOG2_E = 1.4426950408889634
LRU_CHUNK = 256
MOE_BLOCK = 256
ROW_TILE = 256
ROUTE_TILE = 512
MOD_TN = 1536
VMEM_LIMIT = 56 * 1024 * 1024

_C_RQ, _C_RK, _C_RV, _C_RG = 0, 4, 8, 12
_C_HQ, _C_HFF, _C_HFB, _C_HI, _C_HGT = 16, 20, 24, 28, 32
_C_LX, _C_LGT = 36, 40
_C_GATES = 44


def _seg():
    return SEQ + CTX_LEN


def _in_cols():
    return _C_GATES * LANES + 3 * D_MODEL


def _dot(a, b):
    return jnp.dot(a, b, preferred_element_type=F32)


def _dot_nt(a, b):
    return lax.dot_general(a, b, (((1,), (1,)), ((), ())), preferred_element_type=F32)


def _dot_tn(a, b):
    return lax.dot_general(a, b, (((0,), (0,)), ((), ())), preferred_element_type=F32)


def _split3(x):
    hi = x.astype(BF16)
    r = x - hi.astype(F32)
    mid = r.astype(BF16)
    lo = (r - mid.astype(F32)).astype(BF16)
    return hi, mid, lo


def _dot_exact_lhs(a, x):
    hi, mid, lo = _split3(x)
    return _dot(a, hi) + _dot(a, mid) + _dot(a, lo)


def _dot_exact_rhs(x, a):
    hi, mid, lo = _split3(x)
    return _dot(hi, a) + _dot(mid, a) + _dot(lo, a)


def _dot3(a, b):
    ah, am, _ = _split3(a)
    bh, bm, _ = _split3(b)
    return _dot(ah, bh) + (_dot(ah, bm) + _dot(am, bh))


def _sigmoid(x):
    return 1.0 / (1.0 + jnp.exp(-x))


def _log_sigmoid(x):
    return jnp.minimum(x, 0.0) - jnp.log1p(jnp.exp(-jnp.abs(x)))


def _params(sem):
    return pltpu.CompilerParams(dimension_semantics=sem, vmem_limit_bytes=VMEM_LIMIT)


def _mod_kernel(c_ref, w_ref, b_ref, o_ref):
    c = c_ref[...]
    s = c * _sigmoid(c)
    o_ref[0] = _dot3(s, w_ref[0]) + b_ref[0]


def _mod_call(cvec, mod_w, mod_b):
    n6 = N_MOD * D_MODEL
    tn = MOD_TN
    return pl.pallas_call(
        _mod_kernel,
        grid=(DEPTH, n6 // tn),
        in_specs=[pl.BlockSpec((SUBLANES, D_MODEL), lambda l, j: (0, 0)),
                  pl.BlockSpec((1, D_MODEL, tn), lambda l, j: (l, 0, j)),
                  pl.BlockSpec((1, 1, tn), lambda l, j: (l, 0, j))],
        out_specs=pl.BlockSpec((1, SUBLANES, tn), lambda l, j: (l, 0, j)),
        out_shape=jax.ShapeDtypeStruct((DEPTH, SUBLANES, n6), F32),
        compiler_params=_params(("arbitrary", "arbitrary")),
        name="modulation",
    )(cvec, mod_w, mod_b.reshape(DEPTH, 1, n6))


def _norm_modulate(x, g, mod_l, mod_c, is_ctx, first):
    ms = jnp.mean(x * x, axis=-1, keepdims=True)
    y = x * lax.rsqrt(ms + NORM_EPS) * g
    sh = jnp.where(is_ctx, mod_c[first:first + 1, :], mod_l[first:first + 1, :])
    sc = jnp.where(is_ctx, mod_c[first + 1:first + 2, :], mod_l[first + 1:first + 2, :])
    return y * (1.0 + sc) + sh


def _inproj_kernel(x_ref, ml_ref, mc_ref, g_ref, w_ref, o_ref, oz_ref, h_scr, *, tm, z_first, z_count):
    j = pl.program_id(1)

    @pl.when(j == 0)
    def _():
        tiles_per_b = _seg() // tm
        row0 = (pl.program_id(0) % tiles_per_b) * tm
        rows = row0 + lax.broadcasted_iota(jnp.int32, (tm, 1), 0)
        h = _norm_modulate(x_ref[...], g_ref[...], ml_ref[0], mc_ref[0], rows >= SEQ, 0)
        h_scr[...] = h.astype(BF16)

    acc = _dot(h_scr[...], w_ref[0])
    o_ref[...] = acc.astype(BF16)

    @pl.when(jnp.logical_and(j >= z_first, j < z_first + z_count))
    def _():
        oz_ref[...] = acc


def _inproj_call(xs, mod, g, w, layer):
    t = xs.shape[0]
    tm = _seg() // 2
    ncol = _in_cols()
    tn = HG_W
    tiles_per_b = _seg() // tm
    z_first = _C_HFF * LANES // tn
    z_count = 2
    return pl.pallas_call(
        functools.partial(_inproj_kernel, tm=tm, z_first=z_first, z_count=z_count),
        grid=(t // tm, ncol // tn),
        in_specs=[pl.BlockSpec((tm, D_MODEL), lambda i, j: (i, 0)),
                  pl.BlockSpec((1, N_MOD, D_MODEL), lambda i, j: (i // tiles_per_b, 0, 0)),
                  pl.BlockSpec((1, N_MOD, D_MODEL), lambda i, j: (BATCH, 0, 0)),
                  pl.BlockSpec((1, D_MODEL), lambda i, j: (0, 0)),
                  pl.BlockSpec((1, D_MODEL, tn), lambda i, j: (layer, 0, j))],
        out_specs=[pl.BlockSpec((tm, tn), lambda i, j: (i, j)),
                   pl.BlockSpec((tm, tn), lambda i, j: (i, jnp.clip(j - z_first, 0, z_count - 1)))],
        out_shape=[jax.ShapeDtypeStruct((t, ncol), BF16), jax.ShapeDtypeStruct((t, z_count * tn), F32)],
        scratch_shapes=[pltpu.VMEM((tm, D_MODEL), BF16)],
        compiler_params=_params(("arbitrary", "arbitrary")),
        name="in_projection",
    )(xs, mod, mod, g, w)


def _ret_kernel(dec_ref, cos_ref, sin_ref, q_ref, k_ref, v_ref, o_ref, mask_scr, pw_scr):
    c_len = RET_CHUNK
    n_lat = SEQ // c_len
    n_all = _seg() // c_len
    lane = lax.broadcasted_iota(jnp.int32, (1, LANES), 1)
    head0 = lane < RET_DK
    first_half = (lane % RET_DK) < (RET_DK // 2)

    lg = _log_sigmoid(dec_ref[0])
    dvf = jnp.where(head0, lg[0:1], lg[1:2])
    dvb = jnp.where(head0, lg[2:3], lg[3:4])
    r = lax.broadcasted_iota(jnp.int32, (c_len, 1), 0).astype(F32)
    pw_scr[0] = jnp.exp((r + 1.0) * dvf)
    pw_scr[1] = jnp.exp((c_len - 1.0 - r) * dvf)
    pw_scr[2] = jnp.exp((c_len - r) * dvb)
    pw_scr[3] = jnp.exp(r * dvb)
    dcf = jnp.exp(c_len * dvf)
    dcb = jnp.exp(c_len * dvb)
    ri = lax.broadcasted_iota(jnp.int32, (c_len, c_len), 0)
    ci = lax.broadcasted_iota(jnp.int32, (c_len, c_len), 1)
    dm = (ri - ci).astype(F32)
    for h in range(2):
        lgf = jnp.concatenate([lg[h:h + 1]] * (c_len // LANES), axis=1)
        lgb = jnp.concatenate([lg[2 + h:3 + h]] * (c_len // LANES), axis=1)
        mask_scr[h] = jnp.exp(jnp.where(dm >= 0, dm * lgf, -dm * lgb))
    rb = lax.broadcasted_iota(jnp.int32, (LANES, LANES), 0) < RET_DK
    cb = lax.broadcasted_iota(jnp.int32, (LANES, LANES), 1) < RET_DK
    same_head = (rb == cb).astype(F32)

    def load(c):
        rows = pl.ds(pl.multiple_of(c * c_len, c_len), c_len)
        q = q_ref[rows, :].astype(F32)
        k = k_ref[rows, :].astype(F32) * (RET_DK ** -0.5)
        v = v_ref[rows, :].astype(F32)
        return rows, q, k, v

    def rotary(t, c):
        rows = pl.ds(pl.multiple_of(c * c_len, c_len), c_len)
        partner = jnp.where(first_half, pltpu.roll(t, LANES - RET_DK // 2, 1), pltpu.roll(t, RET_DK // 2, 1))
        return t * cos_ref[rows, :] + partner * sin_ref[rows, :]

    def load_rot(c):
        rows, q, k, v = load(c)
        is_lat = c < n_lat
        cc = jnp.minimum(c, n_lat - 1)
        q = jnp.where(is_lat, rotary(q, cc), q)
        k = jnp.where(is_lat, rotary(k, cc), k)
        return rows, q, k, v

    def fwd_step(s, st):
        c = jnp.where(s < n_all - n_lat, n_lat + s, s - (n_all - n_lat))
        rows, q, k, v = load_rot(c)
        kb = k.astype(BF16)
        vb = v.astype(BF16)
        o0 = _dot((_dot_nt(jnp.where(head0, q, 0.0).astype(BF16), kb) * mask_scr[0]).astype(BF16), vb)
        o1 = _dot((_dot_nt(jnp.where(head0, 0.0, q).astype(BF16), kb) * mask_scr[1]).astype(BF16), vb)
        o = jnp.where(head0, o0, o1)
        o = o + _dot_nt((q * pw_scr[0]).astype(BF16), st.astype(BF16))
        o_ref[rows, :] = o
        return st * dcf + _dot_tn(vb, (k * pw_scr[1]).astype(BF16)) * same_head

    def bwd_step(s, st):
        c = n_all - 1 - s
        rows, q, k, v = load_rot(c)
        o_ref[rows, :] += _dot_nt((q * pw_scr[2]).astype(BF16), st.astype(BF16))
        return st * dcb + _dot_tn(v.astype(BF16), (k * pw_scr[3]).astype(BF16)) * same_head

    zero = jnp.zeros((LANES, LANES), F32)
    lax.fori_loop(0, n_all, fwd_step, zero)
    lax.fori_loop(0, n_all, bwd_step, zero)


def _ret_call(u, dec, cos_t, sin_t):
    t = u.shape[0]
    sb = _seg()
    col = lambda off: pl.BlockSpec((sb, LANES), lambda b, p: (b, off + p))
    return pl.pallas_call(
        _ret_kernel,
        grid=(BATCH, RET_HEADS // 2),
        in_specs=[pl.BlockSpec((1, 4, LANES), lambda b, p: (p, 0, 0)),
                  pl.BlockSpec((SEQ, LANES), lambda b, p: (0, 0)),
                  pl.BlockSpec((SEQ, LANES), lambda b, p: (0, 0)),
                  col(_C_RQ), col(_C_RK), col(_C_RV)],
        out_specs=pl.BlockSpec((sb, LANES), lambda b, p: (b, p)),
        out_shape=jax.ShapeDtypeStruct((t, RET_W), F32),
        scratch_shapes=[pltpu.VMEM((2, RET_CHUNK, RET_CHUNK), F32),
                        pltpu.VMEM((4, RET_CHUNK, LANES), F32)],
        compiler_params=_params(("arbitrary", "arbitrary")),
        name="retention",
    )(dec, cos_t, sin_t, u, u, u)


def _hg_codes(fwd):
    c_len = HG_BLOCK
    ri = lax.broadcasted_iota(jnp.int32, (c_len, c_len), 0)
    ci = lax.broadcasted_iota(jnp.int32, (c_len, c_len), 1)
    dist = (ri - ci) if fwd else (ci - ri)
    code = jnp.full((c_len, c_len), -1, jnp.int32)
    half, level = c_len // 2, 0
    while half >= HG_SUB:
        sh = half.bit_length() - 1
        same_seg = jnp.right_shift(ri, sh + 1) == jnp.right_shift(ci, sh + 1)
        other_half = jnp.right_shift(ri, sh) != jnp.right_shift(ci, sh)
        code = jnp.where(same_seg & other_half & (dist > 0), level, code)
        half, level = half // 2, level + 1
    sub_sh = HG_SUB.bit_length() - 1
    same_sub = jnp.right_shift(ri, sub_sh) == jnp.right_shift(ci, sub_sh)
    for dl in range(HG_SUB):
        code = jnp.where(same_sub & (dist == dl), 100 + dl, code)
    return code


def _hg_chunk(fwd, q, z, v, lb, st, code, tri):
    c_len = HG_CHUNK

    a = jnp.log(lb)
    b = jnp.log1p(-lb) + _log_sigmoid(z)
    lf = jnp.maximum(a, b) + jnp.log1p(jnp.exp(-jnp.abs(a - b)))
    kin = (1.0 - lb) * _sigmoid(-z)

    g = _dot_exact_lhs(tri, lf) * LOG2_E
    g_end = g[c_len - 1:c_len, :] if fwd else g[0:1, :]
    vb = v.astype(BF16)

    o = _dot_nt((q * jnp.exp2(g)).astype(BF16), st.astype(BF16))

    blk = HG_BLOCK
    lo, hi = slice(0, blk), slice(blk, 2 * blk)
    qrows, krows = (hi, lo) if fwd else (lo, hi)
    bnd = g[blk - 1:blk, :] if fwd else g[blk:blk + 1, :]
    qs = (q[qrows] * jnp.exp2(g[qrows] - bnd)).astype(BF16)
    ks = (kin[krows] * jnp.exp2(bnd - g[krows])).astype(BF16)
    cross = _dot(_dot_nt(qs, ks).astype(BF16), vb[krows])

    parts = []
    for rows in (lo, hi):
        parts.append(_hg_block(fwd, q[rows], kin[rows], g[rows], vb[rows], code))
    parts[1 if fwd else 0] = parts[1 if fwd else 0] + cross
    o = o + jnp.concatenate(parts, axis=0)
    st_new = st * jnp.exp2(g_end) + _dot_tn(vb, (kin * jnp.exp2(g_end - g)).astype(BF16))
    return o, st_new


def _hg_block(fwd, q, kin, g, vb, code):
    n = HG_BLOCK

    scores = jnp.zeros((n, n), F32)
    half, level = n // 2, 0
    while half >= HG_SUB:
        seg = 2 * half
        g3 = g.reshape(n // seg, seg, LANES)
        e = half - 1 if fwd else half
        bsel = jnp.broadcast_to(g3[:, e:e + 1, :], g3.shape).reshape(n, LANES)
        qs = (q * jnp.exp2(g - bsel)).astype(BF16)
        ks = (kin * jnp.exp2(bsel - g)).astype(BF16)
        scores = jnp.where(code == level, _dot_nt(qs, ks), scores)
        half, level = half // 2, level + 1

    def shift(x, dl):
        x3 = x.reshape(n // SUBLANES, SUBLANES, LANES)
        return pltpu.roll(x3, dl if fwd else SUBLANES - dl, 1).reshape(n, LANES)

    for dl in range(HG_SUB):
        w = q * kin if dl == 0 else q * shift(kin, dl) * jnp.exp2(g - shift(g, dl))
        scores = jnp.where(code == 100 + dl, jnp.sum(w, axis=-1, keepdims=True), scores)

    return _dot(scores.astype(BF16), vb)


def _hg_kernel(lbl_ref, q_ref, zf_ref, zb_ref, v_ref, o_ref, *, layer):
    c_len = HG_CHUNK
    n_lat = SEQ // c_len
    n_all = _seg() // c_len

    lbs = []
    for d in range(2):
        x = lbl_ref[d]
        e = jnp.exp(x - jnp.max(x, axis=0, keepdims=True))
        tot = jnp.sum(e, axis=0, keepdims=True)
        part = jnp.zeros_like(tot)
        for i in range(1, layer + 1):
            part = part + e[i:i + 1, :]
        lbs.append(part / tot)

    ri = lax.broadcasted_iota(jnp.int32, (c_len, c_len), 0)
    ci = lax.broadcasted_iota(jnp.int32, (c_len, c_len), 1)
    tri_f = (ri >= ci).astype(BF16)
    tri_b = (ri <= ci).astype(BF16)
    code_f = _hg_codes(True)
    code_b = _hg_codes(False)

    o_ref[...] = jnp.zeros(o_ref.shape, F32)

    def step(s, carry):
        st_f, st_b = carry
        cf = jnp.where(s < n_all - n_lat, n_lat + s, s - (n_all - n_lat))
        cbk = n_all - 1 - s
        rf = pl.ds(pl.multiple_of(cf * c_len, c_len), c_len)
        rk = pl.ds(pl.multiple_of(cbk * c_len, c_len), c_len)
        qf = q_ref[rf, :].astype(F32)
        of, st_f = _hg_chunk(True, qf * _sigmoid(qf), zf_ref[rf, :], v_ref[rf, :].astype(F32), lbs[0], st_f,
                             code_f, tri_f)
        o_ref[rf, :] += of
        qb = q_ref[rk, :].astype(F32)
        ob, st_b = _hg_chunk(False, qb * _sigmoid(qb), zb_ref[rk, :], v_ref[rk, :].astype(F32), lbs[1], st_b,
                             code_b, tri_b)
        o_ref[rk, :] += ob
        return st_f, st_b

    zero = jnp.zeros((LANES, LANES), F32)
    lax.fori_loop(0, n_all, step, (zero, zero))


def _hg_call(u, uz, lb_logits, layer):
    t = u.shape[0]
    sb = _seg()
    col = lambda off: pl.BlockSpec((sb, LANES), lambda b, h: (b, off + h))
    return pl.pallas_call(
        functools.partial(_hg_kernel, layer=layer),
        grid=(BATCH, HG_HEADS),
        in_specs=[pl.BlockSpec((2, DEPTH, LANES), lambda b, h: (0, 0, h)),
                  col(_C_HQ), col(0), col(HG_HEADS), col(_C_HI)],
        out_specs=pl.BlockSpec((sb, LANES), lambda b, h: (b, h)),
        out_shape=jax.ShapeDtypeStruct((t, HG_W), F32),
        compiler_params=_params(("arbitrary", "arbitrary")),
        name="hgrn2",
    )(lb_logits, u, uz, uz, u)


def _lru_kernel(x_ref, wg_ref, bg_ref, lam_ref, cw_ref, cb_ref, y_ref,
                xp_scr, af_scr, bf_scr, ab_scr, bb_scr, hb_scr):
    sb = _seg()
    c_len = LRU_CHUNK
    halo = SUBLANES
    n_tiles = sb // SUBLANES
    n_lat_t = SEQ // SUBLANES

    xp_scr[pl.ds(0, halo), :] = jnp.zeros((halo, LANES), F32)
    xp_scr[pl.ds(halo + sb, halo), :] = jnp.zeros((halo, LANES), F32)
    xp_scr[pl.ds(halo, sb), :] = x_ref[...].astype(F32)

    lam = lam_ref[0]
    sp = jnp.maximum(-lam, 0.0) + jnp.log1p(jnp.exp(-jnp.abs(lam)))
    cw = cw_ref[...]
    wg = wg_ref[0]
    bg = bg_ref[0]
    r8 = lax.broadcasted_iota(jnp.int32, (c_len, 1), 0)
    rm8 = r8 % SUBLANES

    def coeff_chunk(c, _):
        base = pl.multiple_of(c * c_len, c_len)
        w = xp_scr[pl.ds(base, c_len + 2 * halo), :]
        grow = base + r8
        in_lat = grow < SEQ
        pos = jnp.where(in_lat, grow, grow - SEQ)
        seglen = jnp.where(in_lat, SEQ, CTX_LEN)
        wlen = c_len + 2 * halo
        xm2 = jnp.where(pos >= 2, pltpu.roll(w, 2, 0)[halo:halo + c_len], 0.0)
        xm1 = jnp.where(pos >= 1, pltpu.roll(w, 1, 0)[halo:halo + c_len], 0.0)
        x0 = w[halo:halo + c_len]
        xp1 = jnp.where(pos + 1 < seglen, pltpu.roll(w, wlen - 1, 0)[halo:halo + c_len], 0.0)
        xc = xm2 * cw[0:1] + xm1 * cw[1:2] + x0 * cw[2:3] + xp1 * cw[3:4] + cb_ref[...]
        gates = _sigmoid(_dot(xc.astype(BF16), wg) + bg)
        rows = pl.ds(base, c_len)
        for d, (a_scr, b_scr) in enumerate(((af_scr, bf_scr), (ab_scr, bb_scr))):
            rg = gates[:, (2 * d) * LANES:(2 * d + 1) * LANES]
            ig = gates[:, (2 * d + 1) * LANES:(2 * d + 2) * LANES]
            log_a = -LRU_C * rg * sp[d:d + 1]
            a = jnp.exp(log_a)
            th = jnp.tanh(log_a)
            bv = jnp.sqrt(-2.0 * th / (1.0 - th)) * (ig * xc)
            for sft in (1, 2, 4):
                if d == 0:
                    a_s = pltpu.roll(a, sft, 0)
                    b_s = pltpu.roll(bv, sft, 0)
                    ok = rm8 >= sft
                else:
                    a_s = pltpu.roll(a, c_len - sft, 0)
                    b_s = pltpu.roll(bv, c_len - sft, 0)
                    ok = rm8 < SUBLANES - sft
                bv = jnp.where(ok, a * b_s + bv, bv)
                a = jnp.where(ok, a * a_s, a)
            a_scr[rows, :] = a
            b_scr[rows, :] = bv
        return 0

    lax.fori_loop(0, sb // c_len, coeff_chunk, 0)

    def carry_step(s, carry):
        hf, hb = carry
        tf = jnp.where(s < n_tiles - n_lat_t, n_lat_t + s, s - (n_tiles - n_lat_t))
        tb = n_tiles - 1 - s
        rf = pl.ds(pl.multiple_of(tf * SUBLANES, SUBLANES), SUBLANES)
        rk = pl.ds(pl.multiple_of(tb * SUBLANES, SUBLANES), SUBLANES)
        h1 = af_scr[rf, :] * hf + bf_scr[rf, :]
        y_ref[rf, :] = h1
        h2 = ab_scr[rk, :] * hb + bb_scr[rk, :]
        hb_scr[rk, :] = h2
        return (jnp.broadcast_to(h1[SUBLANES - 1:SUBLANES, :], (SUBLANES, LANES)),
                jnp.broadcast_to(h2[0:1, :], (SUBLANES, LANES)))

    zero = jnp.zeros((SUBLANES, LANES), F32)
    lax.fori_loop(0, n_tiles, carry_step, (zero, zero))
    y_ref[...] += hb_scr[...]


def _lru_call(u, wg, bg, lam, cw, cb):
    t = u.shape[0]
    sb = _seg()
    nb = LRU_W // LANES
    return pl.pallas_call(
        _lru_kernel,
        grid=(BATCH, nb),
        in_specs=[pl.BlockSpec((sb, LANES), lambda b, c: (b, _C_LX + c)),
                  pl.BlockSpec((1, LANES, 4 * LANES), lambda b, c: (c, 0, 0)),
                  pl.BlockSpec((1, 1, 4 * LANES), lambda b, c: (c, 0, 0)),
                  pl.BlockSpec((1, 2, LANES), lambda b, c: (c, 0, 0)),
                  pl.BlockSpec((CONV_W, LANES), lambda b, c: (0, c)),
                  pl.BlockSpec((1, LANES), lambda b, c: (0, c))],
        out_specs=pl.BlockSpec((sb, LANES), lambda b, c: (b, c)),
        out_shape=jax.ShapeDtypeStruct((t, LRU_W), F32),
        scratch_shapes=[pltpu.VMEM((sb + 2 * SUBLANES, LANES), F32)] + [pltpu.VMEM((sb, LANES), F32)] * 5,
        compiler_params=_params(("arbitrary", "arbitrary")),
        name="rglru",
    )(u, wg, bg, lam, cw, cb)


def _group_rms(o, group):
    ri = lax.broadcasted_iota(jnp.int32, (LANES, LANES), 0) // group
    ci = lax.broadcasted_iota(jnp.int32, (LANES, LANES), 1) // group
    ones_bd = (ri == ci).astype(BF16)
    outs = []
    for tix in range(o.shape[1] // LANES):
        x = o[:, tix * LANES:(tix + 1) * LANES]
        sq = x * x
        hi = sq.astype(BF16)
        lo = (sq - hi.astype(F32)).astype(BF16)
        ss = _dot(hi, ones_bd) + _dot(lo, ones_bd)
        outs.append(x * lax.rsqrt(ss * (1.0 / group) + NORM_EPS))
    return jnp.concatenate(outs, axis=1)


def _merge_kernel(*refs, n_gate):
    (x_ref, ml_ref, mc_ref, oret_ref, ohg_ref, ylru_ref, rg_ref, hgt_ref, lgt_ref) = refs[:9]
    gate_refs = refs[9:9 + 3 * n_gate]
    (gng_ref, ghg_ref, wr_ref, wh_ref, wl_ref, wo_ref, n2g_ref, rw_ref, rb_ref,
     xo_ref, h2_ref, lg_ref) = refs[9 + 3 * n_gate:]
    tiles_per_b = _seg() // ROW_TILE
    is_ctx = (pl.program_id(0) % tiles_per_b) >= SEQ // ROW_TILE
    ml = ml_ref[0]
    mc = mc_ref[0]

    def gate(i):
        parts = [gate_refs[i * n_gate + j][...].astype(F32) for j in range(n_gate)]
        return _sigmoid(parts[0] if n_gate == 1 else jnp.concatenate(parts, axis=1))

    rg = rg_ref[...].astype(F32)
    ret_in = _group_rms(oret_ref[...], RET_DK) * gng_ref[...] * (rg * _sigmoid(rg))
    ret = _dot(ret_in.astype(BF16), wr_ref[0])
    hg_in = _group_rms(ohg_ref[...], HG_DK) * ghg_ref[...] * _sigmoid(hgt_ref[...].astype(F32))
    hgr = _dot(hg_in.astype(BF16), wh_ref[0])
    lgt = lgt_ref[...].astype(F32)
    gelu = 0.5 * lgt * (1.0 + jnp.tanh(0.7978845608028654 * (lgt + 0.044715 * lgt * lgt * lgt)))
    lru = _dot((ylru_ref[...] * gelu).astype(BF16), wl_ref[0])
    merged = gate(0) * ret + gate(1) * hgr + gate(2) * lru
    mix = _dot(merged.astype(BF16), wo_ref[0])
    g1 = jnp.where(is_ctx, mc[2:3, :], ml[2:3, :])
    xn = x_ref[...] + g1 * mix
    xo_ref[...] = xn
    h2 = _norm_modulate(xn, n2g_ref[...], ml, mc, is_ctx, 3)
    h2_ref[...] = h2
    lg_ref[...] = _dot3(h2, rw_ref[...]) + rb_ref[...]


def _merge_call(xs, mod, o_ret, o_hg, y_lru, u, gn_g, hgn_g, wr, wh, wl, wo, n2g, rw, rb, layer):
    t = xs.shape[0]
    r = ROW_TILE
    tiles_per_b = _seg() // r
    n_gate = D_MODEL // 512
    row = lambda w: pl.BlockSpec((r, w), lambda i: (i, 0))
    ucol = lambda blk: pl.BlockSpec((r, 512), lambda i: (i, blk))

    def full(a):
        if a.ndim == 3:
            return pl.BlockSpec((1,) + a.shape[1:], lambda i: (layer, 0, 0))
        return pl.BlockSpec(a.shape, lambda i: (0,) * a.ndim)
    gate_base = _C_GATES * LANES // 512
    in_specs = [row(D_MODEL),
                pl.BlockSpec((1, N_MOD, D_MODEL), lambda i: (i // tiles_per_b, 0, 0)),
                pl.BlockSpec((1, N_MOD, D_MODEL), lambda i: (BATCH, 0, 0)),
                row(RET_W), row(HG_W), row(LRU_W),
                ucol(_C_RG * LANES // 512), ucol(_C_HGT * LANES // 512), ucol(_C_LGT * LANES // 512)]
    in_specs += [ucol(gate_base + j) for j in range(3 * n_gate)]
    consts = [gn_g, hgn_g, wr, wh, wl, wo, n2g, rw, rb]
    in_specs += [full(a) for a in consts]
    return pl.pallas_call(
        functools.partial(_merge_kernel, n_gate=n_gate),
        grid=(t // r,),
        in_specs=in_specs,
        out_specs=[row(D_MODEL), row(D_MODEL), row(LANES)],
        out_shape=[jax.ShapeDtypeStruct((t, D_MODEL), F32), jax.ShapeDtypeStruct((t, D_MODEL), F32),
                   jax.ShapeDtypeStruct((t, LANES), F32)],
        compiler_params=_params(("arbitrary",)),
        name="merge",
    )(xs, mod, mod, o_ret, o_hg, y_lru, *([u] * (3 + 3 * n_gate)), *consts)


def _route_kernel(lg_ref, oi_ref, ow_ref, cnt_ref, cnt_scr):
    @pl.when(pl.program_id(0) == 0)
    def _():
        cnt_scr[...] = jnp.zeros(cnt_scr.shape, F32)

    r = lg_ref.shape[0]
    lane = lax.broadcasted_iota(jnp.int32, (r, LANES), 1)
    neg = -jnp.inf
    l = jnp.where(lane < N_EXPERTS, lg_ref[...], neg)
    vals, idxs, hots = [], [], []
    lanef = lane.astype(F32)
    for _ in range(TOP_K):
        m = jnp.max(l, axis=-1, keepdims=True)
        idxf = jnp.min(jnp.where(l == m, lanef, float(LANES)), axis=-1, keepdims=True)
        hot = lanef == idxf
        l = jnp.where(hot, neg, l)
        vals.append(m)
        idxs.append(idxf.astype(jnp.int32))
        hots.append(hot)
    es = [jnp.exp(v - vals[0]) for v in vals]
    den = es[0]
    for e in es[1:]:
        den = den + e
    chosen = hots[0].astype(F32)
    for hot in hots[1:]:
        chosen = chosen + hot.astype(F32)
    ri = lax.broadcasted_iota(jnp.int32, (r, r), 0)
    ci = lax.broadcasted_iota(jnp.int32, (r, r), 1)
    before = _dot((ri > ci).astype(BF16), chosen.astype(BF16)) + cnt_scr[0:1, :]
    oi = jnp.zeros((r, LANES), jnp.int32)
    ow = jnp.zeros((r, LANES), F32)
    for k in range(TOP_K):
        rank = jnp.sum(jnp.where(hots[k], before, 0.0), axis=-1, keepdims=True).astype(jnp.int32)
        oi = jnp.where(lane == k, idxs[k], oi)
        oi = jnp.where(lane == TOP_K + k, rank, oi)
        ow = jnp.where(lane == k, es[k] / den, ow)
    oi_ref[...] = oi
    ow_ref[...] = ow
    cnt_scr[...] = cnt_scr[...] + jnp.sum(chosen, axis=0, keepdims=True)
    cnt_ref[...] = cnt_scr[...]


def _route_call(logits):
    t = logits.shape[0]
    r = ROUTE_TILE
    row = pl.BlockSpec((r, LANES), lambda i: (i, 0))
    return pl.pallas_call(
        _route_kernel,
        grid=(t // r,),
        in_specs=[row],
        out_specs=[row, row, pl.BlockSpec((SUBLANES, LANES), lambda i: (0, 0))],
        out_shape=[jax.ShapeDtypeStruct((t, LANES), jnp.int32), jax.ShapeDtypeStruct((t, LANES), F32),
                   jax.ShapeDtypeStruct((SUBLANES, LANES), F32)],
        scratch_shapes=[pltpu.VMEM((SUBLANES, LANES), F32)],
        compiler_params=_params(("arbitrary",)),
        name="route",
    )(logits)


def _dispatch_kernel(dest_ref, h_ref, xs_in_ref, xs_ref, sem):
    del xs_in_ref
    r = h_ref.shape[0]

    def copy(i, k):
        d = dest_ref[0, 0, i * TOP_K + k]
        return pltpu.make_async_copy(h_ref.at[pl.ds(i, 1)], xs_ref.at[pl.ds(d, 1)], sem)

    def start(i, _):
        for k in range(TOP_K):
            copy(i, k).start()
        return 0

    def wait(i, _):
        for k in range(TOP_K):
            copy(i, k).wait()
        return 0

    lax.fori_loop(0, r, start, 0)
    lax.fori_loop(0, r, wait, 0)


def _dispatch_call(h2, dest, slots):
    t = h2.shape[0]
    r = ROW_TILE
    return pl.pallas_call(
        _dispatch_kernel,
        grid=(t // r,),
        in_specs=[pl.BlockSpec((1, 1, r * TOP_K), lambda i: (i, 0, 0), memory_space=pltpu.SMEM),
                  pl.BlockSpec((r, D_MODEL), lambda i: (i, 0)),
                  pl.BlockSpec(memory_space=pl.ANY)],
        out_specs=pl.BlockSpec(memory_space=pl.ANY),
        out_shape=jax.ShapeDtypeStruct(slots.shape, F32),
        scratch_shapes=[pltpu.SemaphoreType.DMA(())],
        input_output_aliases={2: 0},
        compiler_params=_params(("arbitrary",)),
        name="dispatch",
    )(dest.reshape(t // r, 1, r * TOP_K), h2, slots)


def _expert_kernel(bexp_ref, nv_ref, x_ref, wgu_ref, bgu_ref, wd_ref, bd_ref, y_ref, wgu_scr, wd_scr):
    i = pl.program_id(0)
    f = D_MODEL
    prev = bexp_ref[jnp.maximum(i - 1, 0)]
    new_expert = jnp.logical_or(i == 0, bexp_ref[i] != prev)

    @pl.when(jnp.logical_and(i < nv_ref[0], new_expert))
    def _():
        slab = 256
        for s in range(D_MODEL // slab):
            wgu_scr[pl.ds(s * slab, slab), :] = wgu_ref[0, 0, pl.ds(s * slab, slab), :].astype(BF16)
            wd_scr[pl.ds(s * slab, slab), :] = wd_ref[0, 0, pl.ds(s * slab, slab), :].astype(BF16)

    @pl.when(i < nv_ref[0])
    def _():
        gu = _dot(x_ref[...].astype(BF16), wgu_scr[...]) + bgu_ref[0, 0]
        gate = jnp.minimum(gu[:, :f], SWIGLU_LIMIT)
        up = jnp.clip(gu[:, f:], -SWIGLU_LIMIT, SWIGLU_LIMIT)
        glu = gate * _sigmoid(SWIGLU_ALPHA * gate)
        y_ref[...] = _dot(((up + 1.0) * glu).astype(BF16), wd_scr[...]) + bd_ref[0, 0]

    @pl.when(i >= nv_ref[0])
    def _():
        y_ref[...] = jnp.zeros(y_ref.shape, F32)


def _expert_call(slots, bexp, nvalid, w_gu, b_gu, w_down, b_down, layer):
    s = slots.shape[0]
    nb = s // MOE_BLOCK
    f = D_MODEL
    blk = lambda i, be, nv: (jnp.minimum(i, nv[0] - 1), 0)
    exp3 = lambda i, be, nv: (layer, be[jnp.minimum(i, nv[0] - 1)], 0, 0)
    grid_spec = pltpu.PrefetchScalarGridSpec(
        num_scalar_prefetch=2,
        grid=(nb,),
        in_specs=[pl.BlockSpec((MOE_BLOCK, D_MODEL), blk),
                  pl.BlockSpec((1, 1, D_MODEL, 2 * f), exp3),
                  pl.BlockSpec((1, 1, 1, 2 * f), exp3),
                  pl.BlockSpec((1, 1, f, D_MODEL), exp3),
                  pl.BlockSpec((1, 1, 1, D_MODEL), exp3)],
        out_specs=pl.BlockSpec((MOE_BLOCK, D_MODEL), lambda i, be, nv: (i, 0)),
        scratch_shapes=[pltpu.VMEM((D_MODEL, 2 * f), BF16), pltpu.VMEM((f, D_MODEL), BF16)],
    )
    return pl.pallas_call(
        _expert_kernel,
        grid_spec=grid_spec,
        out_shape=jax.ShapeDtypeStruct((s, D_MODEL), F32),
        compiler_params=_params(("arbitrary",)),
        name="experts",
    )(bexp, nvalid, slots, w_gu, b_gu.reshape(DEPTH, N_EXPERTS, 1, 2 * f), w_down,
      b_down.reshape(DEPTH, N_EXPERTS, 1, D_MODEL))


def _combine_kernel(dest_ref, x_ref, ml_ref, mc_ref, w_ref, fg_ref, y_ref, o_ref, buf, sem, *, last):
    r = x_ref.shape[0]

    def copy(i, k):
        d = dest_ref[0, 0, i * TOP_K + k]
        return pltpu.make_async_copy(y_ref.at[pl.ds(d, 1)], buf.at[k, pl.ds(i, 1)], sem)

    def start(i, _):
        for k in range(TOP_K):
            copy(i, k).start()
        return 0

    def wait(i, _):
        for k in range(TOP_K):
            copy(i, k).wait()
        return 0

    lax.fori_loop(0, r, start, 0)
    lax.fori_loop(0, r, wait, 0)

    if last:
        is_ctx = False
    else:
        tiles_per_b = _seg() // ROW_TILE
        is_ctx = (pl.program_id(1)) >= SEQ // ROW_TILE
    w = w_ref[...]
    ff = w[:, 0:1] * buf[0]
    for k in range(1, TOP_K):
        ff = ff + w[:, k:k + 1] * buf[k]
    g2 = jnp.where(is_ctx, mc_ref[0][5:6, :], ml_ref[0][5:6, :])
    xn = x_ref[...] + g2 * ff
    if last:
        ms = jnp.mean(xn * xn, axis=-1, keepdims=True)
        xn = xn * lax.rsqrt(ms + NORM_EPS) * fg_ref[...]
    o_ref[...] = xn


def _combine_call(xs, mod, dest, wts, y_slots, final_g, last):
    t = xs.shape[0]
    r = ROW_TILE
    tiles_per_b = _seg() // r
    n_t = SEQ // r if last else tiles_per_b
    tile = lambda b, j: b * tiles_per_b + j
    out_rows = BATCH * SEQ if last else t
    return pl.pallas_call(
        functools.partial(_combine_kernel, last=last),
        grid=(BATCH, n_t),
        in_specs=[pl.BlockSpec((1, 1, r * TOP_K), lambda b, j: (tile(b, j), 0, 0), memory_space=pltpu.SMEM),
                  pl.BlockSpec((r, D_MODEL), lambda b, j: (tile(b, j), 0)),
                  pl.BlockSpec((1, N_MOD, D_MODEL), lambda b, j: (b, 0, 0)),
                  pl.BlockSpec((1, N_MOD, D_MODEL), lambda b, j: (BATCH, 0, 0)),
                  pl.BlockSpec((r, LANES), lambda b, j: (tile(b, j), 0)),
                  pl.BlockSpec((1, D_MODEL), lambda b, j: (0, 0)),
                  pl.BlockSpec(memory_space=pl.ANY)],
        out_specs=pl.BlockSpec((r, D_MODEL), lambda b, j: (b * n_t + j, 0)),
        out_shape=jax.ShapeDtypeStruct((out_rows, D_MODEL), F32),
        scratch_shapes=[pltpu.VMEM((TOP_K, r, D_MODEL), F32), pltpu.SemaphoreType.DMA(())],
        compiler_params=_params(("arbitrary", "arbitrary")),
        name="combine",
    )(dest.reshape(t // r, 1, r * TOP_K), xs, mod, mod, wts, final_g, y_slots)


def _rotary_tables():
    rows = SEQ // GRID_W
    row = jnp.repeat(jnp.arange(rows, dtype=F32), GRID_W)
    colp = jnp.tile(jnp.arange(GRID_W, dtype=F32), rows)
    n_freq = RET_DK // 4
    inv = ROPE_BASE ** (-jnp.arange(n_freq, dtype=F32) / n_freq)
    ang = jnp.concatenate([row[:, None] * inv, colp[:, None] * inv], axis=-1)
    cos, sin = jnp.cos(ang), jnp.sin(ang)
    reps = LANES // (RET_DK // 2)
    cos_t = jnp.tile(cos, (1, reps))
    sign = jnp.where((jnp.arange(LANES) % RET_DK) < RET_DK // 2, -1.0, 1.0).astype(F32)
    sin_t = jnp.tile(sin, (1, reps)) * sign
    return cos_t, sin_t


def _lru_gate_weights(wa, ba, wx, bx):
    nb = LRU_W // LANES
    per = LANES // LRU_BD
    eye = jnp.eye(per, dtype=F32)

    def embed(w):
        w = w.reshape(nb, per, LRU_BD, LRU_BD)
        return jnp.einsum('cpij,pq->cpiqj', w, eye).reshape(nb, LANES, LANES)

    wg = jnp.concatenate([embed(wa[0]), embed(wx[0]), embed(wa[1]), embed(wx[1])], axis=-1)
    bias = jnp.concatenate([ba[0].reshape(nb, 1, LANES), bx[0].reshape(nb, 1, LANES),
                            ba[1].reshape(nb, 1, LANES), bx[1].reshape(nb, 1, LANES)], axis=-1)
    return wg.astype(BF16), bias


def kernel(x, c, ctx, c_ctx, mod_w, mod_b, norm1_g, norm2_g, final_g, w_in, ret_decay, ret_gn_g, w_ret_o,
           hgrn_lb_logits, hgrn_gn_g, w_hgrn_o, lru_conv_w, lru_conv_b, lru_wa, lru_ba, lru_wx, lru_bx,
           lru_lambda, w_lru_o, w_out, router_w, router_b, exp_w_gu, exp_b_gu, exp_w_down, exp_b_down):
    assert CTX_LEN == RET_CHUNK and SEQ % RET_CHUNK == 0
    sb = _seg()
    t = BATCH * sb
    xs = jnp.concatenate([x, ctx], axis=1).reshape(t, D_MODEL)

    cvec = jnp.concatenate([c, c_ctx[None, :], jnp.zeros((SUBLANES - BATCH - 1, D_MODEL), F32)], axis=0)
    mod_all = _mod_call(cvec, mod_w, mod_b).reshape(DEPTH, SUBLANES, N_MOD, D_MODEL)

    cos_t, sin_t = _rotary_tables()
    lb_logits = jnp.transpose(hgrn_lb_logits, (1, 0, 2))
    n_slots = t * TOP_K + N_EXPERTS * MOE_BLOCK
    n_blocks = n_slots // MOE_BLOCK
    slots = jnp.zeros((n_slots, D_MODEL), F32)
    rw_pad = jnp.pad(router_w, ((0, 0), (0, 0), (0, LANES - N_EXPERTS)))
    rb_pad = jnp.pad(router_b, ((0, 0), (0, LANES - N_EXPERTS)))
    w_in_b, w_ret_b, w_hg_b, w_lru_b, w_out_b = (w.astype(BF16) for w in (w_in, w_ret_o, w_hgrn_o, w_lru_o, w_out))
    block_first_slot = jnp.arange(n_blocks, dtype=jnp.int32) * MOE_BLOCK

    out = None
    for l in range(DEPTH):
        last = l == DEPTH - 1
        mod = mod_all[l]
        u, uz = _inproj_call(xs, mod, norm1_g[l][None, :], w_in_b, l)

        dec = jnp.broadcast_to(ret_decay[l].reshape(2, RET_HEADS // 2, 2, 1).transpose(1, 0, 2, 3),
                               (RET_HEADS // 2, 2, 2, LANES)).reshape(RET_HEADS // 2, 4, LANES)
        o_ret = _ret_call(u, dec, cos_t, sin_t)
        o_hg = _hg_call(u, uz, lb_logits, l)
        wg, bg = _lru_gate_weights(lru_wa[l], lru_ba[l], lru_wx[l], lru_bx[l])
        lam = lru_lambda[l].reshape(2, LRU_W // LANES, LANES).transpose(1, 0, 2)
        y_lru = _lru_call(u, wg, bg, lam, lru_conv_w[l], lru_conv_b[l][None, :])

        xs, h2, logits = _merge_call(
            xs, mod, o_ret, o_hg, y_lru, u, ret_gn_g[l][None, :], hgrn_gn_g[l][None, :],
            w_ret_b, w_hg_b, w_lru_b, w_out_b, norm2_g[l][None, :], rw_pad[l], rb_pad[l][None, :], l)

        oi, wts, cnt = _route_call(logits)
        idx = oi[:, :TOP_K]
        rank = oi[:, TOP_K:2 * TOP_K]
        counts = cnt[0, :N_EXPERTS].astype(jnp.int32)
        padded = (counts + MOE_BLOCK - 1) // MOE_BLOCK * MOE_BLOCK
        pad_end = jnp.cumsum(padded)
        pad_start = pad_end - padded
        dest = (pad_start[idx] + rank).astype(jnp.int32)
        bexp = jnp.minimum(jnp.sum((pad_end[None, :] <= block_first_slot[:, None]).astype(jnp.int32), axis=1),
                           N_EXPERTS - 1)
        nvalid = (pad_end[-1:] // MOE_BLOCK).astype(jnp.int32)

        slots = _dispatch_call(h2, dest, slots)
        y_slots = _expert_call(slots, bexp, nvalid, exp_w_gu, exp_b_gu, exp_w_down, exp_b_down, l)
        res = _combine_call(xs, mod, dest, wts, y_slots, final_g[None, :], last)
        if last:
            out = res.reshape(BATCH, SEQ, D_MODEL)
        else:
            xs = res
    return out
```

```python
import functools

import jax
import jax.numpy as jnp
from jax import lax
from jax.experimental import pallas as pl
from jax.experimental.pallas import tpu as pltpu

F32 = jnp.float32
BF16 = jnp.bfloat16

D_MODEL = 1024
BATCH = 4
SEQ = 4096
DEPTH = 4
GRID_W = 64
CTX_LEN = 256
N_MOD = 6
NORM_EPS = 1e-6
RET_HEADS = 8
RET_DK = 64
RET_W = 512
ROPE_BASE = 10000.0
HG_HEADS = 4
HG_DK = 128
HG_W = 512
LRU_W = 512
LRU_BLOCKS = 8
LRU_BD = 64
LRU_C = 8.0
CONV_W = 4
N_EXPERTS = 32
TOP_K = 4
SWIGLU_ALPHA = 1.702
SWIGLU_LIMIT = 7.0

LANES = 128
SUBLANES = 8
RET_CHUNK = 256
HG_CHUNK = 256
HG_BLOCK = 128
HG_SUB = 4
LOG2_E = 1.4426950408889634
LRU_CHUNK = 256
MOE_BLOCK = 256
ROW_TILE = 256
ROUTE_TILE = 512
MOD_TN = 1536
VMEM_LIMIT = 56 * 1024 * 1024

_C_RQ, _C_RK, _C_RV, _C_RG = 0, 4, 8, 12
_C_HQ, _C_HFF, _C_HFB, _C_HI, _C_HGT = 16, 20, 24, 28, 32
_C_LX, _C_LGT = 36, 40
_C_GATES = 44


def _seg():
    return SEQ + CTX_LEN


def _in_cols():
    return _C_GATES * LANES + 3 * D_MODEL


def _dot(a, b):
    return jnp.dot(a, b, preferred_element_type=F32)


def _dot_nt(a, b):
    return lax.dot_general(a, b, (((1,), (1,)), ((), ())), preferred_element_type=F32)


def _dot_tn(a, b):
    return lax.dot_general(a, b, (((0,), (0,)), ((), ())), preferred_element_type=F32)


def _split3(x):
    hi = x.astype(BF16)
    r = x - hi.astype(F32)
    mid = r.astype(BF16)
    lo = (r - mid.astype(F32)).astype(BF16)
    return hi, mid, lo


def _dot_exact_lhs(a, x):
    hi, mid, lo = _split3(x)
    return _dot(a, hi) + _dot(a, mid) + _dot(a, lo)


def _dot_exact_rhs(x, a):
    hi, mid, lo = _split3(x)
    return _dot(hi, a) + _dot(mid, a) + _dot(lo, a)


def _dot3(a, b):
    ah, am, _ = _split3(a)
    bh, bm, _ = _split3(b)
    return _dot(ah, bh) + (_dot(ah, bm) + _dot(am, bh))


def _sigmoid(x):
    return 0.5 + 0.5 * jnp.tanh(0.5 * x)


def _log_sigmoid(x):
    return jnp.minimum(x, 0.0) - jnp.log1p(jnp.exp(-jnp.abs(x)))


def _params(sem):
    return pltpu.CompilerParams(dimension_semantics=sem, vmem_limit_bytes=VMEM_LIMIT)


def _mod_kernel(c_ref, w_ref, b_ref, o_ref):
    c = c_ref[...]
    s = c * _sigmoid(c)
    o_ref[0] = _dot3(s, w_ref[0]) + b_ref[0]


def _mod_call(cvec, mod_w, mod_b):
    n6 = N_MOD * D_MODEL
    tn = MOD_TN
    return pl.pallas_call(
        _mod_kernel,
        grid=(DEPTH, n6 // tn),
        in_specs=[pl.BlockSpec((SUBLANES, D_MODEL), lambda l, j: (0, 0)),
                  pl.BlockSpec((1, D_MODEL, tn), lambda l, j: (l, 0, j)),
                  pl.BlockSpec((1, 1, tn), lambda l, j: (l, 0, j))],
        out_specs=pl.BlockSpec((1, SUBLANES, tn), lambda l, j: (l, 0, j)),
        out_shape=jax.ShapeDtypeStruct((DEPTH, SUBLANES, n6), F32),
        compiler_params=_params(("arbitrary", "arbitrary")),
        name="modulation",
    )(cvec, mod_w, mod_b.reshape(DEPTH, 1, n6))


def _norm_modulate(x, g, mod_l, mod_c, is_ctx, first):
    ms = jnp.mean(x * x, axis=-1, keepdims=True)
    y = x * lax.rsqrt(ms + NORM_EPS) * g
    sh = jnp.where(is_ctx, mod_c[first:first + 1, :], mod_l[first:first + 1, :])
    sc = jnp.where(is_ctx, mod_c[first + 1:first + 2, :], mod_l[first + 1:first + 2, :])
    return y * (1.0 + sc) + sh


def _inproj_kernel(x_ref, ml_ref, mc_ref, g_ref, w_ref, o_ref, oz_ref, h_scr, *, tm, z_first, z_count):
    j = pl.program_id(1)

    @pl.when(j == 0)
    def _():
        tiles_per_b = _seg() // tm
        row0 = (pl.program_id(0) % tiles_per_b) * tm
        rows = row0 + lax.broadcasted_iota(jnp.int32, (tm, 1), 0)
        h = _norm_modulate(x_ref[...], g_ref[...], ml_ref[0], mc_ref[0], rows >= SEQ, 0)
        h_scr[...] = h.astype(BF16)

    acc = _dot(h_scr[...], w_ref[0])
    o_ref[...] = acc.astype(BF16)

    @pl.when(jnp.logical_and(j >= z_first, j < z_first + z_count))
    def _():
        oz_ref[...] = acc


def _inproj_call(xs, mod, g, w, layer):
    t = xs.shape[0]
    tm = _seg() // 2
    ncol = _in_cols()
    tn = HG_W
    tiles_per_b = _seg() // tm
    z_first = _C_HFF * LANES // tn
    z_count = 2
    return pl.pallas_call(
        functools.partial(_inproj_kernel, tm=tm, z_first=z_first, z_count=z_count),
        grid=(t // tm, ncol // tn),
        in_specs=[pl.BlockSpec((tm, D_MODEL), lambda i, j: (i, 0)),
                  pl.BlockSpec((1, N_MOD, D_MODEL), lambda i, j: (i // tiles_per_b, 0, 0)),
                  pl.BlockSpec((1, N_MOD, D_MODEL), lambda i, j: (BATCH, 0, 0)),
                  pl.BlockSpec((1, D_MODEL), lambda i, j: (0, 0)),
                  pl.BlockSpec((1, D_MODEL, tn), lambda i, j: (layer, 0, j))],
        out_specs=[pl.BlockSpec((tm, tn), lambda i, j: (i, j)),
                   pl.BlockSpec((tm, tn), lambda i, j: (i, jnp.clip(j - z_first, 0, z_count - 1)))],
        out_shape=[jax.ShapeDtypeStruct((t, ncol), BF16), jax.ShapeDtypeStruct((t, z_count * tn), F32)],
        scratch_shapes=[pltpu.VMEM((tm, D_MODEL), BF16)],
        compiler_params=_params(("arbitrary", "arbitrary")),
        name="in_projection",
    )(xs, mod, mod, g, w)


def _ret_kernel(dec_ref, cos_ref, sin_ref, q_ref, k_ref, v_ref, o_ref, mask_scr, pw_scr):
    c_len = RET_CHUNK
    n_lat = SEQ // c_len
    n_all = _seg() // c_len
    lane = lax.broadcasted_iota(jnp.int32, (1, LANES), 1)
    head0 = lane < RET_DK
    first_half = (lane % RET_DK) < (RET_DK // 2)

    lg = _log_sigmoid(dec_ref[0])
    dvf = jnp.where(head0, lg[0:1], lg[1:2])
    dvb = jnp.where(head0, lg[2:3], lg[3:4])
    r = lax.broadcasted_iota(jnp.int32, (c_len, 1), 0).astype(F32)
    pw_scr[0] = jnp.exp((r + 1.0) * dvf)
    pw_scr[1] = jnp.exp((c_len - 1.0 - r) * dvf)
    pw_scr[2] = jnp.exp((c_len - r) * dvb)
    pw_scr[3] = jnp.exp(r * dvb)
    dcf = jnp.exp(c_len * dvf)
    dcb = jnp.exp(c_len * dvb)
    ri = lax.broadcasted_iota(jnp.int32, (c_len, c_len), 0)
    ci = lax.broadcasted_iota(jnp.int32, (c_len, c_len), 1)
    dm = (ri - ci).astype(F32)
    for h in range(2):
        lgf = jnp.concatenate([lg[h:h + 1]] * (c_len // LANES), axis=1)
        lgb = jnp.concatenate([lg[2 + h:3 + h]] * (c_len // LANES), axis=1)
        mask_scr[h] = jnp.exp(jnp.where(dm >= 0, dm * lgf, -dm * lgb))
    rb = lax.broadcasted_iota(jnp.int32, (LANES, LANES), 0) < RET_DK
    cb = lax.broadcasted_iota(jnp.int32, (LANES, LANES), 1) < RET_DK
    same_head = (rb == cb).astype(F32)

    def load(c):
        rows = pl.ds(pl.multiple_of(c * c_len, c_len), c_len)
        q = q_ref[rows, :].astype(F32)
        k = k_ref[rows, :].astype(F32) * (RET_DK ** -0.5)
        v = v_ref[rows, :].astype(F32)
        return rows, q, k, v

    def rotary(t, c):
        rows = pl.ds(pl.multiple_of(c * c_len, c_len), c_len)
        partner = jnp.where(first_half, pltpu.roll(t, LANES - RET_DK // 2, 1), pltpu.roll(t, RET_DK // 2, 1))
        return t * cos_ref[rows, :] + partner * sin_ref[rows, :]

    def load_rot(c):
        rows, q, k, v = load(c)
        is_lat = c < n_lat
        cc = jnp.minimum(c, n_lat - 1)
        q = jnp.where(is_lat, rotary(q, cc), q)
        k = jnp.where(is_lat, rotary(k, cc), k)
        return rows, q, k, v

    def fwd_step(s, st):
        c = jnp.where(s < n_all - n_lat, n_lat + s, s - (n_all - n_lat))
        rows, q, k, v = load_rot(c)
        kb = k.astype(BF16)
        vb = v.astype(BF16)
        o0 = _dot((_dot_nt(jnp.where(head0, q, 0.0).astype(BF16), kb) * mask_scr[0]).astype(BF16), vb)
        o1 = _dot((_dot_nt(jnp.where(head0, 0.0, q).astype(BF16), kb) * mask_scr[1]).astype(BF16), vb)
        o = jnp.where(head0, o0, o1)
        o = o + _dot_nt((q * pw_scr[0]).astype(BF16), st.astype(BF16))
        o_ref[rows, :] += o
        return st * dcf + _dot_tn(vb, (k * pw_scr[1]).astype(BF16)) * same_head

    def bwd_step(s, st):
        c = n_all - 1 - s
        rows, q, k, v = load_rot(c)
        o_ref[rows, :] += _dot_nt((q * pw_scr[2]).astype(BF16), st.astype(BF16))
        return st * dcb + _dot_tn(v.astype(BF16), (k * pw_scr[3]).astype(BF16)) * same_head

    o_ref[...] = jnp.zeros(o_ref.shape, F32)
    zero = jnp.zeros((LANES, LANES), F32)
    lax.fori_loop(0, n_all, lambda s, sts: (fwd_step(s, sts[0]), bwd_step(s, sts[1])), (zero, zero))


def _ret_call(u, dec, cos_t, sin_t):
    t = u.shape[0]
    sb = _seg()
    col = lambda off: pl.BlockSpec((sb, LANES), lambda b, p: (b, off + p))
    return pl.pallas_call(
        _ret_kernel,
        grid=(BATCH, RET_HEADS // 2),
        in_specs=[pl.BlockSpec((1, 4, LANES), lambda b, p: (p, 0, 0)),
                  pl.BlockSpec((SEQ, LANES), lambda b, p: (0, 0)),
                  pl.BlockSpec((SEQ, LANES), lambda b, p: (0, 0)),
                  col(_C_RQ), col(_C_RK), col(_C_RV)],
        out_specs=pl.BlockSpec((sb, LANES), lambda b, p: (b, p)),
        out_shape=jax.ShapeDtypeStruct((t, RET_W), F32),
        scratch_shapes=[pltpu.VMEM((2, RET_CHUNK, RET_CHUNK), F32),
                        pltpu.VMEM((4, RET_CHUNK, LANES), F32)],
        compiler_params=_params(("arbitrary", "arbitrary")),
        name="retention",
    )(dec, cos_t, sin_t, u, u, u)


def _hg_codes(fwd):
    c_len = HG_BLOCK
    ri = lax.broadcasted_iota(jnp.int32, (c_len, c_len), 0)
    ci = lax.broadcasted_iota(jnp.int32, (c_len, c_len), 1)
    dist = (ri - ci) if fwd else (ci - ri)
    code = jnp.full((c_len, c_len), -1, jnp.int32)
    half, level = c_len // 2, 0
    while half >= HG_SUB:
        sh = half.bit_length() - 1
        same_seg = jnp.right_shift(ri, sh + 1) == jnp.right_shift(ci, sh + 1)
        other_half = jnp.right_shift(ri, sh) != jnp.right_shift(ci, sh)
        code = jnp.where(same_seg & other_half & (dist > 0), level, code)
        half, level = half // 2, level + 1
    sub_sh = HG_SUB.bit_length() - 1
    same_sub = jnp.right_shift(ri, sub_sh) == jnp.right_shift(ci, sub_sh)
    for dl in range(HG_SUB):
        code = jnp.where(same_sub & (dist == dl), 100 + dl, code)
    return code


def _hg_chunk(fwd, q, z, v, lb, st, code, tri):
    c_len = HG_CHUNK

    a = jnp.log(lb)
    b = jnp.log1p(-lb) + _log_sigmoid(z)
    lf = jnp.maximum(a, b) + jnp.log1p(jnp.exp(-jnp.abs(a - b)))
    kin = (1.0 - lb) * _sigmoid(-z)

    g = _dot_exact_lhs(tri, lf) * LOG2_E
    g_end = g[c_len - 1:c_len, :] if fwd else g[0:1, :]
    vb = v.astype(BF16)

    o = _dot_nt((q * jnp.exp2(g)).astype(BF16), st.astype(BF16))

    blk = HG_BLOCK
    lo, hi = slice(0, blk), slice(blk, 2 * blk)
    qrows, krows = (hi, lo) if fwd else (lo, hi)
    bnd = g[blk - 1:blk, :] if fwd else g[blk:blk + 1, :]
    qs = (q[qrows] * jnp.exp2(g[qrows] - bnd)).astype(BF16)
    ks = (kin[krows] * jnp.exp2(bnd - g[krows])).astype(BF16)
    cross = _dot(_dot_nt(qs, ks).astype(BF16), vb[krows])

    parts = []
    for rows in (lo, hi):
        parts.append(_hg_block(fwd, q[rows], kin[rows], g[rows], vb[rows], code))
    parts[1 if fwd else 0] = parts[1 if fwd else 0] + cross
    o = o + jnp.concatenate(parts, axis=0)
    st_new = st * jnp.exp2(g_end) + _dot_tn(vb, (kin * jnp.exp2(g_end - g)).astype(BF16))
    return o, st_new


def _hg_block(fwd, q, kin, g, vb, code):
    n = HG_BLOCK

    scores = jnp.zeros((n, n), F32)
    half, level = n // 2, 0
    while half >= HG_SUB:
        seg = 2 * half
        g3 = g.reshape(n // seg, seg, LANES)
        e = half - 1 if fwd else half
        bsel = jnp.broadcast_to(g3[:, e:e + 1, :], g3.shape).reshape(n, LANES)
        qs = (q * jnp.exp2(g - bsel)).astype(BF16)
        ks = (kin * jnp.exp2(bsel - g)).astype(BF16)
        scores = jnp.where(code == level, _dot_nt(qs, ks), scores)
        half, level = half // 2, level + 1

    def shift(x, dl):
        x3 = x.reshape(n // SUBLANES, SUBLANES, LANES)
        return pltpu.roll(x3, dl if fwd else SUBLANES - dl, 1).reshape(n, LANES)

    for dl in range(HG_SUB):
        w = q * kin if dl == 0 else q * shift(kin, dl) * jnp.exp2(g - shift(g, dl))
        scores = jnp.where(code == 100 + dl, jnp.sum(w, axis=-1, keepdims=True), scores)

    return _dot(scores.astype(BF16), vb)


def _hg_kernel(lbl_ref, q_ref, zf_ref, zb_ref, v_ref, o_ref, *, layer):
    c_len = HG_CHUNK
    n_lat = SEQ // c_len
    n_all = _seg() // c_len

    lbs = []
    for d in range(2):
        x = lbl_ref[d]
        e = jnp.exp(x - jnp.max(x, axis=0, keepdims=True))
        tot = jnp.sum(e, axis=0, keepdims=True)
        part = jnp.zeros_like(tot)
        for i in range(1, layer + 1):
            part = part + e[i:i + 1, :]
        lbs.append(part / tot)

    ri = lax.broadcasted_iota(jnp.int32, (c_len, c_len), 0)
    ci = lax.broadcasted_iota(jnp.int32, (c_len, c_len), 1)
    tri_f = (ri >= ci).astype(BF16)
    tri_b = (ri <= ci).astype(BF16)
    code_f = _hg_codes(True)
    code_b = _hg_codes(False)

    o_ref[...] = jnp.zeros(o_ref.shape, F32)

    def step(s, carry):
        st_f, st_b = carry
        cf = jnp.where(s < n_all - n_lat, n_lat + s, s - (n_all - n_lat))
        cbk = n_all - 1 - s
        rf = pl.ds(pl.multiple_of(cf * c_len, c_len), c_len)
        rk = pl.ds(pl.multiple_of(cbk * c_len, c_len), c_len)
        qf = q_ref[rf, :].astype(F32)
        of, st_f = _hg_chunk(True, qf * _sigmoid(qf), zf_ref[rf, :], v_ref[rf, :].astype(F32), lbs[0], st_f,
                             code_f, tri_f)
        o_ref[rf, :] += of
        qb = q_ref[rk, :].astype(F32)
        ob, st_b = _hg_chunk(False, qb * _sigmoid(qb), zb_ref[rk, :], v_ref[rk, :].astype(F32), lbs[1], st_b,
                             code_b, tri_b)
        o_ref[rk, :] += ob
        return st_f, st_b

    zero = jnp.zeros((LANES, LANES), F32)
    lax.fori_loop(0, n_all, step, (zero, zero))


def _hg_call(u, uz, lb_logits, layer):
    t = u.shape[0]
    sb = _seg()
    col = lambda off: pl.BlockSpec((sb, LANES), lambda b, h: (b, off + h))
    return pl.pallas_call(
        functools.partial(_hg_kernel, layer=layer),
        grid=(BATCH, HG_HEADS),
        in_specs=[pl.BlockSpec((2, DEPTH, LANES), lambda b, h: (0, 0, h)),
                  col(_C_HQ), col(0), col(HG_HEADS), col(_C_HI)],
        out_specs=pl.BlockSpec((sb, LANES), lambda b, h: (b, h)),
        out_shape=jax.ShapeDtypeStruct((t, HG_W), F32),
        compiler_params=_params(("arbitrary", "arbitrary")),
        name="hgrn2",
    )(lb_logits, u, uz, uz, u)


def _lru_kernel(x_ref, wg_ref, bg_ref, lam_ref, cw_ref, cb_ref, y_ref,
                xp_scr, af_scr, bf_scr, ab_scr, bb_scr, hb_scr):
    sb = _seg()
    c_len = LRU_CHUNK
    halo = SUBLANES
    n_tiles = sb // SUBLANES
    n_lat_t = SEQ // SUBLANES

    xp_scr[pl.ds(0, halo), :] = jnp.zeros((halo, LANES), F32)
    xp_scr[pl.ds(halo + sb, halo), :] = jnp.zeros((halo, LANES), F32)
    xp_scr[pl.ds(halo, sb), :] = x_ref[...].astype(F32)

    lam = lam_ref[0]
    sp = jnp.maximum(-lam, 0.0) + jnp.log1p(jnp.exp(-jnp.abs(lam)))
    cw = cw_ref[...]
    wg = wg_ref[0]
    bg = bg_ref[0]
    r8 = lax.broadcasted_iota(jnp.int32, (c_len, 1), 0)
    rm8 = r8 % SUBLANES

    def tile_shift(x, s):
        x3 = x.reshape(c_len // SUBLANES, SUBLANES, LANES)
        return pltpu.roll(x3, s, 1).reshape(c_len, LANES)

    def coeff_chunk(c, _):
        base = pl.multiple_of(c * c_len, c_len)
        w = xp_scr[pl.ds(base, c_len + 2 * halo), :]
        grow = base + r8
        in_lat = grow < SEQ
        pos = jnp.where(in_lat, grow, grow - SEQ)
        seglen = jnp.where(in_lat, SEQ, CTX_LEN)
        wlen = c_len + 2 * halo
        xm2 = jnp.where(pos >= 2, pltpu.roll(w, 2, 0)[halo:halo + c_len], 0.0)
        xm1 = jnp.where(pos >= 1, pltpu.roll(w, 1, 0)[halo:halo + c_len], 0.0)
        x0 = w[halo:halo + c_len]
        xp1 = jnp.where(pos + 1 < seglen, pltpu.roll(w, wlen - 1, 0)[halo:halo + c_len], 0.0)
        xc = xm2 * cw[0:1] + xm1 * cw[1:2] + x0 * cw[2:3] + xp1 * cw[3:4] + cb_ref[...]
        gates = _sigmoid(_dot(xc.astype(BF16), wg) + bg)
        rows = pl.ds(base, c_len)
        for d, (a_scr, b_scr) in enumerate(((af_scr, bf_scr), (ab_scr, bb_scr))):
            rg = gates[:, (2 * d) * LANES:(2 * d + 1) * LANES]
            ig = gates[:, (2 * d + 1) * LANES:(2 * d + 2) * LANES]
            log_a = -LRU_C * rg * sp[d:d + 1]
            a = jnp.exp(log_a)
            th = jnp.tanh(log_a)
            bv = jnp.sqrt(-2.0 * th / (1.0 - th)) * (ig * xc)
            for sft in (1, 2, 4):
                if d == 0:
                    a_s = tile_shift(a, sft)
                    b_s = tile_shift(bv, sft)
                    ok = rm8 >= sft
                else:
                    a_s = tile_shift(a, SUBLANES - sft)
                    b_s = tile_shift(bv, SUBLANES - sft)
                    ok = rm8 < SUBLANES - sft
                bv = jnp.where(ok, a * b_s + bv, bv)
                a = jnp.where(ok, a * a_s, a)
            a_scr[rows, :] = a
            b_scr[rows, :] = bv
        return 0

    lax.fori_loop(0, sb // c_len, coeff_chunk, 0)

    def carry_step(s, carry):
        hf, hb = carry
        tf = jnp.where(s < n_tiles - n_lat_t, n_lat_t + s, s - (n_tiles - n_lat_t))
        tb = n_tiles - 1 - s
        rf = pl.ds(pl.multiple_of(tf * SUBLANES, SUBLANES), SUBLANES)
        rk = pl.ds(pl.multiple_of(tb * SUBLANES, SUBLANES), SUBLANES)
        h1 = af_scr[rf, :] * hf + bf_scr[rf, :]
        y_ref[rf, :] = h1
        h2 = ab_scr[rk, :] * hb + bb_scr[rk, :]
        hb_scr[rk, :] = h2
        return (jnp.broadcast_to(h1[SUBLANES - 1:SUBLANES, :], (SUBLANES, LANES)),
                jnp.broadcast_to(h2[0:1, :], (SUBLANES, LANES)))

    zero = jnp.zeros((SUBLANES, LANES), F32)
    lax.fori_loop(0, n_tiles, carry_step, (zero, zero))
    y_ref[...] += hb_scr[...]


def _lru_call(u, wg, bg, lam, cw, cb):
    t = u.shape[0]
    sb = _seg()
    nb = LRU_W // LANES
    return pl.pallas_call(
        _lru_kernel,
        grid=(BATCH, nb),
        in_specs=[pl.BlockSpec((sb, LANES), lambda b, c: (b, _C_LX + c)),
                  pl.BlockSpec((1, LANES, 4 * LANES), lambda b, c: (c, 0, 0)),
                  pl.BlockSpec((1, 1, 4 * LANES), lambda b, c: (c, 0, 0)),
                  pl.BlockSpec((1, 2, LANES), lambda b, c: (c, 0, 0)),
                  pl.BlockSpec((CONV_W, LANES), lambda b, c: (0, c)),
                  pl.BlockSpec((1, LANES), lambda b, c: (0, c))],
        out_specs=pl.BlockSpec((sb, LANES), lambda b, c: (b, c)),
        out_shape=jax.ShapeDtypeStruct((t, LRU_W), F32),
        scratch_shapes=[pltpu.VMEM((sb + 2 * SUBLANES, LANES), F32)] + [pltpu.VMEM((sb, LANES), F32)] * 5,
        compiler_params=_params(("arbitrary", "arbitrary")),
        name="rglru",
    )(u, wg, bg, lam, cw, cb)


def _group_rms(o, group):
    ri = lax.broadcasted_iota(jnp.int32, (LANES, LANES), 0) // group
    ci = lax.broadcasted_iota(jnp.int32, (LANES, LANES), 1) // group
    ones_bd = (ri == ci).astype(BF16)
    outs = []
    for tix in range(o.shape[1] // LANES):
        x = o[:, tix * LANES:(tix + 1) * LANES]
        sq = x * x
        hi = sq.astype(BF16)
        lo = (sq - hi.astype(F32)).astype(BF16)
        ss = _dot(hi, ones_bd) + _dot(lo, ones_bd)
        outs.append(x * lax.rsqrt(ss * (1.0 / group) + NORM_EPS))
    return jnp.concatenate(outs, axis=1)


def _merge_kernel(*refs, n_gate):
    (x_ref, ml_ref, mc_ref, oret_ref, ohg_ref, ylru_ref, rg_ref, hgt_ref, lgt_ref) = refs[:9]
    gate_refs = refs[9:9 + 3 * n_gate]
    (gng_ref, ghg_ref, wr_ref, wh_ref, wl_ref, wo_ref, n2g_ref, rw_ref, rb_ref,
     xo_ref, h2_ref, lg_ref) = refs[9 + 3 * n_gate:]
    tiles_per_b = _seg() // ROW_TILE
    is_ctx = (pl.program_id(0) % tiles_per_b) >= SEQ // ROW_TILE
    ml = ml_ref[0]
    mc = mc_ref[0]

    def gate(i):
        parts = [gate_refs[i * n_gate + j][...].astype(F32) for j in range(n_gate)]
        return _sigmoid(parts[0] if n_gate == 1 else jnp.concatenate(parts, axis=1))

    rg = rg_ref[...].astype(F32)
    ret_in = _group_rms(oret_ref[...], RET_DK) * gng_ref[...] * (rg * _sigmoid(rg))
    ret = _dot(ret_in.astype(BF16), wr_ref[0])
    hg_in = _group_rms(ohg_ref[...], HG_DK) * ghg_ref[...] * _sigmoid(hgt_ref[...].astype(F32))
    hgr = _dot(hg_in.astype(BF16), wh_ref[0])
    lgt = lgt_ref[...].astype(F32)
    gelu = 0.5 * lgt * (1.0 + jnp.tanh(0.7978845608028654 * (lgt + 0.044715 * lgt * lgt * lgt)))
    lru = _dot((ylru_ref[...] * gelu).astype(BF16), wl_ref[0])
    merged = gate(0) * ret + gate(1) * hgr + gate(2) * lru
    mix = _dot(merged.astype(BF16), wo_ref[0])
    g1 = jnp.where(is_ctx, mc[2:3, :], ml[2:3, :])
    xn = x_ref[...] + g1 * mix
    xo_ref[...] = xn
    h2 = _norm_modulate(xn, n2g_ref[...], ml, mc, is_ctx, 3)
    h2_ref[...] = h2.astype(BF16)
    lg_ref[...] = _dot3(h2, rw_ref[...]) + rb_ref[...]


def _merge_call(xs, mod, o_ret, o_hg, y_lru, u, gn_g, hgn_g, wr, wh, wl, wo, n2g, rw, rb, layer):
    t = xs.shape[0]
    r = ROW_TILE
    tiles_per_b = _seg() // r
    n_gate = D_MODEL // 512
    row = lambda w: pl.BlockSpec((r, w), lambda i: (i, 0))
    ucol = lambda blk: pl.BlockSpec((r, 512), lambda i: (i, blk))

    def full(a):
        if a.ndim == 3:
            return pl.BlockSpec((1,) + a.shape[1:], lambda i: (layer, 0, 0))
        return pl.BlockSpec(a.shape, lambda i: (0,) * a.ndim)
    gate_base = _C_GATES * LANES // 512
    in_specs = [row(D_MODEL),
                pl.BlockSpec((1, N_MOD, D_MODEL), lambda i: (i // tiles_per_b, 0, 0)),
                pl.BlockSpec((1, N_MOD, D_MODEL), lambda i: (BATCH, 0, 0)),
                row(RET_W), row(HG_W), row(LRU_W),
                ucol(_C_RG * LANES // 512), ucol(_C_HGT * LANES // 512), ucol(_C_LGT * LANES // 512)]
    in_specs += [ucol(gate_base + j) for j in range(3 * n_gate)]
    consts = [gn_g, hgn_g, wr, wh, wl, wo, n2g, rw, rb]
    in_specs += [full(a) for a in consts]
    return pl.pallas_call(
        functools.partial(_merge_kernel, n_gate=n_gate),
        grid=(t // r,),
        in_specs=in_specs,
        out_specs=[row(D_MODEL), row(D_MODEL), row(LANES)],
        out_shape=[jax.ShapeDtypeStruct((t, D_MODEL), F32), jax.ShapeDtypeStruct((t, D_MODEL), BF16),
                   jax.ShapeDtypeStruct((t, LANES), F32)],
        compiler_params=_params(("arbitrary",)),
        name="merge",
    )(xs, mod, mod, o_ret, o_hg, y_lru, *([u] * (3 + 3 * n_gate)), *consts)


def _route_kernel(lg_ref, oi_ref, ow_ref, lst_ref, meta_ref, cnt_ref, off_scr):
    @pl.when(pl.program_id(0) == 0)
    def _():
        off_scr[...] = jnp.zeros(off_scr.shape, F32)

    r = lg_ref.shape[0]
    lane = lax.broadcasted_iota(jnp.int32, (r, LANES), 1)
    neg = -jnp.inf
    l = jnp.where(lane < N_EXPERTS, lg_ref[...], neg)
    vals, hots = [], []
    lanef = lane.astype(F32)
    for _ in range(TOP_K):
        m = jnp.max(l, axis=-1, keepdims=True)
        idxf = jnp.min(jnp.where(l == m, lanef, float(LANES)), axis=-1, keepdims=True)
        hot = lanef == idxf
        l = jnp.where(hot, neg, l)
        vals.append(m)
        hots.append(hot)
    es = [jnp.exp(v - vals[0]) for v in vals]
    den = es[0]
    for e in es[1:]:
        den = den + e
    chosen = hots[0].astype(F32)
    for hot in hots[1:]:
        chosen = chosen + hot.astype(F32)
    ri = lax.broadcasted_iota(jnp.int32, (r, r), 0)
    ci = lax.broadcasted_iota(jnp.int32, (r, r), 1)
    before = _dot((ri > ci).astype(BF16), chosen.astype(BF16))
    count = jnp.sum(chosen, axis=0, keepdims=True)
    run = jnp.floor((count + (SUBLANES - 1.0)) * (1.0 / SUBLANES)) * SUBLANES
    li = lax.broadcasted_iota(jnp.int32, (LANES, LANES), 0)
    lj = lax.broadcasted_iota(jnp.int32, (LANES, LANES), 1)
    run8 = jnp.broadcast_to(run, (SUBLANES, LANES))
    start = _dot(run8.astype(BF16), (li < lj).astype(BF16))[0:1, :]
    pos = start + before
    oi = jnp.zeros((r, LANES), jnp.int32)
    ow = jnp.zeros((r, LANES), F32)
    lsf = jnp.zeros((r, LANES), F32)
    for k in range(TOP_K):
        row_k = jnp.sum(jnp.where(hots[k], pos, 0.0), axis=-1, keepdims=True)
        oi = jnp.where(lane == k, row_k.astype(jnp.int32), oi)
        lsf = jnp.where(lane == k, row_k, lsf)
        ow = jnp.where(lane == k, es[k] / den, ow)
    oi_ref[...] = oi
    ow_ref[...] = ow
    lst_ref[0] = jnp.transpose(lsf)[0:SUBLANES, :]
    mrow = lax.broadcasted_iota(jnp.int32, (SUBLANES, LANES), 0)
    meta_ref[0] = jnp.where(mrow == 0, run8, jnp.where(mrow == 1, jnp.broadcast_to(start, (SUBLANES, LANES)),
                                                       off_scr[...]))
    off_scr[...] = off_scr[...] + run
    cnt_ref[...] = off_scr[...]


def _route_call(logits):
    t = logits.shape[0]
    r = ROW_TILE
    row = pl.BlockSpec((r, LANES), lambda i: (i, 0))
    return pl.pallas_call(
        _route_kernel,
        grid=(t // r,),
        in_specs=[row],
        out_specs=[row, row, pl.BlockSpec((1, SUBLANES, r), lambda i: (i, 0, 0)),
                   pl.BlockSpec((1, SUBLANES, LANES), lambda i: (i, 0, 0)),
                   pl.BlockSpec((SUBLANES, LANES), lambda i: (0, 0))],
        out_shape=[jax.ShapeDtypeStruct((t, LANES), jnp.int32), jax.ShapeDtypeStruct((t, LANES), F32),
                   jax.ShapeDtypeStruct((t // r, SUBLANES, r), F32),
                   jax.ShapeDtypeStruct((t // r, SUBLANES, LANES), F32),
                   jax.ShapeDtypeStruct((SUBLANES, LANES), F32)],
        scratch_shapes=[pltpu.VMEM((SUBLANES, LANES), F32)],
        compiler_params=_params(("arbitrary",)),
        name="route",
    )(logits)


def _local_rows():
    rows = ROW_TILE * TOP_K + N_EXPERTS * (SUBLANES - 1)
    return -(-rows // ROW_TILE) * ROW_TILE


def _run_copies(meta_ref, make_copy, do):
    def per_expert(e, _):
        first_slot = meta_ref[0, 0, e]
        first_row = meta_ref[0, 0, N_EXPERTS + e]
        pieces = meta_ref[0, 0, 2 * N_EXPERTS + e]

        def per_piece(j, _):
            do(make_copy(pl.multiple_of(first_row + j * SUBLANES, SUBLANES),
                         pl.multiple_of(first_slot + j * SUBLANES, SUBLANES)))
            return 0

        lax.fori_loop(0, pieces, per_piece, 0)
        return 0

    lax.fori_loop(0, N_EXPERTS, per_expert, 0)


def _dispatch_kernel(meta_ref, lst_ref, h_ref, xs_in_ref, xs_ref, perm_scr, sem):
    del xs_in_ref
    r = h_ref.shape[0]
    ls = lst_ref[0].astype(jnp.int32)
    hb = h_ref[...]
    for s in range(perm_scr.shape[0] // r):
        rows = s * r + lax.broadcasted_iota(jnp.int32, (r, r), 0)
        sel = rows == ls[0:1, :]
        for k in range(1, TOP_K):
            sel = jnp.logical_or(sel, rows == ls[k:k + 1, :])
        perm_scr[pl.ds(s * r, r), :] = _dot(jnp.where(sel, 1.0, 0.0).astype(BF16), hb)

    def make_copy(row, slot):
        return pltpu.make_async_copy(perm_scr.at[pl.ds(row, SUBLANES)], xs_ref.at[pl.ds(slot, SUBLANES)], sem)

    _run_copies(meta_ref, make_copy, lambda c: c.start())
    _run_copies(meta_ref, make_copy, lambda c: c.wait())


def _dispatch_call(h2, meta, lst, slots):
    t = h2.shape[0]
    r = ROW_TILE
    return pl.pallas_call(
        _dispatch_kernel,
        grid=(t // r,),
        in_specs=[pl.BlockSpec((1, 1, LANES), lambda i: (i, 0, 0), memory_space=pltpu.SMEM),
                  pl.BlockSpec((1, SUBLANES, r), lambda i: (i, 0, 0)),
                  pl.BlockSpec((r, D_MODEL), lambda i: (i, 0)),
                  pl.BlockSpec(memory_space=pl.ANY)],
        out_specs=pl.BlockSpec(memory_space=pl.ANY),
        out_shape=jax.ShapeDtypeStruct(slots.shape, F32),
        scratch_shapes=[pltpu.VMEM((_local_rows(), D_MODEL), F32), pltpu.SemaphoreType.DMA(())],
        input_output_aliases={3: 0},
        compiler_params=_params(("arbitrary",)),
        name="dispatch",
    )(meta, lst, h2, slots)


def _expert_kernel(bexp_ref, nv_ref, x_ref, wgu_ref, bgu_ref, wd_ref, bd_ref, y_ref, wgu_scr, wd_scr):
    i = pl.program_id(0)
    f = D_MODEL
    prev = bexp_ref[jnp.maximum(i - 1, 0)]
    new_expert = jnp.logical_or(i == 0, bexp_ref[i] != prev)

    @pl.when(jnp.logical_and(i < nv_ref[0], new_expert))
    def _():
        slab = 256
        for s in range(D_MODEL // slab):
            wgu_scr[pl.ds(s * slab, slab), :] = wgu_ref[0, 0, pl.ds(s * slab, slab), :].astype(BF16)
            wd_scr[pl.ds(s * slab, slab), :] = wd_ref[0, 0, pl.ds(s * slab, slab), :].astype(BF16)

    @pl.when(i < nv_ref[0])
    def _():
        gu = _dot(x_ref[...].astype(BF16), wgu_scr[...]) + bgu_ref[0, 0]
        gate = jnp.minimum(gu[:, :f], SWIGLU_LIMIT)
        up = jnp.clip(gu[:, f:], -SWIGLU_LIMIT, SWIGLU_LIMIT)
        glu = gate * _sigmoid(SWIGLU_ALPHA * gate)
        y_ref[...] = _dot(((up + 1.0) * glu).astype(BF16), wd_scr[...]) + bd_ref[0, 0]

    @pl.when(i >= nv_ref[0])
    def _():
        y_ref[...] = jnp.zeros(y_ref.shape, F32)


def _expert_call(slots, bexp, nvalid, w_gu, b_gu, w_down, b_down, layer):
    s = slots.shape[0]
    nb = s // MOE_BLOCK
    f = D_MODEL
    blk = lambda i, be, nv: (jnp.minimum(i, nv[0] - 1), 0)
    exp3 = lambda i, be, nv: (layer, be[jnp.minimum(i, nv[0] - 1)], 0, 0)
    grid_spec = pltpu.PrefetchScalarGridSpec(
        num_scalar_prefetch=2,
        grid=(nb,),
        in_specs=[pl.BlockSpec((MOE_BLOCK, D_MODEL), blk),
                  pl.BlockSpec((1, 1, D_MODEL, 2 * f), exp3),
                  pl.BlockSpec((1, 1, 1, 2 * f), exp3),
                  pl.BlockSpec((1, 1, f, D_MODEL), exp3),
                  pl.BlockSpec((1, 1, 1, D_MODEL), exp3)],
        out_specs=pl.BlockSpec((MOE_BLOCK, D_MODEL), lambda i, be, nv: (i, 0)),
        scratch_shapes=[pltpu.VMEM((D_MODEL, 2 * f), BF16), pltpu.VMEM((f, D_MODEL), BF16)],
    )
    return pl.pallas_call(
        _expert_kernel,
        grid_spec=grid_spec,
        out_shape=jax.ShapeDtypeStruct((s, D_MODEL), F32),
        compiler_params=_params(("arbitrary",)),
        name="experts",
    )(bexp, nvalid, slots, w_gu, b_gu.reshape(DEPTH, N_EXPERTS, 1, 2 * f), w_down,
      b_down.reshape(DEPTH, N_EXPERTS, 1, D_MODEL))


def _combine_kernel(meta_ref, x_ref, ml_ref, mc_ref, oi_ref, w_ref, fg_ref, y_ref, o_ref, buf, sem, *, last):
    r = x_ref.shape[0]

    @pl.when(jnp.logical_and(pl.program_id(0) == 0, pl.program_id(1) == 0))
    def _():
        buf[...] = jnp.zeros(buf.shape, F32)

    def make_copy(row, slot):
        return pltpu.make_async_copy(y_ref.at[pl.ds(slot, SUBLANES)], buf.at[pl.ds(row, SUBLANES)], sem)

    _run_copies(meta_ref, make_copy, lambda c: c.start())
    _run_copies(meta_ref, make_copy, lambda c: c.wait())

    if last:
        is_ctx = False
    else:
        is_ctx = (pl.program_id(1)) >= SEQ // ROW_TILE
    w = w_ref[...]
    oi = oi_ref[...]
    ff = jnp.zeros((r, D_MODEL), F32)
    for s in range(buf.shape[0] // r):
        cols = s * r + lax.broadcasted_iota(jnp.int32, (r, r), 1)
        wm = jnp.zeros((r, r), F32)
        for k in range(TOP_K):
            wm = jnp.where(cols == oi[:, k:k + 1], w[:, k:k + 1], wm)
        ff = ff + _dot(wm.astype(BF16), buf[pl.ds(s * r, r), :].astype(BF16))
    g2 = jnp.where(is_ctx, mc_ref[0][5:6, :], ml_ref[0][5:6, :])
    xn = x_ref[...] + g2 * ff
    if last:
        ms = jnp.mean(xn * xn, axis=-1, keepdims=True)
        xn = xn * lax.rsqrt(ms + NORM_EPS) * fg_ref[...]
    o_ref[...] = xn


def _combine_call(xs, mod, meta, rows, wts, y_slots, final_g, last):
    t = xs.shape[0]
    r = ROW_TILE
    tiles_per_b = _seg() // r
    n_t = SEQ // r if last else tiles_per_b
    tile = lambda b, j: b * tiles_per_b + j
    out_rows = BATCH * SEQ if last else t
    return pl.pallas_call(
        functools.partial(_combine_kernel, last=last),
        grid=(BATCH, n_t),
        in_specs=[pl.BlockSpec((1, 1, LANES), lambda b, j: (tile(b, j), 0, 0), memory_space=pltpu.SMEM),
                  pl.BlockSpec((r, D_MODEL), lambda b, j: (tile(b, j), 0)),
                  pl.BlockSpec((1, N_MOD, D_MODEL), lambda b, j: (b, 0, 0)),
                  pl.BlockSpec((1, N_MOD, D_MODEL), lambda b, j: (BATCH, 0, 0)),
                  pl.BlockSpec((r, LANES), lambda b, j: (tile(b, j), 0)),
                  pl.BlockSpec((r, LANES), lambda b, j: (tile(b, j), 0)),
                  pl.BlockSpec((1, D_MODEL), lambda b, j: (0, 0)),
                  pl.BlockSpec(memory_space=pl.ANY)],
        out_specs=pl.BlockSpec((r, D_MODEL), lambda b, j: (b * n_t + j, 0)),
        out_shape=jax.ShapeDtypeStruct((out_rows, D_MODEL), F32),
        scratch_shapes=[pltpu.VMEM((_local_rows(), D_MODEL), F32), pltpu.SemaphoreType.DMA(())],
        compiler_params=_params(("arbitrary", "arbitrary")),
        name="combine",
    )(meta, xs, mod, mod, rows, wts, final_g, y_slots)


def _rotary_tables():
    rows = SEQ // GRID_W
    row = jnp.repeat(jnp.arange(rows, dtype=F32), GRID_W)
    colp = jnp.tile(jnp.arange(GRID_W, dtype=F32), rows)
    n_freq = RET_DK // 4
    inv = ROPE_BASE ** (-jnp.arange(n_freq, dtype=F32) / n_freq)
    ang = jnp.concatenate([row[:, None] * inv, colp[:, None] * inv], axis=-1)
    cos, sin = jnp.cos(ang), jnp.sin(ang)
    reps = LANES // (RET_DK // 2)
    cos_t = jnp.tile(cos, (1, reps))
    sign = jnp.where((jnp.arange(LANES) % RET_DK) < RET_DK // 2, -1.0, 1.0).astype(F32)
    sin_t = jnp.tile(sin, (1, reps)) * sign
    return cos_t, sin_t


def _lru_gate_weights(wa, ba, wx, bx):
    nb = LRU_W // LANES
    per = LANES // LRU_BD
    eye = jnp.eye(per, dtype=F32)

    def embed(w):
        w = w.reshape(nb, per, LRU_BD, LRU_BD)
        return jnp.einsum('cpij,pq->cpiqj', w, eye).reshape(nb, LANES, LANES)

    wg = jnp.concatenate([embed(wa[0]), embed(wx[0]), embed(wa[1]), embed(wx[1])], axis=-1)
    bias = jnp.concatenate([ba[0].reshape(nb, 1, LANES), bx[0].reshape(nb, 1, LANES),
                            ba[1].reshape(nb, 1, LANES), bx[1].reshape(nb, 1, LANES)], axis=-1)
    return wg.astype(BF16), bias


def kernel(x, c, ctx, c_ctx, mod_w, mod_b, norm1_g, norm2_g, final_g, w_in, ret_decay, ret_gn_g, w_ret_o,
           hgrn_lb_logits, hgrn_gn_g, w_hgrn_o, lru_conv_w, lru_conv_b, lru_wa, lru_ba, lru_wx, lru_bx,
           lru_lambda, w_lru_o, w_out, router_w, router_b, exp_w_gu, exp_b_gu, exp_w_down, exp_b_down):
    assert CTX_LEN == RET_CHUNK and SEQ % RET_CHUNK == 0
    sb = _seg()
    t = BATCH * sb
    xs = jnp.concatenate([x, ctx], axis=1).reshape(t, D_MODEL)

    cvec = jnp.concatenate([c, c_ctx[None, :], jnp.zeros((SUBLANES - BATCH - 1, D_MODEL), F32)], axis=0)
    mod_all = _mod_call(cvec, mod_w, mod_b).reshape(DEPTH, SUBLANES, N_MOD, D_MODEL)

    cos_t, sin_t = _rotary_tables()
    lb_logits = jnp.transpose(hgrn_lb_logits, (1, 0, 2))
    n_slots = t * TOP_K + (t // ROW_TILE) * N_EXPERTS * (SUBLANES - 1) + N_EXPERTS * (MOE_BLOCK - 1)
    n_blocks = -(-n_slots // MOE_BLOCK)
    n_slots = n_blocks * MOE_BLOCK
    slots = jnp.zeros((n_slots, D_MODEL), F32)
    rw_pad = jnp.pad(router_w, ((0, 0), (0, 0), (0, LANES - N_EXPERTS)))
    rb_pad = jnp.pad(router_b, ((0, 0), (0, LANES - N_EXPERTS)))
    w_in_b, w_ret_b, w_hg_b, w_lru_b, w_out_b = (w.astype(BF16) for w in (w_in, w_ret_o, w_hgrn_o, w_lru_o, w_out))
    block_first_slot = jnp.arange(n_blocks, dtype=jnp.int32) * MOE_BLOCK

    out = None
    for l in range(DEPTH):
        last = l == DEPTH - 1
        mod = mod_all[l]
        u, uz = _inproj_call(xs, mod, norm1_g[l][None, :], w_in_b, l)

        dec = jnp.broadcast_to(ret_decay[l].reshape(2, RET_HEADS // 2, 2, 1).transpose(1, 0, 2, 3),
                               (RET_HEADS // 2, 2, 2, LANES)).reshape(RET_HEADS // 2, 4, LANES)
        o_ret = _ret_call(u, dec, cos_t, sin_t)
        o_hg = _hg_call(u, uz, lb_logits, l)
        wg, bg = _lru_gate_weights(lru_wa[l], lru_ba[l], lru_wx[l], lru_bx[l])
        lam = lru_lambda[l].reshape(2, LRU_W // LANES, LANES).transpose(1, 0, 2)
        y_lru = _lru_call(u, wg, bg, lam, lru_conv_w[l], lru_conv_b[l][None, :])

        xs, h2, logits = _merge_call(
            xs, mod, o_ret, o_hg, y_lru, u, ret_gn_g[l][None, :], hgrn_gn_g[l][None, :],
            w_ret_b, w_hg_b, w_lru_b, w_out_b, norm2_g[l][None, :], rw_pad[l], rb_pad[l][None, :], l)

        rows, wts, lst, tile_meta, cnt = _route_call(logits)
        run = tile_meta[:, 0, :N_EXPERTS].astype(jnp.int32)
        first_row = tile_meta[:, 1, :N_EXPERTS].astype(jnp.int32)
        earlier = tile_meta[:, 2, :N_EXPERTS].astype(jnp.int32)
        counts = cnt[0, :N_EXPERTS].astype(jnp.int32)
        padded = (counts + MOE_BLOCK - 1) // MOE_BLOCK * MOE_BLOCK
        pad_end = jnp.cumsum(padded)
        first_slot = (pad_end - padded)[None, :] + earlier
        meta = jnp.concatenate([first_slot, first_row, run // SUBLANES, jnp.zeros_like(run)], axis=1)
        meta = meta.reshape(t // ROW_TILE, 1, LANES)
        bexp = jnp.minimum(jnp.sum((pad_end[None, :] <= block_first_slot[:, None]).astype(jnp.int32), axis=1),
                           N_EXPERTS - 1)
        nvalid = (pad_end[-1:] // MOE_BLOCK).astype(jnp.int32)

        slots = _dispatch_call(h2, meta, lst, slots)
        y_slots = _expert_call(slots, bexp, nvalid, exp_w_gu, exp_b_gu, exp_w_down, exp_b_down, l)
        res = _combine_call(xs, mod, meta, rows, wts, y_slots, final_g[None, :], last)
        if last:
            out = res.reshape(BATCH, SEQ, D_MODEL)
        else:
            xs = res
    return out
```

```python
import functools

import jax
import jax.numpy as jnp
from jax import lax
from jax.experimental import pallas as pl
from jax.experimental.pallas import tpu as pltpu

F32 = jnp.float32
BF16 = jnp.bfloat16

D_MODEL = 1024
BATCH = 4
SEQ = 4096
DEPTH = 4
GRID_W = 64
CTX_LEN = 256
N_MOD = 6
NORM_EPS = 1e-6
RET_HEADS = 8
RET_DK = 64
RET_W = 512
ROPE_BASE = 10000.0
HG_HEADS = 4
HG_DK = 128
HG_W = 512
LRU_W = 512
LRU_BLOCKS = 8
LRU_BD = 64
LRU_C = 8.0
CONV_W = 4
N_EXPERTS = 32
TOP_K = 4
SWIGLU_ALPHA = 1.702
SWIGLU_LIMIT = 7.0

LANES = 128
SUBLANES = 8
RET_CHUNK = 256
HG_CHUNK = 256
HG_BLOCK = 128
HG_SUB = 4
LOG2_E = 1.4426950408889634
LRU_CHUNK = 256
MOE_BLOCK = 256
ROW_TILE = 256
ROUTE_TILE = 512
MOD_TN = 1536
VMEM_LIMIT = 56 * 1024 * 1024

_C_RQ, _C_RK, _C_RV, _C_RG = 0, 4, 8, 12
_C_HQ, _C_HFF, _C_HFB, _C_HI, _C_HGT = 16, 20, 24, 28, 32
_C_LX, _C_LGT = 36, 40
_C_GATES = 44


def _seg():
    return SEQ + CTX_LEN


def _in_cols():
    return _C_GATES * LANES + 3 * D_MODEL


def _dot(a, b):
    return jnp.dot(a, b, preferred_element_type=F32)


def _dot_nt(a, b):
    return lax.dot_general(a, b, (((1,), (1,)), ((), ())), preferred_element_type=F32)


def _dot_tn(a, b):
    return lax.dot_general(a, b, (((0,), (0,)), ((), ())), preferred_element_type=F32)


def _split3(x):
    hi = x.astype(BF16)
    r = x - hi.astype(F32)
    mid = r.astype(BF16)
    lo = (r - mid.astype(F32)).astype(BF16)
    return hi, mid, lo


def _dot_exact_lhs(a, x):
    hi, mid, lo = _split3(x)
    return _dot(a, hi) + _dot(a, mid) + _dot(a, lo)


def _dot_exact_rhs(x, a):
    hi, mid, lo = _split3(x)
    return _dot(hi, a) + _dot(mid, a) + _dot(lo, a)


def _dot3(a, b):
    ah, am, _ = _split3(a)
    bh, bm, _ = _split3(b)
    return _dot(ah, bh) + (_dot(ah, bm) + _dot(am, bh))


def _sigmoid(x):
    return 0.5 + 0.5 * jnp.tanh(0.5 * x)


def _log_sigmoid(x):
    return jnp.minimum(x, 0.0) - jnp.log1p(jnp.exp(-jnp.abs(x)))


def _params(sem):
    return pltpu.CompilerParams(dimension_semantics=sem, vmem_limit_bytes=VMEM_LIMIT)


def _mod_kernel(c_ref, w_ref, b_ref, o_ref):
    c = c_ref[...]
    s = c * _sigmoid(c)
    o_ref[0] = _dot3(s, w_ref[0]) + b_ref[0]


def _mod_call(cvec, mod_w, mod_b):
    n6 = N_MOD * D_MODEL
    tn = MOD_TN
    return pl.pallas_call(
        _mod_kernel,
        grid=(DEPTH, n6 // tn),
        in_specs=[pl.BlockSpec((SUBLANES, D_MODEL), lambda l, j: (0, 0)),
                  pl.BlockSpec((1, D_MODEL, tn), lambda l, j: (l, 0, j)),
                  pl.BlockSpec((1, 1, tn), lambda l, j: (l, 0, j))],
        out_specs=pl.BlockSpec((1, SUBLANES, tn), lambda l, j: (l, 0, j)),
        out_shape=jax.ShapeDtypeStruct((DEPTH, SUBLANES, n6), F32),
        compiler_params=_params(("arbitrary", "arbitrary")),
        name="modulation",
    )(cvec, mod_w, mod_b.reshape(DEPTH, 1, n6))


def _norm_modulate(x, g, mod_l, mod_c, is_ctx, first):
    ms = jnp.mean(x * x, axis=-1, keepdims=True)
    y = x * lax.rsqrt(ms + NORM_EPS) * g
    sh = jnp.where(is_ctx, mod_c[first:first + 1, :], mod_l[first:first + 1, :])
    sc = jnp.where(is_ctx, mod_c[first + 1:first + 2, :], mod_l[first + 1:first + 2, :])
    return y * (1.0 + sc) + sh


def _inproj_kernel(x_ref, ml_ref, mc_ref, g_ref, w_ref, o_ref, oz_ref, h_scr, *, tm, z_first, z_count):
    j = pl.program_id(1)

    @pl.when(j == 0)
    def _():
        tiles_per_b = _seg() // tm
        row0 = (pl.program_id(0) % tiles_per_b) * tm
        rows = row0 + lax.broadcasted_iota(jnp.int32, (tm, 1), 0)
        h = _norm_modulate(x_ref[...], g_ref[...], ml_ref[0], mc_ref[0], rows >= SEQ, 0)
        h_scr[...] = h.astype(BF16)

    acc = _dot(h_scr[...], w_ref[0])
    o_ref[...] = acc.astype(BF16)

    @pl.when(jnp.logical_and(j >= z_first, j < z_first + z_count))
    def _():
        oz_ref[...] = acc


def _inproj_call(xs, mod, g, w, layer):
    t = xs.shape[0]
    tm = _seg() // 2
    ncol = _in_cols()
    tn = HG_W
    tiles_per_b = _seg() // tm
    z_first = _C_HFF * LANES // tn
    z_count = 2
    return pl.pallas_call(
        functools.partial(_inproj_kernel, tm=tm, z_first=z_first, z_count=z_count),
        grid=(t // tm, ncol // tn),
        in_specs=[pl.BlockSpec((tm, D_MODEL), lambda i, j: (i, 0)),
                  pl.BlockSpec((1, N_MOD, D_MODEL), lambda i, j: (i // tiles_per_b, 0, 0)),
                  pl.BlockSpec((1, N_MOD, D_MODEL), lambda i, j: (BATCH, 0, 0)),
                  pl.BlockSpec((1, D_MODEL), lambda i, j: (0, 0)),
                  pl.BlockSpec((1, D_MODEL, tn), lambda i, j: (layer, 0, j))],
        out_specs=[pl.BlockSpec((tm, tn), lambda i, j: (i, j)),
                   pl.BlockSpec((tm, tn), lambda i, j: (i, jnp.clip(j - z_first, 0, z_count - 1)))],
        out_shape=[jax.ShapeDtypeStruct((t, ncol), BF16), jax.ShapeDtypeStruct((t, z_count * tn), F32)],
        scratch_shapes=[pltpu.VMEM((tm, D_MODEL), BF16)],
        compiler_params=_params(("arbitrary", "arbitrary")),
        name="in_projection",
    )(xs, mod, mod, g, w)


def _ret_kernel(dec_ref, cos_ref, sin_ref, q_ref, k_ref, v_ref, o_ref, mask_scr, pw_scr):
    c_len = RET_CHUNK
    n_lat = SEQ // c_len
    n_all = _seg() // c_len
    lane = lax.broadcasted_iota(jnp.int32, (1, LANES), 1)
    head0 = lane < RET_DK
    first_half = (lane % RET_DK) < (RET_DK // 2)

    lg = _log_sigmoid(dec_ref[0])
    dvf = jnp.where(head0, lg[0:1], lg[1:2])
    dvb = jnp.where(head0, lg[2:3], lg[3:4])
    r = lax.broadcasted_iota(jnp.int32, (c_len, 1), 0).astype(F32)
    pw_scr[0] = jnp.exp((r + 1.0) * dvf)
    pw_scr[1] = jnp.exp((c_len - 1.0 - r) * dvf)
    pw_scr[2] = jnp.exp((c_len - r) * dvb)
    pw_scr[3] = jnp.exp(r * dvb)
    dcf = jnp.exp(c_len * dvf)
    dcb = jnp.exp(c_len * dvb)
    ri = lax.broadcasted_iota(jnp.int32, (c_len, c_len), 0)
    ci = lax.broadcasted_iota(jnp.int32, (c_len, c_len), 1)
    dm = (ri - ci).astype(F32)
    for h in range(2):
        lgf = jnp.concatenate([lg[h:h + 1]] * (c_len // LANES), axis=1)
        lgb = jnp.concatenate([lg[2 + h:3 + h]] * (c_len // LANES), axis=1)
        mask_scr[h] = jnp.exp(jnp.where(dm >= 0, dm * lgf, -dm * lgb))
    rb = lax.broadcasted_iota(jnp.int32, (LANES, LANES), 0) < RET_DK
    cb = lax.broadcasted_iota(jnp.int32, (LANES, LANES), 1) < RET_DK
    same_head = (rb == cb).astype(F32)

    def load(c):
        rows = pl.ds(pl.multiple_of(c * c_len, c_len), c_len)
        q = q_ref[rows, :].astype(F32)
        k = k_ref[rows, :].astype(F32) * (RET_DK ** -0.5)
        v = v_ref[rows, :].astype(F32)
        return rows, q, k, v

    def rotary(t, c):
        rows = pl.ds(pl.multiple_of(c * c_len, c_len), c_len)
        partner = jnp.where(first_half, pltpu.roll(t, LANES - RET_DK // 2, 1), pltpu.roll(t, RET_DK // 2, 1))
        return t * cos_ref[rows, :] + partner * sin_ref[rows, :]

    def load_rot(c):
        rows, q, k, v = load(c)
        is_lat = c < n_lat
        cc = jnp.minimum(c, n_lat - 1)
        q = jnp.where(is_lat, rotary(q, cc), q)
        k = jnp.where(is_lat, rotary(k, cc), k)
        return rows, q, k, v

    def fwd_step(s, st):
        c = jnp.where(s < n_all - n_lat, n_lat + s, s - (n_all - n_lat))
        rows, q, k, v = load_rot(c)
        kb = k.astype(BF16)
        vb = v.astype(BF16)
        o0 = _dot((_dot_nt(jnp.where(head0, q, 0.0).astype(BF16), kb) * mask_scr[0]).astype(BF16), vb)
        o1 = _dot((_dot_nt(jnp.where(head0, 0.0, q).astype(BF16), kb) * mask_scr[1]).astype(BF16), vb)
        o = jnp.where(head0, o0, o1)
        o = o + _dot_nt((q * pw_scr[0]).astype(BF16), st.astype(BF16))
        o_ref[rows, :] += o
        return st * dcf + _dot_tn(vb, (k * pw_scr[1]).astype(BF16)) * same_head

    def bwd_step(s, st):
        c = n_all - 1 - s
        rows, q, k, v = load_rot(c)
        o_ref[rows, :] += _dot_nt((q * pw_scr[2]).astype(BF16), st.astype(BF16))
        return st * dcb + _dot_tn(v.astype(BF16), (k * pw_scr[3]).astype(BF16)) * same_head

    o_ref[...] = jnp.zeros(o_ref.shape, F32)
    zero = jnp.zeros((LANES, LANES), F32)
    lax.fori_loop(0, n_all, lambda s, sts: (fwd_step(s, sts[0]), bwd_step(s, sts[1])), (zero, zero))


def _ret_call(u, dec, cos_t, sin_t):
    t = u.shape[0]
    sb = _seg()
    col = lambda off: pl.BlockSpec((sb, LANES), lambda b, p: (b, off + p))
    return pl.pallas_call(
        _ret_kernel,
        grid=(BATCH, RET_HEADS // 2),
        in_specs=[pl.BlockSpec((1, 4, LANES), lambda b, p: (p, 0, 0)),
                  pl.BlockSpec((SEQ, LANES), lambda b, p: (0, 0)),
                  pl.BlockSpec((SEQ, LANES), lambda b, p: (0, 0)),
                  col(_C_RQ), col(_C_RK), col(_C_RV)],
        out_specs=pl.BlockSpec((sb, LANES), lambda b, p: (b, p)),
        out_shape=jax.ShapeDtypeStruct((t, RET_W), F32),
        scratch_shapes=[pltpu.VMEM((2, RET_CHUNK, RET_CHUNK), F32),
                        pltpu.VMEM((4, RET_CHUNK, LANES), F32)],
        compiler_params=_params(("arbitrary", "arbitrary")),
        name="retention",
    )(dec, cos_t, sin_t, u, u, u)


def _hg_codes(fwd):
    c_len = HG_BLOCK
    ri = lax.broadcasted_iota(jnp.int32, (c_len, c_len), 0)
    ci = lax.broadcasted_iota(jnp.int32, (c_len, c_len), 1)
    dist = (ri - ci) if fwd else (ci - ri)
    code = jnp.full((c_len, c_len), -1, jnp.int32)
    half, level = c_len // 2, 0
    while half >= HG_SUB:
        sh = half.bit_length() - 1
        same_seg = jnp.right_shift(ri, sh + 1) == jnp.right_shift(ci, sh + 1)
        other_half = jnp.right_shift(ri, sh) != jnp.right_shift(ci, sh)
        code = jnp.where(same_seg & other_half & (dist > 0), level, code)
        half, level = half // 2, level + 1
    sub_sh = HG_SUB.bit_length() - 1
    same_sub = jnp.right_shift(ri, sub_sh) == jnp.right_shift(ci, sub_sh)
    for dl in range(HG_SUB):
        code = jnp.where(same_sub & (dist == dl), 100 + dl, code)
    return code


def _hg_chunk(fwd, q, z, v, lb, st, code, tri):
    c_len = HG_CHUNK

    a = jnp.log(lb)
    b = jnp.log1p(-lb) + _log_sigmoid(z)
    lf = jnp.maximum(a, b) + jnp.log1p(jnp.exp(-jnp.abs(a - b)))
    kin = (1.0 - lb) * _sigmoid(-z)

    g = _dot_exact_lhs(tri, lf) * LOG2_E
    g_end = g[c_len - 1:c_len, :] if fwd else g[0:1, :]
    vb = v.astype(BF16)

    o = _dot_nt((q * jnp.exp2(g)).astype(BF16), st.astype(BF16))

    blk = HG_BLOCK
    lo, hi = slice(0, blk), slice(blk, 2 * blk)
    qrows, krows = (hi, lo) if fwd else (lo, hi)
    bnd = g[blk - 1:blk, :] if fwd else g[blk:blk + 1, :]
    qs = (q[qrows] * jnp.exp2(g[qrows] - bnd)).astype(BF16)
    ks = (kin[krows] * jnp.exp2(bnd - g[krows])).astype(BF16)
    cross = _dot(_dot_nt(qs, ks).astype(BF16), vb[krows])

    parts = []
    for rows in (lo, hi):
        parts.append(_hg_block(fwd, q[rows], kin[rows], g[rows], vb[rows], code))
    parts[1 if fwd else 0] = parts[1 if fwd else 0] + cross
    o = o + jnp.concatenate(parts, axis=0)
    st_new = st * jnp.exp2(g_end) + _dot_tn(vb, (kin * jnp.exp2(g_end - g)).astype(BF16))
    return o, st_new


def _hg_block(fwd, q, kin, g, vb, code):
    n = HG_BLOCK

    scores = jnp.zeros((n, n), F32)
    half, level = n // 2, 0
    while half >= HG_SUB:
        seg = 2 * half
        g3 = g.reshape(n // seg, seg, LANES)
        e = half - 1 if fwd else half
        bsel = jnp.broadcast_to(g3[:, e:e + 1, :], g3.shape).reshape(n, LANES)
        qs = (q * jnp.exp2(g - bsel)).astype(BF16)
        ks = (kin * jnp.exp2(bsel - g)).astype(BF16)
        scores = jnp.where(code == level, _dot_nt(qs, ks), scores)
        half, level = half // 2, level + 1

    def shift(x, dl):
        x3 = x.reshape(n // SUBLANES, SUBLANES, LANES)
        return pltpu.roll(x3, dl if fwd else SUBLANES - dl, 1).reshape(n, LANES)

    for dl in range(HG_SUB):
        w = q * kin if dl == 0 else q * shift(kin, dl) * jnp.exp2(g - shift(g, dl))
        scores = jnp.where(code == 100 + dl, jnp.sum(w, axis=-1, keepdims=True), scores)

    return _dot(scores.astype(BF16), vb)


def _hg_kernel(lbl_ref, q_ref, zf_ref, zb_ref, v_ref, o_ref, *, layer):
    c_len = HG_CHUNK
    n_lat = SEQ // c_len
    n_all = _seg() // c_len

    lbs = []
    for d in range(2):
        x = lbl_ref[d]
        e = jnp.exp(x - jnp.max(x, axis=0, keepdims=True))
        tot = jnp.sum(e, axis=0, keepdims=True)
        part = jnp.zeros_like(tot)
        for i in range(1, layer + 1):
            part = part + e[i:i + 1, :]
        lbs.append(part / tot)

    ri = lax.broadcasted_iota(jnp.int32, (c_len, c_len), 0)
    ci = lax.broadcasted_iota(jnp.int32, (c_len, c_len), 1)
    tri_f = (ri >= ci).astype(BF16)
    tri_b = (ri <= ci).astype(BF16)
    code_f = _hg_codes(True)
    code_b = _hg_codes(False)

    o_ref[...] = jnp.zeros(o_ref.shape, F32)

    def step(s, carry):
        st_f, st_b = carry
        cf = jnp.where(s < n_all - n_lat, n_lat + s, s - (n_all - n_lat))
        cbk = n_all - 1 - s
        rf = pl.ds(pl.multiple_of(cf * c_len, c_len), c_len)
        rk = pl.ds(pl.multiple_of(cbk * c_len, c_len), c_len)
        qf = q_ref[rf, :].astype(F32)
        of, st_f = _hg_chunk(True, qf * _sigmoid(qf), zf_ref[rf, :], v_ref[rf, :].astype(F32), lbs[0], st_f,
                             code_f, tri_f)
        o_ref[rf, :] += of
        qb = q_ref[rk, :].astype(F32)
        ob, st_b = _hg_chunk(False, qb * _sigmoid(qb), zb_ref[rk, :], v_ref[rk, :].astype(F32), lbs[1], st_b,
                             code_b, tri_b)
        o_ref[rk, :] += ob
        return st_f, st_b

    zero = jnp.zeros((LANES, LANES), F32)
    lax.fori_loop(0, n_all, step, (zero, zero))


def _hg_call(u, uz, lb_logits, layer):
    t = u.shape[0]
    sb = _seg()
    col = lambda off: pl.BlockSpec((sb, LANES), lambda b, h: (b, off + h))
    return pl.pallas_call(
        functools.partial(_hg_kernel, layer=layer),
        grid=(BATCH, HG_HEADS),
        in_specs=[pl.BlockSpec((2, DEPTH, LANES), lambda b, h: (0, 0, h)),
                  col(_C_HQ), col(0), col(HG_HEADS), col(_C_HI)],
        out_specs=pl.BlockSpec((sb, LANES), lambda b, h: (b, h)),
        out_shape=jax.ShapeDtypeStruct((t, HG_W), F32),
        compiler_params=_params(("arbitrary", "arbitrary")),
        name="hgrn2",
    )(lb_logits, u, uz, uz, u)


def _lru_kernel(x_ref, wg_ref, bg_ref, lam_ref, cw_ref, cb_ref, y_ref,
                xp_scr, af_scr, bf_scr, ab_scr, bb_scr, hb_scr):
    sb = _seg()
    c_len = LRU_CHUNK
    halo = SUBLANES
    n_tiles = sb // SUBLANES
    n_lat_t = SEQ // SUBLANES

    xp_scr[pl.ds(0, halo), :] = jnp.zeros((halo, LANES), F32)
    xp_scr[pl.ds(halo + sb, halo), :] = jnp.zeros((halo, LANES), F32)
    xp_scr[pl.ds(halo, sb), :] = x_ref[...].astype(F32)

    lam = lam_ref[0]
    sp = jnp.maximum(-lam, 0.0) + jnp.log1p(jnp.exp(-jnp.abs(lam)))
    cw = cw_ref[...]
    wg = wg_ref[0]
    bg = bg_ref[0]
    r8 = lax.broadcasted_iota(jnp.int32, (c_len, 1), 0)
    rm8 = r8 % SUBLANES

    def tile_shift(x, s):
        x3 = x.reshape(c_len // SUBLANES, SUBLANES, LANES)
        return pltpu.roll(x3, s, 1).reshape(c_len, LANES)

    def coeff_chunk(c, _):
        base = pl.multiple_of(c * c_len, c_len)
        w = xp_scr[pl.ds(base, c_len + 2 * halo), :]
        grow = base + r8
        in_lat = grow < SEQ
        pos = jnp.where(in_lat, grow, grow - SEQ)
        seglen = jnp.where(in_lat, SEQ, CTX_LEN)
        wlen = c_len + 2 * halo
        xm2 = jnp.where(pos >= 2, pltpu.roll(w, 2, 0)[halo:halo + c_len], 0.0)
        xm1 = jnp.where(pos >= 1, pltpu.roll(w, 1, 0)[halo:halo + c_len], 0.0)
        x0 = w[halo:halo + c_len]
        xp1 = jnp.where(pos + 1 < seglen, pltpu.roll(w, wlen - 1, 0)[halo:halo + c_len], 0.0)
        xc = xm2 * cw[0:1] + xm1 * cw[1:2] + x0 * cw[2:3] + xp1 * cw[3:4] + cb_ref[...]
        gates = _sigmoid(_dot(xc.astype(BF16), wg) + bg)
        rows = pl.ds(base, c_len)
        for d, (a_scr, b_scr) in enumerate(((af_scr, bf_scr), (ab_scr, bb_scr))):
            rg = gates[:, (2 * d) * LANES:(2 * d + 1) * LANES]
            ig = gates[:, (2 * d + 1) * LANES:(2 * d + 2) * LANES]
            log_a = -LRU_C * rg * sp[d:d + 1]
            a = jnp.exp(log_a)
            th = jnp.tanh(log_a)
            bv = jnp.sqrt(-2.0 * th / (1.0 - th)) * (ig * xc)
            for sft in (1, 2, 4):
                if d == 0:
                    a_s = tile_shift(a, sft)
                    b_s = tile_shift(bv, sft)
                    ok = rm8 >= sft
                else:
                    a_s = tile_shift(a, SUBLANES - sft)
                    b_s = tile_shift(bv, SUBLANES - sft)
                    ok = rm8 < SUBLANES - sft
                bv = jnp.where(ok, a * b_s + bv, bv)
                a = jnp.where(ok, a * a_s, a)
            a_scr[rows, :] = a
            b_scr[rows, :] = bv
        return 0

    lax.fori_loop(0, sb // c_len, coeff_chunk, 0)

    def carry_step(s, carry):
        hf, hb = carry
        tf = jnp.where(s < n_tiles - n_lat_t, n_lat_t + s, s - (n_tiles - n_lat_t))
        tb = n_tiles - 1 - s
        rf = pl.ds(pl.multiple_of(tf * SUBLANES, SUBLANES), SUBLANES)
        rk = pl.ds(pl.multiple_of(tb * SUBLANES, SUBLANES), SUBLANES)
        h1 = af_scr[rf, :] * hf + bf_scr[rf, :]
        y_ref[rf, :] = h1
        h2 = ab_scr[rk, :] * hb + bb_scr[rk, :]
        hb_scr[rk, :] = h2
        return (jnp.broadcast_to(h1[SUBLANES - 1:SUBLANES, :], (SUBLANES, LANES)),
                jnp.broadcast_to(h2[0:1, :], (SUBLANES, LANES)))

    zero = jnp.zeros((SUBLANES, LANES), F32)
    lax.fori_loop(0, n_tiles, carry_step, (zero, zero))
    y_ref[...] += hb_scr[...]


def _lru_call(u, wg, bg, lam, cw, cb):
    t = u.shape[0]
    sb = _seg()
    nb = LRU_W // LANES
    return pl.pallas_call(
        _lru_kernel,
        grid=(BATCH, nb),
        in_specs=[pl.BlockSpec((sb, LANES), lambda b, c: (b, _C_LX + c)),
                  pl.BlockSpec((1, LANES, 4 * LANES), lambda b, c: (c, 0, 0)),
                  pl.BlockSpec((1, 1, 4 * LANES), lambda b, c: (c, 0, 0)),
                  pl.BlockSpec((1, 2, LANES), lambda b, c: (c, 0, 0)),
                  pl.BlockSpec((CONV_W, LANES), lambda b, c: (0, c)),
                  pl.BlockSpec((1, LANES), lambda b, c: (0, c))],
        out_specs=pl.BlockSpec((sb, LANES), lambda b, c: (b, c)),
        out_shape=jax.ShapeDtypeStruct((t, LRU_W), F32),
        scratch_shapes=[pltpu.VMEM((sb + 2 * SUBLANES, LANES), F32)] + [pltpu.VMEM((sb, LANES), F32)] * 5,
        compiler_params=_params(("arbitrary", "arbitrary")),
        name="rglru",
    )(u, wg, bg, lam, cw, cb)


def _group_rms(o, group):
    ri = lax.broadcasted_iota(jnp.int32, (LANES, LANES), 0) // group
    ci = lax.broadcasted_iota(jnp.int32, (LANES, LANES), 1) // group
    ones_bd = (ri == ci).astype(BF16)
    outs = []
    for tix in range(o.shape[1] // LANES):
        x = o[:, tix * LANES:(tix + 1) * LANES]
        sq = x * x
        hi = sq.astype(BF16)
        lo = (sq - hi.astype(F32)).astype(BF16)
        ss = _dot(hi, ones_bd) + _dot(lo, ones_bd)
        outs.append(x * lax.rsqrt(ss * (1.0 / group) + NORM_EPS))
    return jnp.concatenate(outs, axis=1)


def _merge_kernel(*refs, n_gate):
    (x_ref, ml_ref, mc_ref, oret_ref, ohg_ref, ylru_ref, rg_ref, hgt_ref, lgt_ref) = refs[:9]
    gate_refs = refs[9:9 + 3 * n_gate]
    (gng_ref, ghg_ref, wr_ref, wh_ref, wl_ref, wo_ref, n2g_ref, rw_ref, rb_ref,
     xo_ref, h2_ref, lg_ref) = refs[9 + 3 * n_gate:]
    tiles_per_b = _seg() // ROW_TILE
    is_ctx = (pl.program_id(0) % tiles_per_b) >= SEQ // ROW_TILE
    ml = ml_ref[0]
    mc = mc_ref[0]

    def gate(i):
        parts = [gate_refs[i * n_gate + j][...].astype(F32) for j in range(n_gate)]
        return _sigmoid(parts[0] if n_gate == 1 else jnp.concatenate(parts, axis=1))

    rg = rg_ref[...].astype(F32)
    ret_in = _group_rms(oret_ref[...], RET_DK) * gng_ref[...] * (rg * _sigmoid(rg))
    ret = _dot(ret_in.astype(BF16), wr_ref[0])
    hg_in = _group_rms(ohg_ref[...], HG_DK) * ghg_ref[...] * _sigmoid(hgt_ref[...].astype(F32))
    hgr = _dot(hg_in.astype(BF16), wh_ref[0])
    lgt = lgt_ref[...].astype(F32)
    gelu = 0.5 * lgt * (1.0 + jnp.tanh(0.7978845608028654 * (lgt + 0.044715 * lgt * lgt * lgt)))
    lru = _dot((ylru_ref[...] * gelu).astype(BF16), wl_ref[0])
    merged = gate(0) * ret + gate(1) * hgr + gate(2) * lru
    mix = _dot(merged.astype(BF16), wo_ref[0])
    g1 = jnp.where(is_ctx, mc[2:3, :], ml[2:3, :])
    xn = x_ref[...] + g1 * mix
    xo_ref[...] = xn
    h2 = _norm_modulate(xn, n2g_ref[...], ml, mc, is_ctx, 3)
    h2_ref[...] = h2.astype(BF16)
    lg_ref[...] = _dot3(h2, rw_ref[...]) + rb_ref[...]


def _merge_call(xs, mod, o_ret, o_hg, y_lru, u, gn_g, hgn_g, wr, wh, wl, wo, n2g, rw, rb, layer):
    t = xs.shape[0]
    r = ROW_TILE
    tiles_per_b = _seg() // r
    n_gate = D_MODEL // 512
    row = lambda w: pl.BlockSpec((r, w), lambda i: (i, 0))
    ucol = lambda blk: pl.BlockSpec((r, 512), lambda i: (i, blk))

    def full(a):
        if a.ndim == 3:
            return pl.BlockSpec((1,) + a.shape[1:], lambda i: (layer, 0, 0))
        return pl.BlockSpec(a.shape, lambda i: (0,) * a.ndim)
    gate_base = _C_GATES * LANES // 512
    in_specs = [row(D_MODEL),
                pl.BlockSpec((1, N_MOD, D_MODEL), lambda i: (i // tiles_per_b, 0, 0)),
                pl.BlockSpec((1, N_MOD, D_MODEL), lambda i: (BATCH, 0, 0)),
                row(RET_W), row(HG_W), row(LRU_W),
                ucol(_C_RG * LANES // 512), ucol(_C_HGT * LANES // 512), ucol(_C_LGT * LANES // 512)]
    in_specs += [ucol(gate_base + j) for j in range(3 * n_gate)]
    consts = [gn_g, hgn_g, wr, wh, wl, wo, n2g, rw, rb]
    in_specs += [full(a) for a in consts]
    return pl.pallas_call(
        functools.partial(_merge_kernel, n_gate=n_gate),
        grid=(t // r,),
        in_specs=in_specs,
        out_specs=[row(D_MODEL), row(D_MODEL), row(LANES)],
        out_shape=[jax.ShapeDtypeStruct((t, D_MODEL), F32), jax.ShapeDtypeStruct((t, D_MODEL), BF16),
                   jax.ShapeDtypeStruct((t, LANES), F32)],
        compiler_params=_params(("arbitrary",)),
        name="merge",
    )(xs, mod, mod, o_ret, o_hg, y_lru, *([u] * (3 + 3 * n_gate)), *consts)


def _route_kernel(lg_ref, oi_ref, ow_ref, lst_ref, meta_ref, cnt_ref, off_scr):
    @pl.when(pl.program_id(0) == 0)
    def _():
        off_scr[...] = jnp.zeros(off_scr.shape, F32)

    r = lg_ref.shape[0]
    lane = lax.broadcasted_iota(jnp.int32, (r, LANES), 1)
    neg = -jnp.inf
    l = jnp.where(lane < N_EXPERTS, lg_ref[...], neg)
    vals, hots = [], []
    lanef = lane.astype(F32)
    for _ in range(TOP_K):
        m = jnp.max(l, axis=-1, keepdims=True)
        idxf = jnp.min(jnp.where(l == m, lanef, float(LANES)), axis=-1, keepdims=True)
        hot = lanef == idxf
        l = jnp.where(hot, neg, l)
        vals.append(m)
        hots.append(hot)
    es = [jnp.exp(v - vals[0]) for v in vals]
    den = es[0]
    for e in es[1:]:
        den = den + e
    chosen = hots[0].astype(F32)
    for hot in hots[1:]:
        chosen = chosen + hot.astype(F32)
    ri = lax.broadcasted_iota(jnp.int32, (r, r), 0)
    ci = lax.broadcasted_iota(jnp.int32, (r, r), 1)
    before = _dot((ri > ci).astype(BF16), chosen.astype(BF16))
    count = jnp.sum(chosen, axis=0, keepdims=True)
    run = jnp.floor((count + (SUBLANES - 1.0)) * (1.0 / SUBLANES)) * SUBLANES
    li = lax.broadcasted_iota(jnp.int32, (LANES, LANES), 0)
    lj = lax.broadcasted_iota(jnp.int32, (LANES, LANES), 1)
    run8 = jnp.broadcast_to(run, (SUBLANES, LANES))
    start = _dot(run8.astype(BF16), (li < lj).astype(BF16))[0:1, :]
    pos = start + before
    oi = jnp.zeros((r, LANES), jnp.int32)
    ow = jnp.zeros((r, LANES), F32)
    lsf = jnp.zeros((r, LANES), F32)
    for k in range(TOP_K):
        row_k = jnp.sum(jnp.where(hots[k], pos, 0.0), axis=-1, keepdims=True)
        oi = jnp.where(lane == k, row_k.astype(jnp.int32), oi)
        lsf = jnp.where(lane == k, row_k, lsf)
        ow = jnp.where(lane == k, es[k] / den, ow)
    oi_ref[...] = oi
    ow_ref[...] = ow
    lst_ref[0] = jnp.transpose(lsf)[0:SUBLANES, :]
    mrow = lax.broadcasted_iota(jnp.int32, (SUBLANES, LANES), 0)
    meta_ref[0] = jnp.where(mrow == 0, run8, jnp.where(mrow == 1, jnp.broadcast_to(start, (SUBLANES, LANES)),
                                                       off_scr[...]))
    off_scr[...] = off_scr[...] + run
    cnt_ref[...] = off_scr[...]


def _route_call(logits):
    t = logits.shape[0]
    r = ROW_TILE
    row = pl.BlockSpec((r, LANES), lambda i: (i, 0))
    return pl.pallas_call(
        _route_kernel,
        grid=(t // r,),
        in_specs=[row],
        out_specs=[row, row, pl.BlockSpec((1, SUBLANES, r), lambda i: (i, 0, 0)),
                   pl.BlockSpec((1, SUBLANES, LANES), lambda i: (i, 0, 0)),
                   pl.BlockSpec((SUBLANES, LANES), lambda i: (0, 0))],
        out_shape=[jax.ShapeDtypeStruct((t, LANES), jnp.int32), jax.ShapeDtypeStruct((t, LANES), F32),
                   jax.ShapeDtypeStruct((t // r, SUBLANES, r), F32),
                   jax.ShapeDtypeStruct((t // r, SUBLANES, LANES), F32),
                   jax.ShapeDtypeStruct((SUBLANES, LANES), F32)],
        scratch_shapes=[pltpu.VMEM((SUBLANES, LANES), F32)],
        compiler_params=_params(("arbitrary",)),
        name="route",
    )(logits)


def _pack_pairs(x):
    m = x.shape[1] // 2
    bits = pltpu.bitcast(x, jnp.uint32)
    return bits[:, :m] | (bits[:, m:] >> 16)


def _unpack_pairs(w):
    hi = pltpu.bitcast(w & jnp.uint32(0xFFFF0000), F32).astype(BF16)
    lo = pltpu.bitcast(w << 16, F32).astype(BF16)
    return hi, lo


def _local_rows():
    rows = ROW_TILE * TOP_K + N_EXPERTS * (SUBLANES - 1)
    return -(-rows // ROW_TILE) * ROW_TILE


def _run_copies(meta_ref, make_copy, do):
    def per_expert(e, _):
        first_slot = meta_ref[0, 0, e]
        first_row = meta_ref[0, 0, N_EXPERTS + e]
        pieces = meta_ref[0, 0, 2 * N_EXPERTS + e]

        def per_piece(j, _):
            do(make_copy(pl.multiple_of(first_row + j * SUBLANES, SUBLANES),
                         pl.multiple_of(first_slot + j * SUBLANES, SUBLANES)))
            return 0

        lax.fori_loop(0, pieces, per_piece, 0)
        return 0

    lax.fori_loop(0, N_EXPERTS, per_expert, 0)


def _dispatch_kernel(meta_ref, lst_ref, h_ref, xs_in_ref, xs_ref, perm_scr, sem):
    del xs_in_ref
    r = h_ref.shape[0]
    ls = lst_ref[0].astype(jnp.int32)
    hb = h_ref[...]
    for s in range(perm_scr.shape[0] // r):
        rows = s * r + lax.broadcasted_iota(jnp.int32, (r, r), 0)
        sel = rows == ls[0:1, :]
        for k in range(1, TOP_K):
            sel = jnp.logical_or(sel, rows == ls[k:k + 1, :])
        perm_scr[pl.ds(s * r, r), :] = _pack_pairs(_dot(jnp.where(sel, 1.0, 0.0).astype(BF16), hb))

    def make_copy(row, slot):
        return pltpu.make_async_copy(perm_scr.at[pl.ds(row, SUBLANES)], xs_ref.at[pl.ds(slot, SUBLANES)], sem)

    _run_copies(meta_ref, make_copy, lambda c: c.start())
    _run_copies(meta_ref, make_copy, lambda c: c.wait())


def _dispatch_call(h2, meta, lst, slots):
    t = h2.shape[0]
    r = ROW_TILE
    return pl.pallas_call(
        _dispatch_kernel,
        grid=(t // r,),
        in_specs=[pl.BlockSpec((1, 1, LANES), lambda i: (i, 0, 0), memory_space=pltpu.SMEM),
                  pl.BlockSpec((1, SUBLANES, r), lambda i: (i, 0, 0)),
                  pl.BlockSpec((r, D_MODEL), lambda i: (i, 0)),
                  pl.BlockSpec(memory_space=pl.ANY)],
        out_specs=pl.BlockSpec(memory_space=pl.ANY),
        out_shape=jax.ShapeDtypeStruct(slots.shape, jnp.uint32),
        scratch_shapes=[pltpu.VMEM((_local_rows(), D_MODEL // 2), jnp.uint32), pltpu.SemaphoreType.DMA(())],
        input_output_aliases={3: 0},
        compiler_params=_params(("arbitrary",)),
        name="dispatch",
    )(meta, lst, h2, slots)


def _expert_kernel(bexp_ref, nv_ref, x_ref, wgu_ref, bgu_ref, wd_ref, bd_ref, y_ref, wgu_scr, wd_scr):
    i = pl.program_id(0)
    f = D_MODEL
    prev = bexp_ref[jnp.maximum(i - 1, 0)]
    new_expert = jnp.logical_or(i == 0, bexp_ref[i] != prev)

    @pl.when(jnp.logical_and(i < nv_ref[0], new_expert))
    def _():
        slab = 256
        for s in range(D_MODEL // slab):
            wgu_scr[pl.ds(s * slab, slab), :] = wgu_ref[0, 0, pl.ds(s * slab, slab), :].astype(BF16)
            wd_scr[pl.ds(s * slab, slab), :] = wd_ref[0, 0, pl.ds(s * slab, slab), :].astype(BF16)

    @pl.when(i < nv_ref[0])
    def _():
        x_hi, x_lo = _unpack_pairs(x_ref[...])
        half = D_MODEL // 2
        gu = _dot(x_hi, wgu_scr[:half, :]) + _dot(x_lo, wgu_scr[half:, :]) + bgu_ref[0, 0]
        gate = jnp.minimum(gu[:, :f], SWIGLU_LIMIT)
        up = jnp.clip(gu[:, f:], -SWIGLU_LIMIT, SWIGLU_LIMIT)
        glu = gate * _sigmoid(SWIGLU_ALPHA * gate)
        y = _dot(((up + 1.0) * glu).astype(BF16), wd_scr[...]) + bd_ref[0, 0]
        y_ref[...] = _pack_pairs(y.astype(BF16).astype(F32))

    @pl.when(i >= nv_ref[0])
    def _():
        y_ref[...] = jnp.zeros(y_ref.shape, jnp.uint32)


def _expert_call(slots, bexp, nvalid, w_gu, b_gu, w_down, b_down, layer):
    s = slots.shape[0]
    nb = s // MOE_BLOCK
    f = D_MODEL
    blk = lambda i, be, nv: (jnp.minimum(i, nv[0] - 1), 0)
    exp3 = lambda i, be, nv: (layer, be[jnp.minimum(i, nv[0] - 1)], 0, 0)
    grid_spec = pltpu.PrefetchScalarGridSpec(
        num_scalar_prefetch=2,
        grid=(nb,),
        in_specs=[pl.BlockSpec((MOE_BLOCK, D_MODEL // 2), blk),
                  pl.BlockSpec((1, 1, D_MODEL, 2 * f), exp3),
                  pl.BlockSpec((1, 1, 1, 2 * f), exp3),
                  pl.BlockSpec((1, 1, f, D_MODEL), exp3),
                  pl.BlockSpec((1, 1, 1, D_MODEL), exp3)],
        out_specs=pl.BlockSpec((MOE_BLOCK, D_MODEL // 2), lambda i, be, nv: (i, 0)),
        scratch_shapes=[pltpu.VMEM((D_MODEL, 2 * f), BF16), pltpu.VMEM((f, D_MODEL), BF16)],
    )
    return pl.pallas_call(
        _expert_kernel,
        grid_spec=grid_spec,
        out_shape=jax.ShapeDtypeStruct((s, D_MODEL // 2), jnp.uint32),
        compiler_params=_params(("arbitrary",)),
        name="experts",
    )(bexp, nvalid, slots, w_gu, b_gu.reshape(DEPTH, N_EXPERTS, 1, 2 * f), w_down,
      b_down.reshape(DEPTH, N_EXPERTS, 1, D_MODEL))


def _combine_kernel(meta_ref, x_ref, ml_ref, mc_ref, oi_ref, w_ref, fg_ref, y_ref, o_ref, buf, sem, *, last):
    r = x_ref.shape[0]

    @pl.when(jnp.logical_and(pl.program_id(0) == 0, pl.program_id(1) == 0))
    def _():
        buf[...] = jnp.zeros(buf.shape, jnp.uint32)

    def make_copy(row, slot):
        return pltpu.make_async_copy(y_ref.at[pl.ds(slot, SUBLANES)], buf.at[pl.ds(row, SUBLANES)], sem)

    _run_copies(meta_ref, make_copy, lambda c: c.start())
    _run_copies(meta_ref, make_copy, lambda c: c.wait())

    if last:
        is_ctx = False
    else:
        is_ctx = (pl.program_id(1)) >= SEQ // ROW_TILE
    w = w_ref[...]
    oi = oi_ref[...]
    ff_hi = jnp.zeros((r, D_MODEL // 2), F32)
    ff_lo = jnp.zeros((r, D_MODEL // 2), F32)
    for s in range(buf.shape[0] // r):
        cols = s * r + lax.broadcasted_iota(jnp.int32, (r, r), 1)
        wm = jnp.zeros((r, r), F32)
        for k in range(TOP_K):
            wm = jnp.where(cols == oi[:, k:k + 1], w[:, k:k + 1], wm)
        y_hi, y_lo = _unpack_pairs(buf[pl.ds(s * r, r), :])
        ff_hi = ff_hi + _dot(wm.astype(BF16), y_hi)
        ff_lo = ff_lo + _dot(wm.astype(BF16), y_lo)
    ff = jnp.concatenate([ff_hi, ff_lo], axis=1)
    g2 = jnp.where(is_ctx, mc_ref[0][5:6, :], ml_ref[0][5:6, :])
    xn = x_ref[...] + g2 * ff
    if last:
        ms = jnp.mean(xn * xn, axis=-1, keepdims=True)
        xn = xn * lax.rsqrt(ms + NORM_EPS) * fg_ref[...]
    o_ref[...] = xn


def _combine_call(xs, mod, meta, rows, wts, y_slots, final_g, last):
    t = xs.shape[0]
    r = ROW_TILE
    tiles_per_b = _seg() // r
    n_t = SEQ // r if last else tiles_per_b
    tile = lambda b, j: b * tiles_per_b + j
    out_rows = BATCH * SEQ if last else t
    return pl.pallas_call(
        functools.partial(_combine_kernel, last=last),
        grid=(BATCH, n_t),
        in_specs=[pl.BlockSpec((1, 1, LANES), lambda b, j: (tile(b, j), 0, 0), memory_space=pltpu.SMEM),
                  pl.BlockSpec((r, D_MODEL), lambda b, j: (tile(b, j), 0)),
                  pl.BlockSpec((1, N_MOD, D_MODEL), lambda b, j: (b, 0, 0)),
                  pl.BlockSpec((1, N_MOD, D_MODEL), lambda b, j: (BATCH, 0, 0)),
                  pl.BlockSpec((r, LANES), lambda b, j: (tile(b, j), 0)),
                  pl.BlockSpec((r, LANES), lambda b, j: (tile(b, j), 0)),
                  pl.BlockSpec((1, D_MODEL), lambda b, j: (0, 0)),
                  pl.BlockSpec(memory_space=pl.ANY)],
        out_specs=pl.BlockSpec((r, D_MODEL), lambda b, j: (b * n_t + j, 0)),
        out_shape=jax.ShapeDtypeStruct((out_rows, D_MODEL), F32),
        scratch_shapes=[pltpu.VMEM((_local_rows(), D_MODEL // 2), jnp.uint32), pltpu.SemaphoreType.DMA(())],
        compiler_params=_params(("arbitrary", "arbitrary")),
        name="combine",
    )(meta, xs, mod, mod, rows, wts, final_g, y_slots)


def _rotary_tables():
    rows = SEQ // GRID_W
    row = jnp.repeat(jnp.arange(rows, dtype=F32), GRID_W)
    colp = jnp.tile(jnp.arange(GRID_W, dtype=F32), rows)
    n_freq = RET_DK // 4
    inv = ROPE_BASE ** (-jnp.arange(n_freq, dtype=F32) / n_freq)
    ang = jnp.concatenate([row[:, None] * inv, colp[:, None] * inv], axis=-1)
    cos, sin = jnp.cos(ang), jnp.sin(ang)
    reps = LANES // (RET_DK // 2)
    cos_t = jnp.tile(cos, (1, reps))
    sign = jnp.where((jnp.arange(LANES) % RET_DK) < RET_DK // 2, -1.0, 1.0).astype(F32)
    sin_t = jnp.tile(sin, (1, reps)) * sign
    return cos_t, sin_t


def _lru_gate_weights(wa, ba, wx, bx):
    nb = LRU_W // LANES
    per = LANES // LRU_BD
    eye = jnp.eye(per, dtype=F32)

    def embed(w):
        w = w.reshape(nb, per, LRU_BD, LRU_BD)
        return jnp.einsum('cpij,pq->cpiqj', w, eye).reshape(nb, LANES, LANES)

    wg = jnp.concatenate([embed(wa[0]), embed(wx[0]), embed(wa[1]), embed(wx[1])], axis=-1)
    bias = jnp.concatenate([ba[0].reshape(nb, 1, LANES), bx[0].reshape(nb, 1, LANES),
                            ba[1].reshape(nb, 1, LANES), bx[1].reshape(nb, 1, LANES)], axis=-1)
    return wg.astype(BF16), bias


def kernel(x, c, ctx, c_ctx, mod_w, mod_b, norm1_g, norm2_g, final_g, w_in, ret_decay, ret_gn_g, w_ret_o,
           hgrn_lb_logits, hgrn_gn_g, w_hgrn_o, lru_conv_w, lru_conv_b, lru_wa, lru_ba, lru_wx, lru_bx,
           lru_lambda, w_lru_o, w_out, router_w, router_b, exp_w_gu, exp_b_gu, exp_w_down, exp_b_down):
    assert CTX_LEN == RET_CHUNK and SEQ % RET_CHUNK == 0
    sb = _seg()
    t = BATCH * sb
    xs = jnp.concatenate([x, ctx], axis=1).reshape(t, D_MODEL)

    cvec = jnp.concatenate([c, c_ctx[None, :], jnp.zeros((SUBLANES - BATCH - 1, D_MODEL), F32)], axis=0)
    mod_all = _mod_call(cvec, mod_w, mod_b).reshape(DEPTH, SUBLANES, N_MOD, D_MODEL)

    cos_t, sin_t = _rotary_tables()
    lb_logits = jnp.transpose(hgrn_lb_logits, (1, 0, 2))
    n_slots = t * TOP_K + (t // ROW_TILE) * N_EXPERTS * (SUBLANES - 1) + N_EXPERTS * (MOE_BLOCK - 1)
    n_blocks = -(-n_slots // MOE_BLOCK)
    n_slots = n_blocks * MOE_BLOCK
    slots = jnp.zeros((n_slots, D_MODEL // 2), jnp.uint32)
    rw_pad = jnp.pad(router_w, ((0, 0), (0, 0), (0, LANES - N_EXPERTS)))
    rb_pad = jnp.pad(router_b, ((0, 0), (0, LANES - N_EXPERTS)))
    w_in_b, w_ret_b, w_hg_b, w_lru_b, w_out_b = (w.astype(BF16) for w in (w_in, w_ret_o, w_hgrn_o, w_lru_o, w_out))
    block_first_slot = jnp.arange(n_blocks, dtype=jnp.int32) * MOE_BLOCK

    out = None
    for l in range(DEPTH):
        last = l == DEPTH - 1
        mod = mod_all[l]
        u, uz = _inproj_call(xs, mod, norm1_g[l][None, :], w_in_b, l)

        dec = jnp.broadcast_to(ret_decay[l].reshape(2, RET_HEADS // 2, 2, 1).transpose(1, 0, 2, 3),
                               (RET_HEADS // 2, 2, 2, LANES)).reshape(RET_HEADS // 2, 4, LANES)
        o_ret = _ret_call(u, dec, cos_t, sin_t)
        o_hg = _hg_call(u, uz, lb_logits, l)
        wg, bg = _lru_gate_weights(lru_wa[l], lru_ba[l], lru_wx[l], lru_bx[l])
        lam = lru_lambda[l].reshape(2, LRU_W // LANES, LANES).transpose(1, 0, 2)
        y_lru = _lru_call(u, wg, bg, lam, lru_conv_w[l], lru_conv_b[l][None, :])

        xs, h2, logits = _merge_call(
            xs, mod, o_ret, o_hg, y_lru, u, ret_gn_g[l][None, :], hgrn_gn_g[l][None, :],
            w_ret_b, w_hg_b, w_lru_b, w_out_b, norm2_g[l][None, :], rw_pad[l], rb_pad[l][None, :], l)

        rows, wts, lst, tile_meta, cnt = _route_call(logits)
        run = tile_meta[:, 0, :N_EXPERTS].astype(jnp.int32)
        first_row = tile_meta[:, 1, :N_EXPERTS].astype(jnp.int32)
        earlier = tile_meta[:, 2, :N_EXPERTS].astype(jnp.int32)
        counts = cnt[0, :N_EXPERTS].astype(jnp.int32)
        padded = (counts + MOE_BLOCK - 1) // MOE_BLOCK * MOE_BLOCK
        pad_end = jnp.cumsum(padded)
        first_slot = (pad_end - padded)[None, :] + earlier
        meta = jnp.concatenate([first_slot, first_row, run // SUBLANES, jnp.zeros_like(run)], axis=1)
        meta = meta.reshape(t // ROW_TILE, 1, LANES)
        bexp = jnp.minimum(jnp.sum((pad_end[None, :] <= block_first_slot[:, None]).astype(jnp.int32), axis=1),
                           N_EXPERTS - 1)
        nvalid = (pad_end[-1:] // MOE_BLOCK).astype(jnp.int32)

        slots = _dispatch_call(h2, meta, lst, slots)
        y_slots = _expert_call(slots, bexp, nvalid, exp_w_gu, exp_b_gu, exp_w_down, exp_b_down, l)
        res = _combine_call(xs, mod, meta, rows, wts, y_slots, final_g[None, :], last)
        if last:
            out = res.reshape(BATCH, SEQ, D_MODEL)
        else:
            xs = res
    return out
```

```python
import functools

import jax
import jax.numpy as jnp
from jax import lax
from jax.experimental import pallas as pl
from jax.experimental.pallas import tpu as pltpu

F32 = jnp.float32
BF16 = jnp.bfloat16

D_MODEL = 1024
BATCH = 4
SEQ = 4096
DEPTH = 4
GRID_W = 64
CTX_LEN = 256
N_MOD = 6
NORM_EPS = 1e-6
RET_HEADS = 8
RET_DK = 64
RET_W = 512
ROPE_BASE = 10000.0
HG_HEADS = 4
HG_DK = 128
HG_W = 512
LRU_W = 512
LRU_BLOCKS = 8
LRU_BD = 64
LRU_C = 8.0
CONV_W = 4
N_EXPERTS = 32
TOP_K = 4
SWIGLU_ALPHA = 1.702
SWIGLU_LIMIT = 7.0

LANES = 128
SUBLANES = 8
RET_CHUNK = 256
HG_CHUNK = 256
HG_BLOCK = 128
HG_SUB = 4
LOG2_E = 1.4426950408889634
LRU_CHUNK = 256
MOE_BLOCK = 256
ROW_TILE = 256
ROUTE_TILE = 512
MOD_TN = 1536
VMEM_LIMIT = 56 * 1024 * 1024

_C_RQ, _C_RK, _C_RV, _C_RG = 0, 4, 8, 12
_C_HQ, _C_HFF, _C_HFB, _C_HI, _C_HGT = 16, 20, 24, 28, 32
_C_LX, _C_LGT = 36, 40
_C_GATES = 44


def _seg():
    return SEQ + CTX_LEN


def _in_cols():
    return _C_GATES * LANES + 3 * D_MODEL


def _dot(a, b):
    return jnp.dot(a, b, preferred_element_type=F32)


def _dot_nt(a, b):
    return lax.dot_general(a, b, (((1,), (1,)), ((), ())), preferred_element_type=F32)


def _dot_tn(a, b):
    return lax.dot_general(a, b, (((0,), (0,)), ((), ())), preferred_element_type=F32)


def _split3(x):
    hi = x.astype(BF16)
    r = x - hi.astype(F32)
    mid = r.astype(BF16)
    lo = (r - mid.astype(F32)).astype(BF16)
    return hi, mid, lo


def _dot_exact_lhs(a, x):
    hi, mid, lo = _split3(x)
    return _dot(a, hi) + _dot(a, mid) + _dot(a, lo)


def _dot_exact_rhs(x, a):
    hi, mid, lo = _split3(x)
    return _dot(hi, a) + _dot(mid, a) + _dot(lo, a)


def _dot3(a, b):
    ah, am, _ = _split3(a)
    bh, bm, _ = _split3(b)
    return _dot(ah, bh) + (_dot(ah, bm) + _dot(am, bh))


def _sigmoid(x):
    return 0.5 + 0.5 * jnp.tanh(0.5 * x)


def _log_sigmoid(x):
    return jnp.minimum(x, 0.0) - jnp.log1p(jnp.exp(-jnp.abs(x)))


def _params(sem):
    return pltpu.CompilerParams(dimension_semantics=sem, vmem_limit_bytes=VMEM_LIMIT)


def _mod_kernel(c_ref, w_ref, b_ref, o_ref):
    c = c_ref[...]
    s = c * _sigmoid(c)
    o_ref[0] = _dot3(s, w_ref[0]) + b_ref[0]


def _mod_call(cvec, mod_w, mod_b):
    n6 = N_MOD * D_MODEL
    tn = MOD_TN
    return pl.pallas_call(
        _mod_kernel,
        grid=(DEPTH, n6 // tn),
        in_specs=[pl.BlockSpec((SUBLANES, D_MODEL), lambda l, j: (0, 0)),
                  pl.BlockSpec((1, D_MODEL, tn), lambda l, j: (l, 0, j)),
                  pl.BlockSpec((1, 1, tn), lambda l, j: (l, 0, j))],
        out_specs=pl.BlockSpec((1, SUBLANES, tn), lambda l, j: (l, 0, j)),
        out_shape=jax.ShapeDtypeStruct((DEPTH, SUBLANES, n6), F32),
        compiler_params=_params(("arbitrary", "arbitrary")),
        name="modulation",
    )(cvec, mod_w, mod_b.reshape(DEPTH, 1, n6))


def _norm_modulate(x, g, mod_l, mod_c, is_ctx, first):
    ms = jnp.mean(x * x, axis=-1, keepdims=True)
    y = x * lax.rsqrt(ms + NORM_EPS) * g
    sh = jnp.where(is_ctx, mod_c[first:first + 1, :], mod_l[first:first + 1, :])
    sc = jnp.where(is_ctx, mod_c[first + 1:first + 2, :], mod_l[first + 1:first + 2, :])
    return y * (1.0 + sc) + sh


def _inproj_kernel(x_ref, ml_ref, mc_ref, g_ref, w_ref, o_ref, oz_ref, h_scr, *, tm, z_first, z_count):
    j = pl.program_id(1)

    @pl.when(j == 0)
    def _():
        tiles_per_b = _seg() // tm
        row0 = (pl.program_id(0) % tiles_per_b) * tm
        rows = row0 + lax.broadcasted_iota(jnp.int32, (tm, 1), 0)
        h = _norm_modulate(x_ref[...], g_ref[...], ml_ref[0], mc_ref[0], rows >= SEQ, 0)
        h_scr[...] = h.astype(BF16)

    acc = _dot(h_scr[...], w_ref[0])
    o_ref[...] = acc.astype(BF16)

    @pl.when(jnp.logical_and(j >= z_first, j < z_first + z_count))
    def _():
        oz_ref[...] = acc


def _inproj_call(xs, mod, g, w, layer):
    t = xs.shape[0]
    tm = _seg() // 2
    ncol = _in_cols()
    tn = HG_W
    tiles_per_b = _seg() // tm
    z_first = _C_HFF * LANES // tn
    z_count = 2
    return pl.pallas_call(
        functools.partial(_inproj_kernel, tm=tm, z_first=z_first, z_count=z_count),
        grid=(t // tm, ncol // tn),
        in_specs=[pl.BlockSpec((tm, D_MODEL), lambda i, j: (i, 0)),
                  pl.BlockSpec((1, N_MOD, D_MODEL), lambda i, j: (i // tiles_per_b, 0, 0)),
                  pl.BlockSpec((1, N_MOD, D_MODEL), lambda i, j: (BATCH, 0, 0)),
                  pl.BlockSpec((1, D_MODEL), lambda i, j: (0, 0)),
                  pl.BlockSpec((1, D_MODEL, tn), lambda i, j: (layer, 0, j))],
        out_specs=[pl.BlockSpec((tm, tn), lambda i, j: (i, j)),
                   pl.BlockSpec((tm, tn), lambda i, j: (i, jnp.clip(j - z_first, 0, z_count - 1)))],
        out_shape=[jax.ShapeDtypeStruct((t, ncol), BF16), jax.ShapeDtypeStruct((t, z_count * tn), F32)],
        scratch_shapes=[pltpu.VMEM((tm, D_MODEL), BF16)],
        compiler_params=_params(("arbitrary", "arbitrary")),
        name="in_projection",
    )(xs, mod, mod, g, w)


def _ret_kernel(dec_ref, cos_ref, sin_ref, q_ref, k_ref, v_ref, o_ref, mask_scr, pw_scr):
    c_len = RET_CHUNK
    n_lat = SEQ // c_len
    n_all = _seg() // c_len
    lane = lax.broadcasted_iota(jnp.int32, (1, LANES), 1)
    head0 = lane < RET_DK
    first_half = (lane % RET_DK) < (RET_DK // 2)

    lg = _log_sigmoid(dec_ref[0])
    dvf = jnp.where(head0, lg[0:1], lg[1:2])
    dvb = jnp.where(head0, lg[2:3], lg[3:4])
    r = lax.broadcasted_iota(jnp.int32, (c_len, 1), 0).astype(F32)
    pw_scr[0] = jnp.exp((r + 1.0) * dvf)
    pw_scr[1] = jnp.exp((c_len - 1.0 - r) * dvf)
    pw_scr[2] = jnp.exp((c_len - r) * dvb)
    pw_scr[3] = jnp.exp(r * dvb)
    dcf = jnp.exp(c_len * dvf)
    dcb = jnp.exp(c_len * dvb)
    ri = lax.broadcasted_iota(jnp.int32, (c_len, c_len), 0)
    ci = lax.broadcasted_iota(jnp.int32, (c_len, c_len), 1)
    dm = (ri - ci).astype(F32)
    for h in range(2):
        lgf = jnp.concatenate([lg[h:h + 1]] * (c_len // LANES), axis=1)
        lgb = jnp.concatenate([lg[2 + h:3 + h]] * (c_len // LANES), axis=1)
        mask_scr[h] = jnp.exp(jnp.where(dm >= 0, dm * lgf, -dm * lgb))
    rb = lax.broadcasted_iota(jnp.int32, (LANES, LANES), 0) < RET_DK
    cb = lax.broadcasted_iota(jnp.int32, (LANES, LANES), 1) < RET_DK
    same_head = (rb == cb).astype(F32)

    def load(c):
        rows = pl.ds(pl.multiple_of(c * c_len, c_len), c_len)
        q = q_ref[rows, :].astype(F32)
        k = k_ref[rows, :].astype(F32) * (RET_DK ** -0.5)
        v = v_ref[rows, :].astype(F32)
        return rows, q, k, v

    def rotary(t, c):
        rows = pl.ds(pl.multiple_of(c * c_len, c_len), c_len)
        partner = jnp.where(first_half, pltpu.roll(t, LANES - RET_DK // 2, 1), pltpu.roll(t, RET_DK // 2, 1))
        return t * cos_ref[rows, :] + partner * sin_ref[rows, :]

    def load_rot(c):
        rows, q, k, v = load(c)
        is_lat = c < n_lat
        cc = jnp.minimum(c, n_lat - 1)
        q = jnp.where(is_lat, rotary(q, cc), q)
        k = jnp.where(is_lat, rotary(k, cc), k)
        return rows, q, k, v

    def fwd_step(s, st):
        c = jnp.where(s < n_all - n_lat, n_lat + s, s - (n_all - n_lat))
        rows, q, k, v = load_rot(c)
        kb = k.astype(BF16)
        vb = v.astype(BF16)
        o0 = _dot((_dot_nt(jnp.where(head0, q, 0.0).astype(BF16), kb) * mask_scr[0]).astype(BF16), vb)
        o1 = _dot((_dot_nt(jnp.where(head0, 0.0, q).astype(BF16), kb) * mask_scr[1]).astype(BF16), vb)
        o = jnp.where(head0, o0, o1)
        o = o + _dot_nt((q * pw_scr[0]).astype(BF16), st.astype(BF16))
        o_ref[rows, :] += o
        return st * dcf + _dot_tn(vb, (k * pw_scr[1]).astype(BF16)) * same_head

    def bwd_step(s, st):
        c = n_all - 1 - s
        rows, q, k, v = load_rot(c)
        o_ref[rows, :] += _dot_nt((q * pw_scr[2]).astype(BF16), st.astype(BF16))
        return st * dcb + _dot_tn(v.astype(BF16), (k * pw_scr[3]).astype(BF16)) * same_head

    o_ref[...] = jnp.zeros(o_ref.shape, F32)
    zero = jnp.zeros((LANES, LANES), F32)
    lax.fori_loop(0, n_all, lambda s, sts: (fwd_step(s, sts[0]), bwd_step(s, sts[1])), (zero, zero))


def _ret_call(u, dec, cos_t, sin_t):
    t = u.shape[0]
    sb = _seg()
    col = lambda off: pl.BlockSpec((sb, LANES), lambda b, p: (b, off + p))
    return pl.pallas_call(
        _ret_kernel,
        grid=(BATCH, RET_HEADS // 2),
        in_specs=[pl.BlockSpec((1, 4, LANES), lambda b, p: (p, 0, 0)),
                  pl.BlockSpec((SEQ, LANES), lambda b, p: (0, 0)),
                  pl.BlockSpec((SEQ, LANES), lambda b, p: (0, 0)),
                  col(_C_RQ), col(_C_RK), col(_C_RV)],
        out_specs=pl.BlockSpec((sb, LANES), lambda b, p: (b, p)),
        out_shape=jax.ShapeDtypeStruct((t, RET_W), F32),
        scratch_shapes=[pltpu.VMEM((2, RET_CHUNK, RET_CHUNK), F32),
                        pltpu.VMEM((4, RET_CHUNK, LANES), F32)],
        compiler_params=_params(("arbitrary", "arbitrary")),
        name="retention",
    )(dec, cos_t, sin_t, u, u, u)


def _hg_codes(fwd):
    c_len = HG_BLOCK
    ri = lax.broadcasted_iota(jnp.int32, (c_len, c_len), 0)
    ci = lax.broadcasted_iota(jnp.int32, (c_len, c_len), 1)
    dist = (ri - ci) if fwd else (ci - ri)
    code = jnp.full((c_len, c_len), -1, jnp.int32)
    half, level = c_len // 2, 0
    while half >= HG_SUB:
        sh = half.bit_length() - 1
        same_seg = jnp.right_shift(ri, sh + 1) == jnp.right_shift(ci, sh + 1)
        other_half = jnp.right_shift(ri, sh) != jnp.right_shift(ci, sh)
        code = jnp.where(same_seg & other_half & (dist > 0), level, code)
        half, level = half // 2, level + 1
    sub_sh = HG_SUB.bit_length() - 1
    same_sub = jnp.right_shift(ri, sub_sh) == jnp.right_shift(ci, sub_sh)
    for dl in range(HG_SUB):
        code = jnp.where(same_sub & (dist == dl), 100 + dl, code)
    return code


def _hg_chunk(fwd, q, z, v, lb, st, code, tri):
    c_len = HG_CHUNK

    a = jnp.log(lb)
    b = jnp.log1p(-lb) + _log_sigmoid(z)
    lf = jnp.maximum(a, b) + jnp.log1p(jnp.exp(-jnp.abs(a - b)))
    kin = (1.0 - lb) * _sigmoid(-z)

    g = _dot_exact_lhs(tri, lf) * LOG2_E
    g_end = g[c_len - 1:c_len, :] if fwd else g[0:1, :]
    vb = v.astype(BF16)

    o = _dot_nt((q * jnp.exp2(g)).astype(BF16), st.astype(BF16))

    blk = HG_BLOCK
    lo, hi = slice(0, blk), slice(blk, 2 * blk)
    qrows, krows = (hi, lo) if fwd else (lo, hi)
    bnd = g[blk - 1:blk, :] if fwd else g[blk:blk + 1, :]
    qs = (q[qrows] * jnp.exp2(g[qrows] - bnd)).astype(BF16)
    ks = (kin[krows] * jnp.exp2(bnd - g[krows])).astype(BF16)
    cross = _dot(_dot_nt(qs, ks).astype(BF16), vb[krows])

    parts = []
    for rows in (lo, hi):
        parts.append(_hg_block(fwd, q[rows], kin[rows], g[rows], vb[rows], code))
    parts[1 if fwd else 0] = parts[1 if fwd else 0] + cross
    o = o + jnp.concatenate(parts, axis=0)
    st_new = st * jnp.exp2(g_end) + _dot_tn(vb, (kin * jnp.exp2(g_end - g)).astype(BF16))
    return o, st_new


def _hg_block(fwd, q, kin, g, vb, code):
    n = HG_BLOCK

    scores = jnp.zeros((n, n), F32)
    half, level = n // 2, 0
    while half >= HG_SUB:
        seg = 2 * half
        g3 = g.reshape(n // seg, seg, LANES)
        e = half - 1 if fwd else half
        bsel = jnp.broadcast_to(g3[:, e:e + 1, :], g3.shape).reshape(n, LANES)
        qs = (q * jnp.exp2(g - bsel)).astype(BF16)
        ks = (kin * jnp.exp2(bsel - g)).astype(BF16)
        scores = jnp.where(code == level, _dot_nt(qs, ks), scores)
        half, level = half // 2, level + 1

    def shift(x, dl):
        x3 = x.reshape(n // SUBLANES, SUBLANES, LANES)
        return pltpu.roll(x3, dl if fwd else SUBLANES - dl, 1).reshape(n, LANES)

    for dl in range(HG_SUB):
        w = q * kin if dl == 0 else q * shift(kin, dl) * jnp.exp2(g - shift(g, dl))
        scores = jnp.where(code == 100 + dl, jnp.sum(w, axis=-1, keepdims=True), scores)

    return _dot(scores.astype(BF16), vb)


def _hg_kernel(lbl_ref, q_ref, zf_ref, zb_ref, v_ref, o_ref, *, layer):
    c_len = HG_CHUNK
    n_lat = SEQ // c_len
    n_all = _seg() // c_len

    lbs = []
    for d in range(2):
        x = lbl_ref[d]
        e = jnp.exp(x - jnp.max(x, axis=0, keepdims=True))
        tot = jnp.sum(e, axis=0, keepdims=True)
        part = jnp.zeros_like(tot)
        for i in range(1, layer + 1):
            part = part + e[i:i + 1, :]
        lbs.append(part / tot)

    ri = lax.broadcasted_iota(jnp.int32, (c_len, c_len), 0)
    ci = lax.broadcasted_iota(jnp.int32, (c_len, c_len), 1)
    tri_f = (ri >= ci).astype(BF16)
    tri_b = (ri <= ci).astype(BF16)
    code_f = _hg_codes(True)
    code_b = _hg_codes(False)

    o_ref[...] = jnp.zeros(o_ref.shape, F32)

    def step(s, carry):
        st_f, st_b = carry
        cf = jnp.where(s < n_all - n_lat, n_lat + s, s - (n_all - n_lat))
        cbk = n_all - 1 - s
        rf = pl.ds(pl.multiple_of(cf * c_len, c_len), c_len)
        rk = pl.ds(pl.multiple_of(cbk * c_len, c_len), c_len)
        qf = q_ref[rf, :].astype(F32)
        of, st_f = _hg_chunk(True, qf * _sigmoid(qf), zf_ref[rf, :], v_ref[rf, :].astype(F32), lbs[0], st_f,
                             code_f, tri_f)
        o_ref[rf, :] += of
        qb = q_ref[rk, :].astype(F32)
        ob, st_b = _hg_chunk(False, qb * _sigmoid(qb), zb_ref[rk, :], v_ref[rk, :].astype(F32), lbs[1], st_b,
                             code_b, tri_b)
        o_ref[rk, :] += ob
        return st_f, st_b

    zero = jnp.zeros((LANES, LANES), F32)
    lax.fori_loop(0, n_all, step, (zero, zero))


def _hg_call(u, uz, lb_logits, layer):
    t = u.shape[0]
    sb = _seg()
    col = lambda off: pl.BlockSpec((sb, LANES), lambda b, h: (b, off + h))
    return pl.pallas_call(
        functools.partial(_hg_kernel, layer=layer),
        grid=(BATCH, HG_HEADS),
        in_specs=[pl.BlockSpec((2, DEPTH, LANES), lambda b, h: (0, 0, h)),
                  col(_C_HQ), col(0), col(HG_HEADS), col(_C_HI)],
        out_specs=pl.BlockSpec((sb, LANES), lambda b, h: (b, h)),
        out_shape=jax.ShapeDtypeStruct((t, HG_W), F32),
        compiler_params=_params(("arbitrary", "arbitrary")),
        name="hgrn2",
    )(lb_logits, u, uz, uz, u)


def _lru_kernel(x_ref, wg_ref, bg_ref, lam_ref, cw_ref, cb_ref, y_ref,
                xp_scr, af_scr, bf_scr, ab_scr, bb_scr, hb_scr):
    sb = _seg()
    c_len = LRU_CHUNK
    halo = SUBLANES
    n_tiles = sb // SUBLANES
    n_lat_t = SEQ // SUBLANES

    xp_scr[pl.ds(0, halo), :] = jnp.zeros((halo, LANES), F32)
    xp_scr[pl.ds(halo + sb, halo), :] = jnp.zeros((halo, LANES), F32)
    xp_scr[pl.ds(halo, sb), :] = x_ref[...].astype(F32)

    lam = lam_ref[0]
    sp = jnp.maximum(-lam, 0.0) + jnp.log1p(jnp.exp(-jnp.abs(lam)))
    cw = cw_ref[...]
    wg = wg_ref[0]
    bg = bg_ref[0]
    r8 = lax.broadcasted_iota(jnp.int32, (c_len, 1), 0)
    rm8 = r8 % SUBLANES

    def tile_shift(x, s):
        x3 = x.reshape(c_len // SUBLANES, SUBLANES, LANES)
        return pltpu.roll(x3, s, 1).reshape(c_len, LANES)

    def coeff_chunk(c, _):
        base = pl.multiple_of(c * c_len, c_len)
        w = xp_scr[pl.ds(base, c_len + 2 * halo), :]
        grow = base + r8
        in_lat = grow < SEQ
        pos = jnp.where(in_lat, grow, grow - SEQ)
        seglen = jnp.where(in_lat, SEQ, CTX_LEN)
        wlen = c_len + 2 * halo
        xm2 = jnp.where(pos >= 2, pltpu.roll(w, 2, 0)[halo:halo + c_len], 0.0)
        xm1 = jnp.where(pos >= 1, pltpu.roll(w, 1, 0)[halo:halo + c_len], 0.0)
        x0 = w[halo:halo + c_len]
        xp1 = jnp.where(pos + 1 < seglen, pltpu.roll(w, wlen - 1, 0)[halo:halo + c_len], 0.0)
        xc = xm2 * cw[0:1] + xm1 * cw[1:2] + x0 * cw[2:3] + xp1 * cw[3:4] + cb_ref[...]
        gates = _sigmoid(_dot(xc.astype(BF16), wg) + bg)
        rows = pl.ds(base, c_len)
        for d, (a_scr, b_scr) in enumerate(((af_scr, bf_scr), (ab_scr, bb_scr))):
            rg = gates[:, (2 * d) * LANES:(2 * d + 1) * LANES]
            ig = gates[:, (2 * d + 1) * LANES:(2 * d + 2) * LANES]
            log_a = -LRU_C * rg * sp[d:d + 1]
            a = jnp.exp(log_a)
            th = jnp.tanh(log_a)
            bv = jnp.sqrt(-2.0 * th / (1.0 - th)) * (ig * xc)
            for sft in (1, 2, 4):
                if d == 0:
                    a_s = tile_shift(a, sft)
                    b_s = tile_shift(bv, sft)
                    ok = rm8 >= sft
                else:
                    a_s = tile_shift(a, SUBLANES - sft)
                    b_s = tile_shift(bv, SUBLANES - sft)
                    ok = rm8 < SUBLANES - sft
                bv = jnp.where(ok, a * b_s + bv, bv)
                a = jnp.where(ok, a * a_s, a)
            a_scr[rows, :] = a
            b_scr[rows, :] = bv
        return 0

    lax.fori_loop(0, sb // c_len, coeff_chunk, 0)

    def carry_step(s, carry):
        hf, hb = carry
        tf = jnp.where(s < n_tiles - n_lat_t, n_lat_t + s, s - (n_tiles - n_lat_t))
        tb = n_tiles - 1 - s
        rf = pl.ds(pl.multiple_of(tf * SUBLANES, SUBLANES), SUBLANES)
        rk = pl.ds(pl.multiple_of(tb * SUBLANES, SUBLANES), SUBLANES)
        h1 = af_scr[rf, :] * hf + bf_scr[rf, :]
        y_ref[rf, :] = h1
        h2 = ab_scr[rk, :] * hb + bb_scr[rk, :]
        hb_scr[rk, :] = h2
        return (jnp.broadcast_to(h1[SUBLANES - 1:SUBLANES, :], (SUBLANES, LANES)),
                jnp.broadcast_to(h2[0:1, :], (SUBLANES, LANES)))

    zero = jnp.zeros((SUBLANES, LANES), F32)
    lax.fori_loop(0, n_tiles, carry_step, (zero, zero))
    y_ref[...] += hb_scr[...]


def _lru_call(u, wg, bg, lam, cw, cb):
    t = u.shape[0]
    sb = _seg()
    nb = LRU_W // LANES
    return pl.pallas_call(
        _lru_kernel,
        grid=(BATCH, nb),
        in_specs=[pl.BlockSpec((sb, LANES), lambda b, c: (b, _C_LX + c)),
                  pl.BlockSpec((1, LANES, 4 * LANES), lambda b, c: (c, 0, 0)),
                  pl.BlockSpec((1, 1, 4 * LANES), lambda b, c: (c, 0, 0)),
                  pl.BlockSpec((1, 2, LANES), lambda b, c: (c, 0, 0)),
                  pl.BlockSpec((CONV_W, LANES), lambda b, c: (0, c)),
                  pl.BlockSpec((1, LANES), lambda b, c: (0, c))],
        out_specs=pl.BlockSpec((sb, LANES), lambda b, c: (b, c)),
        out_shape=jax.ShapeDtypeStruct((t, LRU_W), F32),
        scratch_shapes=[pltpu.VMEM((sb + 2 * SUBLANES, LANES), F32)] + [pltpu.VMEM((sb, LANES), F32)] * 5,
        compiler_params=_params(("arbitrary", "arbitrary")),
        name="rglru",
    )(u, wg, bg, lam, cw, cb)


def _group_rms(o, group):
    ri = lax.broadcasted_iota(jnp.int32, (LANES, LANES), 0) // group
    ci = lax.broadcasted_iota(jnp.int32, (LANES, LANES), 1) // group
    ones_bd = (ri == ci).astype(BF16)
    outs = []
    for tix in range(o.shape[1] // LANES):
        x = o[:, tix * LANES:(tix + 1) * LANES]
        sq = x * x
        hi = sq.astype(BF16)
        lo = (sq - hi.astype(F32)).astype(BF16)
        ss = _dot(hi, ones_bd) + _dot(lo, ones_bd)
        outs.append(x * lax.rsqrt(ss * (1.0 / group) + NORM_EPS))
    return jnp.concatenate(outs, axis=1)


def _merge_kernel(*refs, n_gate):
    (x_ref, ml_ref, mc_ref, oret_ref, ohg_ref, ylru_ref, rg_ref, hgt_ref, lgt_ref) = refs[:9]
    gate_refs = refs[9:9 + 3 * n_gate]
    (gng_ref, ghg_ref, wr_ref, wh_ref, wl_ref, wo_ref, n2g_ref, rw_ref, rb_ref,
     xo_ref, h2_ref, lg_ref) = refs[9 + 3 * n_gate:]
    tiles_per_b = _seg() // ROW_TILE
    is_ctx = (pl.program_id(0) % tiles_per_b) >= SEQ // ROW_TILE
    ml = ml_ref[0]
    mc = mc_ref[0]

    def gate(i):
        parts = [gate_refs[i * n_gate + j][...].astype(F32) for j in range(n_gate)]
        return _sigmoid(parts[0] if n_gate == 1 else jnp.concatenate(parts, axis=1))

    rg = rg_ref[...].astype(F32)
    ret_in = _group_rms(oret_ref[...], RET_DK) * gng_ref[...] * (rg * _sigmoid(rg))
    ret = _dot(ret_in.astype(BF16), wr_ref[0])
    hg_in = _group_rms(ohg_ref[...], HG_DK) * ghg_ref[...] * _sigmoid(hgt_ref[...].astype(F32))
    hgr = _dot(hg_in.astype(BF16), wh_ref[0])
    lgt = lgt_ref[...].astype(F32)
    gelu = 0.5 * lgt * (1.0 + jnp.tanh(0.7978845608028654 * (lgt + 0.044715 * lgt * lgt * lgt)))
    lru = _dot((ylru_ref[...] * gelu).astype(BF16), wl_ref[0])
    merged = gate(0) * ret + gate(1) * hgr + gate(2) * lru
    mix = _dot(merged.astype(BF16), wo_ref[0])
    g1 = jnp.where(is_ctx, mc[2:3, :], ml[2:3, :])
    xn = x_ref[...] + g1 * mix
    xo_ref[...] = xn
    h2 = _norm_modulate(xn, n2g_ref[...], ml, mc, is_ctx, 3)
    h2_ref[...] = h2.astype(BF16)
    lg_ref[...] = _dot3(h2, rw_ref[...]) + rb_ref[...]


def _merge_call(xs, mod, o_ret, o_hg, y_lru, u, gn_g, hgn_g, wr, wh, wl, wo, n2g, rw, rb, layer):
    t = xs.shape[0]
    r = ROW_TILE
    tiles_per_b = _seg() // r
    n_gate = D_MODEL // 512
    row = lambda w: pl.BlockSpec((r, w), lambda i: (i, 0))
    ucol = lambda blk: pl.BlockSpec((r, 512), lambda i: (i, blk))

    def full(a):
        if a.ndim == 3:
            return pl.BlockSpec((1,) + a.shape[1:], lambda i: (layer, 0, 0))
        return pl.BlockSpec(a.shape, lambda i: (0,) * a.ndim)
    gate_base = _C_GATES * LANES // 512
    in_specs = [row(D_MODEL),
                pl.BlockSpec((1, N_MOD, D_MODEL), lambda i: (i // tiles_per_b, 0, 0)),
                pl.BlockSpec((1, N_MOD, D_MODEL), lambda i: (BATCH, 0, 0)),
                row(RET_W), row(HG_W), row(LRU_W),
                ucol(_C_RG * LANES // 512), ucol(_C_HGT * LANES // 512), ucol(_C_LGT * LANES // 512)]
    in_specs += [ucol(gate_base + j) for j in range(3 * n_gate)]
    consts = [gn_g, hgn_g, wr, wh, wl, wo, n2g, rw, rb]
    in_specs += [full(a) for a in consts]
    return pl.pallas_call(
        functools.partial(_merge_kernel, n_gate=n_gate),
        grid=(t // r,),
        in_specs=in_specs,
        out_specs=[row(D_MODEL), row(D_MODEL), row(LANES)],
        out_shape=[jax.ShapeDtypeStruct((t, D_MODEL), F32), jax.ShapeDtypeStruct((t, D_MODEL), BF16),
                   jax.ShapeDtypeStruct((t, LANES), F32)],
        compiler_params=_params(("arbitrary",)),
        name="merge",
    )(xs, mod, mod, o_ret, o_hg, y_lru, *([u] * (3 + 3 * n_gate)), *consts)


def _route_kernel(lg_ref, oi_ref, ow_ref, lst_ref, meta_ref, cnt_ref, off_scr):
    @pl.when(pl.program_id(0) == 0)
    def _():
        off_scr[...] = jnp.zeros(off_scr.shape, F32)

    r = lg_ref.shape[0]
    lane = lax.broadcasted_iota(jnp.int32, (r, LANES), 1)
    neg = -jnp.inf
    l = jnp.where(lane < N_EXPERTS, lg_ref[...], neg)
    vals, hots = [], []
    lanef = lane.astype(F32)
    for _ in range(TOP_K):
        m = jnp.max(l, axis=-1, keepdims=True)
        idxf = jnp.min(jnp.where(l == m, lanef, float(LANES)), axis=-1, keepdims=True)
        hot = lanef == idxf
        l = jnp.where(hot, neg, l)
        vals.append(m)
        hots.append(hot)
    es = [jnp.exp(v - vals[0]) for v in vals]
    den = es[0]
    for e in es[1:]:
        den = den + e
    chosen = hots[0].astype(F32)
    for hot in hots[1:]:
        chosen = chosen + hot.astype(F32)
    ri = lax.broadcasted_iota(jnp.int32, (r, r), 0)
    ci = lax.broadcasted_iota(jnp.int32, (r, r), 1)
    before = _dot((ri > ci).astype(BF16), chosen.astype(BF16))
    count = jnp.sum(chosen, axis=0, keepdims=True)
    run = jnp.floor((count + (SUBLANES - 1.0)) * (1.0 / SUBLANES)) * SUBLANES
    li = lax.broadcasted_iota(jnp.int32, (LANES, LANES), 0)
    lj = lax.broadcasted_iota(jnp.int32, (LANES, LANES), 1)
    run8 = jnp.broadcast_to(run, (SUBLANES, LANES))
    start = _dot(run8.astype(BF16), (li < lj).astype(BF16))[0:1, :]
    pos = start + before
    oi = jnp.zeros((r, LANES), jnp.int32)
    ow = jnp.zeros((r, LANES), F32)
    lsf = jnp.zeros((r, LANES), F32)
    for k in range(TOP_K):
        row_k = jnp.sum(jnp.where(hots[k], pos, 0.0), axis=-1, keepdims=True)
        oi = jnp.where(lane == k, row_k.astype(jnp.int32), oi)
        lsf = jnp.where(lane == k, row_k, lsf)
        ow = jnp.where(lane == k, es[k] / den, ow)
    oi_ref[...] = oi
    ow_ref[...] = ow
    lst_ref[0] = jnp.transpose(lsf)[0:SUBLANES, :]
    mrow = lax.broadcasted_iota(jnp.int32, (SUBLANES, LANES), 0)
    meta_ref[0] = jnp.where(mrow == 0, run8, jnp.where(mrow == 1, jnp.broadcast_to(start, (SUBLANES, LANES)),
                                                       off_scr[...]))
    off_scr[...] = off_scr[...] + run
    cnt_ref[...] = off_scr[...]


def _route_call(logits):
    t = logits.shape[0]
    r = ROW_TILE
    row = pl.BlockSpec((r, LANES), lambda i: (i, 0))
    return pl.pallas_call(
        _route_kernel,
        grid=(t // r,),
        in_specs=[row],
        out_specs=[row, row, pl.BlockSpec((1, SUBLANES, r), lambda i: (i, 0, 0)),
                   pl.BlockSpec((1, SUBLANES, LANES), lambda i: (i, 0, 0)),
                   pl.BlockSpec((SUBLANES, LANES), lambda i: (0, 0))],
        out_shape=[jax.ShapeDtypeStruct((t, LANES), jnp.int32), jax.ShapeDtypeStruct((t, LANES), F32),
                   jax.ShapeDtypeStruct((t // r, SUBLANES, r), F32),
                   jax.ShapeDtypeStruct((t // r, SUBLANES, LANES), F32),
                   jax.ShapeDtypeStruct((SUBLANES, LANES), F32)],
        scratch_shapes=[pltpu.VMEM((SUBLANES, LANES), F32)],
        compiler_params=_params(("arbitrary",)),
        name="route",
    )(logits)


def _pack_pairs(x):
    m = x.shape[1] // 2
    bits = pltpu.bitcast(x, jnp.uint32)
    return bits[:, :m] | (bits[:, m:] >> 16)


def _unpack_pairs(w):
    hi = pltpu.bitcast(w & jnp.uint32(0xFFFF0000), F32).astype(BF16)
    lo = pltpu.bitcast(w << 16, F32).astype(BF16)
    return hi, lo


def _local_rows():
    rows = ROW_TILE * TOP_K + N_EXPERTS * (SUBLANES - 1)
    return -(-rows // ROW_TILE) * ROW_TILE


RUN_PIECE = 4 * SUBLANES


def _run_copies(meta_ref, make_copy, do):
    per_big = RUN_PIECE // SUBLANES

    def per_expert(e, _):
        first_slot = meta_ref[0, 0, e]
        first_row = meta_ref[0, 0, N_EXPERTS + e]
        pieces = meta_ref[0, 0, 2 * N_EXPERTS + e]
        n_big = pieces // per_big

        def big(j, _):
            do(make_copy(pl.multiple_of(first_row + j * RUN_PIECE, SUBLANES),
                         pl.multiple_of(first_slot + j * RUN_PIECE, SUBLANES), RUN_PIECE))
            return 0

        def small(j, _):
            do(make_copy(pl.multiple_of(first_row + j * SUBLANES, SUBLANES),
                         pl.multiple_of(first_slot + j * SUBLANES, SUBLANES), SUBLANES))
            return 0

        lax.fori_loop(0, n_big, big, 0)
        lax.fori_loop(n_big * per_big, pieces, small, 0)
        return 0

    lax.fori_loop(0, N_EXPERTS, per_expert, 0)


def _dispatch_kernel(meta_ref, prev_meta_ref, lst_ref, h_ref, xs_in_ref, xs_ref, perm_scr, sem):
    del xs_in_ref
    i = pl.program_id(0)
    cur = i % 2
    r = h_ref.shape[0]
    ls = lst_ref[0].astype(jnp.int32)
    hb = h_ref[...]
    for s in range(perm_scr.shape[1] // r):
        rows = s * r + lax.broadcasted_iota(jnp.int32, (r, r), 0)
        sel = rows == ls[0:1, :]
        for k in range(1, TOP_K):
            sel = jnp.logical_or(sel, rows == ls[k:k + 1, :])
        perm_scr[cur, pl.ds(s * r, r), :] = _pack_pairs(_dot(jnp.where(sel, 1.0, 0.0).astype(BF16), hb))

    def copies_from(buf):
        def make_copy(row, slot, n):
            return pltpu.make_async_copy(perm_scr.at[buf, pl.ds(row, n)], xs_ref.at[pl.ds(slot, n)], sem.at[buf])
        return make_copy

    _run_copies(meta_ref, copies_from(cur), lambda c: c.start())

    @pl.when(i > 0)
    def _():
        _run_copies(prev_meta_ref, copies_from(1 - cur), lambda c: c.wait())

    @pl.when(i == pl.num_programs(0) - 1)
    def _():
        _run_copies(meta_ref, copies_from(cur), lambda c: c.wait())


def _dispatch_call(h2, meta, lst, slots):
    t = h2.shape[0]
    r = ROW_TILE
    return pl.pallas_call(
        _dispatch_kernel,
        grid=(t // r,),
        in_specs=[pl.BlockSpec((1, 1, LANES), lambda i: (i, 0, 0), memory_space=pltpu.SMEM),
                  pl.BlockSpec((1, 1, LANES), lambda i: (jnp.maximum(i - 1, 0), 0, 0), memory_space=pltpu.SMEM),
                  pl.BlockSpec((1, SUBLANES, r), lambda i: (i, 0, 0)),
                  pl.BlockSpec((r, D_MODEL), lambda i: (i, 0)),
                  pl.BlockSpec(memory_space=pl.ANY)],
        out_specs=pl.BlockSpec(memory_space=pl.ANY),
        out_shape=jax.ShapeDtypeStruct(slots.shape, jnp.uint32),
        scratch_shapes=[pltpu.VMEM((2, _local_rows(), D_MODEL // 2), jnp.uint32), pltpu.SemaphoreType.DMA((2,))],
        input_output_aliases={4: 0},
        compiler_params=_params(("arbitrary",)),
        name="dispatch",
    )(meta, meta, lst, h2, slots)


def _expert_kernel(bexp_ref, nv_ref, x_ref, wgu_ref, bgu_ref, wd_ref, bd_ref, y_ref, wgu_scr, wd_scr):
    i = pl.program_id(0)
    f = D_MODEL
    prev = bexp_ref[jnp.maximum(i - 1, 0)]
    new_expert = jnp.logical_or(i == 0, bexp_ref[i] != prev)

    @pl.when(jnp.logical_and(i < nv_ref[0], new_expert))
    def _():
        slab = 256
        for s in range(D_MODEL // slab):
            wgu_scr[pl.ds(s * slab, slab), :] = wgu_ref[0, 0, pl.ds(s * slab, slab), :].astype(BF16)
            wd_scr[pl.ds(s * slab, slab), :] = wd_ref[0, 0, pl.ds(s * slab, slab), :].astype(BF16)

    @pl.when(i < nv_ref[0])
    def _():
        x_hi, x_lo = _unpack_pairs(x_ref[...])
        half = D_MODEL // 2
        gu = _dot(x_hi, wgu_scr[:half, :]) + _dot(x_lo, wgu_scr[half:, :]) + bgu_ref[0, 0]
        gate = jnp.minimum(gu[:, :f], SWIGLU_LIMIT)
        up = jnp.clip(gu[:, f:], -SWIGLU_LIMIT, SWIGLU_LIMIT)
        glu = gate * _sigmoid(SWIGLU_ALPHA * gate)
        y = _dot(((up + 1.0) * glu).astype(BF16), wd_scr[...]) + bd_ref[0, 0]
        y_ref[...] = _pack_pairs(y.astype(BF16).astype(F32))

    @pl.when(i >= nv_ref[0])
    def _():
        y_ref[...] = jnp.zeros(y_ref.shape, jnp.uint32)


def _expert_call(slots, bexp, nvalid, w_gu, b_gu, w_down, b_down, layer):
    s = slots.shape[0]
    nb = s // MOE_BLOCK
    f = D_MODEL
    blk = lambda i, be, nv: (jnp.minimum(i, nv[0] - 1), 0)
    exp3 = lambda i, be, nv: (layer, be[jnp.minimum(i, nv[0] - 1)], 0, 0)
    grid_spec = pltpu.PrefetchScalarGridSpec(
        num_scalar_prefetch=2,
        grid=(nb,),
        in_specs=[pl.BlockSpec((MOE_BLOCK, D_MODEL // 2), blk),
                  pl.BlockSpec((1, 1, D_MODEL, 2 * f), exp3),
                  pl.BlockSpec((1, 1, 1, 2 * f), exp3),
                  pl.BlockSpec((1, 1, f, D_MODEL), exp3),
                  pl.BlockSpec((1, 1, 1, D_MODEL), exp3)],
        out_specs=pl.BlockSpec((MOE_BLOCK, D_MODEL // 2), lambda i, be, nv: (i, 0)),
        scratch_shapes=[pltpu.VMEM((D_MODEL, 2 * f), BF16), pltpu.VMEM((f, D_MODEL), BF16)],
    )
    return pl.pallas_call(
        _expert_kernel,
        grid_spec=grid_spec,
        out_shape=jax.ShapeDtypeStruct((s, D_MODEL // 2), jnp.uint32),
        compiler_params=_params(("arbitrary",)),
        name="experts",
    )(bexp, nvalid, slots, w_gu, b_gu.reshape(DEPTH, N_EXPERTS, 1, 2 * f), w_down,
      b_down.reshape(DEPTH, N_EXPERTS, 1, D_MODEL))


def _combine_kernel(meta_ref, next_meta_ref, x_ref, ml_ref, mc_ref, oi_ref, w_ref, fg_ref, y_ref, o_ref, buf, sem,
                    *, last, n_t):
    r = x_ref.shape[0]
    i = pl.program_id(0)
    cur = i % 2

    def copies_into(b):
        def make_copy(row, slot, n):
            return pltpu.make_async_copy(y_ref.at[pl.ds(slot, n)], buf.at[b, pl.ds(row, n)], sem.at[b])
        return make_copy

    @pl.when(i == 0)
    def _():
        buf[...] = jnp.zeros(buf.shape, jnp.uint32)
        _run_copies(meta_ref, copies_into(0), lambda c: c.start())

    @pl.when(i + 1 < pl.num_programs(0))
    def _():
        _run_copies(next_meta_ref, copies_into(1 - cur), lambda c: c.start())

    _run_copies(meta_ref, copies_into(cur), lambda c: c.wait())

    if last:
        is_ctx = False
    else:
        is_ctx = (i % n_t) >= SEQ // ROW_TILE
    w = w_ref[...]
    oi = oi_ref[...]
    ff_hi = jnp.zeros((r, D_MODEL // 2), F32)
    ff_lo = jnp.zeros((r, D_MODEL // 2), F32)
    for s in range(buf.shape[1] // r):
        cols = s * r + lax.broadcasted_iota(jnp.int32, (r, r), 1)
        wm = jnp.zeros((r, r), F32)
        for k in range(TOP_K):
            wm = jnp.where(cols == oi[:, k:k + 1], w[:, k:k + 1], wm)
        y_hi, y_lo = _unpack_pairs(buf[cur, pl.ds(s * r, r), :])
        ff_hi = ff_hi + _dot(wm.astype(BF16), y_hi)
        ff_lo = ff_lo + _dot(wm.astype(BF16), y_lo)
    ff = jnp.concatenate([ff_hi, ff_lo], axis=1)
    g2 = jnp.where(is_ctx, mc_ref[0][5:6, :], ml_ref[0][5:6, :])
    xn = x_ref[...] + g2 * ff
    if last:
        ms = jnp.mean(xn * xn, axis=-1, keepdims=True)
        xn = xn * lax.rsqrt(ms + NORM_EPS) * fg_ref[...]
    o_ref[...] = xn


def _combine_call(xs, mod, meta, rows, wts, y_slots, final_g, last):
    t = xs.shape[0]
    r = ROW_TILE
    tiles_per_b = _seg() // r
    n_t = SEQ // r if last else tiles_per_b
    n_steps = BATCH * n_t
    tile = lambda i: (i // n_t) * tiles_per_b + i % n_t
    out_rows = BATCH * SEQ if last else t
    return pl.pallas_call(
        functools.partial(_combine_kernel, last=last, n_t=n_t),
        grid=(n_steps,),
        in_specs=[pl.BlockSpec((1, 1, LANES), lambda i: (tile(i), 0, 0), memory_space=pltpu.SMEM),
                  pl.BlockSpec((1, 1, LANES), lambda i: (tile(jnp.minimum(i + 1, n_steps - 1)), 0, 0),
                               memory_space=pltpu.SMEM),
                  pl.BlockSpec((r, D_MODEL), lambda i: (tile(i), 0)),
                  pl.BlockSpec((1, N_MOD, D_MODEL), lambda i: (i // n_t, 0, 0)),
                  pl.BlockSpec((1, N_MOD, D_MODEL), lambda i: (BATCH, 0, 0)),
                  pl.BlockSpec((r, LANES), lambda i: (tile(i), 0)),
                  pl.BlockSpec((r, LANES), lambda i: (tile(i), 0)),
                  pl.BlockSpec((1, D_MODEL), lambda i: (0, 0)),
                  pl.BlockSpec(memory_space=pl.ANY)],
        out_specs=pl.BlockSpec((r, D_MODEL), lambda i: (i, 0)),
        out_shape=jax.ShapeDtypeStruct((out_rows, D_MODEL), F32),
        scratch_shapes=[pltpu.VMEM((2, _local_rows(), D_MODEL // 2), jnp.uint32), pltpu.SemaphoreType.DMA((2,))],
        compiler_params=_params(("arbitrary",)),
        name="combine",
    )(meta, meta, xs, mod, mod, rows, wts, final_g, y_slots)


def _rotary_tables():
    rows = SEQ // GRID_W
    row = jnp.repeat(jnp.arange(rows, dtype=F32), GRID_W)
    colp = jnp.tile(jnp.arange(GRID_W, dtype=F32), rows)
    n_freq = RET_DK // 4
    inv = ROPE_BASE ** (-jnp.arange(n_freq, dtype=F32) / n_freq)
    ang = jnp.concatenate([row[:, None] * inv, colp[:, None] * inv], axis=-1)
    cos, sin = jnp.cos(ang), jnp.sin(ang)
    reps = LANES // (RET_DK // 2)
    cos_t = jnp.tile(cos, (1, reps))
    sign = jnp.where((jnp.arange(LANES) % RET_DK) < RET_DK // 2, -1.0, 1.0).astype(F32)
    sin_t = jnp.tile(sin, (1, reps)) * sign
    return cos_t, sin_t


def _lru_gate_weights(wa, ba, wx, bx):
    nb = LRU_W // LANES
    per = LANES // LRU_BD
    eye = jnp.eye(per, dtype=F32)

    def embed(w):
        w = w.reshape(nb, per, LRU_BD, LRU_BD)
        return jnp.einsum('cpij,pq->cpiqj', w, eye).reshape(nb, LANES, LANES)

    wg = jnp.concatenate([embed(wa[0]), embed(wx[0]), embed(wa[1]), embed(wx[1])], axis=-1)
    bias = jnp.concatenate([ba[0].reshape(nb, 1, LANES), bx[0].reshape(nb, 1, LANES),
                            ba[1].reshape(nb, 1, LANES), bx[1].reshape(nb, 1, LANES)], axis=-1)
    return wg.astype(BF16), bias


def kernel(x, c, ctx, c_ctx, mod_w, mod_b, norm1_g, norm2_g, final_g, w_in, ret_decay, ret_gn_g, w_ret_o,
           hgrn_lb_logits, hgrn_gn_g, w_hgrn_o, lru_conv_w, lru_conv_b, lru_wa, lru_ba, lru_wx, lru_bx,
           lru_lambda, w_lru_o, w_out, router_w, router_b, exp_w_gu, exp_b_gu, exp_w_down, exp_b_down):
    assert CTX_LEN == RET_CHUNK and SEQ % RET_CHUNK == 0
    sb = _seg()
    t = BATCH * sb
    xs = jnp.concatenate([x, ctx], axis=1).reshape(t, D_MODEL)

    cvec = jnp.concatenate([c, c_ctx[None, :], jnp.zeros((SUBLANES - BATCH - 1, D_MODEL), F32)], axis=0)
    mod_all = _mod_call(cvec, mod_w, mod_b).reshape(DEPTH, SUBLANES, N_MOD, D_MODEL)

    cos_t, sin_t = _rotary_tables()
    lb_logits = jnp.transpose(hgrn_lb_logits, (1, 0, 2))
    n_slots = t * TOP_K + (t // ROW_TILE) * N_EXPERTS * (SUBLANES - 1) + N_EXPERTS * (MOE_BLOCK - 1)
    n_blocks = -(-n_slots // MOE_BLOCK)
    n_slots = n_blocks * MOE_BLOCK
    slots = jnp.zeros((n_slots, D_MODEL // 2), jnp.uint32)
    rw_pad = jnp.pad(router_w, ((0, 0), (0, 0), (0, LANES - N_EXPERTS)))
    rb_pad = jnp.pad(router_b, ((0, 0), (0, LANES - N_EXPERTS)))
    w_in_b, w_ret_b, w_hg_b, w_lru_b, w_out_b = (w.astype(BF16) for w in (w_in, w_ret_o, w_hgrn_o, w_lru_o, w_out))
    block_first_slot = jnp.arange(n_blocks, dtype=jnp.int32) * MOE_BLOCK

    out = None
    for l in range(DEPTH):
        last = l == DEPTH - 1
        mod = mod_all[l]
        u, uz = _inproj_call(xs, mod, norm1_g[l][None, :], w_in_b, l)

        dec = jnp.broadcast_to(ret_decay[l].reshape(2, RET_HEADS // 2, 2, 1).transpose(1, 0, 2, 3),
                               (RET_HEADS // 2, 2, 2, LANES)).reshape(RET_HEADS // 2, 4, LANES)
        o_ret = _ret_call(u, dec, cos_t, sin_t)
        o_hg = _hg_call(u, uz, lb_logits, l)
        wg, bg = _lru_gate_weights(lru_wa[l], lru_ba[l], lru_wx[l], lru_bx[l])
        lam = lru_lambda[l].reshape(2, LRU_W // LANES, LANES).transpose(1, 0, 2)
        y_lru = _lru_call(u, wg, bg, lam, lru_conv_w[l], lru_conv_b[l][None, :])

        xs, h2, logits = _merge_call(
            xs, mod, o_ret, o_hg, y_lru, u, ret_gn_g[l][None, :], hgrn_gn_g[l][None, :],
            w_ret_b, w_hg_b, w_lru_b, w_out_b, norm2_g[l][None, :], rw_pad[l], rb_pad[l][None, :], l)

        rows, wts, lst, tile_meta, cnt = _route_call(logits)
        run = tile_meta[:, 0, :N_EXPERTS].astype(jnp.int32)
        first_row = tile_meta[:, 1, :N_EXPERTS].astype(jnp.int32)
        earlier = tile_meta[:, 2, :N_EXPERTS].astype(jnp.int32)
        counts = cnt[0, :N_EXPERTS].astype(jnp.int32)
        padded = (counts + MOE_BLOCK - 1) // MOE_BLOCK * MOE_BLOCK
        pad_end = jnp.cumsum(padded)
        first_slot = (pad_end - padded)[None, :] + earlier
        meta = jnp.concatenate([first_slot, first_row, run // SUBLANES, jnp.zeros_like(run)], axis=1)
        meta = meta.reshape(t // ROW_TILE, 1, LANES)
        bexp = jnp.minimum(jnp.sum((pad_end[None, :] <= block_first_slot[:, None]).astype(jnp.int32), axis=1),
                           N_EXPERTS - 1)
        nvalid = (pad_end[-1:] // MOE_BLOCK).astype(jnp.int32)

        slots = _dispatch_call(h2, meta, lst, slots)
        y_slots = _expert_call(slots, bexp, nvalid, exp_w_gu, exp_b_gu, exp_w_down, exp_b_down, l)
        res = _combine_call(xs, mod, meta, rows, wts, y_slots, final_g[None, :], last)
        if last:
            out = res.reshape(BATCH, SEQ, D_MODEL)
        else:
            xs = res
    return out
```

```python
import functools

import jax
import jax.numpy as jnp
from jax import lax
from jax.experimental import pallas as pl
from jax.experimental.pallas import tpu as pltpu

F32 = jnp.float32
BF16 = jnp.bfloat16

D_MODEL = 1024
BATCH = 4
SEQ = 4096
DEPTH = 4
GRID_W = 64
CTX_LEN = 256
N_MOD = 6
NORM_EPS = 1e-6
RET_HEADS = 8
RET_DK = 64
RET_W = 512
ROPE_BASE = 10000.0
HG_HEADS = 4
HG_DK = 128
HG_W = 512
LRU_W = 512
LRU_BLOCKS = 8
LRU_BD = 64
LRU_C = 8.0
CONV_W = 4
N_EXPERTS = 32
TOP_K = 4
SWIGLU_ALPHA = 1.702
SWIGLU_LIMIT = 7.0

LANES = 128
SUBLANES = 8
RET_CHUNK = 256
HG_CHUNK = 256
HG_BLOCK = 128
HG_SUB = 4
LOG2_E = 1.4426950408889634
LRU_CHUNK = 256
MOE_BLOCK = 256
ROW_TILE = 256
ROUTE_TILE = 512
MOD_TN = 1536
VMEM_LIMIT = 56 * 1024 * 1024

_C_RQ, _C_RK, _C_RV, _C_RG = 0, 4, 8, 12
_C_HQ, _C_HFF, _C_HFB, _C_HI, _C_HGT = 16, 20, 24, 28, 32
_C_LX, _C_LGT = 36, 40
_C_GATES = 44


def _seg():
    return SEQ + CTX_LEN


def _in_cols():
    return _C_GATES * LANES + 3 * D_MODEL


def _dot(a, b):
    return jnp.dot(a, b, preferred_element_type=F32)


def _dot_nt(a, b):
    return lax.dot_general(a, b, (((1,), (1,)), ((), ())), preferred_element_type=F32)


def _dot_tn(a, b):
    return lax.dot_general(a, b, (((0,), (0,)), ((), ())), preferred_element_type=F32)


def _split3(x):
    hi = x.astype(BF16)
    r = x - hi.astype(F32)
    mid = r.astype(BF16)
    lo = (r - mid.astype(F32)).astype(BF16)
    return hi, mid, lo


def _dot_exact_lhs(a, x):
    hi, mid, lo = _split3(x)
    return _dot(a, hi) + _dot(a, mid) + _dot(a, lo)


def _dot_exact_rhs(x, a):
    hi, mid, lo = _split3(x)
    return _dot(hi, a) + _dot(mid, a) + _dot(lo, a)


def _dot3(a, b):
    ah, am, _ = _split3(a)
    bh, bm, _ = _split3(b)
    return _dot(ah, bh) + (_dot(ah, bm) + _dot(am, bh))


def _sigmoid(x):
    return 0.5 + 0.5 * jnp.tanh(0.5 * x)


def _log_sigmoid(x):
    return jnp.minimum(x, 0.0) - jnp.log1p(jnp.exp(-jnp.abs(x)))


def _params(sem):
    return pltpu.CompilerParams(dimension_semantics=sem, vmem_limit_bytes=VMEM_LIMIT)


def _mod_kernel(c_ref, w_ref, b_ref, o_ref):
    c = c_ref[...]
    s = c * _sigmoid(c)
    o_ref[0] = _dot3(s, w_ref[0]) + b_ref[0]


def _mod_call(cvec, mod_w, mod_b):
    n6 = N_MOD * D_MODEL
    tn = MOD_TN
    return pl.pallas_call(
        _mod_kernel,
        grid=(DEPTH, n6 // tn),
        in_specs=[pl.BlockSpec((SUBLANES, D_MODEL), lambda l, j: (0, 0)),
                  pl.BlockSpec((1, D_MODEL, tn), lambda l, j: (l, 0, j)),
                  pl.BlockSpec((1, 1, tn), lambda l, j: (l, 0, j))],
        out_specs=pl.BlockSpec((1, SUBLANES, tn), lambda l, j: (l, 0, j)),
        out_shape=jax.ShapeDtypeStruct((DEPTH, SUBLANES, n6), F32),
        compiler_params=_params(("arbitrary", "arbitrary")),
        name="modulation",
    )(cvec, mod_w, mod_b.reshape(DEPTH, 1, n6))


def _norm_modulate(x, g, mod_l, mod_c, is_ctx, first):
    ms = jnp.mean(x * x, axis=-1, keepdims=True)
    y = x * lax.rsqrt(ms + NORM_EPS) * g
    sh = jnp.where(is_ctx, mod_c[first:first + 1, :], mod_l[first:first + 1, :])
    sc = jnp.where(is_ctx, mod_c[first + 1:first + 2, :], mod_l[first + 1:first + 2, :])
    return y * (1.0 + sc) + sh


def _inproj_kernel(x_ref, ml_ref, mc_ref, g_ref, w_ref, o_ref, oz_ref, h_scr, *, tm, z_first, z_count):
    j = pl.program_id(1)

    @pl.when(j == 0)
    def _():
        tiles_per_b = _seg() // tm
        row0 = (pl.program_id(0) % tiles_per_b) * tm
        rows = row0 + lax.broadcasted_iota(jnp.int32, (tm, 1), 0)
        h = _norm_modulate(x_ref[...], g_ref[...], ml_ref[0], mc_ref[0], rows >= SEQ, 0)
        h_scr[...] = h.astype(BF16)

    acc = _dot(h_scr[...], w_ref[0])
    o_ref[...] = acc.astype(BF16)

    @pl.when(jnp.logical_and(j >= z_first, j < z_first + z_count))
    def _():
        oz_ref[...] = acc


def _inproj_call(xs, mod, g, w, layer):
    t = xs.shape[0]
    tm = _seg() // 2
    ncol = _in_cols()
    tn = HG_W
    tiles_per_b = _seg() // tm
    z_first = _C_HFF * LANES // tn
    z_count = 2
    return pl.pallas_call(
        functools.partial(_inproj_kernel, tm=tm, z_first=z_first, z_count=z_count),
        grid=(t // tm, ncol // tn),
        in_specs=[pl.BlockSpec((tm, D_MODEL), lambda i, j: (i, 0)),
                  pl.BlockSpec((1, N_MOD, D_MODEL), lambda i, j: (i // tiles_per_b, 0, 0)),
                  pl.BlockSpec((1, N_MOD, D_MODEL), lambda i, j: (BATCH, 0, 0)),
                  pl.BlockSpec((1, D_MODEL), lambda i, j: (0, 0)),
                  pl.BlockSpec((1, D_MODEL, tn), lambda i, j: (layer, 0, j))],
        out_specs=[pl.BlockSpec((tm, tn), lambda i, j: (i, j)),
                   pl.BlockSpec((tm, tn), lambda i, j: (i, jnp.clip(j - z_first, 0, z_count - 1)))],
        out_shape=[jax.ShapeDtypeStruct((t, ncol), BF16), jax.ShapeDtypeStruct((t, z_count * tn), F32)],
        scratch_shapes=[pltpu.VMEM((tm, D_MODEL), BF16)],
        compiler_params=_params(("arbitrary", "arbitrary")),
        name="in_projection",
    )(xs, mod, mod, g, w)


def _ret_kernel(dec_ref, cos_ref, sin_ref, q_ref, k_ref, v_ref, o_ref, mask_scr, pw_scr):
    c_len = RET_CHUNK
    n_lat = SEQ // c_len
    n_all = _seg() // c_len
    lane = lax.broadcasted_iota(jnp.int32, (1, LANES), 1)
    head0 = lane < RET_DK
    first_half = (lane % RET_DK) < (RET_DK // 2)

    lg = _log_sigmoid(dec_ref[0])
    dvf = jnp.where(head0, lg[0:1], lg[1:2])
    dvb = jnp.where(head0, lg[2:3], lg[3:4])
    r = lax.broadcasted_iota(jnp.int32, (c_len, 1), 0).astype(F32)
    pw_scr[0] = jnp.exp((r + 1.0) * dvf)
    pw_scr[1] = jnp.exp((c_len - 1.0 - r) * dvf)
    pw_scr[2] = jnp.exp((c_len - r) * dvb)
    pw_scr[3] = jnp.exp(r * dvb)
    dcf = jnp.exp(c_len * dvf)
    dcb = jnp.exp(c_len * dvb)
    ri = lax.broadcasted_iota(jnp.int32, (c_len, c_len), 0)
    ci = lax.broadcasted_iota(jnp.int32, (c_len, c_len), 1)
    dm = (ri - ci).astype(F32)
    for h in range(2):
        lgf = jnp.concatenate([lg[h:h + 1]] * (c_len // LANES), axis=1)
        lgb = jnp.concatenate([lg[2 + h:3 + h]] * (c_len // LANES), axis=1)
        mask_scr[h] = jnp.exp(jnp.where(dm >= 0, dm * lgf, -dm * lgb))
    rb = lax.broadcasted_iota(jnp.int32, (LANES, LANES), 0) < RET_DK
    cb = lax.broadcasted_iota(jnp.int32, (LANES, LANES), 1) < RET_DK
    same_head = (rb == cb).astype(F32)

    def load(c):
        rows = pl.ds(pl.multiple_of(c * c_len, c_len), c_len)
        q = q_ref[rows, :].astype(F32)
        k = k_ref[rows, :].astype(F32) * (RET_DK ** -0.5)
        v = v_ref[rows, :].astype(F32)
        return rows, q, k, v

    def rotary(t, c):
        rows = pl.ds(pl.multiple_of(c * c_len, c_len), c_len)
        partner = jnp.where(first_half, pltpu.roll(t, LANES - RET_DK // 2, 1), pltpu.roll(t, RET_DK // 2, 1))
        return t * cos_ref[rows, :] + partner * sin_ref[rows, :]

    def load_rot(c):
        rows, q, k, v = load(c)
        is_lat = c < n_lat
        cc = jnp.minimum(c, n_lat - 1)
        q = jnp.where(is_lat, rotary(q, cc), q)
        k = jnp.where(is_lat, rotary(k, cc), k)
        return rows, q, k, v

    def fwd_step(s, st):
        c = jnp.where(s < n_all - n_lat, n_lat + s, s - (n_all - n_lat))
        rows, q, k, v = load_rot(c)
        kb = k.astype(BF16)
        vb = v.astype(BF16)
        o0 = _dot((_dot_nt(jnp.where(head0, q, 0.0).astype(BF16), kb) * mask_scr[0]).astype(BF16), vb)
        o1 = _dot((_dot_nt(jnp.where(head0, 0.0, q).astype(BF16), kb) * mask_scr[1]).astype(BF16), vb)
        o = jnp.where(head0, o0, o1)
        o = o + _dot_nt((q * pw_scr[0]).astype(BF16), st.astype(BF16))
        o_ref[rows, :] += o
        return st * dcf + _dot_tn(vb, (k * pw_scr[1]).astype(BF16)) * same_head

    def bwd_step(s, st):
        c = n_all - 1 - s
        rows, q, k, v = load_rot(c)
        o_ref[rows, :] += _dot_nt((q * pw_scr[2]).astype(BF16), st.astype(BF16))
        return st * dcb + _dot_tn(v.astype(BF16), (k * pw_scr[3]).astype(BF16)) * same_head

    o_ref[...] = jnp.zeros(o_ref.shape, F32)
    zero = jnp.zeros((LANES, LANES), F32)
    lax.fori_loop(0, n_all, lambda s, sts: (fwd_step(s, sts[0]), bwd_step(s, sts[1])), (zero, zero))


def _ret_call(u, dec, cos_t, sin_t):
    t = u.shape[0]
    sb = _seg()
    col = lambda off: pl.BlockSpec((sb, LANES), lambda b, p: (b, off + p))
    return pl.pallas_call(
        _ret_kernel,
        grid=(BATCH, RET_HEADS // 2),
        in_specs=[pl.BlockSpec((1, 4, LANES), lambda b, p: (p, 0, 0)),
                  pl.BlockSpec((SEQ, LANES), lambda b, p: (0, 0)),
                  pl.BlockSpec((SEQ, LANES), lambda b, p: (0, 0)),
                  col(_C_RQ), col(_C_RK), col(_C_RV)],
        out_specs=pl.BlockSpec((sb, LANES), lambda b, p: (b, p)),
        out_shape=jax.ShapeDtypeStruct((t, RET_W), F32),
        scratch_shapes=[pltpu.VMEM((2, RET_CHUNK, RET_CHUNK), F32),
                        pltpu.VMEM((4, RET_CHUNK, LANES), F32)],
        compiler_params=_params(("arbitrary", "arbitrary")),
        name="retention",
    )(dec, cos_t, sin_t, u, u, u)


def _hg_codes(fwd):
    c_len = HG_BLOCK
    ri = lax.broadcasted_iota(jnp.int32, (c_len, c_len), 0)
    ci = lax.broadcasted_iota(jnp.int32, (c_len, c_len), 1)
    dist = (ri - ci) if fwd else (ci - ri)
    code = jnp.full((c_len, c_len), -1, jnp.int32)
    half, level = c_len // 2, 0
    while half >= HG_SUB:
        sh = half.bit_length() - 1
        same_seg = jnp.right_shift(ri, sh + 1) == jnp.right_shift(ci, sh + 1)
        other_half = jnp.right_shift(ri, sh) != jnp.right_shift(ci, sh)
        code = jnp.where(same_seg & other_half & (dist > 0), level, code)
        half, level = half // 2, level + 1
    sub_sh = HG_SUB.bit_length() - 1
    same_sub = jnp.right_shift(ri, sub_sh) == jnp.right_shift(ci, sub_sh)
    for dl in range(HG_SUB):
        code = jnp.where(same_sub & (dist == dl), 100 + dl, code)
    return code


def _hg_chunk(fwd, q, z, v, lb, st, code, tri):
    c_len = HG_CHUNK

    a = jnp.log(lb)
    b = jnp.log1p(-lb) + _log_sigmoid(z)
    lf = jnp.maximum(a, b) + jnp.log1p(jnp.exp(-jnp.abs(a - b)))
    kin = (1.0 - lb) * _sigmoid(-z)

    g = _dot_exact_lhs(tri, lf) * LOG2_E
    g_end = g[c_len - 1:c_len, :] if fwd else g[0:1, :]
    vb = v.astype(BF16)

    o = _dot_nt((q * jnp.exp2(g)).astype(BF16), st.astype(BF16))

    blk = HG_BLOCK
    lo, hi = slice(0, blk), slice(blk, 2 * blk)
    qrows, krows = (hi, lo) if fwd else (lo, hi)
    bnd = g[blk - 1:blk, :] if fwd else g[blk:blk + 1, :]
    qs = (q[qrows] * jnp.exp2(g[qrows] - bnd)).astype(BF16)
    ks = (kin[krows] * jnp.exp2(bnd - g[krows])).astype(BF16)
    cross = _dot(_dot_nt(qs, ks).astype(BF16), vb[krows])

    parts = []
    for rows in (lo, hi):
        parts.append(_hg_block(fwd, q[rows], kin[rows], g[rows], vb[rows], code))
    parts[1 if fwd else 0] = parts[1 if fwd else 0] + cross
    o = o + jnp.concatenate(parts, axis=0)
    st_new = st * jnp.exp2(g_end) + _dot_tn(vb, (kin * jnp.exp2(g_end - g)).astype(BF16))
    return o, st_new


def _hg_block(fwd, q, kin, g, vb, code):
    n = HG_BLOCK

    scores = jnp.zeros((n, n), F32)
    half, level = n // 2, 0
    while half >= HG_SUB:
        seg = 2 * half
        g3 = g.reshape(n // seg, seg, LANES)
        e = half - 1 if fwd else half
        bsel = jnp.broadcast_to(g3[:, e:e + 1, :], g3.shape).reshape(n, LANES)
        qs = (q * jnp.exp2(g - bsel)).astype(BF16)
        ks = (kin * jnp.exp2(bsel - g)).astype(BF16)
        scores = jnp.where(code == level, _dot_nt(qs, ks), scores)
        half, level = half // 2, level + 1

    def shift(x, dl):
        x3 = x.reshape(n // SUBLANES, SUBLANES, LANES)
        return pltpu.roll(x3, dl if fwd else SUBLANES - dl, 1).reshape(n, LANES)

    for dl in range(HG_SUB):
        w = q * kin if dl == 0 else q * shift(kin, dl) * jnp.exp2(g - shift(g, dl))
        scores = jnp.where(code == 100 + dl, jnp.sum(w, axis=-1, keepdims=True), scores)

    return _dot(scores.astype(BF16), vb)


def _hg_kernel(lbl_ref, q_ref, zf_ref, zb_ref, v_ref, o_ref, *, layer):
    c_len = HG_CHUNK
    n_lat = SEQ // c_len
    n_all = _seg() // c_len

    lbs = []
    for d in range(2):
        x = lbl_ref[d]
        e = jnp.exp(x - jnp.max(x, axis=0, keepdims=True))
        tot = jnp.sum(e, axis=0, keepdims=True)
        part = jnp.zeros_like(tot)
        for i in range(1, layer + 1):
            part = part + e[i:i + 1, :]
        lbs.append(part / tot)

    ri = lax.broadcasted_iota(jnp.int32, (c_len, c_len), 0)
    ci = lax.broadcasted_iota(jnp.int32, (c_len, c_len), 1)
    tri_f = (ri >= ci).astype(BF16)
    tri_b = (ri <= ci).astype(BF16)
    code_f = _hg_codes(True)
    code_b = _hg_codes(False)

    o_ref[...] = jnp.zeros(o_ref.shape, F32)

    def step(s, carry):
        st_f, st_b = carry
        cf = jnp.where(s < n_all - n_lat, n_lat + s, s - (n_all - n_lat))
        cbk = n_all - 1 - s
        rf = pl.ds(pl.multiple_of(cf * c_len, c_len), c_len)
        rk = pl.ds(pl.multiple_of(cbk * c_len, c_len), c_len)
        qf = q_ref[rf, :].astype(F32)
        of, st_f = _hg_chunk(True, qf * _sigmoid(qf), zf_ref[rf, :], v_ref[rf, :].astype(F32), lbs[0], st_f,
                             code_f, tri_f)
        o_ref[rf, :] += of
        qb = q_ref[rk, :].astype(F32)
        ob, st_b = _hg_chunk(False, qb * _sigmoid(qb), zb_ref[rk, :], v_ref[rk, :].astype(F32), lbs[1], st_b,
                             code_b, tri_b)
        o_ref[rk, :] += ob
        return st_f, st_b

    zero = jnp.zeros((LANES, LANES), F32)
    lax.fori_loop(0, n_all, step, (zero, zero))


def _hg_call(u, uz, lb_logits, layer):
    t = u.shape[0]
    sb = _seg()
    col = lambda off: pl.BlockSpec((sb, LANES), lambda b, h: (b, off + h))
    return pl.pallas_call(
        functools.partial(_hg_kernel, layer=layer),
        grid=(BATCH, HG_HEADS),
        in_specs=[pl.BlockSpec((2, DEPTH, LANES), lambda b, h: (0, 0, h)),
                  col(_C_HQ), col(0), col(HG_HEADS), col(_C_HI)],
        out_specs=pl.BlockSpec((sb, LANES), lambda b, h: (b, h)),
        out_shape=jax.ShapeDtypeStruct((t, HG_W), F32),
        compiler_params=_params(("arbitrary", "arbitrary")),
        name="hgrn2",
    )(lb_logits, u, uz, uz, u)


def _lru_kernel(x_ref, wg_ref, bg_ref, lam_ref, cw_ref, cb_ref, y_ref,
                xp_scr, af_scr, bf_scr, ab_scr, bb_scr, hb_scr):
    sb = _seg()
    c_len = LRU_CHUNK
    halo = SUBLANES
    n_tiles = sb // SUBLANES
    n_lat_t = SEQ // SUBLANES

    xp_scr[pl.ds(0, halo), :] = jnp.zeros((halo, LANES), F32)
    xp_scr[pl.ds(halo + sb, halo), :] = jnp.zeros((halo, LANES), F32)
    xp_scr[pl.ds(halo, sb), :] = x_ref[...].astype(F32)

    lam = lam_ref[0]
    sp = jnp.maximum(-lam, 0.0) + jnp.log1p(jnp.exp(-jnp.abs(lam)))
    cw = cw_ref[...]
    wg = wg_ref[0]
    bg = bg_ref[0]
    r8 = lax.broadcasted_iota(jnp.int32, (c_len, 1), 0)
    rm8 = r8 % SUBLANES

    def tile_shift(x, s):
        x3 = x.reshape(c_len // SUBLANES, SUBLANES, LANES)
        return pltpu.roll(x3, s, 1).reshape(c_len, LANES)

    def coeff_chunk(c, _):
        base = pl.multiple_of(c * c_len, c_len)
        w = xp_scr[pl.ds(base, c_len + 2 * halo), :]
        grow = base + r8
        in_lat = grow < SEQ
        pos = jnp.where(in_lat, grow, grow - SEQ)
        seglen = jnp.where(in_lat, SEQ, CTX_LEN)
        wlen = c_len + 2 * halo
        xm2 = jnp.where(pos >= 2, pltpu.roll(w, 2, 0)[halo:halo + c_len], 0.0)
        xm1 = jnp.where(pos >= 1, pltpu.roll(w, 1, 0)[halo:halo + c_len], 0.0)
        x0 = w[halo:halo + c_len]
        xp1 = jnp.where(pos + 1 < seglen, pltpu.roll(w, wlen - 1, 0)[halo:halo + c_len], 0.0)
        xc = xm2 * cw[0:1] + xm1 * cw[1:2] + x0 * cw[2:3] + xp1 * cw[3:4] + cb_ref[...]
        gates = _sigmoid(_dot(xc.astype(BF16), wg) + bg)
        rows = pl.ds(base, c_len)
        for d, (a_scr, b_scr) in enumerate(((af_scr, bf_scr), (ab_scr, bb_scr))):
            rg = gates[:, (2 * d) * LANES:(2 * d + 1) * LANES]
            ig = gates[:, (2 * d + 1) * LANES:(2 * d + 2) * LANES]
            log_a = -LRU_C * rg * sp[d:d + 1]
            a = jnp.exp(log_a)
            th = jnp.tanh(log_a)
            bv = jnp.sqrt(-2.0 * th / (1.0 - th)) * (ig * xc)
            for sft in (1, 2, 4):
                if d == 0:
                    a_s = tile_shift(a, sft)
                    b_s = tile_shift(bv, sft)
                    ok = rm8 >= sft
                else:
                    a_s = tile_shift(a, SUBLANES - sft)
                    b_s = tile_shift(bv, SUBLANES - sft)
                    ok = rm8 < SUBLANES - sft
                bv = jnp.where(ok, a * b_s + bv, bv)
                a = jnp.where(ok, a * a_s, a)
            a_scr[rows, :] = a
            b_scr[rows, :] = bv
        return 0

    lax.fori_loop(0, sb // c_len, coeff_chunk, 0)

    def carry_step(s, carry):
        hf, hb = carry
        tf = jnp.where(s < n_tiles - n_lat_t, n_lat_t + s, s - (n_tiles - n_lat_t))
        tb = n_tiles - 1 - s
        rf = pl.ds(pl.multiple_of(tf * SUBLANES, SUBLANES), SUBLANES)
        rk = pl.ds(pl.multiple_of(tb * SUBLANES, SUBLANES), SUBLANES)
        h1 = af_scr[rf, :] * hf + bf_scr[rf, :]
        y_ref[rf, :] = h1
        h2 = ab_scr[rk, :] * hb + bb_scr[rk, :]
        hb_scr[rk, :] = h2
        return (jnp.broadcast_to(h1[SUBLANES - 1:SUBLANES, :], (SUBLANES, LANES)),
                jnp.broadcast_to(h2[0:1, :], (SUBLANES, LANES)))

    zero = jnp.zeros((SUBLANES, LANES), F32)
    lax.fori_loop(0, n_tiles, carry_step, (zero, zero))
    y_ref[...] += hb_scr[...]


def _lru_call(u, wg, bg, lam, cw, cb):
    t = u.shape[0]
    sb = _seg()
    nb = LRU_W // LANES
    return pl.pallas_call(
        _lru_kernel,
        grid=(BATCH, nb),
        in_specs=[pl.BlockSpec((sb, LANES), lambda b, c: (b, _C_LX + c)),
                  pl.BlockSpec((1, LANES, 4 * LANES), lambda b, c: (c, 0, 0)),
                  pl.BlockSpec((1, 1, 4 * LANES), lambda b, c: (c, 0, 0)),
                  pl.BlockSpec((1, 2, LANES), lambda b, c: (c, 0, 0)),
                  pl.BlockSpec((CONV_W, LANES), lambda b, c: (0, c)),
                  pl.BlockSpec((1, LANES), lambda b, c: (0, c))],
        out_specs=pl.BlockSpec((sb, LANES), lambda b, c: (b, c)),
        out_shape=jax.ShapeDtypeStruct((t, LRU_W), F32),
        scratch_shapes=[pltpu.VMEM((sb + 2 * SUBLANES, LANES), F32)] + [pltpu.VMEM((sb, LANES), F32)] * 5,
        compiler_params=_params(("arbitrary", "arbitrary")),
        name="rglru",
    )(u, wg, bg, lam, cw, cb)


def _group_rms(o, group):
    ri = lax.broadcasted_iota(jnp.int32, (LANES, LANES), 0) // group
    ci = lax.broadcasted_iota(jnp.int32, (LANES, LANES), 1) // group
    ones_bd = (ri == ci).astype(BF16)
    outs = []
    for tix in range(o.shape[1] // LANES):
        x = o[:, tix * LANES:(tix + 1) * LANES]
        sq = x * x
        hi = sq.astype(BF16)
        lo = (sq - hi.astype(F32)).astype(BF16)
        ss = _dot(hi, ones_bd) + _dot(lo, ones_bd)
        outs.append(x * lax.rsqrt(ss * (1.0 / group) + NORM_EPS))
    return jnp.concatenate(outs, axis=1)


def _merge_kernel(*refs, n_gate):
    (x_ref, ml_ref, mc_ref, oret_ref, ohg_ref, ylru_ref, rg_ref, hgt_ref, lgt_ref) = refs[:9]
    gate_refs = refs[9:9 + 3 * n_gate]
    (gng_ref, ghg_ref, wr_ref, wh_ref, wl_ref, wo_ref, n2g_ref, rw_ref, rb_ref,
     xo_ref, h2_ref, lg_ref) = refs[9 + 3 * n_gate:]
    tiles_per_b = _seg() // ROW_TILE
    is_ctx = (pl.program_id(0) % tiles_per_b) >= SEQ // ROW_TILE
    ml = ml_ref[0]
    mc = mc_ref[0]

    def gate(i):
        parts = [gate_refs[i * n_gate + j][...].astype(F32) for j in range(n_gate)]
        return _sigmoid(parts[0] if n_gate == 1 else jnp.concatenate(parts, axis=1))

    rg = rg_ref[...].astype(F32)
    ret_in = _group_rms(oret_ref[...], RET_DK) * gng_ref[...] * (rg * _sigmoid(rg))
    ret = _dot(ret_in.astype(BF16), wr_ref[0])
    hg_in = _group_rms(ohg_ref[...], HG_DK) * ghg_ref[...] * _sigmoid(hgt_ref[...].astype(F32))
    hgr = _dot(hg_in.astype(BF16), wh_ref[0])
    lgt = lgt_ref[...].astype(F32)
    gelu = 0.5 * lgt * (1.0 + jnp.tanh(0.7978845608028654 * (lgt + 0.044715 * lgt * lgt * lgt)))
    lru = _dot((ylru_ref[...] * gelu).astype(BF16), wl_ref[0])
    merged = gate(0) * ret + gate(1) * hgr + gate(2) * lru
    mix = _dot(merged.astype(BF16), wo_ref[0])
    g1 = jnp.where(is_ctx, mc[2:3, :], ml[2:3, :])
    xn = x_ref[...] + g1 * mix
    xo_ref[...] = xn
    h2 = _norm_modulate(xn, n2g_ref[...], ml, mc, is_ctx, 3)
    h2_ref[...] = h2.astype(BF16)
    lg_ref[...] = _dot3(h2, rw_ref[...]) + rb_ref[...]


def _merge_call(xs, mod, o_ret, o_hg, y_lru, u, gn_g, hgn_g, wr, wh, wl, wo, n2g, rw, rb, layer):
    t = xs.shape[0]
    r = ROW_TILE
    tiles_per_b = _seg() // r
    n_gate = D_MODEL // 512
    row = lambda w: pl.BlockSpec((r, w), lambda i: (i, 0))
    ucol = lambda blk: pl.BlockSpec((r, 512), lambda i: (i, blk))

    def full(a):
        if a.ndim == 3:
            return pl.BlockSpec((1,) + a.shape[1:], lambda i: (layer, 0, 0))
        return pl.BlockSpec(a.shape, lambda i: (0,) * a.ndim)
    gate_base = _C_GATES * LANES // 512
    in_specs = [row(D_MODEL),
                pl.BlockSpec((1, N_MOD, D_MODEL), lambda i: (i // tiles_per_b, 0, 0)),
                pl.BlockSpec((1, N_MOD, D_MODEL), lambda i: (BATCH, 0, 0)),
                row(RET_W), row(HG_W), row(LRU_W),
                ucol(_C_RG * LANES // 512), ucol(_C_HGT * LANES // 512), ucol(_C_LGT * LANES // 512)]
    in_specs += [ucol(gate_base + j) for j in range(3 * n_gate)]
    consts = [gn_g, hgn_g, wr, wh, wl, wo, n2g, rw, rb]
    in_specs += [full(a) for a in consts]
    return pl.pallas_call(
        functools.partial(_merge_kernel, n_gate=n_gate),
        grid=(t // r,),
        in_specs=in_specs,
        out_specs=[row(D_MODEL), row(D_MODEL), row(LANES)],
        out_shape=[jax.ShapeDtypeStruct((t, D_MODEL), F32), jax.ShapeDtypeStruct((t, D_MODEL), BF16),
                   jax.ShapeDtypeStruct((t, LANES), F32)],
        compiler_params=_params(("arbitrary",)),
        name="merge",
    )(xs, mod, mod, o_ret, o_hg, y_lru, *([u] * (3 + 3 * n_gate)), *consts)


def _route_kernel(lg_ref, oi_ref, ow_ref, lst_ref, meta_ref, cnt_ref, off_scr):
    @pl.when(pl.program_id(0) == 0)
    def _():
        off_scr[...] = jnp.zeros(off_scr.shape, F32)

    r = lg_ref.shape[0]
    lane = lax.broadcasted_iota(jnp.int32, (r, LANES), 1)
    neg = -jnp.inf
    l = jnp.where(lane < N_EXPERTS, lg_ref[...], neg)
    vals, hots = [], []
    lanef = lane.astype(F32)
    for _ in range(TOP_K):
        m = jnp.max(l, axis=-1, keepdims=True)
        idxf = jnp.min(jnp.where(l == m, lanef, float(LANES)), axis=-1, keepdims=True)
        hot = lanef == idxf
        l = jnp.where(hot, neg, l)
        vals.append(m)
        hots.append(hot)
    es = [jnp.exp(v - vals[0]) for v in vals]
    den = es[0]
    for e in es[1:]:
        den = den + e
    chosen = hots[0].astype(F32)
    for hot in hots[1:]:
        chosen = chosen + hot.astype(F32)
    ri = lax.broadcasted_iota(jnp.int32, (r, r), 0)
    ci = lax.broadcasted_iota(jnp.int32, (r, r), 1)
    before = _dot((ri > ci).astype(BF16), chosen.astype(BF16))
    count = jnp.sum(chosen, axis=0, keepdims=True)
    run = jnp.floor((count + (SUBLANES - 1.0)) * (1.0 / SUBLANES)) * SUBLANES
    li = lax.broadcasted_iota(jnp.int32, (LANES, LANES), 0)
    lj = lax.broadcasted_iota(jnp.int32, (LANES, LANES), 1)
    run8 = jnp.broadcast_to(run, (SUBLANES, LANES))
    start = _dot(run8.astype(BF16), (li < lj).astype(BF16))[0:1, :]
    pos = start + before
    oi = jnp.zeros((r, LANES), jnp.int32)
    ow = jnp.zeros((r, LANES), F32)
    lsf = jnp.zeros((r, LANES), F32)
    for k in range(TOP_K):
        row_k = jnp.sum(jnp.where(hots[k], pos, 0.0), axis=-1, keepdims=True)
        oi = jnp.where(lane == k, row_k.astype(jnp.int32), oi)
        lsf = jnp.where(lane == k, row_k, lsf)
        ow = jnp.where(lane == k, es[k] / den, ow)
    oi_ref[...] = oi
    ow_ref[...] = ow
    lst_ref[0] = jnp.transpose(lsf)[0:SUBLANES, :]
    mrow = lax.broadcasted_iota(jnp.int32, (SUBLANES, LANES), 0)
    meta_ref[0] = jnp.where(mrow == 0, run8, jnp.where(mrow == 1, jnp.broadcast_to(start, (SUBLANES, LANES)),
                                                       off_scr[...]))
    off_scr[...] = off_scr[...] + run
    cnt_ref[...] = off_scr[...]


def _route_call(logits):
    t = logits.shape[0]
    r = ROW_TILE
    row = pl.BlockSpec((r, LANES), lambda i: (i, 0))
    return pl.pallas_call(
        _route_kernel,
        grid=(t // r,),
        in_specs=[row],
        out_specs=[row, row, pl.BlockSpec((1, SUBLANES, r), lambda i: (i, 0, 0)),
                   pl.BlockSpec((1, SUBLANES, LANES), lambda i: (i, 0, 0)),
                   pl.BlockSpec((SUBLANES, LANES), lambda i: (0, 0))],
        out_shape=[jax.ShapeDtypeStruct((t, LANES), jnp.int32), jax.ShapeDtypeStruct((t, LANES), F32),
                   jax.ShapeDtypeStruct((t // r, SUBLANES, r), F32),
                   jax.ShapeDtypeStruct((t // r, SUBLANES, LANES), F32),
                   jax.ShapeDtypeStruct((SUBLANES, LANES), F32)],
        scratch_shapes=[pltpu.VMEM((SUBLANES, LANES), F32)],
        compiler_params=_params(("arbitrary",)),
        name="route",
    )(logits)


def _pack_pairs(x):
    m = x.shape[1] // 2
    bits = pltpu.bitcast(x, jnp.uint32)
    return bits[:, :m] | (bits[:, m:] >> 16)


def _unpack_pairs(w):
    hi = pltpu.bitcast(w & jnp.uint32(0xFFFF0000), F32).astype(BF16)
    lo = pltpu.bitcast(w << 16, F32).astype(BF16)
    return hi, lo


def _local_rows():
    rows = ROW_TILE * TOP_K + N_EXPERTS * (SUBLANES - 1)
    return -(-rows // ROW_TILE) * ROW_TILE


RUN_PIECE = 4 * SUBLANES


def _max_pieces():
    return _local_rows() // RUN_PIECE, N_EXPERTS * (RUN_PIECE // SUBLANES - 1)


def _meta_width():
    n_big, n_small = _max_pieces()
    return -(-(2 + 2 * n_big + 2 * n_small) // LANES) * LANES


def _piece_tables(first_slot, first_row, pieces):
    per_big = RUN_PIECE // SUBLANES
    max_big, max_small = _max_pieces()

    def flat(count, limit, offset_rows, step):
        ends = jnp.cumsum(count, axis=1)
        p = jnp.arange(limit, dtype=jnp.int32)[None, :, None]
        e = jnp.minimum(jnp.sum((ends[:, None, :] <= p).astype(jnp.int32), axis=2), N_EXPERTS - 1)
        take = lambda a: jnp.take_along_axis(a, e, axis=1)
        j = p[:, :, 0] - (take(ends) - take(count))
        off = take(offset_rows) + j * step
        return take(first_row) + off, take(first_slot) + off, ends[:, -1:]

    n_big = pieces // per_big
    big_row, big_slot, tot_big = flat(n_big, max_big, jnp.zeros_like(pieces), RUN_PIECE)
    small_row, small_slot, tot_small = flat(pieces - n_big * per_big, max_small, n_big * RUN_PIECE, SUBLANES)
    table = jnp.concatenate([tot_big, tot_small, big_row, big_slot, small_row, small_slot], axis=1)
    return jnp.pad(table, ((0, 0), (0, _meta_width() - table.shape[1])))[:, None, :]


def _run_copies(meta_ref, make_copy, do):
    max_big, max_small = _max_pieces()
    big0 = 2
    small0 = 2 + 2 * max_big

    def big(p, _):
        do(make_copy(pl.multiple_of(meta_ref[0, 0, big0 + p], SUBLANES),
                     pl.multiple_of(meta_ref[0, 0, big0 + max_big + p], SUBLANES), RUN_PIECE))
        return 0

    def small(p, _):
        do(make_copy(pl.multiple_of(meta_ref[0, 0, small0 + p], SUBLANES),
                     pl.multiple_of(meta_ref[0, 0, small0 + max_small + p], SUBLANES), SUBLANES))
        return 0

    lax.fori_loop(0, meta_ref[0, 0, 0], big, 0)
    lax.fori_loop(0, meta_ref[0, 0, 1], small, 0)


def _dispatch_kernel(meta_ref, prev_meta_ref, lst_ref, h_ref, xs_in_ref, xs_ref, perm_scr, sem):
    del xs_in_ref
    i = pl.program_id(0)
    cur = i % 2
    r = h_ref.shape[0]
    ls = lst_ref[0].astype(jnp.int32)
    hb = h_ref[...]
    for s in range(perm_scr.shape[1] // r):
        rows = s * r + lax.broadcasted_iota(jnp.int32, (r, r), 0)
        sel = rows == ls[0:1, :]
        for k in range(1, TOP_K):
            sel = jnp.logical_or(sel, rows == ls[k:k + 1, :])
        perm_scr[cur, pl.ds(s * r, r), :] = _pack_pairs(_dot(jnp.where(sel, 1.0, 0.0).astype(BF16), hb))

    def copies_from(buf):
        def make_copy(row, slot, n):
            return pltpu.make_async_copy(perm_scr.at[buf, pl.ds(row, n)], xs_ref.at[pl.ds(slot, n)], sem.at[buf])
        return make_copy

    _run_copies(meta_ref, copies_from(cur), lambda c: c.start())

    @pl.when(i > 0)
    def _():
        _run_copies(prev_meta_ref, copies_from(1 - cur), lambda c: c.wait())

    @pl.when(i == pl.num_programs(0) - 1)
    def _():
        _run_copies(meta_ref, copies_from(cur), lambda c: c.wait())


def _dispatch_call(h2, meta, lst, slots):
    t = h2.shape[0]
    r = ROW_TILE
    return pl.pallas_call(
        _dispatch_kernel,
        grid=(t // r,),
        in_specs=[pl.BlockSpec((1, 1, _meta_width()), lambda i: (i, 0, 0), memory_space=pltpu.SMEM),
                  pl.BlockSpec((1, 1, _meta_width()), lambda i: (jnp.maximum(i - 1, 0), 0, 0), memory_space=pltpu.SMEM),
                  pl.BlockSpec((1, SUBLANES, r), lambda i: (i, 0, 0)),
                  pl.BlockSpec((r, D_MODEL), lambda i: (i, 0)),
                  pl.BlockSpec(memory_space=pl.ANY)],
        out_specs=pl.BlockSpec(memory_space=pl.ANY),
        out_shape=jax.ShapeDtypeStruct(slots.shape, jnp.uint32),
        scratch_shapes=[pltpu.VMEM((2, _local_rows(), D_MODEL // 2), jnp.uint32), pltpu.SemaphoreType.DMA((2,))],
        input_output_aliases={4: 0},
        compiler_params=_params(("arbitrary",)),
        name="dispatch",
    )(meta, meta, lst, h2, slots)


def _expert_kernel(bexp_ref, nv_ref, x_ref, wgu_ref, bgu_ref, wd_ref, bd_ref, y_ref, wgu_scr, wd_scr):
    i = pl.program_id(0)
    f = D_MODEL
    prev = bexp_ref[jnp.maximum(i - 1, 0)]
    new_expert = jnp.logical_or(i == 0, bexp_ref[i] != prev)

    @pl.when(jnp.logical_and(i < nv_ref[0], new_expert))
    def _():
        slab = 256
        for s in range(D_MODEL // slab):
            wgu_scr[pl.ds(s * slab, slab), :] = wgu_ref[0, 0, pl.ds(s * slab, slab), :].astype(BF16)
            wd_scr[pl.ds(s * slab, slab), :] = wd_ref[0, 0, pl.ds(s * slab, slab), :].astype(BF16)

    @pl.when(i < nv_ref[0])
    def _():
        x_hi, x_lo = _unpack_pairs(x_ref[...])
        half = D_MODEL // 2
        gu = _dot(x_hi, wgu_scr[:half, :]) + _dot(x_lo, wgu_scr[half:, :]) + bgu_ref[0, 0]
        gate = jnp.minimum(gu[:, :f], SWIGLU_LIMIT)
        up = jnp.clip(gu[:, f:], -SWIGLU_LIMIT, SWIGLU_LIMIT)
        glu = gate * _sigmoid(SWIGLU_ALPHA * gate)
        y = _dot(((up + 1.0) * glu).astype(BF16), wd_scr[...]) + bd_ref[0, 0]
        y_ref[...] = _pack_pairs(y.astype(BF16).astype(F32))

    @pl.when(i >= nv_ref[0])
    def _():
        y_ref[...] = jnp.zeros(y_ref.shape, jnp.uint32)


def _expert_call(slots, bexp, nvalid, w_gu, b_gu, w_down, b_down, layer):
    s = slots.shape[0]
    nb = s // MOE_BLOCK
    f = D_MODEL
    blk = lambda i, be, nv: (jnp.minimum(i, nv[0] - 1), 0)
    exp3 = lambda i, be, nv: (layer, be[jnp.minimum(i, nv[0] - 1)], 0, 0)
    grid_spec = pltpu.PrefetchScalarGridSpec(
        num_scalar_prefetch=2,
        grid=(nb,),
        in_specs=[pl.BlockSpec((MOE_BLOCK, D_MODEL // 2), blk),
                  pl.BlockSpec((1, 1, D_MODEL, 2 * f), exp3),
                  pl.BlockSpec((1, 1, 1, 2 * f), exp3),
                  pl.BlockSpec((1, 1, f, D_MODEL), exp3),
                  pl.BlockSpec((1, 1, 1, D_MODEL), exp3)],
        out_specs=pl.BlockSpec((MOE_BLOCK, D_MODEL // 2), lambda i, be, nv: (i, 0)),
        scratch_shapes=[pltpu.VMEM((D_MODEL, 2 * f), BF16), pltpu.VMEM((f, D_MODEL), BF16)],
    )
    return pl.pallas_call(
        _expert_kernel,
        grid_spec=grid_spec,
        out_shape=jax.ShapeDtypeStruct((s, D_MODEL // 2), jnp.uint32),
        compiler_params=_params(("arbitrary",)),
        name="experts",
    )(bexp, nvalid, slots, w_gu, b_gu.reshape(DEPTH, N_EXPERTS, 1, 2 * f), w_down,
      b_down.reshape(DEPTH, N_EXPERTS, 1, D_MODEL))


def _combine_kernel(meta_ref, next_meta_ref, x_ref, ml_ref, mc_ref, oi_ref, w_ref, fg_ref, y_ref, o_ref, buf, sem,
                    *, last, n_t):
    r = x_ref.shape[0]
    i = pl.program_id(0)
    cur = i % 2

    def copies_into(b):
        def make_copy(row, slot, n):
            return pltpu.make_async_copy(y_ref.at[pl.ds(slot, n)], buf.at[b, pl.ds(row, n)], sem.at[b])
        return make_copy

    @pl.when(i == 0)
    def _():
        buf[...] = jnp.zeros(buf.shape, jnp.uint32)
        _run_copies(meta_ref, copies_into(0), lambda c: c.start())

    @pl.when(i + 1 < pl.num_programs(0))
    def _():
        _run_copies(next_meta_ref, copies_into(1 - cur), lambda c: c.start())

    _run_copies(meta_ref, copies_into(cur), lambda c: c.wait())

    if last:
        is_ctx = False
    else:
        is_ctx = (i % n_t) >= SEQ // ROW_TILE
    w = w_ref[...]
    oi = oi_ref[...]
    ff_hi = jnp.zeros((r, D_MODEL // 2), F32)
    ff_lo = jnp.zeros((r, D_MODEL // 2), F32)
    for s in range(buf.shape[1] // r):
        cols = s * r + lax.broadcasted_iota(jnp.int32, (r, r), 1)
        wm = jnp.zeros((r, r), F32)
        for k in range(TOP_K):
            wm = jnp.where(cols == oi[:, k:k + 1], w[:, k:k + 1], wm)
        y_hi, y_lo = _unpack_pairs(buf[cur, pl.ds(s * r, r), :])
        ff_hi = ff_hi + _dot(wm.astype(BF16), y_hi)
        ff_lo = ff_lo + _dot(wm.astype(BF16), y_lo)
    ff = jnp.concatenate([ff_hi, ff_lo], axis=1)
    g2 = jnp.where(is_ctx, mc_ref[0][5:6, :], ml_ref[0][5:6, :])
    xn = x_ref[...] + g2 * ff
    if last:
        ms = jnp.mean(xn * xn, axis=-1, keepdims=True)
        xn = xn * lax.rsqrt(ms + NORM_EPS) * fg_ref[...]
    o_ref[...] = xn


def _combine_call(xs, mod, meta, rows, wts, y_slots, final_g, last):
    t = xs.shape[0]
    r = ROW_TILE
    tiles_per_b = _seg() // r
    n_t = SEQ // r if last else tiles_per_b
    n_steps = BATCH * n_t
    tile = lambda i: (i // n_t) * tiles_per_b + i % n_t
    out_rows = BATCH * SEQ if last else t
    return pl.pallas_call(
        functools.partial(_combine_kernel, last=last, n_t=n_t),
        grid=(n_steps,),
        in_specs=[pl.BlockSpec((1, 1, _meta_width()), lambda i: (tile(i), 0, 0), memory_space=pltpu.SMEM),
                  pl.BlockSpec((1, 1, _meta_width()), lambda i: (tile(jnp.minimum(i + 1, n_steps - 1)), 0, 0),
                               memory_space=pltpu.SMEM),
                  pl.BlockSpec((r, D_MODEL), lambda i: (tile(i), 0)),
                  pl.BlockSpec((1, N_MOD, D_MODEL), lambda i: (i // n_t, 0, 0)),
                  pl.BlockSpec((1, N_MOD, D_MODEL), lambda i: (BATCH, 0, 0)),
                  pl.BlockSpec((r, LANES), lambda i: (tile(i), 0)),
                  pl.BlockSpec((r, LANES), lambda i: (tile(i), 0)),
                  pl.BlockSpec((1, D_MODEL), lambda i: (0, 0)),
                  pl.BlockSpec(memory_space=pl.ANY)],
        out_specs=pl.BlockSpec((r, D_MODEL), lambda i: (i, 0)),
        out_shape=jax.ShapeDtypeStruct((out_rows, D_MODEL), F32),
        scratch_shapes=[pltpu.VMEM((2, _local_rows(), D_MODEL // 2), jnp.uint32), pltpu.SemaphoreType.DMA((2,))],
        compiler_params=_params(("arbitrary",)),
        name="combine",
    )(meta, meta, xs, mod, mod, rows, wts, final_g, y_slots)


def _rotary_tables():
    rows = SEQ // GRID_W
    row = jnp.repeat(jnp.arange(rows, dtype=F32), GRID_W)
    colp = jnp.tile(jnp.arange(GRID_W, dtype=F32), rows)
    n_freq = RET_DK // 4
    inv = ROPE_BASE ** (-jnp.arange(n_freq, dtype=F32) / n_freq)
    ang = jnp.concatenate([row[:, None] * inv, colp[:, None] * inv], axis=-1)
    cos, sin = jnp.cos(ang), jnp.sin(ang)
    reps = LANES // (RET_DK // 2)
    cos_t = jnp.tile(cos, (1, reps))
    sign = jnp.where((jnp.arange(LANES) % RET_DK) < RET_DK // 2, -1.0, 1.0).astype(F32)
    sin_t = jnp.tile(sin, (1, reps)) * sign
    return cos_t, sin_t


def _lru_gate_weights(wa, ba, wx, bx):
    nb = LRU_W // LANES
    per = LANES // LRU_BD
    eye = jnp.eye(per, dtype=F32)

    def embed(w):
        w = w.reshape(nb, per, LRU_BD, LRU_BD)
        return jnp.einsum('cpij,pq->cpiqj', w, eye).reshape(nb, LANES, LANES)

    wg = jnp.concatenate([embed(wa[0]), embed(wx[0]), embed(wa[1]), embed(wx[1])], axis=-1)
    bias = jnp.concatenate([ba[0].reshape(nb, 1, LANES), bx[0].reshape(nb, 1, LANES),
                            ba[1].reshape(nb, 1, LANES), bx[1].reshape(nb, 1, LANES)], axis=-1)
    return wg.astype(BF16), bias


def kernel(x, c, ctx, c_ctx, mod_w, mod_b, norm1_g, norm2_g, final_g, w_in, ret_decay, ret_gn_g, w_ret_o,
           hgrn_lb_logits, hgrn_gn_g, w_hgrn_o, lru_conv_w, lru_conv_b, lru_wa, lru_ba, lru_wx, lru_bx,
           lru_lambda, w_lru_o, w_out, router_w, router_b, exp_w_gu, exp_b_gu, exp_w_down, exp_b_down):
    assert CTX_LEN == RET_CHUNK and SEQ % RET_CHUNK == 0
    sb = _seg()
    t = BATCH * sb
    xs = jnp.concatenate([x, ctx], axis=1).reshape(t, D_MODEL)

    cvec = jnp.concatenate([c, c_ctx[None, :], jnp.zeros((SUBLANES - BATCH - 1, D_MODEL), F32)], axis=0)
    mod_all = _mod_call(cvec, mod_w, mod_b).reshape(DEPTH, SUBLANES, N_MOD, D_MODEL)

    cos_t, sin_t = _rotary_tables()
    lb_logits = jnp.transpose(hgrn_lb_logits, (1, 0, 2))
    n_slots = t * TOP_K + (t // ROW_TILE) * N_EXPERTS * (SUBLANES - 1) + N_EXPERTS * (MOE_BLOCK - 1)
    n_blocks = -(-n_slots // MOE_BLOCK)
    n_slots = n_blocks * MOE_BLOCK
    slots = jnp.zeros((n_slots, D_MODEL // 2), jnp.uint32)
    rw_pad = jnp.pad(router_w, ((0, 0), (0, 0), (0, LANES - N_EXPERTS)))
    rb_pad = jnp.pad(router_b, ((0, 0), (0, LANES - N_EXPERTS)))
    w_in_b, w_ret_b, w_hg_b, w_lru_b, w_out_b = (w.astype(BF16) for w in (w_in, w_ret_o, w_hgrn_o, w_lru_o, w_out))
    block_first_slot = jnp.arange(n_blocks, dtype=jnp.int32) * MOE_BLOCK

    out = None
    for l in range(DEPTH):
        last = l == DEPTH - 1
        mod = mod_all[l]
        u, uz = _inproj_call(xs, mod, norm1_g[l][None, :], w_in_b, l)

        dec = jnp.broadcast_to(ret_decay[l].reshape(2, RET_HEADS // 2, 2, 1).transpose(1, 0, 2, 3),
                               (RET_HEADS // 2, 2, 2, LANES)).reshape(RET_HEADS // 2, 4, LANES)
        o_ret = _ret_call(u, dec, cos_t, sin_t)
        o_hg = _hg_call(u, uz, lb_logits, l)
        wg, bg = _lru_gate_weights(lru_wa[l], lru_ba[l], lru_wx[l], lru_bx[l])
        lam = lru_lambda[l].reshape(2, LRU_W // LANES, LANES).transpose(1, 0, 2)
        y_lru = _lru_call(u, wg, bg, lam, lru_conv_w[l], lru_conv_b[l][None, :])

        xs, h2, logits = _merge_call(
            xs, mod, o_ret, o_hg, y_lru, u, ret_gn_g[l][None, :], hgrn_gn_g[l][None, :],
            w_ret_b, w_hg_b, w_lru_b, w_out_b, norm2_g[l][None, :], rw_pad[l], rb_pad[l][None, :], l)

        rows, wts, lst, tile_meta, cnt = _route_call(logits)
        run = tile_meta[:, 0, :N_EXPERTS].astype(jnp.int32)
        first_row = tile_meta[:, 1, :N_EXPERTS].astype(jnp.int32)
        earlier = tile_meta[:, 2, :N_EXPERTS].astype(jnp.int32)
        counts = cnt[0, :N_EXPERTS].astype(jnp.int32)
        padded = (counts + MOE_BLOCK - 1) // MOE_BLOCK * MOE_BLOCK
        pad_end = jnp.cumsum(padded)
        first_slot = (pad_end - padded)[None, :] + earlier
        meta = _piece_tables(first_slot, first_row, run // SUBLANES)
        bexp = jnp.minimum(jnp.sum((pad_end[None, :] <= block_first_slot[:, None]).astype(jnp.int32), axis=1),
                           N_EXPERTS - 1)
        nvalid = (pad_end[-1:] // MOE_BLOCK).astype(jnp.int32)

        slots = _dispatch_call(h2, meta, lst, slots)
        y_slots = _expert_call(slots, bexp, nvalid, exp_w_gu, exp_b_gu, exp_w_down, exp_b_down, l)
        res = _combine_call(xs, mod, meta, rows, wts, y_slots, final_g[None, :], last)
        if last:
            out = res.reshape(BATCH, SEQ, D_MODEL)
        else:
            xs = res
    return out
```

```python
import functools

import jax
import jax.numpy as jnp
from jax import lax
from jax.experimental import pallas as pl
from jax.experimental.pallas import tpu as pltpu

F32 = jnp.float32
BF16 = jnp.bfloat16

D_MODEL = 1024
BATCH = 4
SEQ = 4096
DEPTH = 4
GRID_W = 64
CTX_LEN = 256
N_MOD = 6
NORM_EPS = 1e-6
RET_HEADS = 8
RET_DK = 64
RET_W = 512
ROPE_BASE = 10000.0
HG_HEADS = 4
HG_DK = 128
HG_W = 512
LRU_W = 512
LRU_BLOCKS = 8
LRU_BD = 64
LRU_C = 8.0
CONV_W = 4
N_EXPERTS = 32
TOP_K = 4
SWIGLU_ALPHA = 1.702
SWIGLU_LIMIT = 7.0

LANES = 128
SUBLANES = 8
RET_CHUNK = 256
HG_CHUNK = 256
HG_BLOCK = 128
HG_SUB = 4
LOG2_E = 1.4426950408889634
LRU_CHUNK = 256
MOE_BLOCK = 256
ROW_TILE = 256
ROUTE_TILE = 512
MOD_TN = 1536
VMEM_LIMIT = 56 * 1024 * 1024

_C_RQ, _C_RK, _C_RV, _C_RG = 0, 4, 8, 12
_C_HQ, _C_HFF, _C_HFB, _C_HI, _C_HGT = 16, 20, 24, 28, 32
_C_LX, _C_LGT = 36, 40
_C_GATES = 44


def _seg():
    return SEQ + CTX_LEN


def _in_cols():
    return _C_GATES * LANES + 3 * D_MODEL


def _dot(a, b):
    return jnp.dot(a, b, preferred_element_type=F32)


def _dot_nt(a, b):
    return lax.dot_general(a, b, (((1,), (1,)), ((), ())), preferred_element_type=F32)


def _dot_tn(a, b):
    return lax.dot_general(a, b, (((0,), (0,)), ((), ())), preferred_element_type=F32)


def _split3(x):
    hi = x.astype(BF16)
    r = x - hi.astype(F32)
    mid = r.astype(BF16)
    lo = (r - mid.astype(F32)).astype(BF16)
    return hi, mid, lo


def _dot_exact_lhs(a, x):
    hi, mid, lo = _split3(x)
    return _dot(a, hi) + _dot(a, mid) + _dot(a, lo)


def _dot_exact_rhs(x, a):
    hi, mid, lo = _split3(x)
    return _dot(hi, a) + _dot(mid, a) + _dot(lo, a)


def _dot3(a, b):
    ah, am, _ = _split3(a)
    bh, bm, _ = _split3(b)
    return _dot(ah, bh) + (_dot(ah, bm) + _dot(am, bh))


def _sigmoid(x):
    return 0.5 + 0.5 * jnp.tanh(0.5 * x)


def _log_sigmoid(x):
    return jnp.minimum(x, 0.0) - jnp.log1p(jnp.exp(-jnp.abs(x)))


def _params(sem):
    return pltpu.CompilerParams(dimension_semantics=sem, vmem_limit_bytes=VMEM_LIMIT)


def _mod_kernel(c_ref, w_ref, b_ref, o_ref):
    c = c_ref[...]
    s = c * _sigmoid(c)
    o_ref[0] = _dot3(s, w_ref[0]) + b_ref[0]


def _mod_call(cvec, mod_w, mod_b):
    n6 = N_MOD * D_MODEL
    tn = MOD_TN
    return pl.pallas_call(
        _mod_kernel,
        grid=(DEPTH, n6 // tn),
        in_specs=[pl.BlockSpec((SUBLANES, D_MODEL), lambda l, j: (0, 0)),
                  pl.BlockSpec((1, D_MODEL, tn), lambda l, j: (l, 0, j)),
                  pl.BlockSpec((1, 1, tn), lambda l, j: (l, 0, j))],
        out_specs=pl.BlockSpec((1, SUBLANES, tn), lambda l, j: (l, 0, j)),
        out_shape=jax.ShapeDtypeStruct((DEPTH, SUBLANES, n6), F32),
        compiler_params=_params(("arbitrary", "arbitrary")),
        name="modulation",
    )(cvec, mod_w, mod_b.reshape(DEPTH, 1, n6))


def _norm_modulate(x, g, mod_l, mod_c, is_ctx, first):
    ms = jnp.mean(x * x, axis=-1, keepdims=True)
    y = x * lax.rsqrt(ms + NORM_EPS) * g
    sh = jnp.where(is_ctx, mod_c[first:first + 1, :], mod_l[first:first + 1, :])
    sc = jnp.where(is_ctx, mod_c[first + 1:first + 2, :], mod_l[first + 1:first + 2, :])
    return y * (1.0 + sc) + sh


def _inproj_kernel(x_ref, ml_ref, mc_ref, g_ref, w_ref, o_ref, oz_ref, h_scr, *, tm, z_first, z_count):
    j = pl.program_id(1)

    @pl.when(j == 0)
    def _():
        tiles_per_b = _seg() // tm
        row0 = (pl.program_id(0) % tiles_per_b) * tm
        rows = row0 + lax.broadcasted_iota(jnp.int32, (tm, 1), 0)
        h = _norm_modulate(x_ref[...], g_ref[...], ml_ref[0], mc_ref[0], rows >= SEQ, 0)
        h_scr[...] = h.astype(BF16)

    acc = _dot(h_scr[...], w_ref[0])
    o_ref[...] = acc.astype(BF16)

    @pl.when(jnp.logical_and(j >= z_first, j < z_first + z_count))
    def _():
        oz_ref[...] = acc


def _inproj_call(xs, mod, g, w, layer):
    t = xs.shape[0]
    tm = _seg() // 2
    ncol = _in_cols()
    tn = HG_W
    tiles_per_b = _seg() // tm
    z_first = _C_HFF * LANES // tn
    z_count = 2
    return pl.pallas_call(
        functools.partial(_inproj_kernel, tm=tm, z_first=z_first, z_count=z_count),
        grid=(t // tm, ncol // tn),
        in_specs=[pl.BlockSpec((tm, D_MODEL), lambda i, j: (i, 0)),
                  pl.BlockSpec((1, N_MOD, D_MODEL), lambda i, j: (i // tiles_per_b, 0, 0)),
                  pl.BlockSpec((1, N_MOD, D_MODEL), lambda i, j: (BATCH, 0, 0)),
                  pl.BlockSpec((1, D_MODEL), lambda i, j: (0, 0)),
                  pl.BlockSpec((1, D_MODEL, tn), lambda i, j: (layer, 0, j))],
        out_specs=[pl.BlockSpec((tm, tn), lambda i, j: (i, j)),
                   pl.BlockSpec((tm, tn), lambda i, j: (i, jnp.clip(j - z_first, 0, z_count - 1)))],
        out_shape=[jax.ShapeDtypeStruct((t, ncol), BF16), jax.ShapeDtypeStruct((t, z_count * tn), F32)],
        scratch_shapes=[pltpu.VMEM((tm, D_MODEL), BF16)],
        compiler_params=_params(("arbitrary", "arbitrary")),
        name="in_projection",
    )(xs, mod, mod, g, w)


def _ret_kernel(dec_ref, cos_ref, sin_ref, q_ref, k_ref, v_ref, o_ref, mask_scr, pw_scr):
    c_len = RET_CHUNK
    n_lat = SEQ // c_len
    n_all = _seg() // c_len
    lane = lax.broadcasted_iota(jnp.int32, (1, LANES), 1)
    head0 = lane < RET_DK
    first_half = (lane % RET_DK) < (RET_DK // 2)

    lg = _log_sigmoid(dec_ref[0])
    dvf = jnp.where(head0, lg[0:1], lg[1:2])
    dvb = jnp.where(head0, lg[2:3], lg[3:4])
    r = lax.broadcasted_iota(jnp.int32, (c_len, 1), 0).astype(F32)
    pw_scr[0] = jnp.exp((r + 1.0) * dvf)
    pw_scr[1] = jnp.exp((c_len - 1.0 - r) * dvf)
    pw_scr[2] = jnp.exp((c_len - r) * dvb)
    pw_scr[3] = jnp.exp(r * dvb)
    dcf = jnp.exp(c_len * dvf)
    dcb = jnp.exp(c_len * dvb)
    ri = lax.broadcasted_iota(jnp.int32, (c_len, c_len), 0)
    ci = lax.broadcasted_iota(jnp.int32, (c_len, c_len), 1)
    dm = (ri - ci).astype(F32)
    for h in range(2):
        lgf = jnp.concatenate([lg[h:h + 1]] * (c_len // LANES), axis=1)
        lgb = jnp.concatenate([lg[2 + h:3 + h]] * (c_len // LANES), axis=1)
        mask_scr[h] = jnp.exp(jnp.where(dm >= 0, dm * lgf, -dm * lgb))
    rb = lax.broadcasted_iota(jnp.int32, (LANES, LANES), 0) < RET_DK
    cb = lax.broadcasted_iota(jnp.int32, (LANES, LANES), 1) < RET_DK
    same_head = (rb == cb).astype(F32)

    def load(c):
        rows = pl.ds(pl.multiple_of(c * c_len, c_len), c_len)
        q = q_ref[rows, :].astype(F32)
        k = k_ref[rows, :].astype(F32) * (RET_DK ** -0.5)
        v = v_ref[rows, :].astype(F32)
        return rows, q, k, v

    def rotary(t, c):
        rows = pl.ds(pl.multiple_of(c * c_len, c_len), c_len)
        partner = jnp.where(first_half, pltpu.roll(t, LANES - RET_DK // 2, 1), pltpu.roll(t, RET_DK // 2, 1))
        return t * cos_ref[rows, :] + partner * sin_ref[rows, :]

    def load_rot(c):
        rows, q, k, v = load(c)
        is_lat = c < n_lat
        cc = jnp.minimum(c, n_lat - 1)
        q = jnp.where(is_lat, rotary(q, cc), q)
        k = jnp.where(is_lat, rotary(k, cc), k)
        return rows, q, k, v

    def fwd_step(s, st):
        c = jnp.where(s < n_all - n_lat, n_lat + s, s - (n_all - n_lat))
        rows, q, k, v = load_rot(c)
        kb = k.astype(BF16)
        vb = v.astype(BF16)
        o0 = _dot((_dot_nt(jnp.where(head0, q, 0.0).astype(BF16), kb) * mask_scr[0]).astype(BF16), vb)
        o1 = _dot((_dot_nt(jnp.where(head0, 0.0, q).astype(BF16), kb) * mask_scr[1]).astype(BF16), vb)
        o = jnp.where(head0, o0, o1)
        o = o + _dot_nt((q * pw_scr[0]).astype(BF16), st.astype(BF16))
        o_ref[rows, :] += o
        return st * dcf + _dot_tn(vb, (k * pw_scr[1]).astype(BF16)) * same_head

    def bwd_step(s, st):
        c = n_all - 1 - s
        rows, q, k, v = load_rot(c)
        o_ref[rows, :] += _dot_nt((q * pw_scr[2]).astype(BF16), st.astype(BF16))
        return st * dcb + _dot_tn(v.astype(BF16), (k * pw_scr[3]).astype(BF16)) * same_head

    o_ref[...] = jnp.zeros(o_ref.shape, F32)
    zero = jnp.zeros((LANES, LANES), F32)
    lax.fori_loop(0, n_all, lambda s, sts: (fwd_step(s, sts[0]), bwd_step(s, sts[1])), (zero, zero))


def _ret_call(u, dec, cos_t, sin_t):
    t = u.shape[0]
    sb = _seg()
    col = lambda off: pl.BlockSpec((sb, LANES), lambda b, p: (b, off + p))
    return pl.pallas_call(
        _ret_kernel,
        grid=(BATCH, RET_HEADS // 2),
        in_specs=[pl.BlockSpec((1, 4, LANES), lambda b, p: (p, 0, 0)),
                  pl.BlockSpec((SEQ, LANES), lambda b, p: (0, 0)),
                  pl.BlockSpec((SEQ, LANES), lambda b, p: (0, 0)),
                  col(_C_RQ), col(_C_RK), col(_C_RV)],
        out_specs=pl.BlockSpec((sb, LANES), lambda b, p: (b, p)),
        out_shape=jax.ShapeDtypeStruct((t, RET_W), F32),
        scratch_shapes=[pltpu.VMEM((2, RET_CHUNK, RET_CHUNK), F32),
                        pltpu.VMEM((4, RET_CHUNK, LANES), F32)],
        compiler_params=_params(("arbitrary", "arbitrary")),
        name="retention",
    )(dec, cos_t, sin_t, u, u, u)


def _hg_codes(fwd):
    c_len = HG_BLOCK
    ri = lax.broadcasted_iota(jnp.int32, (c_len, c_len), 0)
    ci = lax.broadcasted_iota(jnp.int32, (c_len, c_len), 1)
    dist = (ri - ci) if fwd else (ci - ri)
    code = jnp.full((c_len, c_len), -1, jnp.int32)
    half, level = c_len // 2, 0
    while half >= HG_SUB:
        sh = half.bit_length() - 1
        same_seg = jnp.right_shift(ri, sh + 1) == jnp.right_shift(ci, sh + 1)
        other_half = jnp.right_shift(ri, sh) != jnp.right_shift(ci, sh)
        code = jnp.where(same_seg & other_half & (dist > 0), level, code)
        half, level = half // 2, level + 1
    sub_sh = HG_SUB.bit_length() - 1
    same_sub = jnp.right_shift(ri, sub_sh) == jnp.right_shift(ci, sub_sh)
    for dl in range(HG_SUB):
        code = jnp.where(same_sub & (dist == dl), 100 + dl, code)
    return code


def _hg_chunk(fwd, q, z, v, lb, st, code, tri):
    c_len = HG_CHUNK

    a = jnp.log(lb)
    b = jnp.log1p(-lb) + _log_sigmoid(z)
    lf = jnp.maximum(a, b) + jnp.log1p(jnp.exp(-jnp.abs(a - b)))
    kin = (1.0 - lb) * _sigmoid(-z)

    g = _dot_exact_lhs(tri, lf) * LOG2_E
    g_end = g[c_len - 1:c_len, :] if fwd else g[0:1, :]
    vb = v.astype(BF16)

    o = _dot_nt((q * jnp.exp2(g)).astype(BF16), st.astype(BF16))

    blk = HG_BLOCK
    lo, hi = slice(0, blk), slice(blk, 2 * blk)
    qrows, krows = (hi, lo) if fwd else (lo, hi)
    bnd = g[blk - 1:blk, :] if fwd else g[blk:blk + 1, :]
    qs = (q[qrows] * jnp.exp2(g[qrows] - bnd)).astype(BF16)
    ks = (kin[krows] * jnp.exp2(bnd - g[krows])).astype(BF16)
    cross = _dot(_dot_nt(qs, ks).astype(BF16), vb[krows])

    parts = []
    for rows in (lo, hi):
        parts.append(_hg_block(fwd, q[rows], kin[rows], g[rows], vb[rows], code))
    parts[1 if fwd else 0] = parts[1 if fwd else 0] + cross
    o = o + jnp.concatenate(parts, axis=0)
    st_new = st * jnp.exp2(g_end) + _dot_tn(vb, (kin * jnp.exp2(g_end - g)).astype(BF16))
    return o, st_new


def _hg_block(fwd, q, kin, g, vb, code):
    n = HG_BLOCK

    scores = jnp.zeros((n, n), F32)
    half, level = n // 2, 0
    while half >= HG_SUB:
        seg = 2 * half
        g3 = g.reshape(n // seg, seg, LANES)
        e = half - 1 if fwd else half
        bsel = jnp.broadcast_to(g3[:, e:e + 1, :], g3.shape).reshape(n, LANES)
        qs = (q * jnp.exp2(g - bsel)).astype(BF16)
        ks = (kin * jnp.exp2(bsel - g)).astype(BF16)
        scores = jnp.where(code == level, _dot_nt(qs, ks), scores)
        half, level = half // 2, level + 1

    def shift(x, dl):
        x3 = x.reshape(n // SUBLANES, SUBLANES, LANES)
        return pltpu.roll(x3, dl if fwd else SUBLANES - dl, 1).reshape(n, LANES)

    for dl in range(HG_SUB):
        w = q * kin if dl == 0 else q * shift(kin, dl) * jnp.exp2(g - shift(g, dl))
        scores = jnp.where(code == 100 + dl, jnp.sum(w, axis=-1, keepdims=True), scores)

    return _dot(scores.astype(BF16), vb)


def _hg_kernel(lbl_ref, q_ref, zf_ref, zb_ref, v_ref, o_ref, *, layer):
    c_len = HG_CHUNK
    n_lat = SEQ // c_len
    n_all = _seg() // c_len

    lbs = []
    for d in range(2):
        x = lbl_ref[d]
        e = jnp.exp(x - jnp.max(x, axis=0, keepdims=True))
        tot = jnp.sum(e, axis=0, keepdims=True)
        part = jnp.zeros_like(tot)
        for i in range(1, layer + 1):
            part = part + e[i:i + 1, :]
        lbs.append(part / tot)

    ri = lax.broadcasted_iota(jnp.int32, (c_len, c_len), 0)
    ci = lax.broadcasted_iota(jnp.int32, (c_len, c_len), 1)
    tri_f = (ri >= ci).astype(BF16)
    tri_b = (ri <= ci).astype(BF16)
    code_f = _hg_codes(True)
    code_b = _hg_codes(False)

    o_ref[...] = jnp.zeros(o_ref.shape, F32)

    def step(s, carry):
        st_f, st_b = carry
        cf = jnp.where(s < n_all - n_lat, n_lat + s, s - (n_all - n_lat))
        cbk = n_all - 1 - s
        rf = pl.ds(pl.multiple_of(cf * c_len, c_len), c_len)
        rk = pl.ds(pl.multiple_of(cbk * c_len, c_len), c_len)
        qf = q_ref[rf, :].astype(F32)
        of, st_f = _hg_chunk(True, qf * _sigmoid(qf), zf_ref[rf, :], v_ref[rf, :].astype(F32), lbs[0], st_f,
                             code_f, tri_f)
        o_ref[rf, :] += of
        qb = q_ref[rk, :].astype(F32)
        ob, st_b = _hg_chunk(False, qb * _sigmoid(qb), zb_ref[rk, :], v_ref[rk, :].astype(F32), lbs[1], st_b,
                             code_b, tri_b)
        o_ref[rk, :] += ob
        return st_f, st_b

    zero = jnp.zeros((LANES, LANES), F32)
    lax.fori_loop(0, n_all, step, (zero, zero))


def _hg_call(u, uz, lb_logits, layer):
    t = u.shape[0]
    sb = _seg()
    col = lambda off: pl.BlockSpec((sb, LANES), lambda b, h: (b, off + h))
    return pl.pallas_call(
        functools.partial(_hg_kernel, layer=layer),
        grid=(BATCH, HG_HEADS),
        in_specs=[pl.BlockSpec((2, DEPTH, LANES), lambda b, h: (0, 0, h)),
                  col(_C_HQ), col(0), col(HG_HEADS), col(_C_HI)],
        out_specs=pl.BlockSpec((sb, LANES), lambda b, h: (b, h)),
        out_shape=jax.ShapeDtypeStruct((t, HG_W), F32),
        compiler_params=_params(("arbitrary", "arbitrary")),
        name="hgrn2",
    )(lb_logits, u, uz, uz, u)


def _lru_kernel(x_ref, wg_ref, bg_ref, lam_ref, cw_ref, cb_ref, y_ref,
                xp_scr, af_scr, bf_scr, ab_scr, bb_scr, hb_scr):
    sb = _seg()
    c_len = LRU_CHUNK
    halo = SUBLANES
    n_tiles = sb // SUBLANES
    n_lat_t = SEQ // SUBLANES

    xp_scr[pl.ds(0, halo), :] = jnp.zeros((halo, LANES), F32)
    xp_scr[pl.ds(halo + sb, halo), :] = jnp.zeros((halo, LANES), F32)
    xp_scr[pl.ds(halo, sb), :] = x_ref[...].astype(F32)

    lam = lam_ref[0]
    sp = jnp.maximum(-lam, 0.0) + jnp.log1p(jnp.exp(-jnp.abs(lam)))
    cw = cw_ref[...]
    wg = wg_ref[0]
    bg = bg_ref[0]
    r8 = lax.broadcasted_iota(jnp.int32, (c_len, 1), 0)
    rm8 = r8 % SUBLANES

    def tile_shift(x, s):
        x3 = x.reshape(c_len // SUBLANES, SUBLANES, LANES)
        return pltpu.roll(x3, s, 1).reshape(c_len, LANES)

    def coeff_chunk(c, _):
        base = pl.multiple_of(c * c_len, c_len)
        w = xp_scr[pl.ds(base, c_len + 2 * halo), :]
        grow = base + r8
        in_lat = grow < SEQ
        pos = jnp.where(in_lat, grow, grow - SEQ)
        seglen = jnp.where(in_lat, SEQ, CTX_LEN)
        wlen = c_len + 2 * halo
        xm2 = jnp.where(pos >= 2, pltpu.roll(w, 2, 0)[halo:halo + c_len], 0.0)
        xm1 = jnp.where(pos >= 1, pltpu.roll(w, 1, 0)[halo:halo + c_len], 0.0)
        x0 = w[halo:halo + c_len]
        xp1 = jnp.where(pos + 1 < seglen, pltpu.roll(w, wlen - 1, 0)[halo:halo + c_len], 0.0)
        xc = xm2 * cw[0:1] + xm1 * cw[1:2] + x0 * cw[2:3] + xp1 * cw[3:4] + cb_ref[...]
        gates = _sigmoid(_dot(xc.astype(BF16), wg) + bg)
        rows = pl.ds(base, c_len)
        for d, (a_scr, b_scr) in enumerate(((af_scr, bf_scr), (ab_scr, bb_scr))):
            rg = gates[:, (2 * d) * LANES:(2 * d + 1) * LANES]
            ig = gates[:, (2 * d + 1) * LANES:(2 * d + 2) * LANES]
            log_a = -LRU_C * rg * sp[d:d + 1]
            a = jnp.exp(log_a)
            th = jnp.tanh(log_a)
            bv = jnp.sqrt(-2.0 * th / (1.0 - th)) * (ig * xc)
            for sft in (1, 2, 4):
                if d == 0:
                    a_s = tile_shift(a, sft)
                    b_s = tile_shift(bv, sft)
                    ok = rm8 >= sft
                else:
                    a_s = tile_shift(a, SUBLANES - sft)
                    b_s = tile_shift(bv, SUBLANES - sft)
                    ok = rm8 < SUBLANES - sft
                bv = jnp.where(ok, a * b_s + bv, bv)
                a = jnp.where(ok, a * a_s, a)
            a_scr[rows, :] = a
            b_scr[rows, :] = bv
        return 0

    lax.fori_loop(0, sb // c_len, coeff_chunk, 0)

    def carry_step(s, carry):
        hf, hb = carry
        tf = jnp.where(s < n_tiles - n_lat_t, n_lat_t + s, s - (n_tiles - n_lat_t))
        tb = n_tiles - 1 - s
        rf = pl.ds(pl.multiple_of(tf * SUBLANES, SUBLANES), SUBLANES)
        rk = pl.ds(pl.multiple_of(tb * SUBLANES, SUBLANES), SUBLANES)
        h1 = af_scr[rf, :] * hf + bf_scr[rf, :]
        y_ref[rf, :] = h1
        h2 = ab_scr[rk, :] * hb + bb_scr[rk, :]
        hb_scr[rk, :] = h2
        return (jnp.broadcast_to(h1[SUBLANES - 1:SUBLANES, :], (SUBLANES, LANES)),
                jnp.broadcast_to(h2[0:1, :], (SUBLANES, LANES)))

    zero = jnp.zeros((SUBLANES, LANES), F32)
    lax.fori_loop(0, n_tiles, carry_step, (zero, zero))
    y_ref[...] += hb_scr[...]


def _lru_call(u, wg, bg, lam, cw, cb):
    t = u.shape[0]
    sb = _seg()
    nb = LRU_W // LANES
    return pl.pallas_call(
        _lru_kernel,
        grid=(BATCH, nb),
        in_specs=[pl.BlockSpec((sb, LANES), lambda b, c: (b, _C_LX + c)),
                  pl.BlockSpec((1, LANES, 4 * LANES), lambda b, c: (c, 0, 0)),
                  pl.BlockSpec((1, 1, 4 * LANES), lambda b, c: (c, 0, 0)),
                  pl.BlockSpec((1, 2, LANES), lambda b, c: (c, 0, 0)),
                  pl.BlockSpec((CONV_W, LANES), lambda b, c: (0, c)),
                  pl.BlockSpec((1, LANES), lambda b, c: (0, c))],
        out_specs=pl.BlockSpec((sb, LANES), lambda b, c: (b, c)),
        out_shape=jax.ShapeDtypeStruct((t, LRU_W), F32),
        scratch_shapes=[pltpu.VMEM((sb + 2 * SUBLANES, LANES), F32)] + [pltpu.VMEM((sb, LANES), F32)] * 5,
        compiler_params=_params(("arbitrary", "arbitrary")),
        name="rglru",
    )(u, wg, bg, lam, cw, cb)


def _group_rms(o, group):
    ri = lax.broadcasted_iota(jnp.int32, (LANES, LANES), 0) // group
    ci = lax.broadcasted_iota(jnp.int32, (LANES, LANES), 1) // group
    ones_bd = (ri == ci).astype(BF16)
    outs = []
    for tix in range(o.shape[1] // LANES):
        x = o[:, tix * LANES:(tix + 1) * LANES]
        sq = x * x
        hi = sq.astype(BF16)
        lo = (sq - hi.astype(F32)).astype(BF16)
        ss = _dot(hi, ones_bd) + _dot(lo, ones_bd)
        outs.append(x * lax.rsqrt(ss * (1.0 / group) + NORM_EPS))
    return jnp.concatenate(outs, axis=1)


def _merge_kernel(*refs, n_gate):
    (x_ref, ml_ref, mc_ref, oret_ref, ohg_ref, ylru_ref, rg_ref, hgt_ref, lgt_ref) = refs[:9]
    gate_refs = refs[9:9 + 3 * n_gate]
    (gng_ref, ghg_ref, wr_ref, wh_ref, wl_ref, wo_ref, n2g_ref, rw_ref, rb_ref,
     xo_ref, h2_ref, lg_ref) = refs[9 + 3 * n_gate:]
    tiles_per_b = _seg() // ROW_TILE
    is_ctx = (pl.program_id(0) % tiles_per_b) >= SEQ // ROW_TILE
    ml = ml_ref[0]
    mc = mc_ref[0]

    def gate(i):
        parts = [gate_refs[i * n_gate + j][...].astype(F32) for j in range(n_gate)]
        return _sigmoid(parts[0] if n_gate == 1 else jnp.concatenate(parts, axis=1))

    rg = rg_ref[...].astype(F32)
    ret_in = _group_rms(oret_ref[...], RET_DK) * gng_ref[...] * (rg * _sigmoid(rg))
    ret = _dot(ret_in.astype(BF16), wr_ref[0])
    hg_in = _group_rms(ohg_ref[...], HG_DK) * ghg_ref[...] * _sigmoid(hgt_ref[...].astype(F32))
    hgr = _dot(hg_in.astype(BF16), wh_ref[0])
    lgt = lgt_ref[...].astype(F32)
    gelu = 0.5 * lgt * (1.0 + jnp.tanh(0.7978845608028654 * (lgt + 0.044715 * lgt * lgt * lgt)))
    lru = _dot((ylru_ref[...] * gelu).astype(BF16), wl_ref[0])
    merged = gate(0) * ret + gate(1) * hgr + gate(2) * lru
    mix = _dot(merged.astype(BF16), wo_ref[0])
    g1 = jnp.where(is_ctx, mc[2:3, :], ml[2:3, :])
    xn = x_ref[...] + g1 * mix
    xo_ref[...] = xn
    h2 = _norm_modulate(xn, n2g_ref[...], ml, mc, is_ctx, 3)
    h2_ref[...] = h2.astype(BF16)
    lg_ref[...] = _dot3(h2, rw_ref[...]) + rb_ref[...]


def _merge_call(xs, mod, o_ret, o_hg, y_lru, u, gn_g, hgn_g, wr, wh, wl, wo, n2g, rw, rb, layer):
    t = xs.shape[0]
    r = ROW_TILE
    tiles_per_b = _seg() // r
    n_gate = D_MODEL // 512
    row = lambda w: pl.BlockSpec((r, w), lambda i: (i, 0))
    ucol = lambda blk: pl.BlockSpec((r, 512), lambda i: (i, blk))

    def full(a):
        if a.ndim == 3:
            return pl.BlockSpec((1,) + a.shape[1:], lambda i: (layer, 0, 0))
        return pl.BlockSpec(a.shape, lambda i: (0,) * a.ndim)
    gate_base = _C_GATES * LANES // 512
    in_specs = [row(D_MODEL),
                pl.BlockSpec((1, N_MOD, D_MODEL), lambda i: (i // tiles_per_b, 0, 0)),
                pl.BlockSpec((1, N_MOD, D_MODEL), lambda i: (BATCH, 0, 0)),
                row(RET_W), row(HG_W), row(LRU_W),
                ucol(_C_RG * LANES // 512), ucol(_C_HGT * LANES // 512), ucol(_C_LGT * LANES // 512)]
    in_specs += [ucol(gate_base + j) for j in range(3 * n_gate)]
    consts = [gn_g, hgn_g, wr, wh, wl, wo, n2g, rw, rb]
    in_specs += [full(a) for a in consts]
    return pl.pallas_call(
        functools.partial(_merge_kernel, n_gate=n_gate),
        grid=(t // r,),
        in_specs=in_specs,
        out_specs=[row(D_MODEL), row(D_MODEL), row(LANES)],
        out_shape=[jax.ShapeDtypeStruct((t, D_MODEL), F32), jax.ShapeDtypeStruct((t, D_MODEL), BF16),
                   jax.ShapeDtypeStruct((t, LANES), F32)],
        compiler_params=_params(("arbitrary",)),
        name="merge",
    )(xs, mod, mod, o_ret, o_hg, y_lru, *([u] * (3 + 3 * n_gate)), *consts)


def _route_kernel(lg_ref, oi_ref, ow_ref, lst_ref, meta_ref, cnt_ref, off_scr):
    @pl.when(pl.program_id(0) == 0)
    def _():
        off_scr[...] = jnp.zeros(off_scr.shape, F32)

    r = lg_ref.shape[0]
    lane = lax.broadcasted_iota(jnp.int32, (r, LANES), 1)
    neg = -jnp.inf
    l = jnp.where(lane < N_EXPERTS, lg_ref[...], neg)
    vals, hots = [], []
    lanef = lane.astype(F32)
    for _ in range(TOP_K):
        m = jnp.max(l, axis=-1, keepdims=True)
        idxf = jnp.min(jnp.where(l == m, lanef, float(LANES)), axis=-1, keepdims=True)
        hot = lanef == idxf
        l = jnp.where(hot, neg, l)
        vals.append(m)
        hots.append(hot)
    es = [jnp.exp(v - vals[0]) for v in vals]
    den = es[0]
    for e in es[1:]:
        den = den + e
    chosen = hots[0].astype(F32)
    for hot in hots[1:]:
        chosen = chosen + hot.astype(F32)
    ri = lax.broadcasted_iota(jnp.int32, (r, r), 0)
    ci = lax.broadcasted_iota(jnp.int32, (r, r), 1)
    before = _dot((ri > ci).astype(BF16), chosen.astype(BF16))
    count = jnp.sum(chosen, axis=0, keepdims=True)
    run = jnp.floor((count + (SUBLANES - 1.0)) * (1.0 / SUBLANES)) * SUBLANES
    li = lax.broadcasted_iota(jnp.int32, (LANES, LANES), 0)
    lj = lax.broadcasted_iota(jnp.int32, (LANES, LANES), 1)
    run8 = jnp.broadcast_to(run, (SUBLANES, LANES))
    start = _dot(run8.astype(BF16), (li < lj).astype(BF16))[0:1, :]
    pos = start + before
    oi = jnp.zeros((r, LANES), jnp.int32)
    ow = jnp.zeros((r, LANES), F32)
    lsf = jnp.zeros((r, LANES), F32)
    for k in range(TOP_K):
        row_k = jnp.sum(jnp.where(hots[k], pos, 0.0), axis=-1, keepdims=True)
        oi = jnp.where(lane == k, row_k.astype(jnp.int32), oi)
        lsf = jnp.where(lane == k, row_k, lsf)
        ow = jnp.where(lane == k, es[k] / den, ow)
    oi_ref[...] = oi
    ow_ref[...] = ow
    lst_ref[0] = jnp.transpose(lsf)[0:SUBLANES, :]
    mrow = lax.broadcasted_iota(jnp.int32, (SUBLANES, LANES), 0)
    meta_ref[0] = jnp.where(mrow == 0, run8, jnp.where(mrow == 1, jnp.broadcast_to(start, (SUBLANES, LANES)),
                                                       off_scr[...]))
    off_scr[...] = off_scr[...] + run
    cnt_ref[...] = off_scr[...]


def _route_call(logits):
    t = logits.shape[0]
    r = ROW_TILE
    row = pl.BlockSpec((r, LANES), lambda i: (i, 0))
    return pl.pallas_call(
        _route_kernel,
        grid=(t // r,),
        in_specs=[row],
        out_specs=[row, row, pl.BlockSpec((1, SUBLANES, r), lambda i: (i, 0, 0)),
                   pl.BlockSpec((1, SUBLANES, LANES), lambda i: (i, 0, 0)),
                   pl.BlockSpec((SUBLANES, LANES), lambda i: (0, 0))],
        out_shape=[jax.ShapeDtypeStruct((t, LANES), jnp.int32), jax.ShapeDtypeStruct((t, LANES), F32),
                   jax.ShapeDtypeStruct((t // r, SUBLANES, r), F32),
                   jax.ShapeDtypeStruct((t // r, SUBLANES, LANES), F32),
                   jax.ShapeDtypeStruct((SUBLANES, LANES), F32)],
        scratch_shapes=[pltpu.VMEM((SUBLANES, LANES), F32)],
        compiler_params=_params(("arbitrary",)),
        name="route",
    )(logits)


def _pack_pairs(x):
    m = x.shape[1] // 2
    bits = pltpu.bitcast(x, jnp.uint32)
    return bits[:, :m] | (bits[:, m:] >> 16)


def _unpack_pairs(w):
    hi = pltpu.bitcast(w & jnp.uint32(0xFFFF0000), F32).astype(BF16)
    lo = pltpu.bitcast(w << 16, F32).astype(BF16)
    return hi, lo


def _local_rows():
    rows = ROW_TILE * TOP_K + N_EXPERTS * (SUBLANES - 1)
    return -(-rows // ROW_TILE) * ROW_TILE


RUN_PIECE = 4 * SUBLANES


def _max_pieces():
    return _local_rows() // RUN_PIECE, N_EXPERTS * (RUN_PIECE // SUBLANES - 1)


def _meta_width():
    n_big, n_small = _max_pieces()
    return -(-(2 + 2 * n_big + 2 * n_small) // LANES) * LANES


def _piece_tables(first_slot, first_row, pieces):
    per_big = RUN_PIECE // SUBLANES
    max_big, max_small = _max_pieces()

    def flat(count, limit, offset_rows, step):
        ends = jnp.cumsum(count, axis=1)
        p = jnp.arange(limit, dtype=jnp.int32)[None, :, None]
        mine = ((ends - count)[:, None, :] <= p) & (p < ends[:, None, :])
        take = lambda a: jnp.sum(jnp.where(mine, a[:, None, :], 0), axis=2)
        j = p[:, :, 0] - take(ends - count)
        off = take(offset_rows) + j * step
        return take(first_row) + off, take(first_slot) + off, ends[:, -1:]

    n_big = pieces // per_big
    big_row, big_slot, tot_big = flat(n_big, max_big, jnp.zeros_like(pieces), RUN_PIECE)
    small_row, small_slot, tot_small = flat(pieces - n_big * per_big, max_small, n_big * RUN_PIECE, SUBLANES)
    table = jnp.concatenate([tot_big, tot_small, big_row, big_slot, small_row, small_slot], axis=1)
    return jnp.pad(table, ((0, 0), (0, _meta_width() - table.shape[1])))[:, None, :]


def _run_copies(meta_ref, make_copy, do):
    max_big, max_small = _max_pieces()
    big0 = 2
    small0 = 2 + 2 * max_big

    def big(p, _):
        do(make_copy(pl.multiple_of(meta_ref[0, 0, big0 + p], SUBLANES),
                     pl.multiple_of(meta_ref[0, 0, big0 + max_big + p], SUBLANES), RUN_PIECE))
        return 0

    def small(p, _):
        do(make_copy(pl.multiple_of(meta_ref[0, 0, small0 + p], SUBLANES),
                     pl.multiple_of(meta_ref[0, 0, small0 + max_small + p], SUBLANES), SUBLANES))
        return 0

    lax.fori_loop(0, meta_ref[0, 0, 0], big, 0)
    lax.fori_loop(0, meta_ref[0, 0, 1], small, 0)


def _dispatch_kernel(meta_ref, prev_meta_ref, lst_ref, h_ref, xs_in_ref, xs_ref, perm_scr, sem):
    del xs_in_ref
    i = pl.program_id(0)
    cur = i % 2
    r = h_ref.shape[0]
    ls = lst_ref[0].astype(jnp.int32)
    hb = h_ref[...]
    for s in range(perm_scr.shape[1] // r):
        rows = s * r + lax.broadcasted_iota(jnp.int32, (r, r), 0)
        sel = rows == ls[0:1, :]
        for k in range(1, TOP_K):
            sel = jnp.logical_or(sel, rows == ls[k:k + 1, :])
        perm_scr[cur, pl.ds(s * r, r), :] = _pack_pairs(_dot(jnp.where(sel, 1.0, 0.0).astype(BF16), hb))

    def copies_from(buf):
        def make_copy(row, slot, n):
            return pltpu.make_async_copy(perm_scr.at[buf, pl.ds(row, n)], xs_ref.at[pl.ds(slot, n)], sem.at[buf])
        return make_copy

    _run_copies(meta_ref, copies_from(cur), lambda c: c.start())

    @pl.when(i > 0)
    def _():
        _run_copies(prev_meta_ref, copies_from(1 - cur), lambda c: c.wait())

    @pl.when(i == pl.num_programs(0) - 1)
    def _():
        _run_copies(meta_ref, copies_from(cur), lambda c: c.wait())


def _dispatch_call(h2, meta, lst, slots):
    t = h2.shape[0]
    r = ROW_TILE
    return pl.pallas_call(
        _dispatch_kernel,
        grid=(t // r,),
        in_specs=[pl.BlockSpec((1, 1, _meta_width()), lambda i: (i, 0, 0), memory_space=pltpu.SMEM),
                  pl.BlockSpec((1, 1, _meta_width()), lambda i: (jnp.maximum(i - 1, 0), 0, 0), memory_space=pltpu.SMEM),
                  pl.BlockSpec((1, SUBLANES, r), lambda i: (i, 0, 0)),
                  pl.BlockSpec((r, D_MODEL), lambda i: (i, 0)),
                  pl.BlockSpec(memory_space=pl.ANY)],
        out_specs=pl.BlockSpec(memory_space=pl.ANY),
        out_shape=jax.ShapeDtypeStruct(slots.shape, jnp.uint32),
        scratch_shapes=[pltpu.VMEM((2, _local_rows(), D_MODEL // 2), jnp.uint32), pltpu.SemaphoreType.DMA((2,))],
        input_output_aliases={4: 0},
        compiler_params=_params(("arbitrary",)),
        name="dispatch",
    )(meta, meta, lst, h2, slots)


def _expert_kernel(bexp_ref, nv_ref, next_ref, par_ref, x_ref, wgu_hbm, bgu_ref, wd_hbm, bd_ref, y_ref,
                   wgu_scr, wd_scr, gu_stage, d_stage, sem, *, layer):
    i = pl.program_id(0)
    f = D_MODEL
    prev = bexp_ref[jnp.maximum(i - 1, 0)]
    new_expert = jnp.logical_or(i == 0, bexp_ref[i] != prev)

    def weight_copies(e, b):
        return (pltpu.make_async_copy(wgu_hbm.at[layer, e], gu_stage.at[b], sem.at[b, 0]),
                pltpu.make_async_copy(wd_hbm.at[layer, e], d_stage.at[b], sem.at[b, 1]))

    @pl.when(jnp.logical_and(i < nv_ref[0], new_expert))
    def _():
        b = par_ref[i]

        @pl.when(i == 0)
        def _():
            for c in weight_copies(bexp_ref[i], b):
                c.start()

        for c in weight_copies(bexp_ref[i], b):
            c.wait()

        @pl.when(next_ref[i] >= 0)
        def _():
            for c in weight_copies(next_ref[i], 1 - b):
                c.start()

        slab = 256
        for s in range(D_MODEL // slab):
            wgu_scr[pl.ds(s * slab, slab), :] = gu_stage[b, pl.ds(s * slab, slab), :].astype(BF16)
            wd_scr[pl.ds(s * slab, slab), :] = d_stage[b, pl.ds(s * slab, slab), :].astype(BF16)

    @pl.when(i < nv_ref[0])
    def _():
        x_hi, x_lo = _unpack_pairs(x_ref[...])
        half = D_MODEL // 2
        gu = _dot(x_hi, wgu_scr[:half, :]) + _dot(x_lo, wgu_scr[half:, :]) + bgu_ref[0, 0]
        gate = jnp.minimum(gu[:, :f], SWIGLU_LIMIT)
        up = jnp.clip(gu[:, f:], -SWIGLU_LIMIT, SWIGLU_LIMIT)
        glu = gate * _sigmoid(SWIGLU_ALPHA * gate)
        y = _dot(((up + 1.0) * glu).astype(BF16), wd_scr[...]) + bd_ref[0, 0]
        y_ref[...] = _pack_pairs(y.astype(BF16).astype(F32))

    @pl.when(i >= nv_ref[0])
    def _():
        y_ref[...] = jnp.zeros(y_ref.shape, jnp.uint32)


def _expert_call(slots, bexp, nvalid, next_expert, stage, w_gu, b_gu, w_down, b_down, layer):
    s = slots.shape[0]
    nb = s // MOE_BLOCK
    f = D_MODEL
    blk = lambda i, be, nv, nx, pr: (jnp.minimum(i, nv[0] - 1), 0)
    exp3 = lambda i, be, nv, nx, pr: (layer, be[jnp.minimum(i, nv[0] - 1)], 0, 0)
    grid_spec = pltpu.PrefetchScalarGridSpec(
        num_scalar_prefetch=4,
        grid=(nb,),
        in_specs=[pl.BlockSpec((MOE_BLOCK, D_MODEL // 2), blk),
                  pl.BlockSpec(memory_space=pl.ANY),
                  pl.BlockSpec((1, 1, 1, 2 * f), exp3),
                  pl.BlockSpec(memory_space=pl.ANY),
                  pl.BlockSpec((1, 1, 1, D_MODEL), exp3)],
        out_specs=pl.BlockSpec((MOE_BLOCK, D_MODEL // 2), lambda i, be, nv, nx, pr: (i, 0)),
        scratch_shapes=[pltpu.VMEM((D_MODEL, 2 * f), BF16), pltpu.VMEM((f, D_MODEL), BF16),
                        pltpu.VMEM((2, D_MODEL, 2 * f), F32), pltpu.VMEM((2, f, D_MODEL), F32),
                        pltpu.SemaphoreType.DMA((2, 2))],
    )
    return pl.pallas_call(
        functools.partial(_expert_kernel, layer=layer),
        grid_spec=grid_spec,
        out_shape=jax.ShapeDtypeStruct((s, D_MODEL // 2), jnp.uint32),
        compiler_params=_params(("arbitrary",)),
        name="experts",
    )(bexp, nvalid, next_expert, stage, slots, w_gu, b_gu.reshape(DEPTH, N_EXPERTS, 1, 2 * f), w_down,
      b_down.reshape(DEPTH, N_EXPERTS, 1, D_MODEL))


def _combine_kernel(meta_ref, next_meta_ref, x_ref, ml_ref, mc_ref, oi_ref, w_ref, fg_ref, y_ref, o_ref, buf, sem,
                    *, last, n_t):
    r = x_ref.shape[0]
    i = pl.program_id(0)
    cur = i % 2

    def copies_into(b):
        def make_copy(row, slot, n):
            return pltpu.make_async_copy(y_ref.at[pl.ds(slot, n)], buf.at[b, pl.ds(row, n)], sem.at[b])
        return make_copy

    @pl.when(i == 0)
    def _():
        buf[...] = jnp.zeros(buf.shape, jnp.uint32)
        _run_copies(meta_ref, copies_into(0), lambda c: c.start())

    @pl.when(i + 1 < pl.num_programs(0))
    def _():
        _run_copies(next_meta_ref, copies_into(1 - cur), lambda c: c.start())

    _run_copies(meta_ref, copies_into(cur), lambda c: c.wait())

    if last:
        is_ctx = False
    else:
        is_ctx = (i % n_t) >= SEQ // ROW_TILE
    w = w_ref[...]
    oi = oi_ref[...]
    ff_hi = jnp.zeros((r, D_MODEL // 2), F32)
    ff_lo = jnp.zeros((r, D_MODEL // 2), F32)
    for s in range(buf.shape[1] // r):
        cols = s * r + lax.broadcasted_iota(jnp.int32, (r, r), 1)
        wm = jnp.zeros((r, r), F32)
        for k in range(TOP_K):
            wm = jnp.where(cols == oi[:, k:k + 1], w[:, k:k + 1], wm)
        y_hi, y_lo = _unpack_pairs(buf[cur, pl.ds(s * r, r), :])
        ff_hi = ff_hi + _dot(wm.astype(BF16), y_hi)
        ff_lo = ff_lo + _dot(wm.astype(BF16), y_lo)
    ff = jnp.concatenate([ff_hi, ff_lo], axis=1)
    g2 = jnp.where(is_ctx, mc_ref[0][5:6, :], ml_ref[0][5:6, :])
    xn = x_ref[...] + g2 * ff
    if last:
        ms = jnp.mean(xn * xn, axis=-1, keepdims=True)
        xn = xn * lax.rsqrt(ms + NORM_EPS) * fg_ref[...]
    o_ref[...] = xn


def _combine_call(xs, mod, meta, rows, wts, y_slots, final_g, last):
    t = xs.shape[0]
    r = ROW_TILE
    tiles_per_b = _seg() // r
    n_t = SEQ // r if last else tiles_per_b
    n_steps = BATCH * n_t
    tile = lambda i: (i // n_t) * tiles_per_b + i % n_t
    out_rows = BATCH * SEQ if last else t
    return pl.pallas_call(
        functools.partial(_combine_kernel, last=last, n_t=n_t),
        grid=(n_steps,),
        in_specs=[pl.BlockSpec((1, 1, _meta_width()), lambda i: (tile(i), 0, 0), memory_space=pltpu.SMEM),
                  pl.BlockSpec((1, 1, _meta_width()), lambda i: (tile(jnp.minimum(i + 1, n_steps - 1)), 0, 0),
                               memory_space=pltpu.SMEM),
                  pl.BlockSpec((r, D_MODEL), lambda i: (tile(i), 0)),
                  pl.BlockSpec((1, N_MOD, D_MODEL), lambda i: (i // n_t, 0, 0)),
                  pl.BlockSpec((1, N_MOD, D_MODEL), lambda i: (BATCH, 0, 0)),
                  pl.BlockSpec((r, LANES), lambda i: (tile(i), 0)),
                  pl.BlockSpec((r, LANES), lambda i: (tile(i), 0)),
                  pl.BlockSpec((1, D_MODEL), lambda i: (0, 0)),
                  pl.BlockSpec(memory_space=pl.ANY)],
        out_specs=pl.BlockSpec((r, D_MODEL), lambda i: (i, 0)),
        out_shape=jax.ShapeDtypeStruct((out_rows, D_MODEL), F32),
        scratch_shapes=[pltpu.VMEM((2, _local_rows(), D_MODEL // 2), jnp.uint32), pltpu.SemaphoreType.DMA((2,))],
        compiler_params=_params(("arbitrary",)),
        name="combine",
    )(meta, meta, xs, mod, mod, rows, wts, final_g, y_slots)


def _rotary_tables():
    rows = SEQ // GRID_W
    row = jnp.repeat(jnp.arange(rows, dtype=F32), GRID_W)
    colp = jnp.tile(jnp.arange(GRID_W, dtype=F32), rows)
    n_freq = RET_DK // 4
    inv = ROPE_BASE ** (-jnp.arange(n_freq, dtype=F32) / n_freq)
    ang = jnp.concatenate([row[:, None] * inv, colp[:, None] * inv], axis=-1)
    cos, sin = jnp.cos(ang), jnp.sin(ang)
    reps = LANES // (RET_DK // 2)
    cos_t = jnp.tile(cos, (1, reps))
    sign = jnp.where((jnp.arange(LANES) % RET_DK) < RET_DK // 2, -1.0, 1.0).astype(F32)
    sin_t = jnp.tile(sin, (1, reps)) * sign
    return cos_t, sin_t


def _lru_gate_weights(wa, ba, wx, bx):
    nb = LRU_W // LANES
    per = LANES // LRU_BD
    eye = jnp.eye(per, dtype=F32)

    def embed(w):
        w = w.reshape(nb, per, LRU_BD, LRU_BD)
        return jnp.einsum('cpij,pq->cpiqj', w, eye).reshape(nb, LANES, LANES)

    wg = jnp.concatenate([embed(wa[0]), embed(wx[0]), embed(wa[1]), embed(wx[1])], axis=-1)
    bias = jnp.concatenate([ba[0].reshape(nb, 1, LANES), bx[0].reshape(nb, 1, LANES),
                            ba[1].reshape(nb, 1, LANES), bx[1].reshape(nb, 1, LANES)], axis=-1)
    return wg.astype(BF16), bias


def kernel(x, c, ctx, c_ctx, mod_w, mod_b, norm1_g, norm2_g, final_g, w_in, ret_decay, ret_gn_g, w_ret_o,
           hgrn_lb_logits, hgrn_gn_g, w_hgrn_o, lru_conv_w, lru_conv_b, lru_wa, lru_ba, lru_wx, lru_bx,
           lru_lambda, w_lru_o, w_out, router_w, router_b, exp_w_gu, exp_b_gu, exp_w_down, exp_b_down):
    assert CTX_LEN == RET_CHUNK and SEQ % RET_CHUNK == 0
    sb = _seg()
    t = BATCH * sb
    xs = jnp.concatenate([x, ctx], axis=1).reshape(t, D_MODEL)

    cvec = jnp.concatenate([c, c_ctx[None, :], jnp.zeros((SUBLANES - BATCH - 1, D_MODEL), F32)], axis=0)
    mod_all = _mod_call(cvec, mod_w, mod_b).reshape(DEPTH, SUBLANES, N_MOD, D_MODEL)

    cos_t, sin_t = _rotary_tables()
    lb_logits = jnp.transpose(hgrn_lb_logits, (1, 0, 2))
    n_slots = t * TOP_K + (t // ROW_TILE) * N_EXPERTS * (SUBLANES - 1) + N_EXPERTS * (MOE_BLOCK - 1)
    n_blocks = -(-n_slots // MOE_BLOCK)
    n_slots = n_blocks * MOE_BLOCK
    slots = jnp.zeros((n_slots, D_MODEL // 2), jnp.uint32)
    rw_pad = jnp.pad(router_w, ((0, 0), (0, 0), (0, LANES - N_EXPERTS)))
    rb_pad = jnp.pad(router_b, ((0, 0), (0, LANES - N_EXPERTS)))
    w_in_b, w_ret_b, w_hg_b, w_lru_b, w_out_b = (w.astype(BF16) for w in (w_in, w_ret_o, w_hgrn_o, w_lru_o, w_out))
    block_first_slot = jnp.arange(n_blocks, dtype=jnp.int32) * MOE_BLOCK
    expert_ids = jnp.arange(N_EXPERTS, dtype=jnp.int32)

    out = None
    for l in range(DEPTH):
        last = l == DEPTH - 1
        mod = mod_all[l]
        u, uz = _inproj_call(xs, mod, norm1_g[l][None, :], w_in_b, l)

        dec = jnp.broadcast_to(ret_decay[l].reshape(2, RET_HEADS // 2, 2, 1).transpose(1, 0, 2, 3),
                               (RET_HEADS // 2, 2, 2, LANES)).reshape(RET_HEADS // 2, 4, LANES)
        o_ret = _ret_call(u, dec, cos_t, sin_t)
        o_hg = _hg_call(u, uz, lb_logits, l)
        wg, bg = _lru_gate_weights(lru_wa[l], lru_ba[l], lru_wx[l], lru_bx[l])
        lam = lru_lambda[l].reshape(2, LRU_W // LANES, LANES).transpose(1, 0, 2)
        y_lru = _lru_call(u, wg, bg, lam, lru_conv_w[l], lru_conv_b[l][None, :])

        xs, h2, logits = _merge_call(
            xs, mod, o_ret, o_hg, y_lru, u, ret_gn_g[l][None, :], hgrn_gn_g[l][None, :],
            w_ret_b, w_hg_b, w_lru_b, w_out_b, norm2_g[l][None, :], rw_pad[l], rb_pad[l][None, :], l)

        rows, wts, lst, tile_meta, cnt = _route_call(logits)
        run = tile_meta[:, 0, :N_EXPERTS].astype(jnp.int32)
        first_row = tile_meta[:, 1, :N_EXPERTS].astype(jnp.int32)
        earlier = tile_meta[:, 2, :N_EXPERTS].astype(jnp.int32)
        counts = cnt[0, :N_EXPERTS].astype(jnp.int32)
        padded = (counts + MOE_BLOCK - 1) // MOE_BLOCK * MOE_BLOCK
        pad_end = jnp.cumsum(padded)
        first_slot = (pad_end - padded)[None, :] + earlier
        meta = _piece_tables(first_slot, first_row, run // SUBLANES)
        bexp = jnp.minimum(jnp.sum((pad_end[None, :] <= block_first_slot[:, None]).astype(jnp.int32), axis=1),
                           N_EXPERTS - 1)
        nvalid = (pad_end[-1:] // MOE_BLOCK).astype(jnp.int32)
        present = padded > 0
        later = jnp.where(present[None, :] & (expert_ids[None, :] > expert_ids[:, None]), expert_ids[None, :],
                          N_EXPERTS)
        next_of = jnp.min(later, axis=1)
        next_of = jnp.where(next_of >= N_EXPERTS, -1, next_of)
        stage_of = (jnp.cumsum(present.astype(jnp.int32)) - present.astype(jnp.int32)) % 2
        is_e = bexp[:, None] == expert_ids[None, :]
        next_expert = jnp.sum(jnp.where(is_e, next_of[None, :], 0), axis=1)
        stage = jnp.sum(jnp.where(is_e, stage_of[None, :], 0), axis=1)

        slots = _dispatch_call(h2, meta, lst, slots)
        y_slots = _expert_call(slots, bexp, nvalid, next_expert, stage, exp_w_gu, exp_b_gu, exp_w_down,
                               exp_b_down, l)
        res = _combine_call(xs, mod, meta, rows, wts, y_slots, final_g[None, :], last)
        if last:
            out = res.reshape(BATCH, SEQ, D_MODEL)
        else:
            xs = res
    return out
```

```python
import functools

import jax
import jax.numpy as jnp
from jax import lax
from jax.experimental import pallas as pl
from jax.experimental.pallas import tpu as pltpu

F32 = jnp.float32
BF16 = jnp.bfloat16

D_MODEL = 1024
BATCH = 4
SEQ = 4096
DEPTH = 4
GRID_W = 64
CTX_LEN = 256
N_MOD = 6
NORM_EPS = 1e-6
RET_HEADS = 8
RET_DK = 64
RET_W = 512
ROPE_BASE = 10000.0
HG_HEADS = 4
HG_DK = 128
HG_W = 512
LRU_W = 512
LRU_BLOCKS = 8
LRU_BD = 64
LRU_C = 8.0
CONV_W = 4
N_EXPERTS = 32
TOP_K = 4
SWIGLU_ALPHA = 1.702
SWIGLU_LIMIT = 7.0

LANES = 128
SUBLANES = 8
RET_CHUNK = 256
HG_CHUNK = 256
HG_BLOCK = 128
HG_SUB = 4
LOG2_E = 1.4426950408889634
LRU_CHUNK = 256
MOE_BLOCK = 256
ROW_TILE = 256
MOD_TN = 1536
VMEM_LIMIT = 56 * 1024 * 1024

_C_RQ, _C_RK, _C_RV, _C_RG = 0, 4, 8, 12
_C_HQ, _C_HFF, _C_HFB, _C_HI, _C_HGT = 16, 20, 24, 28, 32
_C_LX, _C_LGT = 36, 40
_C_GATES = 44


def _seg():
    return SEQ + CTX_LEN


def _in_cols():
    return _C_GATES * LANES + 3 * D_MODEL


def _dot(a, b):
    return jnp.dot(a, b, preferred_element_type=F32)


def _dot_nt(a, b):
    return lax.dot_general(a, b, (((1,), (1,)), ((), ())), preferred_element_type=F32)


def _dot_tn(a, b):
    return lax.dot_general(a, b, (((0,), (0,)), ((), ())), preferred_element_type=F32)


def _split3(x):
    hi = x.astype(BF16)
    r = x - hi.astype(F32)
    mid = r.astype(BF16)
    lo = (r - mid.astype(F32)).astype(BF16)
    return hi, mid, lo


def _dot_exact_lhs(a, x):
    hi, mid, lo = _split3(x)
    return _dot(a, hi) + _dot(a, mid) + _dot(a, lo)


def _dot3(a, b):
    ah, am, _ = _split3(a)
    bh, bm, _ = _split3(b)
    return _dot(ah, bh) + (_dot(ah, bm) + _dot(am, bh))


def _sigmoid(x):
    return 0.5 + 0.5 * jnp.tanh(0.5 * x)


def _log_sigmoid(x):
    return jnp.minimum(x, 0.0) - jnp.log1p(jnp.exp(-jnp.abs(x)))


def _params(sem):
    return pltpu.CompilerParams(dimension_semantics=sem, vmem_limit_bytes=VMEM_LIMIT)


def _mod_kernel(c_ref, w_ref, b_ref, o_ref):
    c = c_ref[...]
    s = c * _sigmoid(c)
    o_ref[0] = _dot3(s, w_ref[0]) + b_ref[0]


def _mod_call(cvec, mod_w, mod_b):
    n6 = N_MOD * D_MODEL
    tn = MOD_TN
    return pl.pallas_call(
        _mod_kernel,
        grid=(DEPTH, n6 // tn),
        in_specs=[pl.BlockSpec((SUBLANES, D_MODEL), lambda l, j: (0, 0)),
                  pl.BlockSpec((1, D_MODEL, tn), lambda l, j: (l, 0, j)),
                  pl.BlockSpec((1, 1, tn), lambda l, j: (l, 0, j))],
        out_specs=pl.BlockSpec((1, SUBLANES, tn), lambda l, j: (l, 0, j)),
        out_shape=jax.ShapeDtypeStruct((DEPTH, SUBLANES, n6), F32),
        compiler_params=_params(("arbitrary", "arbitrary")),
        name="modulation",
    )(cvec, mod_w, mod_b.reshape(DEPTH, 1, n6))


def _norm_modulate(x, g, mod_l, mod_c, is_ctx, first):
    ms = jnp.mean(x * x, axis=-1, keepdims=True)
    y = x * lax.rsqrt(ms + NORM_EPS) * g
    sh = jnp.where(is_ctx, mod_c[first:first + 1, :], mod_l[first:first + 1, :])
    sc = jnp.where(is_ctx, mod_c[first + 1:first + 2, :], mod_l[first + 1:first + 2, :])
    return y * (1.0 + sc) + sh


def _inproj_kernel(x_ref, ml_ref, mc_ref, g_ref, w_ref, o_ref, oz_ref, h_scr, *, tm, z_first, z_count):
    j = pl.program_id(1)

    @pl.when(j == 0)
    def _():
        tiles_per_b = _seg() // tm
        row0 = (pl.program_id(0) % tiles_per_b) * tm
        rows = row0 + lax.broadcasted_iota(jnp.int32, (tm, 1), 0)
        h = _norm_modulate(x_ref[...], g_ref[...], ml_ref[0], mc_ref[0], rows >= SEQ, 0)
        h_scr[...] = h.astype(BF16)

    acc = _dot(h_scr[...], w_ref[0].astype(BF16))
    o_ref[...] = acc.astype(BF16)

    @pl.when(jnp.logical_and(j >= z_first, j < z_first + z_count))
    def _():
        oz_ref[...] = acc


def _inproj_call(xs, mod, g, w, layer):
    t = xs.shape[0]
    tm = _seg() // 2
    ncol = _in_cols()
    tn = HG_W
    tiles_per_b = _seg() // tm
    z_first = _C_HFF * LANES // tn
    z_count = 2
    return pl.pallas_call(
        functools.partial(_inproj_kernel, tm=tm, z_first=z_first, z_count=z_count),
        grid=(t // tm, ncol // tn),
        in_specs=[pl.BlockSpec((tm, D_MODEL), lambda i, j: (i, 0)),
                  pl.BlockSpec((1, N_MOD, D_MODEL), lambda i, j: (i // tiles_per_b, 0, 0)),
                  pl.BlockSpec((1, N_MOD, D_MODEL), lambda i, j: (BATCH, 0, 0)),
                  pl.BlockSpec((1, D_MODEL), lambda i, j: (0, 0)),
                  pl.BlockSpec((1, D_MODEL, tn), lambda i, j: (layer, 0, j))],
        out_specs=[pl.BlockSpec((tm, tn), lambda i, j: (i, j)),
                   pl.BlockSpec((tm, tn), lambda i, j: (i, jnp.clip(j - z_first, 0, z_count - 1)))],
        out_shape=[jax.ShapeDtypeStruct((t, ncol), BF16), jax.ShapeDtypeStruct((t, z_count * tn), F32)],
        scratch_shapes=[pltpu.VMEM((tm, D_MODEL), BF16)],
        compiler_params=_params(("arbitrary", "arbitrary")),
        name="in_projection",
    )(xs, mod, mod, g, w)


def _ret_kernel(dec_ref, cos_ref, sin_ref, q_ref, k_ref, v_ref, o_ref, mask_scr, pw_scr):
    c_len = RET_CHUNK
    n_lat = SEQ // c_len
    n_all = _seg() // c_len
    lane = lax.broadcasted_iota(jnp.int32, (1, LANES), 1)
    head0 = lane < RET_DK
    first_half = (lane % RET_DK) < (RET_DK // 2)

    lg = _log_sigmoid(dec_ref[0])
    dvf = jnp.where(head0, lg[0:1], lg[1:2])
    dvb = jnp.where(head0, lg[2:3], lg[3:4])
    r = lax.broadcasted_iota(jnp.int32, (c_len, 1), 0).astype(F32)
    pw_scr[0] = jnp.exp((r + 1.0) * dvf)
    pw_scr[1] = jnp.exp((c_len - 1.0 - r) * dvf)
    pw_scr[2] = jnp.exp((c_len - r) * dvb)
    pw_scr[3] = jnp.exp(r * dvb)
    dcf = jnp.exp(c_len * dvf)
    dcb = jnp.exp(c_len * dvb)
    ri = lax.broadcasted_iota(jnp.int32, (c_len, c_len), 0)
    ci = lax.broadcasted_iota(jnp.int32, (c_len, c_len), 1)
    dm = (ri - ci).astype(F32)
    for h in range(2):
        lgf = jnp.concatenate([lg[h:h + 1]] * (c_len // LANES), axis=1)
        lgb = jnp.concatenate([lg[2 + h:3 + h]] * (c_len // LANES), axis=1)
        mask_scr[h] = jnp.exp(jnp.where(dm >= 0, dm * lgf, -dm * lgb))
    rb = lax.broadcasted_iota(jnp.int32, (LANES, LANES), 0) < RET_DK
    cb = lax.broadcasted_iota(jnp.int32, (LANES, LANES), 1) < RET_DK
    same_head = (rb == cb).astype(F32)

    def load(c):
        rows = pl.ds(pl.multiple_of(c * c_len, c_len), c_len)
        q = q_ref[rows, :].astype(F32)
        k = k_ref[rows, :].astype(F32) * (RET_DK ** -0.5)
        v = v_ref[rows, :].astype(F32)
        return rows, q, k, v

    def rotary(t, c):
        rows = pl.ds(pl.multiple_of(c * c_len, c_len), c_len)
        partner = jnp.where(first_half, pltpu.roll(t, LANES - RET_DK // 2, 1), pltpu.roll(t, RET_DK // 2, 1))
        return t * cos_ref[rows, :] + partner * sin_ref[rows, :]

    def load_rot(c):
        rows, q, k, v = load(c)
        is_lat = c < n_lat
        cc = jnp.minimum(c, n_lat - 1)
        q = jnp.where(is_lat, rotary(q, cc), q)
        k = jnp.where(is_lat, rotary(k, cc), k)
        return rows, q, k, v

    def fwd_step(s, st):
        c = jnp.where(s < n_all - n_lat, n_lat + s, s - (n_all - n_lat))
        rows, q, k, v = load_rot(c)
        kb = k.astype(BF16)
        vb = v.astype(BF16)
        o0 = _dot((_dot_nt(jnp.where(head0, q, 0.0).astype(BF16), kb) * mask_scr[0]).astype(BF16), vb)
        o1 = _dot((_dot_nt(jnp.where(head0, 0.0, q).astype(BF16), kb) * mask_scr[1]).astype(BF16), vb)
        o = jnp.where(head0, o0, o1)
        o = o + _dot_nt((q * pw_scr[0]).astype(BF16), st.astype(BF16))
        o_ref[rows, :] += o
        return st * dcf + _dot_tn(vb, (k * pw_scr[1]).astype(BF16)) * same_head

    def bwd_step(s, st):
        c = n_all - 1 - s
        rows, q, k, v = load_rot(c)
        o_ref[rows, :] += _dot_nt((q * pw_scr[2]).astype(BF16), st.astype(BF16))
        return st * dcb + _dot_tn(v.astype(BF16), (k * pw_scr[3]).astype(BF16)) * same_head

    o_ref[...] = jnp.zeros(o_ref.shape, F32)
    zero = jnp.zeros((LANES, LANES), F32)
    lax.fori_loop(0, n_all, lambda s, sts: (fwd_step(s, sts[0]), bwd_step(s, sts[1])), (zero, zero))


def _ret_call(u, dec, cos_t, sin_t):
    t = u.shape[0]
    sb = _seg()
    col = lambda off: pl.BlockSpec((sb, LANES), lambda b, p: (b, off + p))
    return pl.pallas_call(
        _ret_kernel,
        grid=(BATCH, RET_HEADS // 2),
        in_specs=[pl.BlockSpec((1, 4, LANES), lambda b, p: (p, 0, 0)),
                  pl.BlockSpec((SEQ, LANES), lambda b, p: (0, 0)),
                  pl.BlockSpec((SEQ, LANES), lambda b, p: (0, 0)),
                  col(_C_RQ), col(_C_RK), col(_C_RV)],
        out_specs=pl.BlockSpec((sb, LANES), lambda b, p: (b, p)),
        out_shape=jax.ShapeDtypeStruct((t, RET_W), F32),
        scratch_shapes=[pltpu.VMEM((2, RET_CHUNK, RET_CHUNK), F32),
                        pltpu.VMEM((4, RET_CHUNK, LANES), F32)],
        compiler_params=_params(("arbitrary", "arbitrary")),
        name="retention",
    )(dec, cos_t, sin_t, u, u, u)


def _hg_codes(fwd):
    c_len = HG_BLOCK
    ri = lax.broadcasted_iota(jnp.int32, (c_len, c_len), 0)
    ci = lax.broadcasted_iota(jnp.int32, (c_len, c_len), 1)
    dist = (ri - ci) if fwd else (ci - ri)
    code = jnp.full((c_len, c_len), -1, jnp.int32)
    half, level = c_len // 2, 0
    while half >= HG_SUB:
        sh = half.bit_length() - 1
        same_seg = jnp.right_shift(ri, sh + 1) == jnp.right_shift(ci, sh + 1)
        other_half = jnp.right_shift(ri, sh) != jnp.right_shift(ci, sh)
        code = jnp.where(same_seg & other_half & (dist > 0), level, code)
        half, level = half // 2, level + 1
    sub_sh = HG_SUB.bit_length() - 1
    same_sub = jnp.right_shift(ri, sub_sh) == jnp.right_shift(ci, sub_sh)
    for dl in range(HG_SUB):
        code = jnp.where(same_sub & (dist == dl), 100 + dl, code)
    return code


def _hg_chunk(fwd, q, z, v, lb, st, code, tri):
    c_len = HG_CHUNK

    a = jnp.log(lb)
    b = jnp.log1p(-lb) + _log_sigmoid(z)
    lf = jnp.maximum(a, b) + jnp.log1p(jnp.exp(-jnp.abs(a - b)))
    kin = (1.0 - lb) * _sigmoid(-z)

    g = _dot_exact_lhs(tri, lf) * LOG2_E
    g_end = g[c_len - 1:c_len, :] if fwd else g[0:1, :]
    vb = v.astype(BF16)

    o = _dot_nt((q * jnp.exp2(g)).astype(BF16), st.astype(BF16))

    blk = HG_BLOCK
    lo, hi = slice(0, blk), slice(blk, 2 * blk)
    qrows, krows = (hi, lo) if fwd else (lo, hi)
    bnd = g[blk - 1:blk, :] if fwd else g[blk:blk + 1, :]
    qs = (q[qrows] * jnp.exp2(g[qrows] - bnd)).astype(BF16)
    ks = (kin[krows] * jnp.exp2(bnd - g[krows])).astype(BF16)
    cross = _dot(_dot_nt(qs, ks).astype(BF16), vb[krows])

    parts = []
    for rows in (lo, hi):
        parts.append(_hg_block(fwd, q[rows], kin[rows], g[rows], vb[rows], code))
    parts[1 if fwd else 0] = parts[1 if fwd else 0] + cross
    o = o + jnp.concatenate(parts, axis=0)
    st_new = st * jnp.exp2(g_end) + _dot_tn(vb, (kin * jnp.exp2(g_end - g)).astype(BF16))
    return o, st_new


def _hg_block(fwd, q, kin, g, vb, code):
    n = HG_BLOCK

    scores = jnp.zeros((n, n), F32)
    half, level = n // 2, 0
    while half >= HG_SUB:
        seg = 2 * half
        g3 = g.reshape(n // seg, seg, LANES)
        e = half - 1 if fwd else half
        bsel = jnp.broadcast_to(g3[:, e:e + 1, :], g3.shape).reshape(n, LANES)
        qs = (q * jnp.exp2(g - bsel)).astype(BF16)
        ks = (kin * jnp.exp2(bsel - g)).astype(BF16)
        scores = jnp.where(code == level, _dot_nt(qs, ks), scores)
        half, level = half // 2, level + 1

    def shift(x, dl):
        x3 = x.reshape(n // SUBLANES, SUBLANES, LANES)
        return pltpu.roll(x3, dl if fwd else SUBLANES - dl, 1).reshape(n, LANES)

    for dl in range(HG_SUB):
        w = q * kin if dl == 0 else q * shift(kin, dl) * jnp.exp2(g - shift(g, dl))
        scores = jnp.where(code == 100 + dl, jnp.sum(w, axis=-1, keepdims=True), scores)

    return _dot(scores.astype(BF16), vb)


def _hg_kernel(lbl_ref, q_ref, zf_ref, zb_ref, v_ref, o_ref, *, layer):
    c_len = HG_CHUNK
    n_lat = SEQ // c_len
    n_all = _seg() // c_len

    lbs = []
    for d in range(2):
        x = lbl_ref[d]
        e = jnp.exp(x - jnp.max(x, axis=0, keepdims=True))
        tot = jnp.sum(e, axis=0, keepdims=True)
        part = jnp.zeros_like(tot)
        for i in range(1, layer + 1):
            part = part + e[i:i + 1, :]
        lbs.append(part / tot)

    ri = lax.broadcasted_iota(jnp.int32, (c_len, c_len), 0)
    ci = lax.broadcasted_iota(jnp.int32, (c_len, c_len), 1)
    tri_f = (ri >= ci).astype(BF16)
    tri_b = (ri <= ci).astype(BF16)
    code_f = _hg_codes(True)
    code_b = _hg_codes(False)

    o_ref[...] = jnp.zeros(o_ref.shape, F32)

    def step(s, carry):
        st_f, st_b = carry
        cf = jnp.where(s < n_all - n_lat, n_lat + s, s - (n_all - n_lat))
        cbk = n_all - 1 - s
        rf = pl.ds(pl.multiple_of(cf * c_len, c_len), c_len)
        rk = pl.ds(pl.multiple_of(cbk * c_len, c_len), c_len)
        qf = q_ref[rf, :].astype(F32)
        of, st_f = _hg_chunk(True, qf * _sigmoid(qf), zf_ref[rf, :], v_ref[rf, :].astype(F32), lbs[0], st_f,
                             code_f, tri_f)
        o_ref[rf, :] += of
        qb = q_ref[rk, :].astype(F32)
        ob, st_b = _hg_chunk(False, qb * _sigmoid(qb), zb_ref[rk, :], v_ref[rk, :].astype(F32), lbs[1], st_b,
                             code_b, tri_b)
        o_ref[rk, :] += ob
        return st_f, st_b

    zero = jnp.zeros((LANES, LANES), F32)
    lax.fori_loop(0, n_all, step, (zero, zero))


def _hg_call(u, uz, lb_logits, layer):
    t = u.shape[0]
    sb = _seg()
    col = lambda off: pl.BlockSpec((sb, LANES), lambda b, h: (b, off + h))
    return pl.pallas_call(
        functools.partial(_hg_kernel, layer=layer),
        grid=(BATCH, HG_HEADS),
        in_specs=[pl.BlockSpec((2, DEPTH, LANES), lambda b, h: (0, 0, h)),
                  col(_C_HQ), col(0), col(HG_HEADS), col(_C_HI)],
        out_specs=pl.BlockSpec((sb, LANES), lambda b, h: (b, h)),
        out_shape=jax.ShapeDtypeStruct((t, HG_W), F32),
        compiler_params=_params(("arbitrary", "arbitrary")),
        name="hgrn2",
    )(lb_logits, u, uz, uz, u)


def _lru_kernel(x_ref, wg_ref, bg_ref, lam_ref, cw_ref, cb_ref, y_ref,
                xp_scr, af_scr, bf_scr, ab_scr, bb_scr, hb_scr):
    sb = _seg()
    c_len = LRU_CHUNK
    halo = SUBLANES
    n_tiles = sb // SUBLANES
    n_lat_t = SEQ // SUBLANES

    xp_scr[pl.ds(0, halo), :] = jnp.zeros((halo, LANES), F32)
    xp_scr[pl.ds(halo + sb, halo), :] = jnp.zeros((halo, LANES), F32)
    xp_scr[pl.ds(halo, sb), :] = x_ref[...].astype(F32)

    lam = lam_ref[0]
    sp = jnp.maximum(-lam, 0.0) + jnp.log1p(jnp.exp(-jnp.abs(lam)))
    cw = cw_ref[...]
    wg = wg_ref[0]
    bg = bg_ref[0]
    r8 = lax.broadcasted_iota(jnp.int32, (c_len, 1), 0)
    rm8 = r8 % SUBLANES

    def tile_shift(x, s):
        x3 = x.reshape(c_len // SUBLANES, SUBLANES, LANES)
        return pltpu.roll(x3, s, 1).reshape(c_len, LANES)

    def coeff_chunk(c, _):
        base = pl.multiple_of(c * c_len, c_len)
        w = xp_scr[pl.ds(base, c_len + 2 * halo), :]
        grow = base + r8
        in_lat = grow < SEQ
        pos = jnp.where(in_lat, grow, grow - SEQ)
        seglen = jnp.where(in_lat, SEQ, CTX_LEN)
        wlen = c_len + 2 * halo
        xm2 = jnp.where(pos >= 2, pltpu.roll(w, 2, 0)[halo:halo + c_len], 0.0)
        xm1 = jnp.where(pos >= 1, pltpu.roll(w, 1, 0)[halo:halo + c_len], 0.0)
        x0 = w[halo:halo + c_len]
        xp1 = jnp.where(pos + 1 < seglen, pltpu.roll(w, wlen - 1, 0)[halo:halo + c_len], 0.0)
        xc = xm2 * cw[0:1] + xm1 * cw[1:2] + x0 * cw[2:3] + xp1 * cw[3:4] + cb_ref[...]
        gates = _sigmoid(_dot(xc.astype(BF16), wg) + bg)
        rows = pl.ds(base, c_len)
        for d, (a_scr, b_scr) in enumerate(((af_scr, bf_scr), (ab_scr, bb_scr))):
            rg = gates[:, (2 * d) * LANES:(2 * d + 1) * LANES]
            ig = gates[:, (2 * d + 1) * LANES:(2 * d + 2) * LANES]
            log_a = -LRU_C * rg * sp[d:d + 1]
            a = jnp.exp(log_a)
            th = jnp.tanh(log_a)
            bv = jnp.sqrt(-2.0 * th / (1.0 - th)) * (ig * xc)
            for sft in (1, 2, 4):
                if d == 0:
                    a_s = tile_shift(a, sft)
                    b_s = tile_shift(bv, sft)
                    ok = rm8 >= sft
                else:
                    a_s = tile_shift(a, SUBLANES - sft)
                    b_s = tile_shift(bv, SUBLANES - sft)
                    ok = rm8 < SUBLANES - sft
                bv = jnp.where(ok, a * b_s + bv, bv)
                a = jnp.where(ok, a * a_s, a)
            a_scr[rows, :] = a
            b_scr[rows, :] = bv
        return 0

    lax.fori_loop(0, sb // c_len, coeff_chunk, 0)

    def carry_step(s, carry):
        hf, hb = carry
        tf = jnp.where(s < n_tiles - n_lat_t, n_lat_t + s, s - (n_tiles - n_lat_t))
        tb = n_tiles - 1 - s
        rf = pl.ds(pl.multiple_of(tf * SUBLANES, SUBLANES), SUBLANES)
        rk = pl.ds(pl.multiple_of(tb * SUBLANES, SUBLANES), SUBLANES)
        h1 = af_scr[rf, :] * hf + bf_scr[rf, :]
        y_ref[rf, :] = h1
        h2 = ab_scr[rk, :] * hb + bb_scr[rk, :]
        hb_scr[rk, :] = h2
        return (jnp.broadcast_to(h1[SUBLANES - 1:SUBLANES, :], (SUBLANES, LANES)),
                jnp.broadcast_to(h2[0:1, :], (SUBLANES, LANES)))

    zero = jnp.zeros((SUBLANES, LANES), F32)
    lax.fori_loop(0, n_tiles, carry_step, (zero, zero), unroll=8)
    y_ref[...] += hb_scr[...]


def _lru_call(u, wg, bg, lam, cw, cb):
    t = u.shape[0]
    sb = _seg()
    nb = LRU_W // LANES
    return pl.pallas_call(
        _lru_kernel,
        grid=(BATCH, nb),
        in_specs=[pl.BlockSpec((sb, LANES), lambda b, c: (b, _C_LX + c)),
                  pl.BlockSpec((1, LANES, 4 * LANES), lambda b, c: (c, 0, 0)),
                  pl.BlockSpec((1, 1, 4 * LANES), lambda b, c: (c, 0, 0)),
                  pl.BlockSpec((1, 2, LANES), lambda b, c: (c, 0, 0)),
                  pl.BlockSpec((CONV_W, LANES), lambda b, c: (0, c)),
                  pl.BlockSpec((1, LANES), lambda b, c: (0, c))],
        out_specs=pl.BlockSpec((sb, LANES), lambda b, c: (b, c)),
        out_shape=jax.ShapeDtypeStruct((t, LRU_W), F32),
        scratch_shapes=[pltpu.VMEM((sb + 2 * SUBLANES, LANES), F32)] + [pltpu.VMEM((sb, LANES), F32)] * 5,
        compiler_params=_params(("arbitrary", "arbitrary")),
        name="rglru",
    )(u, wg, bg, lam, cw, cb)


def _group_rms(o, group):
    ri = lax.broadcasted_iota(jnp.int32, (LANES, LANES), 0) // group
    ci = lax.broadcasted_iota(jnp.int32, (LANES, LANES), 1) // group
    ones_bd = (ri == ci).astype(BF16)
    outs = []
    for tix in range(o.shape[1] // LANES):
        x = o[:, tix * LANES:(tix + 1) * LANES]
        sq = x * x
        hi = sq.astype(BF16)
        lo = (sq - hi.astype(F32)).astype(BF16)
        ss = _dot(hi, ones_bd) + _dot(lo, ones_bd)
        outs.append(x * lax.rsqrt(ss * (1.0 / group) + NORM_EPS))
    return jnp.concatenate(outs, axis=1)


def _merge_kernel(*refs, n_gate):
    (x_ref, ml_ref, mc_ref, oret_ref, ohg_ref, ylru_ref, rg_ref, hgt_ref, lgt_ref) = refs[:9]
    gate_refs = refs[9:9 + 3 * n_gate]
    (gng_ref, ghg_ref, wr_ref, wh_ref, wl_ref, wo_ref, n2g_ref, rwh_ref, rwm_ref, rb_ref,
     xo_ref, h2_ref, lg_ref) = refs[9 + 3 * n_gate:]
    tiles_per_b = _seg() // ROW_TILE
    is_ctx = (pl.program_id(0) % tiles_per_b) >= SEQ // ROW_TILE
    ml = ml_ref[0]
    mc = mc_ref[0]

    def gate(i):
        parts = [gate_refs[i * n_gate + j][...].astype(F32) for j in range(n_gate)]
        return _sigmoid(parts[0] if n_gate == 1 else jnp.concatenate(parts, axis=1))

    rg = rg_ref[...].astype(F32)
    ret_in = _group_rms(oret_ref[...], RET_DK) * gng_ref[...] * (rg * _sigmoid(rg))
    ret = _dot(ret_in.astype(BF16), wr_ref[0])
    hg_in = _group_rms(ohg_ref[...], HG_DK) * ghg_ref[...] * _sigmoid(hgt_ref[...].astype(F32))
    hgr = _dot(hg_in.astype(BF16), wh_ref[0])
    lgt = lgt_ref[...].astype(F32)
    gelu = 0.5 * lgt * (1.0 + jnp.tanh(0.7978845608028654 * (lgt + 0.044715 * lgt * lgt * lgt)))
    lru = _dot((ylru_ref[...] * gelu).astype(BF16), wl_ref[0])
    merged = gate(0) * ret + gate(1) * hgr + gate(2) * lru
    mix = _dot(merged.astype(BF16), wo_ref[0])
    g1 = jnp.where(is_ctx, mc[2:3, :], ml[2:3, :])
    xn = x_ref[...] + g1 * mix
    xo_ref[...] = xn
    h2 = _norm_modulate(xn, n2g_ref[...], ml, mc, is_ctx, 3)
    h2_ref[...] = h2.astype(BF16)
    hh, hm, _ = _split3(h2)
    lg_ref[...] = _dot(hh, rwh_ref[...]) + (_dot(hh, rwm_ref[...]) + _dot(hm, rwh_ref[...])) + rb_ref[...]


def _merge_call(xs, mod, o_ret, o_hg, y_lru, u, gn_g, hgn_g, wr, wh, wl, wo, n2g, rw_hi, rw_mid, rb, layer):
    t = xs.shape[0]
    r = ROW_TILE
    tiles_per_b = _seg() // r
    n_gate = D_MODEL // 512
    row = lambda w: pl.BlockSpec((r, w), lambda i: (i, 0))
    ucol = lambda blk: pl.BlockSpec((r, 512), lambda i: (i, blk))

    def full(a):
        if a.ndim == 3:
            return pl.BlockSpec((1,) + a.shape[1:], lambda i: (layer, 0, 0))
        return pl.BlockSpec(a.shape, lambda i: (0,) * a.ndim)
    gate_base = _C_GATES * LANES // 512
    in_specs = [row(D_MODEL),
                pl.BlockSpec((1, N_MOD, D_MODEL), lambda i: (i // tiles_per_b, 0, 0)),
                pl.BlockSpec((1, N_MOD, D_MODEL), lambda i: (BATCH, 0, 0)),
                row(RET_W), row(HG_W), row(LRU_W),
                ucol(_C_RG * LANES // 512), ucol(_C_HGT * LANES // 512), ucol(_C_LGT * LANES // 512)]
    in_specs += [ucol(gate_base + j) for j in range(3 * n_gate)]
    consts = [gn_g, hgn_g, wr, wh, wl, wo, n2g, rw_hi, rw_mid, rb]
    in_specs += [full(a) for a in consts]
    return pl.pallas_call(
        functools.partial(_merge_kernel, n_gate=n_gate),
        grid=(t // r,),
        in_specs=in_specs,
        out_specs=[row(D_MODEL), row(D_MODEL), row(LANES)],
        out_shape=[jax.ShapeDtypeStruct((t, D_MODEL), F32), jax.ShapeDtypeStruct((t, D_MODEL), BF16),
                   jax.ShapeDtypeStruct((t, LANES), F32)],
        compiler_params=_params(("arbitrary",)),
        name="merge",
    )(xs, mod, mod, o_ret, o_hg, y_lru, *([u] * (3 + 3 * n_gate)), *consts)


def _route_kernel(lg_ref, oi_ref, ow_ref, lst_ref, meta_ref, cnt_ref, off_scr):
    @pl.when(pl.program_id(0) == 0)
    def _():
        off_scr[...] = jnp.zeros(off_scr.shape, F32)

    r = lg_ref.shape[0]
    lane = lax.broadcasted_iota(jnp.int32, (r, LANES), 1)
    neg = -jnp.inf
    l = jnp.where(lane < N_EXPERTS, lg_ref[...], neg)
    vals, hots = [], []
    lanef = lane.astype(F32)
    for _ in range(TOP_K):
        m = jnp.max(l, axis=-1, keepdims=True)
        idxf = jnp.min(jnp.where(l == m, lanef, float(LANES)), axis=-1, keepdims=True)
        hot = lanef == idxf
        l = jnp.where(hot, neg, l)
        vals.append(m)
        hots.append(hot)
    es = [jnp.exp(v - vals[0]) for v in vals]
    den = es[0]
    for e in es[1:]:
        den = den + e
    chosen = hots[0].astype(F32)
    for hot in hots[1:]:
        chosen = chosen + hot.astype(F32)
    ri = lax.broadcasted_iota(jnp.int32, (r, r), 0)
    ci = lax.broadcasted_iota(jnp.int32, (r, r), 1)
    before = _dot((ri > ci).astype(BF16), chosen.astype(BF16))
    count = jnp.sum(chosen, axis=0, keepdims=True)
    run = jnp.floor((count + (SUBLANES - 1.0)) * (1.0 / SUBLANES)) * SUBLANES
    li = lax.broadcasted_iota(jnp.int32, (LANES, LANES), 0)
    lj = lax.broadcasted_iota(jnp.int32, (LANES, LANES), 1)
    run8 = jnp.broadcast_to(run, (SUBLANES, LANES))
    start = _dot(run8.astype(BF16), (li < lj).astype(BF16))[0:1, :]
    pos = start + before
    oi = jnp.zeros((r, LANES), jnp.int32)
    ow = jnp.zeros((r, LANES), F32)
    lsf = jnp.zeros((r, LANES), F32)
    for k in range(TOP_K):
        row_k = jnp.sum(jnp.where(hots[k], pos, 0.0), axis=-1, keepdims=True)
        oi = jnp.where(lane == k, row_k.astype(jnp.int32), oi)
        lsf = jnp.where(lane == k, row_k, lsf)
        ow = jnp.where(lane == k, es[k] / den, ow)
    oi_ref[...] = oi
    ow_ref[...] = ow
    lst_ref[0] = jnp.transpose(lsf)[0:SUBLANES, :]
    mrow = lax.broadcasted_iota(jnp.int32, (SUBLANES, LANES), 0)
    meta_ref[0] = jnp.where(mrow == 0, run8, jnp.where(mrow == 1, jnp.broadcast_to(start, (SUBLANES, LANES)),
                                                       off_scr[...]))
    off_scr[...] = off_scr[...] + run
    cnt_ref[...] = off_scr[...]


def _route_call(logits):
    t = logits.shape[0]
    r = ROW_TILE
    row = pl.BlockSpec((r, LANES), lambda i: (i, 0))
    return pl.pallas_call(
        _route_kernel,
        grid=(t // r,),
        in_specs=[row],
        out_specs=[row, row, pl.BlockSpec((1, SUBLANES, r), lambda i: (i, 0, 0)),
                   pl.BlockSpec((1, SUBLANES, LANES), lambda i: (i, 0, 0)),
                   pl.BlockSpec((SUBLANES, LANES), lambda i: (0, 0))],
        out_shape=[jax.ShapeDtypeStruct((t, LANES), jnp.int32), jax.ShapeDtypeStruct((t, LANES), F32),
                   jax.ShapeDtypeStruct((t // r, SUBLANES, r), F32),
                   jax.ShapeDtypeStruct((t // r, SUBLANES, LANES), F32),
                   jax.ShapeDtypeStruct((SUBLANES, LANES), F32)],
        scratch_shapes=[pltpu.VMEM((SUBLANES, LANES), F32)],
        compiler_params=_params(("arbitrary",)),
        name="route",
    )(logits)


def _pack_pairs(x):
    m = x.shape[1] // 2
    bits = pltpu.bitcast(x, jnp.uint32)
    return bits[:, :m] | (bits[:, m:] >> 16)


def _unpack_pairs(w):
    hi = pltpu.bitcast(w & jnp.uint32(0xFFFF0000), F32).astype(BF16)
    lo = pltpu.bitcast(w << 16, F32).astype(BF16)
    return hi, lo


def _local_rows():
    rows = ROW_TILE * TOP_K + N_EXPERTS * (SUBLANES - 1)
    return -(-rows // ROW_TILE) * ROW_TILE


RUN_PIECE = 4 * SUBLANES


def _max_pieces():
    return _local_rows() // RUN_PIECE, N_EXPERTS * (RUN_PIECE // SUBLANES - 1)


def _meta_width():
    n_big, n_small = _max_pieces()
    return -(-(2 + 2 * n_big + 2 * n_small) // LANES) * LANES


def _piece_tables(first_slot, first_row, pieces):
    per_big = RUN_PIECE // SUBLANES
    max_big, max_small = _max_pieces()

    def flat(count, limit, offset_rows, step):
        ends = jnp.cumsum(count, axis=1)
        p = jnp.arange(limit, dtype=jnp.int32)[None, :, None]
        mine = ((ends - count)[:, None, :] <= p) & (p < ends[:, None, :])
        take = lambda a: jnp.sum(jnp.where(mine, a[:, None, :], 0), axis=2)
        j = p[:, :, 0] - take(ends - count)
        off = take(offset_rows) + j * step
        return take(first_row) + off, take(first_slot) + off, ends[:, -1:]

    n_big = pieces // per_big
    big_row, big_slot, tot_big = flat(n_big, max_big, jnp.zeros_like(pieces), RUN_PIECE)
    small_row, small_slot, tot_small = flat(pieces - n_big * per_big, max_small, n_big * RUN_PIECE, SUBLANES)
    table = jnp.concatenate([tot_big, tot_small, big_row, big_slot, small_row, small_slot], axis=1)
    return jnp.pad(table, ((0, 0), (0, _meta_width() - table.shape[1])))[:, None, :]


def _run_copies(meta_ref, make_copy, do):
    max_big, max_small = _max_pieces()
    big0 = 2
    small0 = 2 + 2 * max_big

    def big(p, _):
        do(make_copy(pl.multiple_of(meta_ref[0, 0, big0 + p], SUBLANES),
                     pl.multiple_of(meta_ref[0, 0, big0 + max_big + p], SUBLANES), RUN_PIECE))
        return 0

    def small(p, _):
        do(make_copy(pl.multiple_of(meta_ref[0, 0, small0 + p], SUBLANES),
                     pl.multiple_of(meta_ref[0, 0, small0 + max_small + p], SUBLANES), SUBLANES))
        return 0

    lax.fori_loop(0, meta_ref[0, 0, 0], big, 0)
    lax.fori_loop(0, meta_ref[0, 0, 1], small, 0)


def _dispatch_kernel(meta_ref, prev_meta_ref, lst_ref, h_ref, xs_in_ref, xs_ref, perm_scr, sem):
    del xs_in_ref
    i = pl.program_id(0)
    cur = i % 2
    r = h_ref.shape[0]
    ls = lst_ref[0].astype(jnp.int32)
    hb = h_ref[...]
    for s in range(perm_scr.shape[1] // r):
        rows = s * r + lax.broadcasted_iota(jnp.int32, (r, r), 0)
        sel = rows == ls[0:1, :]
        for k in range(1, TOP_K):
            sel = jnp.logical_or(sel, rows == ls[k:k + 1, :])
        perm_scr[cur, pl.ds(s * r, r), :] = _pack_pairs(_dot(jnp.where(sel, 1.0, 0.0).astype(BF16), hb))

    def copies_from(buf):
        def make_copy(row, slot, n):
            return pltpu.make_async_copy(perm_scr.at[buf, pl.ds(row, n)], xs_ref.at[pl.ds(slot, n)], sem.at[buf])
        return make_copy

    _run_copies(meta_ref, copies_from(cur), lambda c: c.start())

    @pl.when(i > 0)
    def _():
        _run_copies(prev_meta_ref, copies_from(1 - cur), lambda c: c.wait())

    @pl.when(i == pl.num_programs(0) - 1)
    def _():
        _run_copies(meta_ref, copies_from(cur), lambda c: c.wait())


def _dispatch_call(h2, meta, lst, slots):
    t = h2.shape[0]
    r = ROW_TILE
    return pl.pallas_call(
        _dispatch_kernel,
        grid=(t // r,),
        in_specs=[pl.BlockSpec((1, 1, _meta_width()), lambda i: (i, 0, 0), memory_space=pltpu.SMEM),
                  pl.BlockSpec((1, 1, _meta_width()), lambda i: (jnp.maximum(i - 1, 0), 0, 0), memory_space=pltpu.SMEM),
                  pl.BlockSpec((1, SUBLANES, r), lambda i: (i, 0, 0)),
                  pl.BlockSpec((r, D_MODEL), lambda i: (i, 0)),
                  pl.BlockSpec(memory_space=pl.ANY)],
        out_specs=pl.BlockSpec(memory_space=pl.ANY),
        out_shape=jax.ShapeDtypeStruct(slots.shape, jnp.uint32),
        scratch_shapes=[pltpu.VMEM((2, _local_rows(), D_MODEL // 2), jnp.uint32), pltpu.SemaphoreType.DMA((2,))],
        input_output_aliases={4: 0},
        compiler_params=_params(("arbitrary",)),
        name="dispatch",
    )(meta, meta, lst, h2, slots)


def _expert_kernel(bexp_ref, nv_ref, next_ref, par_ref, x_ref, wgu_hbm, bgu_ref, wd_hbm, bd_ref, y_ref,
                   wgu_scr, wd_scr, gu_stage, d_stage, sem, *, layer):
    i = pl.program_id(0)
    f = D_MODEL
    prev = bexp_ref[jnp.maximum(i - 1, 0)]
    new_expert = jnp.logical_or(i == 0, bexp_ref[i] != prev)

    def weight_copies(e, b):
        return (pltpu.make_async_copy(wgu_hbm.at[layer, e], gu_stage.at[b], sem.at[b, 0]),
                pltpu.make_async_copy(wd_hbm.at[layer, e], d_stage.at[b], sem.at[b, 1]))

    @pl.when(jnp.logical_and(i < nv_ref[0], new_expert))
    def _():
        b = par_ref[i]

        @pl.when(i == 0)
        def _():
            for c in weight_copies(bexp_ref[i], b):
                c.start()

        for c in weight_copies(bexp_ref[i], b):
            c.wait()

        @pl.when(next_ref[i] >= 0)
        def _():
            for c in weight_copies(next_ref[i], 1 - b):
                c.start()

        slab = 256
        for s in range(D_MODEL // slab):
            wgu_scr[pl.ds(s * slab, slab), :] = gu_stage[b, pl.ds(s * slab, slab), :].astype(BF16)
            wd_scr[pl.ds(s * slab, slab), :] = d_stage[b, pl.ds(s * slab, slab), :].astype(BF16)

    @pl.when(i < nv_ref[0])
    def _():
        x_hi, x_lo = _unpack_pairs(x_ref[...])
        half = D_MODEL // 2
        gu = _dot(x_hi, wgu_scr[:half, :]) + _dot(x_lo, wgu_scr[half:, :]) + bgu_ref[0, 0]
        gate = jnp.minimum(gu[:, :f], SWIGLU_LIMIT)
        up = jnp.clip(gu[:, f:], -SWIGLU_LIMIT, SWIGLU_LIMIT)
        glu = gate * _sigmoid(SWIGLU_ALPHA * gate)
        y = _dot(((up + 1.0) * glu).astype(BF16), wd_scr[...]) + bd_ref[0, 0]
        y_ref[...] = _pack_pairs(y.astype(BF16).astype(F32))

    @pl.when(i >= nv_ref[0])
    def _():
        y_ref[...] = jnp.zeros(y_ref.shape, jnp.uint32)


def _expert_call(slots, bexp, nvalid, next_expert, stage, w_gu, b_gu, w_down, b_down, layer):
    s = slots.shape[0]
    nb = s // MOE_BLOCK
    f = D_MODEL
    blk = lambda i, be, nv, nx, pr: (jnp.minimum(i, nv[0] - 1), 0)
    exp3 = lambda i, be, nv, nx, pr: (layer, be[jnp.minimum(i, nv[0] - 1)], 0, 0)
    grid_spec = pltpu.PrefetchScalarGridSpec(
        num_scalar_prefetch=4,
        grid=(nb,),
        in_specs=[pl.BlockSpec((MOE_BLOCK, D_MODEL // 2), blk),
                  pl.BlockSpec(memory_space=pl.ANY),
                  pl.BlockSpec((1, 1, 1, 2 * f), exp3),
                  pl.BlockSpec(memory_space=pl.ANY),
                  pl.BlockSpec((1, 1, 1, D_MODEL), exp3)],
        out_specs=pl.BlockSpec((MOE_BLOCK, D_MODEL // 2), lambda i, be, nv, nx, pr: (i, 0)),
        scratch_shapes=[pltpu.VMEM((D_MODEL, 2 * f), BF16), pltpu.VMEM((f, D_MODEL), BF16),
                        pltpu.VMEM((2, D_MODEL, 2 * f), F32), pltpu.VMEM((2, f, D_MODEL), F32),
                        pltpu.SemaphoreType.DMA((2, 2))],
    )
    return pl.pallas_call(
        functools.partial(_expert_kernel, layer=layer),
        grid_spec=grid_spec,
        out_shape=jax.ShapeDtypeStruct((s, D_MODEL // 2), jnp.uint32),
        compiler_params=_params(("arbitrary",)),
        name="experts",
    )(bexp, nvalid, next_expert, stage, slots, w_gu, b_gu.reshape(DEPTH, N_EXPERTS, 1, 2 * f), w_down,
      b_down.reshape(DEPTH, N_EXPERTS, 1, D_MODEL))


def _combine_kernel(meta_ref, next_meta_ref, x_ref, ml_ref, mc_ref, oi_ref, w_ref, fg_ref, y_ref, o_ref, buf, sem,
                    *, last, n_t):
    r = x_ref.shape[0]
    i = pl.program_id(0)
    cur = i % 2

    def copies_into(b):
        def make_copy(row, slot, n):
            return pltpu.make_async_copy(y_ref.at[pl.ds(slot, n)], buf.at[b, pl.ds(row, n)], sem.at[b])
        return make_copy

    @pl.when(i == 0)
    def _():
        buf[...] = jnp.zeros(buf.shape, jnp.uint32)
        _run_copies(meta_ref, copies_into(0), lambda c: c.start())

    @pl.when(i + 1 < pl.num_programs(0))
    def _():
        _run_copies(next_meta_ref, copies_into(1 - cur), lambda c: c.start())

    _run_copies(meta_ref, copies_into(cur), lambda c: c.wait())

    if last:
        is_ctx = False
    else:
        is_ctx = (i % n_t) >= SEQ // ROW_TILE
    w = w_ref[...]
    oi = oi_ref[...]
    ff_hi = jnp.zeros((r, D_MODEL // 2), F32)
    ff_lo = jnp.zeros((r, D_MODEL // 2), F32)
    for s in range(buf.shape[1] // r):
        cols = s * r + lax.broadcasted_iota(jnp.int32, (r, r), 1)
        wm = jnp.zeros((r, r), F32)
        for k in range(TOP_K):
            wm = jnp.where(cols == oi[:, k:k + 1], w[:, k:k + 1], wm)
        y_hi, y_lo = _unpack_pairs(buf[cur, pl.ds(s * r, r), :])
        ff_hi = ff_hi + _dot(wm.astype(BF16), y_hi)
        ff_lo = ff_lo + _dot(wm.astype(BF16), y_lo)
    ff = jnp.concatenate([ff_hi, ff_lo], axis=1)
    g2 = jnp.where(is_ctx, mc_ref[0][5:6, :], ml_ref[0][5:6, :])
    xn = x_ref[...] + g2 * ff
    if last:
        ms = jnp.mean(xn * xn, axis=-1, keepdims=True)
        xn = xn * lax.rsqrt(ms + NORM_EPS) * fg_ref[...]
    o_ref[...] = xn


def _combine_call(xs, mod, meta, rows, wts, y_slots, final_g, last):
    t = xs.shape[0]
    r = ROW_TILE
    tiles_per_b = _seg() // r
    n_t = SEQ // r if last else tiles_per_b
    n_steps = BATCH * n_t
    tile = lambda i: (i // n_t) * tiles_per_b + i % n_t
    out_rows = BATCH * SEQ if last else t
    return pl.pallas_call(
        functools.partial(_combine_kernel, last=last, n_t=n_t),
        grid=(n_steps,),
        in_specs=[pl.BlockSpec((1, 1, _meta_width()), lambda i: (tile(i), 0, 0), memory_space=pltpu.SMEM),
                  pl.BlockSpec((1, 1, _meta_width()), lambda i: (tile(jnp.minimum(i + 1, n_steps - 1)), 0, 0),
                               memory_space=pltpu.SMEM),
                  pl.BlockSpec((r, D_MODEL), lambda i: (tile(i), 0)),
                  pl.BlockSpec((1, N_MOD, D_MODEL), lambda i: (i // n_t, 0, 0)),
                  pl.BlockSpec((1, N_MOD, D_MODEL), lambda i: (BATCH, 0, 0)),
                  pl.BlockSpec((r, LANES), lambda i: (tile(i), 0)),
                  pl.BlockSpec((r, LANES), lambda i: (tile(i), 0)),
                  pl.BlockSpec((1, D_MODEL), lambda i: (0, 0)),
                  pl.BlockSpec(memory_space=pl.ANY)],
        out_specs=pl.BlockSpec((r, D_MODEL), lambda i: (i, 0)),
        out_shape=jax.ShapeDtypeStruct((out_rows, D_MODEL), F32),
        scratch_shapes=[pltpu.VMEM((2, _local_rows(), D_MODEL // 2), jnp.uint32), pltpu.SemaphoreType.DMA((2,))],
        compiler_params=_params(("arbitrary",)),
        name="combine",
    )(meta, meta, xs, mod, mod, rows, wts, final_g, y_slots)


def _rotary_tables():
    rows = SEQ // GRID_W
    row = jnp.repeat(jnp.arange(rows, dtype=F32), GRID_W)
    colp = jnp.tile(jnp.arange(GRID_W, dtype=F32), rows)
    n_freq = RET_DK // 4
    inv = ROPE_BASE ** (-jnp.arange(n_freq, dtype=F32) / n_freq)
    ang = jnp.concatenate([row[:, None] * inv, colp[:, None] * inv], axis=-1)
    cos, sin = jnp.cos(ang), jnp.sin(ang)
    reps = LANES // (RET_DK // 2)
    cos_t = jnp.tile(cos, (1, reps))
    sign = jnp.where((jnp.arange(LANES) % RET_DK) < RET_DK // 2, -1.0, 1.0).astype(F32)
    sin_t = jnp.tile(sin, (1, reps)) * sign
    return cos_t, sin_t


def _lru_gate_weights(wa, ba, wx, bx):
    nb = LRU_W // LANES
    per = LANES // LRU_BD
    eye = jnp.eye(per, dtype=F32)

    def embed(w):
        w = w.reshape(nb, per, LRU_BD, LRU_BD)
        return jnp.einsum('cpij,pq->cpiqj', w, eye).reshape(nb, LANES, LANES)

    wg = jnp.concatenate([embed(wa[0]), embed(wx[0]), embed(wa[1]), embed(wx[1])], axis=-1)
    bias = jnp.concatenate([ba[0].reshape(nb, 1, LANES), bx[0].reshape(nb, 1, LANES),
                            ba[1].reshape(nb, 1, LANES), bx[1].reshape(nb, 1, LANES)], axis=-1)
    return wg.astype(BF16), bias


def kernel(x, c, ctx, c_ctx, mod_w, mod_b, norm1_g, norm2_g, final_g, w_in, ret_decay, ret_gn_g, w_ret_o,
           hgrn_lb_logits, hgrn_gn_g, w_hgrn_o, lru_conv_w, lru_conv_b, lru_wa, lru_ba, lru_wx, lru_bx,
           lru_lambda, w_lru_o, w_out, router_w, router_b, exp_w_gu, exp_b_gu, exp_w_down, exp_b_down):
    assert CTX_LEN == RET_CHUNK and SEQ % RET_CHUNK == 0
    sb = _seg()
    t = BATCH * sb
    xs = jnp.concatenate([x, ctx], axis=1).reshape(t, D_MODEL)

    cvec = jnp.concatenate([c, c_ctx[None, :], jnp.zeros((SUBLANES - BATCH - 1, D_MODEL), F32)], axis=0)
    mod_all = _mod_call(cvec, mod_w, mod_b).reshape(DEPTH, SUBLANES, N_MOD, D_MODEL)

    cos_t, sin_t = _rotary_tables()
    lb_logits = jnp.transpose(hgrn_lb_logits, (1, 0, 2))
    n_slots = t * TOP_K + (t // ROW_TILE) * N_EXPERTS * (SUBLANES - 1) + N_EXPERTS * (MOE_BLOCK - 1)
    n_blocks = -(-n_slots // MOE_BLOCK)
    n_slots = n_blocks * MOE_BLOCK
    slots = jnp.zeros((n_slots, D_MODEL // 2), jnp.uint32)
    rw_pad = jnp.pad(router_w, ((0, 0), (0, 0), (0, LANES - N_EXPERTS)))
    rw_hi = rw_pad.astype(BF16)
    rw_mid = (rw_pad - rw_hi.astype(F32)).astype(BF16)
    rb_pad = jnp.pad(router_b, ((0, 0), (0, LANES - N_EXPERTS)))
    w_ret_b, w_hg_b, w_lru_b, w_out_b = (w.astype(BF16) for w in (w_ret_o, w_hgrn_o, w_lru_o, w_out))
    block_first_slot = jnp.arange(n_blocks, dtype=jnp.int32) * MOE_BLOCK
    expert_ids = jnp.arange(N_EXPERTS, dtype=jnp.int32)

    out = None
    for l in range(DEPTH):
        last = l == DEPTH - 1
        mod = mod_all[l]
        u, uz = _inproj_call(xs, mod, norm1_g[l][None, :], w_in, l)

        dec = jnp.broadcast_to(ret_decay[l].reshape(2, RET_HEADS // 2, 2, 1).transpose(1, 0, 2, 3),
                               (RET_HEADS // 2, 2, 2, LANES)).reshape(RET_HEADS // 2, 4, LANES)
        o_ret = _ret_call(u, dec, cos_t, sin_t)
        o_hg = _hg_call(u, uz, lb_logits, l)
        wg, bg = _lru_gate_weights(lru_wa[l], lru_ba[l], lru_wx[l], lru_bx[l])
        lam = lru_lambda[l].reshape(2, LRU_W // LANES, LANES).transpose(1, 0, 2)
        y_lru = _lru_call(u, wg, bg, lam, lru_conv_w[l], lru_conv_b[l][None, :])

        xs, h2, logits = _merge_call(
            xs, mod, o_ret, o_hg, y_lru, u, ret_gn_g[l][None, :], hgrn_gn_g[l][None, :],
            w_ret_b, w_hg_b, w_lru_b, w_out_b, norm2_g[l][None, :], rw_hi[l], rw_mid[l], rb_pad[l][None, :], l)

        rows, wts, lst, tile_meta, cnt = _route_call(logits)
        run = tile_meta[:, 0, :N_EXPERTS].astype(jnp.int32)
        first_row = tile_meta[:, 1, :N_EXPERTS].astype(jnp.int32)
        earlier = tile_meta[:, 2, :N_EXPERTS].astype(jnp.int32)
        counts = cnt[0, :N_EXPERTS].astype(jnp.int32)
        padded = (counts + MOE_BLOCK - 1) // MOE_BLOCK * MOE_BLOCK
        pad_end = jnp.cumsum(padded)
        first_slot = (pad_end - padded)[None, :] + earlier
        meta = _piece_tables(first_slot, first_row, run // SUBLANES)
        bexp = jnp.minimum(jnp.sum((pad_end[None, :] <= block_first_slot[:, None]).astype(jnp.int32), axis=1),
                           N_EXPERTS - 1)
        nvalid = (pad_end[-1:] // MOE_BLOCK).astype(jnp.int32)
        present = padded > 0
        later = jnp.where(present[None, :] & (expert_ids[None, :] > expert_ids[:, None]), expert_ids[None, :],
                          N_EXPERTS)
        next_of = jnp.min(later, axis=1)
        next_of = jnp.where(next_of >= N_EXPERTS, -1, next_of)
        stage_of = (jnp.cumsum(present.astype(jnp.int32)) - present.astype(jnp.int32)) % 2
        is_e = bexp[:, None] == expert_ids[None, :]
        next_expert = jnp.sum(jnp.where(is_e, next_of[None, :], 0), axis=1)
        stage = jnp.sum(jnp.where(is_e, stage_of[None, :], 0), axis=1)

        slots = _dispatch_call(h2, meta, lst, slots)
        y_slots = _expert_call(slots, bexp, nvalid, next_expert, stage, exp_w_gu, exp_b_gu, exp_w_down,
                               exp_b_down, l)
        res = _combine_call(xs, mod, meta, rows, wts, y_slots, final_g[None, :], last)
        if last:
            out = res.reshape(BATCH, SEQ, D_MODEL)
        else:
            xs = res
    return out
```

```python
import functools

import jax
import jax.numpy as jnp
from jax import lax
from jax.experimental import pallas as pl
from jax.experimental.pallas import tpu as pltpu

F32 = jnp.float32
BF16 = jnp.bfloat16

D_MODEL = 1024
BATCH = 4
SEQ = 4096
DEPTH = 4
GRID_W = 64
CTX_LEN = 256
N_MOD = 6
NORM_EPS = 1e-6
RET_HEADS = 8
RET_DK = 64
RET_W = 512
ROPE_BASE = 10000.0
HG_HEADS = 4
HG_DK = 128
HG_W = 512
LRU_W = 512
LRU_BLOCKS = 8
LRU_BD = 64
LRU_C = 8.0
CONV_W = 4
N_EXPERTS = 32
TOP_K = 4
SWIGLU_ALPHA = 1.702
SWIGLU_LIMIT = 7.0

LANES = 128
SUBLANES = 8
RET_CHUNK = 256
HG_CHUNK = 256
HG_BLOCK = 128
HG_SUB = 4
HG_HEADS_PER_STEP = 2
LOG2_E = 1.4426950408889634
LRU_CHUNK = 256
MOE_BLOCK = 256
ROW_TILE = 256
MOD_TN = 1536
VMEM_LIMIT = 56 * 1024 * 1024

_C_RQ, _C_RK, _C_RV, _C_RG = 0, 4, 8, 12
_C_HQ, _C_HFF, _C_HFB, _C_HI, _C_HGT = 16, 20, 24, 28, 32
_C_LX, _C_LGT = 36, 40
_C_GATES = 44


def _seg():
    return SEQ + CTX_LEN


def _in_cols():
    return _C_GATES * LANES + 3 * D_MODEL


def _dot(a, b):
    return jnp.dot(a, b, preferred_element_type=F32)


def _dot_nt(a, b):
    return lax.dot_general(a, b, (((1,), (1,)), ((), ())), preferred_element_type=F32)


def _dot_tn(a, b):
    return lax.dot_general(a, b, (((0,), (0,)), ((), ())), preferred_element_type=F32)


def _split3(x):
    hi = x.astype(BF16)
    r = x - hi.astype(F32)
    mid = r.astype(BF16)
    lo = (r - mid.astype(F32)).astype(BF16)
    return hi, mid, lo


def _dot_exact_lhs(a, x):
    hi, mid, lo = _split3(x)
    return _dot(a, hi) + _dot(a, mid) + _dot(a, lo)


def _dot3(a, b):
    ah, am, _ = _split3(a)
    bh, bm, _ = _split3(b)
    return _dot(ah, bh) + (_dot(ah, bm) + _dot(am, bh))


def _sigmoid(x):
    return 0.5 + 0.5 * jnp.tanh(0.5 * x)


def _log_sigmoid(x):
    return jnp.minimum(x, 0.0) - jnp.log1p(jnp.exp(-jnp.abs(x)))


def _params(sem):
    return pltpu.CompilerParams(dimension_semantics=sem, vmem_limit_bytes=VMEM_LIMIT)


def _mod_kernel(c_ref, w_ref, b_ref, o_ref):
    c = c_ref[...]
    s = c * _sigmoid(c)
    o_ref[0] = _dot3(s, w_ref[0]) + b_ref[0]


def _mod_call(cvec, mod_w, mod_b):
    n6 = N_MOD * D_MODEL
    tn = MOD_TN
    return pl.pallas_call(
        _mod_kernel,
        grid=(DEPTH, n6 // tn),
        in_specs=[pl.BlockSpec((SUBLANES, D_MODEL), lambda l, j: (0, 0)),
                  pl.BlockSpec((1, D_MODEL, tn), lambda l, j: (l, 0, j)),
                  pl.BlockSpec((1, 1, tn), lambda l, j: (l, 0, j))],
        out_specs=pl.BlockSpec((1, SUBLANES, tn), lambda l, j: (l, 0, j)),
        out_shape=jax.ShapeDtypeStruct((DEPTH, SUBLANES, n6), F32),
        compiler_params=_params(("arbitrary", "arbitrary")),
        name="modulation",
    )(cvec, mod_w, mod_b.reshape(DEPTH, 1, n6))


def _norm_modulate(x, g, mod_l, mod_c, is_ctx, first):
    ms = jnp.mean(x * x, axis=-1, keepdims=True)
    y = x * lax.rsqrt(ms + NORM_EPS) * g
    sh = jnp.where(is_ctx, mod_c[first:first + 1, :], mod_l[first:first + 1, :])
    sc = jnp.where(is_ctx, mod_c[first + 1:first + 2, :], mod_l[first + 1:first + 2, :])
    return y * (1.0 + sc) + sh


def _inproj_kernel(x_ref, ml_ref, mc_ref, g_ref, w_ref, o_ref, oz_ref, h_scr, *, tm, z_first, z_count):
    j = pl.program_id(1)

    @pl.when(j == 0)
    def _():
        tiles_per_b = _seg() // tm
        row0 = (pl.program_id(0) % tiles_per_b) * tm
        rows = row0 + lax.broadcasted_iota(jnp.int32, (tm, 1), 0)
        h = _norm_modulate(x_ref[...], g_ref[...], ml_ref[0], mc_ref[0], rows >= SEQ, 0)
        h_scr[...] = h.astype(BF16)

    acc = _dot(h_scr[...], w_ref[0].astype(BF16))
    o_ref[...] = acc.astype(BF16)

    @pl.when(jnp.logical_and(j >= z_first, j < z_first + z_count))
    def _():
        oz_ref[...] = acc


def _inproj_call(xs, mod, g, w, layer):
    t = xs.shape[0]
    tm = _seg() // 2
    ncol = _in_cols()
    tn = HG_W
    tiles_per_b = _seg() // tm
    z_first = _C_HFF * LANES // tn
    z_count = 2
    return pl.pallas_call(
        functools.partial(_inproj_kernel, tm=tm, z_first=z_first, z_count=z_count),
        grid=(t // tm, ncol // tn),
        in_specs=[pl.BlockSpec((tm, D_MODEL), lambda i, j: (i, 0)),
                  pl.BlockSpec((1, N_MOD, D_MODEL), lambda i, j: (i // tiles_per_b, 0, 0)),
                  pl.BlockSpec((1, N_MOD, D_MODEL), lambda i, j: (BATCH, 0, 0)),
                  pl.BlockSpec((1, D_MODEL), lambda i, j: (0, 0)),
                  pl.BlockSpec((1, D_MODEL, tn), lambda i, j: (layer, 0, j))],
        out_specs=[pl.BlockSpec((tm, tn), lambda i, j: (i, j)),
                   pl.BlockSpec((tm, tn), lambda i, j: (i, jnp.clip(j - z_first, 0, z_count - 1)))],
        out_shape=[jax.ShapeDtypeStruct((t, ncol), BF16), jax.ShapeDtypeStruct((t, z_count * tn), F32)],
        scratch_shapes=[pltpu.VMEM((tm, D_MODEL), BF16)],
        compiler_params=_params(("arbitrary", "arbitrary")),
        name="in_projection",
    )(xs, mod, mod, g, w)


def _ret_kernel(dec_ref, cos_ref, sin_ref, q_ref, k_ref, v_ref, o_ref, mask_scr, pw_scr):
    c_len = RET_CHUNK
    n_lat = SEQ // c_len
    n_all = _seg() // c_len
    lane = lax.broadcasted_iota(jnp.int32, (1, LANES), 1)
    head0 = lane < RET_DK
    first_half = (lane % RET_DK) < (RET_DK // 2)

    lg = _log_sigmoid(dec_ref[0])
    dvf = jnp.where(head0, lg[0:1], lg[1:2])
    dvb = jnp.where(head0, lg[2:3], lg[3:4])
    r = lax.broadcasted_iota(jnp.int32, (c_len, 1), 0).astype(F32)
    pw_scr[0] = jnp.exp((r + 1.0) * dvf)
    pw_scr[1] = jnp.exp((c_len - 1.0 - r) * dvf)
    pw_scr[2] = jnp.exp((c_len - r) * dvb)
    pw_scr[3] = jnp.exp(r * dvb)
    dcf = jnp.exp(c_len * dvf)
    dcb = jnp.exp(c_len * dvb)
    ri = lax.broadcasted_iota(jnp.int32, (c_len, c_len), 0)
    ci = lax.broadcasted_iota(jnp.int32, (c_len, c_len), 1)
    dm = (ri - ci).astype(F32)
    for h in range(2):
        lgf = jnp.concatenate([lg[h:h + 1]] * (c_len // LANES), axis=1)
        lgb = jnp.concatenate([lg[2 + h:3 + h]] * (c_len // LANES), axis=1)
        mask_scr[h] = jnp.exp(jnp.where(dm >= 0, dm * lgf, -dm * lgb))
    rb = lax.broadcasted_iota(jnp.int32, (LANES, LANES), 0) < RET_DK
    cb = lax.broadcasted_iota(jnp.int32, (LANES, LANES), 1) < RET_DK
    same_head = (rb == cb).astype(F32)

    def load(c):
        rows = pl.ds(pl.multiple_of(c * c_len, c_len), c_len)
        q = q_ref[rows, :].astype(F32)
        k = k_ref[rows, :].astype(F32) * (RET_DK ** -0.5)
        v = v_ref[rows, :].astype(F32)
        return rows, q, k, v

    def rotary(t, c):
        rows = pl.ds(pl.multiple_of(c * c_len, c_len), c_len)
        partner = jnp.where(first_half, pltpu.roll(t, LANES - RET_DK // 2, 1), pltpu.roll(t, RET_DK // 2, 1))
        return t * cos_ref[rows, :] + partner * sin_ref[rows, :]

    def load_rot(c):
        rows, q, k, v = load(c)
        is_lat = c < n_lat
        cc = jnp.minimum(c, n_lat - 1)
        q = jnp.where(is_lat, rotary(q, cc), q)
        k = jnp.where(is_lat, rotary(k, cc), k)
        return rows, q, k, v

    def fwd_step(s, st):
        c = jnp.where(s < n_all - n_lat, n_lat + s, s - (n_all - n_lat))
        rows, q, k, v = load_rot(c)
        kb = k.astype(BF16)
        vb = v.astype(BF16)
        o0 = _dot((_dot_nt(jnp.where(head0, q, 0.0).astype(BF16), kb) * mask_scr[0]).astype(BF16), vb)
        o1 = _dot((_dot_nt(jnp.where(head0, 0.0, q).astype(BF16), kb) * mask_scr[1]).astype(BF16), vb)
        o = jnp.where(head0, o0, o1)
        o = o + _dot_nt((q * pw_scr[0]).astype(BF16), st.astype(BF16))
        o_ref[rows, :] += o
        return st * dcf + _dot_tn(vb, (k * pw_scr[1]).astype(BF16)) * same_head

    def bwd_step(s, st):
        c = n_all - 1 - s
        rows, q, k, v = load_rot(c)
        o_ref[rows, :] += _dot_nt((q * pw_scr[2]).astype(BF16), st.astype(BF16))
        return st * dcb + _dot_tn(v.astype(BF16), (k * pw_scr[3]).astype(BF16)) * same_head

    o_ref[...] = jnp.zeros(o_ref.shape, F32)
    zero = jnp.zeros((LANES, LANES), F32)
    lax.fori_loop(0, n_all, lambda s, sts: (fwd_step(s, sts[0]), bwd_step(s, sts[1])), (zero, zero))


def _ret_call(u, dec, cos_t, sin_t):
    t = u.shape[0]
    sb = _seg()
    col = lambda off: pl.BlockSpec((sb, LANES), lambda b, p: (b, off + p))
    return pl.pallas_call(
        _ret_kernel,
        grid=(BATCH, RET_HEADS // 2),
        in_specs=[pl.BlockSpec((1, 4, LANES), lambda b, p: (p, 0, 0)),
                  pl.BlockSpec((SEQ, LANES), lambda b, p: (0, 0)),
                  pl.BlockSpec((SEQ, LANES), lambda b, p: (0, 0)),
                  col(_C_RQ), col(_C_RK), col(_C_RV)],
        out_specs=pl.BlockSpec((sb, LANES), lambda b, p: (b, p)),
        out_shape=jax.ShapeDtypeStruct((t, RET_W), F32),
        scratch_shapes=[pltpu.VMEM((2, RET_CHUNK, RET_CHUNK), F32),
                        pltpu.VMEM((4, RET_CHUNK, LANES), F32)],
        compiler_params=_params(("arbitrary", "arbitrary")),
        name="retention",
    )(dec, cos_t, sin_t, u, u, u)


def _hg_codes(fwd):
    c_len = HG_BLOCK
    ri = lax.broadcasted_iota(jnp.int32, (c_len, c_len), 0)
    ci = lax.broadcasted_iota(jnp.int32, (c_len, c_len), 1)
    dist = (ri - ci) if fwd else (ci - ri)
    code = jnp.full((c_len, c_len), -1, jnp.int32)
    half, level = c_len // 2, 0
    while half >= HG_SUB:
        sh = half.bit_length() - 1
        same_seg = jnp.right_shift(ri, sh + 1) == jnp.right_shift(ci, sh + 1)
        other_half = jnp.right_shift(ri, sh) != jnp.right_shift(ci, sh)
        code = jnp.where(same_seg & other_half & (dist > 0), level, code)
        half, level = half // 2, level + 1
    sub_sh = HG_SUB.bit_length() - 1
    same_sub = jnp.right_shift(ri, sub_sh) == jnp.right_shift(ci, sub_sh)
    for dl in range(HG_SUB):
        code = jnp.where(same_sub & (dist == dl), 100 + dl, code)
    return code


def _hg_chunk(fwd, q, z, v, lb, st, code, tri):
    c_len = HG_CHUNK

    a = jnp.log(lb)
    b = jnp.log1p(-lb) + _log_sigmoid(z)
    lf = jnp.maximum(a, b) + jnp.log1p(jnp.exp(-jnp.abs(a - b)))
    kin = (1.0 - lb) * _sigmoid(-z)

    g = _dot_exact_lhs(tri, lf) * LOG2_E
    g_end = g[c_len - 1:c_len, :] if fwd else g[0:1, :]
    vb = v.astype(BF16)

    o = _dot_nt((q * jnp.exp2(g)).astype(BF16), st.astype(BF16))

    blk = HG_BLOCK
    lo, hi = slice(0, blk), slice(blk, 2 * blk)
    qrows, krows = (hi, lo) if fwd else (lo, hi)
    bnd = g[blk - 1:blk, :] if fwd else g[blk:blk + 1, :]
    qs = (q[qrows] * jnp.exp2(g[qrows] - bnd)).astype(BF16)
    ks = (kin[krows] * jnp.exp2(bnd - g[krows])).astype(BF16)
    cross = _dot(_dot_nt(qs, ks).astype(BF16), vb[krows])

    parts = []
    for rows in (lo, hi):
        parts.append(_hg_block(fwd, q[rows], kin[rows], g[rows], vb[rows], code))
    parts[1 if fwd else 0] = parts[1 if fwd else 0] + cross
    o = o + jnp.concatenate(parts, axis=0)
    st_new = st * jnp.exp2(g_end) + _dot_tn(vb, (kin * jnp.exp2(g_end - g)).astype(BF16))
    return o, st_new


def _hg_block(fwd, q, kin, g, vb, code):
    n = HG_BLOCK

    scores = jnp.zeros((n, n), F32)
    half, level = n // 2, 0
    while half >= HG_SUB:
        seg = 2 * half
        g3 = g.reshape(n // seg, seg, LANES)
        e = half - 1 if fwd else half
        bsel = jnp.broadcast_to(g3[:, e:e + 1, :], g3.shape).reshape(n, LANES)
        qs = (q * jnp.exp2(g - bsel)).astype(BF16)
        ks = (kin * jnp.exp2(bsel - g)).astype(BF16)
        scores = jnp.where(code == level, _dot_nt(qs, ks), scores)
        half, level = half // 2, level + 1

    def shift(x, dl):
        x3 = x.reshape(n // SUBLANES, SUBLANES, LANES)
        return pltpu.roll(x3, dl if fwd else SUBLANES - dl, 1).reshape(n, LANES)

    for dl in range(HG_SUB):
        w = q * kin if dl == 0 else q * shift(kin, dl) * jnp.exp2(g - shift(g, dl))
        scores = jnp.where(code == 100 + dl, jnp.sum(w, axis=-1, keepdims=True), scores)

    return _dot(scores.astype(BF16), vb)


def _hg_kernel(lbl_ref, q_ref, zf_ref, zb_ref, v_ref, o_ref, *, layer):
    c_len = HG_CHUNK
    n_lat = SEQ // c_len
    n_all = _seg() // c_len
    heads = o_ref.shape[1] // LANES

    lbs = []
    for d in range(2):
        x = lbl_ref[d]
        e = jnp.exp(x - jnp.max(x, axis=0, keepdims=True))
        tot = jnp.sum(e, axis=0, keepdims=True)
        part = jnp.zeros_like(tot)
        for i in range(1, layer + 1):
            part = part + e[i:i + 1, :]
        lbs.append(part / tot)

    ri = lax.broadcasted_iota(jnp.int32, (c_len, c_len), 0)
    ci = lax.broadcasted_iota(jnp.int32, (c_len, c_len), 1)
    tri_f = (ri >= ci).astype(BF16)
    tri_b = (ri <= ci).astype(BF16)
    code_f = _hg_codes(True)
    code_b = _hg_codes(False)

    o_ref[...] = jnp.zeros(o_ref.shape, F32)

    def step(s, carry):
        cf = jnp.where(s < n_all - n_lat, n_lat + s, s - (n_all - n_lat))
        cbk = n_all - 1 - s
        rf = pl.ds(pl.multiple_of(cf * c_len, c_len), c_len)
        rk = pl.ds(pl.multiple_of(cbk * c_len, c_len), c_len)
        out = []
        for hh in range(heads):
            ln = pl.ds(hh * LANES, LANES)
            qf = q_ref[rf, ln].astype(F32)
            of, st_f = _hg_chunk(True, qf * _sigmoid(qf), zf_ref[rf, ln], v_ref[rf, ln].astype(F32),
                                 lbs[0][:, hh * LANES:(hh + 1) * LANES], carry[2 * hh], code_f, tri_f)
            o_ref[rf, ln] += of
            qb = q_ref[rk, ln].astype(F32)
            ob, st_b = _hg_chunk(False, qb * _sigmoid(qb), zb_ref[rk, ln], v_ref[rk, ln].astype(F32),
                                 lbs[1][:, hh * LANES:(hh + 1) * LANES], carry[2 * hh + 1], code_b, tri_b)
            o_ref[rk, ln] += ob
            out += [st_f, st_b]
        return tuple(out)

    zero = jnp.zeros((LANES, LANES), F32)
    lax.fori_loop(0, n_all, step, (zero,) * (2 * heads))


def _hg_call(u, uz, lb_logits, layer):
    t = u.shape[0]
    sb = _seg()
    hp = HG_HEADS_PER_STEP
    w = hp * LANES
    col = lambda off: pl.BlockSpec((sb, w), lambda b, h: (b, off // hp + h))
    return pl.pallas_call(
        functools.partial(_hg_kernel, layer=layer),
        grid=(BATCH, HG_HEADS // hp),
        in_specs=[pl.BlockSpec((2, DEPTH, w), lambda b, h: (0, 0, h)),
                  col(_C_HQ), col(0), col(HG_HEADS), col(_C_HI)],
        out_specs=pl.BlockSpec((sb, w), lambda b, h: (b, h)),
        out_shape=jax.ShapeDtypeStruct((t, HG_W), F32),
        compiler_params=_params(("arbitrary", "arbitrary")),
        name="hgrn2",
    )(lb_logits, u, uz, uz, u)


def _lru_kernel(x_ref, wg_ref, bg_ref, lam_ref, cw_ref, cb_ref, y_ref,
                xp_scr, af_scr, bf_scr, ab_scr, bb_scr, hb_scr):
    sb = _seg()
    c_len = LRU_CHUNK
    halo = SUBLANES
    n_tiles = sb // SUBLANES
    n_lat_t = SEQ // SUBLANES

    xp_scr[pl.ds(0, halo), :] = jnp.zeros((halo, LANES), F32)
    xp_scr[pl.ds(halo + sb, halo), :] = jnp.zeros((halo, LANES), F32)
    xp_scr[pl.ds(halo, sb), :] = x_ref[...].astype(F32)

    lam = lam_ref[0]
    sp = jnp.maximum(-lam, 0.0) + jnp.log1p(jnp.exp(-jnp.abs(lam)))
    cw = cw_ref[...]
    wg = wg_ref[0]
    bg = bg_ref[0]
    r8 = lax.broadcasted_iota(jnp.int32, (c_len, 1), 0)
    rm8 = r8 % SUBLANES

    def tile_shift(x, s):
        x3 = x.reshape(c_len // SUBLANES, SUBLANES, LANES)
        return pltpu.roll(x3, s, 1).reshape(c_len, LANES)

    def coeff_chunk(c, _):
        base = pl.multiple_of(c * c_len, c_len)
        w = xp_scr[pl.ds(base, c_len + 2 * halo), :]
        grow = base + r8
        in_lat = grow < SEQ
        pos = jnp.where(in_lat, grow, grow - SEQ)
        seglen = jnp.where(in_lat, SEQ, CTX_LEN)
        wlen = c_len + 2 * halo
        xm2 = jnp.where(pos >= 2, pltpu.roll(w, 2, 0)[halo:halo + c_len], 0.0)
        xm1 = jnp.where(pos >= 1, pltpu.roll(w, 1, 0)[halo:halo + c_len], 0.0)
        x0 = w[halo:halo + c_len]
        xp1 = jnp.where(pos + 1 < seglen, pltpu.roll(w, wlen - 1, 0)[halo:halo + c_len], 0.0)
        xc = xm2 * cw[0:1] + xm1 * cw[1:2] + x0 * cw[2:3] + xp1 * cw[3:4] + cb_ref[...]
        gates = _sigmoid(_dot(xc.astype(BF16), wg) + bg)
        rows = pl.ds(base, c_len)
        for d, (a_scr, b_scr) in enumerate(((af_scr, bf_scr), (ab_scr, bb_scr))):
            rg = gates[:, (2 * d) * LANES:(2 * d + 1) * LANES]
            ig = gates[:, (2 * d + 1) * LANES:(2 * d + 2) * LANES]
            log_a = -LRU_C * rg * sp[d:d + 1]
            a = jnp.exp(log_a)
            th = jnp.tanh(log_a)
            bv = jnp.sqrt(-2.0 * th / (1.0 - th)) * (ig * xc)
            for sft in (1, 2, 4):
                if d == 0:
                    a_s = tile_shift(a, sft)
                    b_s = tile_shift(bv, sft)
                    ok = rm8 >= sft
                else:
                    a_s = tile_shift(a, SUBLANES - sft)
                    b_s = tile_shift(bv, SUBLANES - sft)
                    ok = rm8 < SUBLANES - sft
                bv = jnp.where(ok, a * b_s + bv, bv)
                a = jnp.where(ok, a * a_s, a)
            a_scr[rows, :] = a
            b_scr[rows, :] = bv
        return 0

    lax.fori_loop(0, sb // c_len, coeff_chunk, 0)

    def carry_step(s, carry):
        hf, hb = carry
        tf = jnp.where(s < n_tiles - n_lat_t, n_lat_t + s, s - (n_tiles - n_lat_t))
        tb = n_tiles - 1 - s
        rf = pl.ds(pl.multiple_of(tf * SUBLANES, SUBLANES), SUBLANES)
        rk = pl.ds(pl.multiple_of(tb * SUBLANES, SUBLANES), SUBLANES)
        h1 = af_scr[rf, :] * hf + bf_scr[rf, :]
        y_ref[rf, :] = h1
        h2 = ab_scr[rk, :] * hb + bb_scr[rk, :]
        hb_scr[rk, :] = h2
        return (jnp.broadcast_to(h1[SUBLANES - 1:SUBLANES, :], (SUBLANES, LANES)),
                jnp.broadcast_to(h2[0:1, :], (SUBLANES, LANES)))

    zero = jnp.zeros((SUBLANES, LANES), F32)
    lax.fori_loop(0, n_tiles, carry_step, (zero, zero), unroll=8)
    y_ref[...] += hb_scr[...]


def _lru_call(u, wg, bg, lam, cw, cb):
    t = u.shape[0]
    sb = _seg()
    nb = LRU_W // LANES
    return pl.pallas_call(
        _lru_kernel,
        grid=(BATCH, nb),
        in_specs=[pl.BlockSpec((sb, LANES), lambda b, c: (b, _C_LX + c)),
                  pl.BlockSpec((1, LANES, 4 * LANES), lambda b, c: (c, 0, 0)),
                  pl.BlockSpec((1, 1, 4 * LANES), lambda b, c: (c, 0, 0)),
                  pl.BlockSpec((1, 2, LANES), lambda b, c: (c, 0, 0)),
                  pl.BlockSpec((CONV_W, LANES), lambda b, c: (0, c)),
                  pl.BlockSpec((1, LANES), lambda b, c: (0, c))],
        out_specs=pl.BlockSpec((sb, LANES), lambda b, c: (b, c)),
        out_shape=jax.ShapeDtypeStruct((t, LRU_W), F32),
        scratch_shapes=[pltpu.VMEM((sb + 2 * SUBLANES, LANES), F32)] + [pltpu.VMEM((sb, LANES), F32)] * 5,
        compiler_params=_params(("arbitrary", "arbitrary")),
        name="rglru",
    )(u, wg, bg, lam, cw, cb)


def _group_rms(o, group):
    ri = lax.broadcasted_iota(jnp.int32, (LANES, LANES), 0) // group
    ci = lax.broadcasted_iota(jnp.int32, (LANES, LANES), 1) // group
    ones_bd = (ri == ci).astype(BF16)
    outs = []
    for tix in range(o.shape[1] // LANES):
        x = o[:, tix * LANES:(tix + 1) * LANES]
        sq = x * x
        hi = sq.astype(BF16)
        lo = (sq - hi.astype(F32)).astype(BF16)
        ss = _dot(hi, ones_bd) + _dot(lo, ones_bd)
        outs.append(x * lax.rsqrt(ss * (1.0 / group) + NORM_EPS))
    return jnp.concatenate(outs, axis=1)


def _merge_kernel(*refs, n_gate):
    (x_ref, ml_ref, mc_ref, oret_ref, ohg_ref, ylru_ref, rg_ref, hgt_ref, lgt_ref) = refs[:9]
    gate_refs = refs[9:9 + 3 * n_gate]
    (gng_ref, ghg_ref, wr_ref, wh_ref, wl_ref, wo_ref, n2g_ref, rwh_ref, rwm_ref, rb_ref,
     xo_ref, h2_ref, lg_ref) = refs[9 + 3 * n_gate:]
    tiles_per_b = _seg() // ROW_TILE
    is_ctx = (pl.program_id(0) % tiles_per_b) >= SEQ // ROW_TILE
    ml = ml_ref[0]
    mc = mc_ref[0]

    def gate(i):
        parts = [gate_refs[i * n_gate + j][...].astype(F32) for j in range(n_gate)]
        return _sigmoid(parts[0] if n_gate == 1 else jnp.concatenate(parts, axis=1))

    rg = rg_ref[...].astype(F32)
    ret_in = _group_rms(oret_ref[...], RET_DK) * gng_ref[...] * (rg * _sigmoid(rg))
    ret = _dot(ret_in.astype(BF16), wr_ref[0])
    hg_in = _group_rms(ohg_ref[...], HG_DK) * ghg_ref[...] * _sigmoid(hgt_ref[...].astype(F32))
    hgr = _dot(hg_in.astype(BF16), wh_ref[0])
    lgt = lgt_ref[...].astype(F32)
    gelu = 0.5 * lgt * (1.0 + jnp.tanh(0.7978845608028654 * (lgt + 0.044715 * lgt * lgt * lgt)))
    lru = _dot((ylru_ref[...] * gelu).astype(BF16), wl_ref[0])
    merged = gate(0) * ret + gate(1) * hgr + gate(2) * lru
    mix = _dot(merged.astype(BF16), wo_ref[0])
    g1 = jnp.where(is_ctx, mc[2:3, :], ml[2:3, :])
    xn = x_ref[...] + g1 * mix
    xo_ref[...] = xn
    h2 = _norm_modulate(xn, n2g_ref[...], ml, mc, is_ctx, 3)
    h2_ref[...] = h2.astype(BF16)
    hh, hm, _ = _split3(h2)
    lg_ref[...] = _dot(hh, rwh_ref[...]) + (_dot(hh, rwm_ref[...]) + _dot(hm, rwh_ref[...])) + rb_ref[...]


def _merge_call(xs, mod, o_ret, o_hg, y_lru, u, gn_g, hgn_g, wr, wh, wl, wo, n2g, rw_hi, rw_mid, rb, layer):
    t = xs.shape[0]
    r = ROW_TILE
    tiles_per_b = _seg() // r
    n_gate = D_MODEL // 512
    row = lambda w: pl.BlockSpec((r, w), lambda i: (i, 0))
    ucol = lambda blk: pl.BlockSpec((r, 512), lambda i: (i, blk))

    def full(a):
        if a.ndim == 3:
            return pl.BlockSpec((1,) + a.shape[1:], lambda i: (layer, 0, 0))
        return pl.BlockSpec(a.shape, lambda i: (0,) * a.ndim)
    gate_base = _C_GATES * LANES // 512
    in_specs = [row(D_MODEL),
                pl.BlockSpec((1, N_MOD, D_MODEL), lambda i: (i // tiles_per_b, 0, 0)),
                pl.BlockSpec((1, N_MOD, D_MODEL), lambda i: (BATCH, 0, 0)),
                row(RET_W), row(HG_W), row(LRU_W),
                ucol(_C_RG * LANES // 512), ucol(_C_HGT * LANES // 512), ucol(_C_LGT * LANES // 512)]
    in_specs += [ucol(gate_base + j) for j in range(3 * n_gate)]
    consts = [gn_g, hgn_g, wr, wh, wl, wo, n2g, rw_hi, rw_mid, rb]
    in_specs += [full(a) for a in consts]
    return pl.pallas_call(
        functools.partial(_merge_kernel, n_gate=n_gate),
        grid=(t // r,),
        in_specs=in_specs,
        out_specs=[row(D_MODEL), row(D_MODEL), row(LANES)],
        out_shape=[jax.ShapeDtypeStruct((t, D_MODEL), F32), jax.ShapeDtypeStruct((t, D_MODEL), BF16),
                   jax.ShapeDtypeStruct((t, LANES), F32)],
        compiler_params=_params(("arbitrary",)),
        name="merge",
    )(xs, mod, mod, o_ret, o_hg, y_lru, *([u] * (3 + 3 * n_gate)), *consts)


def _route_kernel(lg_ref, oi_ref, ow_ref, lst_ref, meta_ref, cnt_ref, off_scr):
    @pl.when(pl.program_id(0) == 0)
    def _():
        off_scr[...] = jnp.zeros(off_scr.shape, F32)

    r = lg_ref.shape[0]
    lane = lax.broadcasted_iota(jnp.int32, (r, LANES), 1)
    neg = -jnp.inf
    l = jnp.where(lane < N_EXPERTS, lg_ref[...], neg)
    vals, hots = [], []
    lanef = lane.astype(F32)
    for _ in range(TOP_K):
        m = jnp.max(l, axis=-1, keepdims=True)
        idxf = jnp.min(jnp.where(l == m, lanef, float(LANES)), axis=-1, keepdims=True)
        hot = lanef == idxf
        l = jnp.where(hot, neg, l)
        vals.append(m)
        hots.append(hot)
    es = [jnp.exp(v - vals[0]) for v in vals]
    den = es[0]
    for e in es[1:]:
        den = den + e
    chosen = hots[0].astype(F32)
    for hot in hots[1:]:
        chosen = chosen + hot.astype(F32)
    ri = lax.broadcasted_iota(jnp.int32, (r, r), 0)
    ci = lax.broadcasted_iota(jnp.int32, (r, r), 1)
    before = _dot((ri > ci).astype(BF16), chosen.astype(BF16))
    count = jnp.sum(chosen, axis=0, keepdims=True)
    run = jnp.floor((count + (SUBLANES - 1.0)) * (1.0 / SUBLANES)) * SUBLANES
    li = lax.broadcasted_iota(jnp.int32, (LANES, LANES), 0)
    lj = lax.broadcasted_iota(jnp.int32, (LANES, LANES), 1)
    run8 = jnp.broadcast_to(run, (SUBLANES, LANES))
    start = _dot(run8.astype(BF16), (li < lj).astype(BF16))[0:1, :]
    pos = start + before
    oi = jnp.zeros((r, LANES), jnp.int32)
    ow = jnp.zeros((r, LANES), F32)
    lsf = jnp.zeros((r, LANES), F32)
    for k in range(TOP_K):
        row_k = jnp.sum(jnp.where(hots[k], pos, 0.0), axis=-1, keepdims=True)
        oi = jnp.where(lane == k, row_k.astype(jnp.int32), oi)
        lsf = jnp.where(lane == k, row_k, lsf)
        ow = jnp.where(lane == k, es[k] / den, ow)
    oi_ref[...] = oi
    ow_ref[...] = ow
    lst_ref[0] = jnp.transpose(lsf)[0:SUBLANES, :]
    mrow = lax.broadcasted_iota(jnp.int32, (SUBLANES, LANES), 0)
    meta_ref[0] = jnp.where(mrow == 0, run8, jnp.where(mrow == 1, jnp.broadcast_to(start, (SUBLANES, LANES)),
                                                       off_scr[...]))
    off_scr[...] = off_scr[...] + run
    cnt_ref[...] = off_scr[...]


def _route_call(logits):
    t = logits.shape[0]
    r = ROW_TILE
    row = pl.BlockSpec((r, LANES), lambda i: (i, 0))
    return pl.pallas_call(
        _route_kernel,
        grid=(t // r,),
        in_specs=[row],
        out_specs=[row, row, pl.BlockSpec((1, SUBLANES, r), lambda i: (i, 0, 0)),
                   pl.BlockSpec((1, SUBLANES, LANES), lambda i: (i, 0, 0)),
                   pl.BlockSpec((SUBLANES, LANES), lambda i: (0, 0))],
        out_shape=[jax.ShapeDtypeStruct((t, LANES), jnp.int32), jax.ShapeDtypeStruct((t, LANES), F32),
                   jax.ShapeDtypeStruct((t // r, SUBLANES, r), F32),
                   jax.ShapeDtypeStruct((t // r, SUBLANES, LANES), F32),
                   jax.ShapeDtypeStruct((SUBLANES, LANES), F32)],
        scratch_shapes=[pltpu.VMEM((SUBLANES, LANES), F32)],
        compiler_params=_params(("arbitrary",)),
        name="route",
    )(logits)


def _pack_pairs(x):
    m = x.shape[1] // 2
    bits = pltpu.bitcast(x, jnp.uint32)
    return bits[:, :m] | (bits[:, m:] >> 16)


def _unpack_pairs(w):
    hi = pltpu.bitcast(w & jnp.uint32(0xFFFF0000), F32).astype(BF16)
    lo = pltpu.bitcast(w << 16, F32).astype(BF16)
    return hi, lo


def _local_rows():
    rows = ROW_TILE * TOP_K + N_EXPERTS * (SUBLANES - 1)
    return -(-rows // ROW_TILE) * ROW_TILE


RUN_PIECE = 4 * SUBLANES


def _max_pieces():
    return _local_rows() // RUN_PIECE, N_EXPERTS * (RUN_PIECE // SUBLANES - 1)


def _meta_width():
    n_big, n_small = _max_pieces()
    return -(-(2 + 2 * n_big + 2 * n_small) // LANES) * LANES


def _piece_tables(first_slot, first_row, pieces):
    per_big = RUN_PIECE // SUBLANES
    max_big, max_small = _max_pieces()

    def flat(count, limit, offset_rows, step):
        ends = jnp.cumsum(count, axis=1)
        p = jnp.arange(limit, dtype=jnp.int32)[None, :, None]
        mine = ((ends - count)[:, None, :] <= p) & (p < ends[:, None, :])
        take = lambda a: jnp.sum(jnp.where(mine, a[:, None, :], 0), axis=2)
        j = p[:, :, 0] - take(ends - count)
        off = take(offset_rows) + j * step
        return take(first_row) + off, take(first_slot) + off, ends[:, -1:]

    n_big = pieces // per_big
    big_row, big_slot, tot_big = flat(n_big, max_big, jnp.zeros_like(pieces), RUN_PIECE)
    small_row, small_slot, tot_small = flat(pieces - n_big * per_big, max_small, n_big * RUN_PIECE, SUBLANES)
    table = jnp.concatenate([tot_big, tot_small, big_row, big_slot, small_row, small_slot], axis=1)
    return jnp.pad(table, ((0, 0), (0, _meta_width() - table.shape[1])))[:, None, :]


def _run_copies(meta_ref, make_copy, do):
    max_big, max_small = _max_pieces()
    big0 = 2
    small0 = 2 + 2 * max_big

    def big(p, _):
        do(make_copy(pl.multiple_of(meta_ref[0, 0, big0 + p], SUBLANES),
                     pl.multiple_of(meta_ref[0, 0, big0 + max_big + p], SUBLANES), RUN_PIECE))
        return 0

    def small(p, _):
        do(make_copy(pl.multiple_of(meta_ref[0, 0, small0 + p], SUBLANES),
                     pl.multiple_of(meta_ref[0, 0, small0 + max_small + p], SUBLANES), SUBLANES))
        return 0

    lax.fori_loop(0, meta_ref[0, 0, 0], big, 0)
    lax.fori_loop(0, meta_ref[0, 0, 1], small, 0)


def _dispatch_kernel(meta_ref, prev_meta_ref, lst_ref, h_ref, xs_in_ref, xs_ref, perm_scr, sem):
    del xs_in_ref
    i = pl.program_id(0)
    cur = i % 2
    r = h_ref.shape[0]
    ls = lst_ref[0].astype(jnp.int32)
    hb = h_ref[...]
    for s in range(perm_scr.shape[1] // r):
        rows = s * r + lax.broadcasted_iota(jnp.int32, (r, r), 0)
        sel = rows == ls[0:1, :]
        for k in range(1, TOP_K):
            sel = jnp.logical_or(sel, rows == ls[k:k + 1, :])
        perm_scr[cur, pl.ds(s * r, r), :] = _pack_pairs(_dot(jnp.where(sel, 1.0, 0.0).astype(BF16), hb))

    def copies_from(buf):
        def make_copy(row, slot, n):
            return pltpu.make_async_copy(perm_scr.at[buf, pl.ds(row, n)], xs_ref.at[pl.ds(slot, n)], sem.at[buf])
        return make_copy

    _run_copies(meta_ref, copies_from(cur), lambda c: c.start())

    @pl.when(i > 0)
    def _():
        _run_copies(prev_meta_ref, copies_from(1 - cur), lambda c: c.wait())

    @pl.when(i == pl.num_programs(0) - 1)
    def _():
        _run_copies(meta_ref, copies_from(cur), lambda c: c.wait())


def _dispatch_call(h2, meta, lst, slots):
    t = h2.shape[0]
    r = ROW_TILE
    return pl.pallas_call(
        _dispatch_kernel,
        grid=(t // r,),
        in_specs=[pl.BlockSpec((1, 1, _meta_width()), lambda i: (i, 0, 0), memory_space=pltpu.SMEM),
                  pl.BlockSpec((1, 1, _meta_width()), lambda i: (jnp.maximum(i - 1, 0), 0, 0), memory_space=pltpu.SMEM),
                  pl.BlockSpec((1, SUBLANES, r), lambda i: (i, 0, 0)),
                  pl.BlockSpec((r, D_MODEL), lambda i: (i, 0)),
                  pl.BlockSpec(memory_space=pl.ANY)],
        out_specs=pl.BlockSpec(memory_space=pl.ANY),
        out_shape=jax.ShapeDtypeStruct(slots.shape, jnp.uint32),
        scratch_shapes=[pltpu.VMEM((2, _local_rows(), D_MODEL // 2), jnp.uint32), pltpu.SemaphoreType.DMA((2,))],
        input_output_aliases={4: 0},
        compiler_params=_params(("arbitrary",)),
        name="dispatch",
    )(meta, meta, lst, h2, slots)


def _expert_kernel(bexp_ref, nv_ref, next_ref, par_ref, x_ref, wgu_hbm, bgu_ref, wd_hbm, bd_ref, y_ref,
                   wgu_scr, wd_scr, gu_stage, d_stage, sem, *, layer):
    i = pl.program_id(0)
    f = D_MODEL
    prev = bexp_ref[jnp.maximum(i - 1, 0)]
    new_expert = jnp.logical_or(i == 0, bexp_ref[i] != prev)

    def weight_copies(e, b):
        return (pltpu.make_async_copy(wgu_hbm.at[layer, e], gu_stage.at[b], sem.at[b, 0]),
                pltpu.make_async_copy(wd_hbm.at[layer, e], d_stage.at[b], sem.at[b, 1]))

    @pl.when(jnp.logical_and(i < nv_ref[0], new_expert))
    def _():
        b = par_ref[i]

        @pl.when(i == 0)
        def _():
            for c in weight_copies(bexp_ref[i], b):
                c.start()

        for c in weight_copies(bexp_ref[i], b):
            c.wait()

        @pl.when(next_ref[i] >= 0)
        def _():
            for c in weight_copies(next_ref[i], 1 - b):
                c.start()

        slab = 256
        for s in range(D_MODEL // slab):
            wgu_scr[pl.ds(s * slab, slab), :] = gu_stage[b, pl.ds(s * slab, slab), :].astype(BF16)
            wd_scr[pl.ds(s * slab, slab), :] = d_stage[b, pl.ds(s * slab, slab), :].astype(BF16)

    @pl.when(i < nv_ref[0])
    def _():
        x_hi, x_lo = _unpack_pairs(x_ref[...])
        half = D_MODEL // 2
        gu = _dot(x_hi, wgu_scr[:half, :]) + _dot(x_lo, wgu_scr[half:, :]) + bgu_ref[0, 0]
        gate = jnp.minimum(gu[:, :f], SWIGLU_LIMIT)
        up = jnp.clip(gu[:, f:], -SWIGLU_LIMIT, SWIGLU_LIMIT)
        glu = gate * _sigmoid(SWIGLU_ALPHA * gate)
        y = _dot(((up + 1.0) * glu).astype(BF16), wd_scr[...]) + bd_ref[0, 0]
        y_ref[...] = _pack_pairs(y.astype(BF16).astype(F32))

    @pl.when(i >= nv_ref[0])
    def _():
        y_ref[...] = jnp.zeros(y_ref.shape, jnp.uint32)


def _expert_call(slots, bexp, nvalid, next_expert, stage, w_gu, b_gu, w_down, b_down, layer):
    s = slots.shape[0]
    nb = s // MOE_BLOCK
    f = D_MODEL
    blk = lambda i, be, nv, nx, pr: (jnp.minimum(i, nv[0] - 1), 0)
    exp3 = lambda i, be, nv, nx, pr: (layer, be[jnp.minimum(i, nv[0] - 1)], 0, 0)
    grid_spec = pltpu.PrefetchScalarGridSpec(
        num_scalar_prefetch=4,
        grid=(nb,),
        in_specs=[pl.BlockSpec((MOE_BLOCK, D_MODEL // 2), blk),
                  pl.BlockSpec(memory_space=pl.ANY),
                  pl.BlockSpec((1, 1, 1, 2 * f), exp3),
                  pl.BlockSpec(memory_space=pl.ANY),
                  pl.BlockSpec((1, 1, 1, D_MODEL), exp3)],
        out_specs=pl.BlockSpec((MOE_BLOCK, D_MODEL // 2), lambda i, be, nv, nx, pr: (i, 0)),
        scratch_shapes=[pltpu.VMEM((D_MODEL, 2 * f), BF16), pltpu.VMEM((f, D_MODEL), BF16),
                        pltpu.VMEM((2, D_MODEL, 2 * f), F32), pltpu.VMEM((2, f, D_MODEL), F32),
                        pltpu.SemaphoreType.DMA((2, 2))],
    )
    return pl.pallas_call(
        functools.partial(_expert_kernel, layer=layer),
        grid_spec=grid_spec,
        out_shape=jax.ShapeDtypeStruct((s, D_MODEL // 2), jnp.uint32),
        compiler_params=_params(("arbitrary",)),
        name="experts",
    )(bexp, nvalid, next_expert, stage, slots, w_gu, b_gu.reshape(DEPTH, N_EXPERTS, 1, 2 * f), w_down,
      b_down.reshape(DEPTH, N_EXPERTS, 1, D_MODEL))


def _combine_kernel(meta_ref, next_meta_ref, x_ref, ml_ref, mc_ref, oi_ref, w_ref, fg_ref, y_ref, o_ref, buf, sem,
                    *, last, n_t):
    r = x_ref.shape[0]
    i = pl.program_id(0)
    cur = i % 2

    def copies_into(b):
        def make_copy(row, slot, n):
            return pltpu.make_async_copy(y_ref.at[pl.ds(slot, n)], buf.at[b, pl.ds(row, n)], sem.at[b])
        return make_copy

    @pl.when(i == 0)
    def _():
        buf[...] = jnp.zeros(buf.shape, jnp.uint32)
        _run_copies(meta_ref, copies_into(0), lambda c: c.start())

    @pl.when(i + 1 < pl.num_programs(0))
    def _():
        _run_copies(next_meta_ref, copies_into(1 - cur), lambda c: c.start())

    _run_copies(meta_ref, copies_into(cur), lambda c: c.wait())

    if last:
        is_ctx = False
    else:
        is_ctx = (i % n_t) >= SEQ // ROW_TILE
    w = w_ref[...]
    oi = oi_ref[...]
    ff_hi = jnp.zeros((r, D_MODEL // 2), F32)
    ff_lo = jnp.zeros((r, D_MODEL // 2), F32)
    for s in range(buf.shape[1] // r):
        cols = s * r + lax.broadcasted_iota(jnp.int32, (r, r), 1)
        wm = jnp.zeros((r, r), F32)
        for k in range(TOP_K):
            wm = jnp.where(cols == oi[:, k:k + 1], w[:, k:k + 1], wm)
        y_hi, y_lo = _unpack_pairs(buf[cur, pl.ds(s * r, r), :])
        ff_hi = ff_hi + _dot(wm.astype(BF16), y_hi)
        ff_lo = ff_lo + _dot(wm.astype(BF16), y_lo)
    ff = jnp.concatenate([ff_hi, ff_lo], axis=1)
    g2 = jnp.where(is_ctx, mc_ref[0][5:6, :], ml_ref[0][5:6, :])
    xn = x_ref[...] + g2 * ff
    if last:
        ms = jnp.mean(xn * xn, axis=-1, keepdims=True)
        xn = xn * lax.rsqrt(ms + NORM_EPS) * fg_ref[...]
    o_ref[...] = xn


def _combine_call(xs, mod, meta, rows, wts, y_slots, final_g, last):
    t = xs.shape[0]
    r = ROW_TILE
    tiles_per_b = _seg() // r
    n_t = SEQ // r if last else tiles_per_b
    n_steps = BATCH * n_t
    tile = lambda i: (i // n_t) * tiles_per_b + i % n_t
    out_rows = BATCH * SEQ if last else t
    return pl.pallas_call(
        functools.partial(_combine_kernel, last=last, n_t=n_t),
        grid=(n_steps,),
        in_specs=[pl.BlockSpec((1, 1, _meta_width()), lambda i: (tile(i), 0, 0), memory_space=pltpu.SMEM),
                  pl.BlockSpec((1, 1, _meta_width()), lambda i: (tile(jnp.minimum(i + 1, n_steps - 1)), 0, 0),
                               memory_space=pltpu.SMEM),
                  pl.BlockSpec((r, D_MODEL), lambda i: (tile(i), 0)),
                  pl.BlockSpec((1, N_MOD, D_MODEL), lambda i: (i // n_t, 0, 0)),
                  pl.BlockSpec((1, N_MOD, D_MODEL), lambda i: (BATCH, 0, 0)),
                  pl.BlockSpec((r, LANES), lambda i: (tile(i), 0)),
                  pl.BlockSpec((r, LANES), lambda i: (tile(i), 0)),
                  pl.BlockSpec((1, D_MODEL), lambda i: (0, 0)),
                  pl.BlockSpec(memory_space=pl.ANY)],
        out_specs=pl.BlockSpec((r, D_MODEL), lambda i: (i, 0)),
        out_shape=jax.ShapeDtypeStruct((out_rows, D_MODEL), F32),
        scratch_shapes=[pltpu.VMEM((2, _local_rows(), D_MODEL // 2), jnp.uint32), pltpu.SemaphoreType.DMA((2,))],
        compiler_params=_params(("arbitrary",)),
        name="combine",
    )(meta, meta, xs, mod, mod, rows, wts, final_g, y_slots)


def _rotary_tables():
    rows = SEQ // GRID_W
    row = jnp.repeat(jnp.arange(rows, dtype=F32), GRID_W)
    colp = jnp.tile(jnp.arange(GRID_W, dtype=F32), rows)
    n_freq = RET_DK // 4
    inv = ROPE_BASE ** (-jnp.arange(n_freq, dtype=F32) / n_freq)
    ang = jnp.concatenate([row[:, None] * inv, colp[:, None] * inv], axis=-1)
    cos, sin = jnp.cos(ang), jnp.sin(ang)
    reps = LANES // (RET_DK // 2)
    cos_t = jnp.tile(cos, (1, reps))
    sign = jnp.where((jnp.arange(LANES) % RET_DK) < RET_DK // 2, -1.0, 1.0).astype(F32)
    sin_t = jnp.tile(sin, (1, reps)) * sign
    return cos_t, sin_t


def _lru_gate_weights(wa, ba, wx, bx):
    nb = LRU_W // LANES
    per = LANES // LRU_BD
    eye = jnp.eye(per, dtype=F32)

    def embed(w):
        w = w.reshape(nb, per, LRU_BD, LRU_BD)
        return jnp.einsum('cpij,pq->cpiqj', w, eye).reshape(nb, LANES, LANES)

    wg = jnp.concatenate([embed(wa[0]), embed(wx[0]), embed(wa[1]), embed(wx[1])], axis=-1)
    bias = jnp.concatenate([ba[0].reshape(nb, 1, LANES), bx[0].reshape(nb, 1, LANES),
                            ba[1].reshape(nb, 1, LANES), bx[1].reshape(nb, 1, LANES)], axis=-1)
    return wg.astype(BF16), bias


def kernel(x, c, ctx, c_ctx, mod_w, mod_b, norm1_g, norm2_g, final_g, w_in, ret_decay, ret_gn_g, w_ret_o,
           hgrn_lb_logits, hgrn_gn_g, w_hgrn_o, lru_conv_w, lru_conv_b, lru_wa, lru_ba, lru_wx, lru_bx,
           lru_lambda, w_lru_o, w_out, router_w, router_b, exp_w_gu, exp_b_gu, exp_w_down, exp_b_down):
    assert CTX_LEN == RET_CHUNK and SEQ % RET_CHUNK == 0
    sb = _seg()
    t = BATCH * sb
    xs = jnp.concatenate([x, ctx], axis=1).reshape(t, D_MODEL)

    cvec = jnp.concatenate([c, c_ctx[None, :], jnp.zeros((SUBLANES - BATCH - 1, D_MODEL), F32)], axis=0)
    mod_all = _mod_call(cvec, mod_w, mod_b).reshape(DEPTH, SUBLANES, N_MOD, D_MODEL)

    cos_t, sin_t = _rotary_tables()
    lb_logits = jnp.transpose(hgrn_lb_logits, (1, 0, 2))
    n_slots = t * TOP_K + (t // ROW_TILE) * N_EXPERTS * (SUBLANES - 1) + N_EXPERTS * (MOE_BLOCK - 1)
    n_blocks = -(-n_slots // MOE_BLOCK)
    n_slots = n_blocks * MOE_BLOCK
    slots = jnp.zeros((n_slots, D_MODEL // 2), jnp.uint32)
    rw_pad = jnp.pad(router_w, ((0, 0), (0, 0), (0, LANES - N_EXPERTS)))
    rw_hi = rw_pad.astype(BF16)
    rw_mid = (rw_pad - rw_hi.astype(F32)).astype(BF16)
    rb_pad = jnp.pad(router_b, ((0, 0), (0, LANES - N_EXPERTS)))
    w_ret_b, w_hg_b, w_lru_b, w_out_b = (w.astype(BF16) for w in (w_ret_o, w_hgrn_o, w_lru_o, w_out))
    block_first_slot = jnp.arange(n_blocks, dtype=jnp.int32) * MOE_BLOCK
    expert_ids = jnp.arange(N_EXPERTS, dtype=jnp.int32)

    out = None
    for l in range(DEPTH):
        last = l == DEPTH - 1
        mod = mod_all[l]
        u, uz = _inproj_call(xs, mod, norm1_g[l][None, :], w_in, l)

        dec = jnp.broadcast_to(ret_decay[l].reshape(2, RET_HEADS // 2, 2, 1).transpose(1, 0, 2, 3),
                               (RET_HEADS // 2, 2, 2, LANES)).reshape(RET_HEADS // 2, 4, LANES)
        o_ret = _ret_call(u, dec, cos_t, sin_t)
        o_hg = _hg_call(u, uz, lb_logits, l)
        wg, bg = _lru_gate_weights(lru_wa[l], lru_ba[l], lru_wx[l], lru_bx[l])
        lam = lru_lambda[l].reshape(2, LRU_W // LANES, LANES).transpose(1, 0, 2)
        y_lru = _lru_call(u, wg, bg, lam, lru_conv_w[l], lru_conv_b[l][None, :])

        xs, h2, logits = _merge_call(
            xs, mod, o_ret, o_hg, y_lru, u, ret_gn_g[l][None, :], hgrn_gn_g[l][None, :],
            w_ret_b, w_hg_b, w_lru_b, w_out_b, norm2_g[l][None, :], rw_hi[l], rw_mid[l], rb_pad[l][None, :], l)

        rows, wts, lst, tile_meta, cnt = _route_call(logits)
        run = tile_meta[:, 0, :N_EXPERTS].astype(jnp.int32)
        first_row = tile_meta[:, 1, :N_EXPERTS].astype(jnp.int32)
        earlier = tile_meta[:, 2, :N_EXPERTS].astype(jnp.int32)
        counts = cnt[0, :N_EXPERTS].astype(jnp.int32)
        padded = (counts + MOE_BLOCK - 1) // MOE_BLOCK * MOE_BLOCK
        pad_end = jnp.cumsum(padded)
        first_slot = (pad_end - padded)[None, :] + earlier
        meta = _piece_tables(first_slot, first_row, run // SUBLANES)
        bexp = jnp.minimum(jnp.sum((pad_end[None, :] <= block_first_slot[:, None]).astype(jnp.int32), axis=1),
                           N_EXPERTS - 1)
        nvalid = (pad_end[-1:] // MOE_BLOCK).astype(jnp.int32)
        present = padded > 0
        later = jnp.where(present[None, :] & (expert_ids[None, :] > expert_ids[:, None]), expert_ids[None, :],
                          N_EXPERTS)
        next_of = jnp.min(later, axis=1)
        next_of = jnp.where(next_of >= N_EXPERTS, -1, next_of)
        stage_of = (jnp.cumsum(present.astype(jnp.int32)) - present.astype(jnp.int32)) % 2
        is_e = bexp[:, None] == expert_ids[None, :]
        next_expert = jnp.sum(jnp.where(is_e, next_of[None, :], 0), axis=1)
        stage = jnp.sum(jnp.where(is_e, stage_of[None, :], 0), axis=1)

        slots = _dispatch_call(h2, meta, lst, slots)
        y_slots = _expert_call(slots, bexp, nvalid, next_expert, stage, exp_w_gu, exp_b_gu, exp_w_down,
                               exp_b_down, l)
        res = _combine_call(xs, mod, meta, rows, wts, y_slots, final_g[None, :], last)
        if last:
            out = res.reshape(BATCH, SEQ, D_MODEL)
        else:
            xs = res
    return out
```

```python
import functools

import jax
import jax.numpy as jnp
from jax import lax
from jax.experimental import pallas as pl
from jax.experimental.pallas import tpu as pltpu

F32 = jnp.float32
BF16 = jnp.bfloat16

D_MODEL = 1024
BATCH = 4
SEQ = 4096
DEPTH = 4
GRID_W = 64
CTX_LEN = 256
N_MOD = 6
NORM_EPS = 1e-6
RET_HEADS = 8
RET_DK = 64
RET_W = 512
ROPE_BASE = 10000.0
HG_HEADS = 4
HG_DK = 128
HG_W = 512
LRU_W = 512
LRU_BLOCKS = 8
LRU_BD = 64
LRU_C = 8.0
CONV_W = 4
N_EXPERTS = 32
TOP_K = 4
SWIGLU_ALPHA = 1.702
SWIGLU_LIMIT = 7.0

LANES = 128
SUBLANES = 8
RET_CHUNK = 256
HG_CHUNK = 256
HG_BLOCK = 128
HG_SUB = 4
HG_HEADS_PER_STEP = 2
LOG2_E = 1.4426950408889634
LRU_CHUNK = 256
MOE_BLOCK = 512
ROW_TILE = 256
MOD_TN = 1536
VMEM_LIMIT = 56 * 1024 * 1024

_C_RQ, _C_RK, _C_RV, _C_RG = 0, 4, 8, 12
_C_HQ, _C_HFF, _C_HFB, _C_HI, _C_HGT = 16, 20, 24, 28, 32
_C_LX, _C_LGT = 36, 40
_C_GATES = 44


def _seg():
    return SEQ + CTX_LEN


def _in_cols():
    return _C_GATES * LANES + 3 * D_MODEL


def _dot(a, b):
    return jnp.dot(a, b, preferred_element_type=F32)


def _dot_nt(a, b):
    return lax.dot_general(a, b, (((1,), (1,)), ((), ())), preferred_element_type=F32)


def _dot_tn(a, b):
    return lax.dot_general(a, b, (((0,), (0,)), ((), ())), preferred_element_type=F32)


def _split3(x):
    hi = x.astype(BF16)
    r = x - hi.astype(F32)
    mid = r.astype(BF16)
    lo = (r - mid.astype(F32)).astype(BF16)
    return hi, mid, lo


def _dot_exact_lhs(a, x):
    hi, mid, lo = _split3(x)
    return _dot(a, hi) + _dot(a, mid) + _dot(a, lo)


def _dot3(a, b):
    ah, am, _ = _split3(a)
    bh, bm, _ = _split3(b)
    return _dot(ah, bh) + (_dot(ah, bm) + _dot(am, bh))


def _sigmoid(x):
    return 0.5 + 0.5 * jnp.tanh(0.5 * x)


def _log_sigmoid(x):
    return jnp.minimum(x, 0.0) - jnp.log1p(jnp.exp(-jnp.abs(x)))


def _params(sem):
    return pltpu.CompilerParams(dimension_semantics=sem, vmem_limit_bytes=VMEM_LIMIT)


def _mod_kernel(c_ref, w_ref, b_ref, o_ref):
    c = c_ref[...]
    s = c * _sigmoid(c)
    o_ref[0] = _dot3(s, w_ref[0]) + b_ref[0]


def _mod_call(cvec, mod_w, mod_b):
    n6 = N_MOD * D_MODEL
    tn = MOD_TN
    return pl.pallas_call(
        _mod_kernel,
        grid=(DEPTH, n6 // tn),
        in_specs=[pl.BlockSpec((SUBLANES, D_MODEL), lambda l, j: (0, 0)),
                  pl.BlockSpec((1, D_MODEL, tn), lambda l, j: (l, 0, j)),
                  pl.BlockSpec((1, 1, tn), lambda l, j: (l, 0, j))],
        out_specs=pl.BlockSpec((1, SUBLANES, tn), lambda l, j: (l, 0, j)),
        out_shape=jax.ShapeDtypeStruct((DEPTH, SUBLANES, n6), F32),
        compiler_params=_params(("arbitrary", "arbitrary")),
        name="modulation",
    )(cvec, mod_w, mod_b.reshape(DEPTH, 1, n6))


def _norm_modulate(x, g, mod_l, mod_c, is_ctx, first):
    ms = jnp.mean(x * x, axis=-1, keepdims=True)
    y = x * lax.rsqrt(ms + NORM_EPS) * g
    sh = jnp.where(is_ctx, mod_c[first:first + 1, :], mod_l[first:first + 1, :])
    sc = jnp.where(is_ctx, mod_c[first + 1:first + 2, :], mod_l[first + 1:first + 2, :])
    return y * (1.0 + sc) + sh


def _inproj_kernel(x_ref, ml_ref, mc_ref, g_ref, w_ref, o_ref, oz_ref, h_scr, *, tm, z_first, z_count):
    j = pl.program_id(1)

    @pl.when(j == 0)
    def _():
        tiles_per_b = _seg() // tm
        row0 = (pl.program_id(0) % tiles_per_b) * tm
        rows = row0 + lax.broadcasted_iota(jnp.int32, (tm, 1), 0)
        h = _norm_modulate(x_ref[...], g_ref[...], ml_ref[0], mc_ref[0], rows >= SEQ, 0)
        h_scr[...] = h.astype(BF16)

    acc = _dot(h_scr[...], w_ref[0].astype(BF16))
    o_ref[...] = acc.astype(BF16)

    @pl.when(jnp.logical_and(j >= z_first, j < z_first + z_count))
    def _():
        oz_ref[...] = acc


def _inproj_call(xs, mod, g, w, layer):
    t = xs.shape[0]
    tm = _seg() // 2
    ncol = _in_cols()
    tn = HG_W
    tiles_per_b = _seg() // tm
    z_first = _C_HFF * LANES // tn
    z_count = 2
    return pl.pallas_call(
        functools.partial(_inproj_kernel, tm=tm, z_first=z_first, z_count=z_count),
        grid=(t // tm, ncol // tn),
        in_specs=[pl.BlockSpec((tm, D_MODEL), lambda i, j: (i, 0)),
                  pl.BlockSpec((1, N_MOD, D_MODEL), lambda i, j: (i // tiles_per_b, 0, 0)),
                  pl.BlockSpec((1, N_MOD, D_MODEL), lambda i, j: (BATCH, 0, 0)),
                  pl.BlockSpec((1, D_MODEL), lambda i, j: (0, 0)),
                  pl.BlockSpec((1, D_MODEL, tn), lambda i, j: (layer, 0, j))],
        out_specs=[pl.BlockSpec((tm, tn), lambda i, j: (i, j)),
                   pl.BlockSpec((tm, tn), lambda i, j: (i, jnp.clip(j - z_first, 0, z_count - 1)))],
        out_shape=[jax.ShapeDtypeStruct((t, ncol), BF16), jax.ShapeDtypeStruct((t, z_count * tn), F32)],
        scratch_shapes=[pltpu.VMEM((tm, D_MODEL), BF16)],
        compiler_params=_params(("arbitrary", "arbitrary")),
        name="in_projection",
    )(xs, mod, mod, g, w)


def _ret_kernel(dec_ref, cos_ref, sin_ref, q_ref, k_ref, v_ref, o_ref, mask_scr, pw_scr):
    c_len = RET_CHUNK
    n_lat = SEQ // c_len
    n_all = _seg() // c_len
    lane = lax.broadcasted_iota(jnp.int32, (1, LANES), 1)
    head0 = lane < RET_DK
    first_half = (lane % RET_DK) < (RET_DK // 2)

    lg = _log_sigmoid(dec_ref[0])
    dvf = jnp.where(head0, lg[0:1], lg[1:2])
    dvb = jnp.where(head0, lg[2:3], lg[3:4])
    r = lax.broadcasted_iota(jnp.int32, (c_len, 1), 0).astype(F32)
    pw_scr[0] = jnp.exp((r + 1.0) * dvf)
    pw_scr[1] = jnp.exp((c_len - 1.0 - r) * dvf)
    pw_scr[2] = jnp.exp((c_len - r) * dvb)
    pw_scr[3] = jnp.exp(r * dvb)
    dcf = jnp.exp(c_len * dvf)
    dcb = jnp.exp(c_len * dvb)
    ri = lax.broadcasted_iota(jnp.int32, (c_len, c_len), 0)
    ci = lax.broadcasted_iota(jnp.int32, (c_len, c_len), 1)
    dm = (ri - ci).astype(F32)
    for h in range(2):
        lgf = jnp.concatenate([lg[h:h + 1]] * (c_len // LANES), axis=1)
        lgb = jnp.concatenate([lg[2 + h:3 + h]] * (c_len // LANES), axis=1)
        mask_scr[h] = jnp.exp(jnp.where(dm >= 0, dm * lgf, -dm * lgb))
    rb = lax.broadcasted_iota(jnp.int32, (LANES, LANES), 0) < RET_DK
    cb = lax.broadcasted_iota(jnp.int32, (LANES, LANES), 1) < RET_DK
    same_head = (rb == cb).astype(F32)

    def load(c):
        rows = pl.ds(pl.multiple_of(c * c_len, c_len), c_len)
        q = q_ref[rows, :].astype(F32)
        k = k_ref[rows, :].astype(F32) * (RET_DK ** -0.5)
        v = v_ref[rows, :].astype(F32)
        return rows, q, k, v

    def rotary(t, c):
        rows = pl.ds(pl.multiple_of(c * c_len, c_len), c_len)
        partner = jnp.where(first_half, pltpu.roll(t, LANES - RET_DK // 2, 1), pltpu.roll(t, RET_DK // 2, 1))
        return t * cos_ref[rows, :] + partner * sin_ref[rows, :]

    def load_rot(c):
        rows, q, k, v = load(c)
        is_lat = c < n_lat
        cc = jnp.minimum(c, n_lat - 1)
        q = jnp.where(is_lat, rotary(q, cc), q)
        k = jnp.where(is_lat, rotary(k, cc), k)
        return rows, q, k, v

    def fwd_step(s, st):
        c = jnp.where(s < n_all - n_lat, n_lat + s, s - (n_all - n_lat))
        rows, q, k, v = load_rot(c)
        kb = k.astype(BF16)
        vb = v.astype(BF16)
        o0 = _dot((_dot_nt(jnp.where(head0, q, 0.0).astype(BF16), kb) * mask_scr[0]).astype(BF16), vb)
        o1 = _dot((_dot_nt(jnp.where(head0, 0.0, q).astype(BF16), kb) * mask_scr[1]).astype(BF16), vb)
        o = jnp.where(head0, o0, o1)
        o = o + _dot_nt((q * pw_scr[0]).astype(BF16), st.astype(BF16))
        o_ref[rows, :] += o
        return st * dcf + _dot_tn(vb, (k * pw_scr[1]).astype(BF16)) * same_head

    def bwd_step(s, st):
        c = n_all - 1 - s
        rows, q, k, v = load_rot(c)
        o_ref[rows, :] += _dot_nt((q * pw_scr[2]).astype(BF16), st.astype(BF16))
        return st * dcb + _dot_tn(v.astype(BF16), (k * pw_scr[3]).astype(BF16)) * same_head

    o_ref[...] = jnp.zeros(o_ref.shape, F32)
    zero = jnp.zeros((LANES, LANES), F32)
    lax.fori_loop(0, n_all, lambda s, sts: (fwd_step(s, sts[0]), bwd_step(s, sts[1])), (zero, zero))


def _ret_call(u, dec, cos_t, sin_t):
    t = u.shape[0]
    sb = _seg()
    col = lambda off: pl.BlockSpec((sb, LANES), lambda b, p: (b, off + p))
    return pl.pallas_call(
        _ret_kernel,
        grid=(BATCH, RET_HEADS // 2),
        in_specs=[pl.BlockSpec((1, 4, LANES), lambda b, p: (p, 0, 0)),
                  pl.BlockSpec((SEQ, LANES), lambda b, p: (0, 0)),
                  pl.BlockSpec((SEQ, LANES), lambda b, p: (0, 0)),
                  col(_C_RQ), col(_C_RK), col(_C_RV)],
        out_specs=pl.BlockSpec((sb, LANES), lambda b, p: (b, p)),
        out_shape=jax.ShapeDtypeStruct((t, RET_W), F32),
        scratch_shapes=[pltpu.VMEM((2, RET_CHUNK, RET_CHUNK), F32),
                        pltpu.VMEM((4, RET_CHUNK, LANES), F32)],
        compiler_params=_params(("arbitrary", "arbitrary")),
        name="retention",
    )(dec, cos_t, sin_t, u, u, u)


def _hg_codes(fwd):
    c_len = HG_BLOCK
    ri = lax.broadcasted_iota(jnp.int32, (c_len, c_len), 0)
    ci = lax.broadcasted_iota(jnp.int32, (c_len, c_len), 1)
    dist = (ri - ci) if fwd else (ci - ri)
    code = jnp.full((c_len, c_len), -1, jnp.int32)
    half, level = c_len // 2, 0
    while half >= HG_SUB:
        sh = half.bit_length() - 1
        same_seg = jnp.right_shift(ri, sh + 1) == jnp.right_shift(ci, sh + 1)
        other_half = jnp.right_shift(ri, sh) != jnp.right_shift(ci, sh)
        code = jnp.where(same_seg & other_half & (dist > 0), level, code)
        half, level = half // 2, level + 1
    sub_sh = HG_SUB.bit_length() - 1
    same_sub = jnp.right_shift(ri, sub_sh) == jnp.right_shift(ci, sub_sh)
    for dl in range(HG_SUB):
        code = jnp.where(same_sub & (dist == dl), 100 + dl, code)
    return code


def _hg_chunk(fwd, q, z, v, lb, st, code, tri):
    c_len = HG_CHUNK

    a = jnp.log(lb)
    b = jnp.log1p(-lb) + _log_sigmoid(z)
    lf = jnp.maximum(a, b) + jnp.log1p(jnp.exp(-jnp.abs(a - b)))
    kin = (1.0 - lb) * _sigmoid(-z)

    g = _dot_exact_lhs(tri, lf) * LOG2_E
    g_end = g[c_len - 1:c_len, :] if fwd else g[0:1, :]
    vb = v.astype(BF16)

    o = _dot_nt((q * jnp.exp2(g)).astype(BF16), st.astype(BF16))

    blk = HG_BLOCK
    lo, hi = slice(0, blk), slice(blk, 2 * blk)
    qrows, krows = (hi, lo) if fwd else (lo, hi)
    bnd = g[blk - 1:blk, :] if fwd else g[blk:blk + 1, :]
    qs = (q[qrows] * jnp.exp2(g[qrows] - bnd)).astype(BF16)
    ks = (kin[krows] * jnp.exp2(bnd - g[krows])).astype(BF16)
    cross = _dot(_dot_nt(qs, ks).astype(BF16), vb[krows])

    parts = []
    for rows in (lo, hi):
        parts.append(_hg_block(fwd, q[rows], kin[rows], g[rows], vb[rows], code))
    parts[1 if fwd else 0] = parts[1 if fwd else 0] + cross
    o = o + jnp.concatenate(parts, axis=0)
    st_new = st * jnp.exp2(g_end) + _dot_tn(vb, (kin * jnp.exp2(g_end - g)).astype(BF16))
    return o, st_new


def _hg_block(fwd, q, kin, g, vb, code):
    n = HG_BLOCK

    scores = jnp.zeros((n, n), F32)
    half, level = n // 2, 0
    while half >= HG_SUB:
        seg = 2 * half
        g3 = g.reshape(n // seg, seg, LANES)
        e = half - 1 if fwd else half
        bsel = jnp.broadcast_to(g3[:, e:e + 1, :], g3.shape).reshape(n, LANES)
        qs = (q * jnp.exp2(g - bsel)).astype(BF16)
        ks = (kin * jnp.exp2(bsel - g)).astype(BF16)
        scores = jnp.where(code == level, _dot_nt(qs, ks), scores)
        half, level = half // 2, level + 1

    def shift(x, dl):
        x3 = x.reshape(n // SUBLANES, SUBLANES, LANES)
        return pltpu.roll(x3, dl if fwd else SUBLANES - dl, 1).reshape(n, LANES)

    for dl in range(HG_SUB):
        w = q * kin if dl == 0 else q * shift(kin, dl) * jnp.exp2(g - shift(g, dl))
        scores = jnp.where(code == 100 + dl, jnp.sum(w, axis=-1, keepdims=True), scores)

    return _dot(scores.astype(BF16), vb)


def _hg_kernel(lbl_ref, q_ref, zf_ref, zb_ref, v_ref, o_ref, *, layer):
    c_len = HG_CHUNK
    n_lat = SEQ // c_len
    n_all = _seg() // c_len
    heads = o_ref.shape[1] // LANES

    lbs = []
    for d in range(2):
        x = lbl_ref[d]
        e = jnp.exp(x - jnp.max(x, axis=0, keepdims=True))
        tot = jnp.sum(e, axis=0, keepdims=True)
        part = jnp.zeros_like(tot)
        for i in range(1, layer + 1):
            part = part + e[i:i + 1, :]
        lbs.append(part / tot)

    ri = lax.broadcasted_iota(jnp.int32, (c_len, c_len), 0)
    ci = lax.broadcasted_iota(jnp.int32, (c_len, c_len), 1)
    tri_f = (ri >= ci).astype(BF16)
    tri_b = (ri <= ci).astype(BF16)
    code_f = _hg_codes(True)
    code_b = _hg_codes(False)

    o_ref[...] = jnp.zeros(o_ref.shape, F32)

    def step(s, carry):
        cf = jnp.where(s < n_all - n_lat, n_lat + s, s - (n_all - n_lat))
        cbk = n_all - 1 - s
        rf = pl.ds(pl.multiple_of(cf * c_len, c_len), c_len)
        rk = pl.ds(pl.multiple_of(cbk * c_len, c_len), c_len)
        out = []
        for hh in range(heads):
            ln = pl.ds(hh * LANES, LANES)
            qf = q_ref[rf, ln].astype(F32)
            of, st_f = _hg_chunk(True, qf * _sigmoid(qf), zf_ref[rf, ln], v_ref[rf, ln].astype(F32),
                                 lbs[0][:, hh * LANES:(hh + 1) * LANES], carry[2 * hh], code_f, tri_f)
            o_ref[rf, ln] += of
            qb = q_ref[rk, ln].astype(F32)
            ob, st_b = _hg_chunk(False, qb * _sigmoid(qb), zb_ref[rk, ln], v_ref[rk, ln].astype(F32),
                                 lbs[1][:, hh * LANES:(hh + 1) * LANES], carry[2 * hh + 1], code_b, tri_b)
            o_ref[rk, ln] += ob
            out += [st_f, st_b]
        return tuple(out)

    zero = jnp.zeros((LANES, LANES), F32)
    lax.fori_loop(0, n_all, step, (zero,) * (2 * heads))


def _hg_call(u, uz, lb_logits, layer):
    t = u.shape[0]
    sb = _seg()
    hp = HG_HEADS_PER_STEP
    w = hp * LANES
    col = lambda off: pl.BlockSpec((sb, w), lambda b, h: (b, off // hp + h))
    return pl.pallas_call(
        functools.partial(_hg_kernel, layer=layer),
        grid=(BATCH, HG_HEADS // hp),
        in_specs=[pl.BlockSpec((2, DEPTH, w), lambda b, h: (0, 0, h)),
                  col(_C_HQ), col(0), col(HG_HEADS), col(_C_HI)],
        out_specs=pl.BlockSpec((sb, w), lambda b, h: (b, h)),
        out_shape=jax.ShapeDtypeStruct((t, HG_W), F32),
        compiler_params=_params(("arbitrary", "arbitrary")),
        name="hgrn2",
    )(lb_logits, u, uz, uz, u)


def _lru_kernel(x_ref, wg_ref, bg_ref, lam_ref, cw_ref, cb_ref, y_ref,
                xp_scr, af_scr, bf_scr, ab_scr, bb_scr, hb_scr):
    sb = _seg()
    c_len = LRU_CHUNK
    halo = SUBLANES
    n_tiles = sb // SUBLANES
    n_lat_t = SEQ // SUBLANES

    xp_scr[pl.ds(0, halo), :] = jnp.zeros((halo, LANES), F32)
    xp_scr[pl.ds(halo + sb, halo), :] = jnp.zeros((halo, LANES), F32)
    xp_scr[pl.ds(halo, sb), :] = x_ref[...].astype(F32)

    lam = lam_ref[0]
    sp = jnp.maximum(-lam, 0.0) + jnp.log1p(jnp.exp(-jnp.abs(lam)))
    cw = cw_ref[...]
    wg = wg_ref[0]
    bg = bg_ref[0]
    r8 = lax.broadcasted_iota(jnp.int32, (c_len, 1), 0)
    rm8 = r8 % SUBLANES

    def tile_shift(x, s):
        x3 = x.reshape(c_len // SUBLANES, SUBLANES, LANES)
        return pltpu.roll(x3, s, 1).reshape(c_len, LANES)

    def coeff_chunk(c, _):
        base = pl.multiple_of(c * c_len, c_len)
        w = xp_scr[pl.ds(base, c_len + 2 * halo), :]
        grow = base + r8
        in_lat = grow < SEQ
        pos = jnp.where(in_lat, grow, grow - SEQ)
        seglen = jnp.where(in_lat, SEQ, CTX_LEN)
        wlen = c_len + 2 * halo
        xm2 = jnp.where(pos >= 2, pltpu.roll(w, 2, 0)[halo:halo + c_len], 0.0)
        xm1 = jnp.where(pos >= 1, pltpu.roll(w, 1, 0)[halo:halo + c_len], 0.0)
        x0 = w[halo:halo + c_len]
        xp1 = jnp.where(pos + 1 < seglen, pltpu.roll(w, wlen - 1, 0)[halo:halo + c_len], 0.0)
        xc = xm2 * cw[0:1] + xm1 * cw[1:2] + x0 * cw[2:3] + xp1 * cw[3:4] + cb_ref[...]
        gates = _sigmoid(_dot(xc.astype(BF16), wg) + bg)
        rows = pl.ds(base, c_len)
        for d, (a_scr, b_scr) in enumerate(((af_scr, bf_scr), (ab_scr, bb_scr))):
            rg = gates[:, (2 * d) * LANES:(2 * d + 1) * LANES]
            ig = gates[:, (2 * d + 1) * LANES:(2 * d + 2) * LANES]
            log_a = -LRU_C * rg * sp[d:d + 1]
            a = jnp.exp(log_a)
            th = jnp.tanh(log_a)
            bv = jnp.sqrt(-2.0 * th / (1.0 - th)) * (ig * xc)
            for sft in (1, 2, 4):
                if d == 0:
                    a_s = tile_shift(a, sft)
                    b_s = tile_shift(bv, sft)
                    ok = rm8 >= sft
                else:
                    a_s = tile_shift(a, SUBLANES - sft)
                    b_s = tile_shift(bv, SUBLANES - sft)
                    ok = rm8 < SUBLANES - sft
                bv = jnp.where(ok, a * b_s + bv, bv)
                a = jnp.where(ok, a * a_s, a)
            a_scr[rows, :] = a
            b_scr[rows, :] = bv
        return 0

    lax.fori_loop(0, sb // c_len, coeff_chunk, 0)

    def carry_step(s, carry):
        hf, hb = carry
        tf = jnp.where(s < n_tiles - n_lat_t, n_lat_t + s, s - (n_tiles - n_lat_t))
        tb = n_tiles - 1 - s
        rf = pl.ds(pl.multiple_of(tf * SUBLANES, SUBLANES), SUBLANES)
        rk = pl.ds(pl.multiple_of(tb * SUBLANES, SUBLANES), SUBLANES)
        h1 = af_scr[rf, :] * hf + bf_scr[rf, :]
        y_ref[rf, :] = h1
        h2 = ab_scr[rk, :] * hb + bb_scr[rk, :]
        hb_scr[rk, :] = h2
        return (jnp.broadcast_to(h1[SUBLANES - 1:SUBLANES, :], (SUBLANES, LANES)),
                jnp.broadcast_to(h2[0:1, :], (SUBLANES, LANES)))

    zero = jnp.zeros((SUBLANES, LANES), F32)
    lax.fori_loop(0, n_tiles, carry_step, (zero, zero), unroll=8)
    y_ref[...] += hb_scr[...]


def _lru_call(u, wg, bg, lam, cw, cb):
    t = u.shape[0]
    sb = _seg()
    nb = LRU_W // LANES
    return pl.pallas_call(
        _lru_kernel,
        grid=(BATCH, nb),
        in_specs=[pl.BlockSpec((sb, LANES), lambda b, c: (b, _C_LX + c)),
                  pl.BlockSpec((1, LANES, 4 * LANES), lambda b, c: (c, 0, 0)),
                  pl.BlockSpec((1, 1, 4 * LANES), lambda b, c: (c, 0, 0)),
                  pl.BlockSpec((1, 2, LANES), lambda b, c: (c, 0, 0)),
                  pl.BlockSpec((CONV_W, LANES), lambda b, c: (0, c)),
                  pl.BlockSpec((1, LANES), lambda b, c: (0, c))],
        out_specs=pl.BlockSpec((sb, LANES), lambda b, c: (b, c)),
        out_shape=jax.ShapeDtypeStruct((t, LRU_W), F32),
        scratch_shapes=[pltpu.VMEM((sb + 2 * SUBLANES, LANES), F32)] + [pltpu.VMEM((sb, LANES), F32)] * 5,
        compiler_params=_params(("arbitrary", "arbitrary")),
        name="rglru",
    )(u, wg, bg, lam, cw, cb)


def _group_rms(o, group):
    ri = lax.broadcasted_iota(jnp.int32, (LANES, LANES), 0) // group
    ci = lax.broadcasted_iota(jnp.int32, (LANES, LANES), 1) // group
    ones_bd = (ri == ci).astype(BF16)
    outs = []
    for tix in range(o.shape[1] // LANES):
        x = o[:, tix * LANES:(tix + 1) * LANES]
        sq = x * x
        hi = sq.astype(BF16)
        lo = (sq - hi.astype(F32)).astype(BF16)
        ss = _dot(hi, ones_bd) + _dot(lo, ones_bd)
        outs.append(x * lax.rsqrt(ss * (1.0 / group) + NORM_EPS))
    return jnp.concatenate(outs, axis=1)


def _merge_kernel(*refs, n_gate):
    (x_ref, ml_ref, mc_ref, oret_ref, ohg_ref, ylru_ref, rg_ref, hgt_ref, lgt_ref) = refs[:9]
    gate_refs = refs[9:9 + 3 * n_gate]
    (gng_ref, ghg_ref, wr_ref, wh_ref, wl_ref, wo_ref, n2g_ref, rwh_ref, rwm_ref, rb_ref,
     xo_ref, h2_ref, lg_ref) = refs[9 + 3 * n_gate:]
    tiles_per_b = _seg() // ROW_TILE
    is_ctx = (pl.program_id(0) % tiles_per_b) >= SEQ // ROW_TILE
    ml = ml_ref[0]
    mc = mc_ref[0]

    def gate(i):
        parts = [gate_refs[i * n_gate + j][...].astype(F32) for j in range(n_gate)]
        return _sigmoid(parts[0] if n_gate == 1 else jnp.concatenate(parts, axis=1))

    rg = rg_ref[...].astype(F32)
    ret_in = _group_rms(oret_ref[...], RET_DK) * gng_ref[...] * (rg * _sigmoid(rg))
    ret = _dot(ret_in.astype(BF16), wr_ref[0])
    hg_in = _group_rms(ohg_ref[...], HG_DK) * ghg_ref[...] * _sigmoid(hgt_ref[...].astype(F32))
    hgr = _dot(hg_in.astype(BF16), wh_ref[0])
    lgt = lgt_ref[...].astype(F32)
    gelu = 0.5 * lgt * (1.0 + jnp.tanh(0.7978845608028654 * (lgt + 0.044715 * lgt * lgt * lgt)))
    lru = _dot((ylru_ref[...] * gelu).astype(BF16), wl_ref[0])
    merged = gate(0) * ret + gate(1) * hgr + gate(2) * lru
    mix = _dot(merged.astype(BF16), wo_ref[0])
    g1 = jnp.where(is_ctx, mc[2:3, :], ml[2:3, :])
    xn = x_ref[...] + g1 * mix
    xo_ref[...] = xn
    h2 = _norm_modulate(xn, n2g_ref[...], ml, mc, is_ctx, 3)
    h2_ref[...] = h2.astype(BF16)
    hh, hm, _ = _split3(h2)
    lg_ref[...] = _dot(hh, rwh_ref[...]) + (_dot(hh, rwm_ref[...]) + _dot(hm, rwh_ref[...])) + rb_ref[...]


def _merge_call(xs, mod, o_ret, o_hg, y_lru, u, gn_g, hgn_g, wr, wh, wl, wo, n2g, rw_hi, rw_mid, rb, layer):
    t = xs.shape[0]
    r = ROW_TILE
    tiles_per_b = _seg() // r
    n_gate = D_MODEL // 512
    row = lambda w: pl.BlockSpec((r, w), lambda i: (i, 0))
    ucol = lambda blk: pl.BlockSpec((r, 512), lambda i: (i, blk))

    def full(a):
        if a.ndim == 3:
            return pl.BlockSpec((1,) + a.shape[1:], lambda i: (layer, 0, 0))
        return pl.BlockSpec(a.shape, lambda i: (0,) * a.ndim)
    gate_base = _C_GATES * LANES // 512
    in_specs = [row(D_MODEL),
                pl.BlockSpec((1, N_MOD, D_MODEL), lambda i: (i // tiles_per_b, 0, 0)),
                pl.BlockSpec((1, N_MOD, D_MODEL), lambda i: (BATCH, 0, 0)),
                row(RET_W), row(HG_W), row(LRU_W),
                ucol(_C_RG * LANES // 512), ucol(_C_HGT * LANES // 512), ucol(_C_LGT * LANES // 512)]
    in_specs += [ucol(gate_base + j) for j in range(3 * n_gate)]
    consts = [gn_g, hgn_g, wr, wh, wl, wo, n2g, rw_hi, rw_mid, rb]
    in_specs += [full(a) for a in consts]
    return pl.pallas_call(
        functools.partial(_merge_kernel, n_gate=n_gate),
        grid=(t // r,),
        in_specs=in_specs,
        out_specs=[row(D_MODEL), row(D_MODEL), row(LANES)],
        out_shape=[jax.ShapeDtypeStruct((t, D_MODEL), F32), jax.ShapeDtypeStruct((t, D_MODEL), BF16),
                   jax.ShapeDtypeStruct((t, LANES), F32)],
        compiler_params=_params(("arbitrary",)),
        name="merge",
    )(xs, mod, mod, o_ret, o_hg, y_lru, *([u] * (3 + 3 * n_gate)), *consts)


def _route_kernel(lg_ref, oi_ref, ow_ref, lst_ref, meta_ref, cnt_ref, off_scr):
    @pl.when(pl.program_id(0) == 0)
    def _():
        off_scr[...] = jnp.zeros(off_scr.shape, F32)

    r = lg_ref.shape[0]
    lane = lax.broadcasted_iota(jnp.int32, (r, LANES), 1)
    neg = -jnp.inf
    l = jnp.where(lane < N_EXPERTS, lg_ref[...], neg)
    vals, hots = [], []
    lanef = lane.astype(F32)
    for _ in range(TOP_K):
        m = jnp.max(l, axis=-1, keepdims=True)
        idxf = jnp.min(jnp.where(l == m, lanef, float(LANES)), axis=-1, keepdims=True)
        hot = lanef == idxf
        l = jnp.where(hot, neg, l)
        vals.append(m)
        hots.append(hot)
    es = [jnp.exp(v - vals[0]) for v in vals]
    den = es[0]
    for e in es[1:]:
        den = den + e
    chosen = hots[0].astype(F32)
    for hot in hots[1:]:
        chosen = chosen + hot.astype(F32)
    ri = lax.broadcasted_iota(jnp.int32, (r, r), 0)
    ci = lax.broadcasted_iota(jnp.int32, (r, r), 1)
    before = _dot((ri > ci).astype(BF16), chosen.astype(BF16))
    count = jnp.sum(chosen, axis=0, keepdims=True)
    run = jnp.floor((count + (SUBLANES - 1.0)) * (1.0 / SUBLANES)) * SUBLANES
    li = lax.broadcasted_iota(jnp.int32, (LANES, LANES), 0)
    lj = lax.broadcasted_iota(jnp.int32, (LANES, LANES), 1)
    run8 = jnp.broadcast_to(run, (SUBLANES, LANES))
    start = _dot(run8.astype(BF16), (li < lj).astype(BF16))[0:1, :]
    pos = start + before
    oi = jnp.zeros((r, LANES), jnp.int32)
    ow = jnp.zeros((r, LANES), F32)
    lsf = jnp.zeros((r, LANES), F32)
    for k in range(TOP_K):
        row_k = jnp.sum(jnp.where(hots[k], pos, 0.0), axis=-1, keepdims=True)
        oi = jnp.where(lane == k, row_k.astype(jnp.int32), oi)
        lsf = jnp.where(lane == k, row_k, lsf)
        ow = jnp.where(lane == k, es[k] / den, ow)
    oi_ref[...] = oi
    ow_ref[...] = ow
    lst_ref[0] = jnp.transpose(lsf)[0:SUBLANES, :]
    mrow = lax.broadcasted_iota(jnp.int32, (SUBLANES, LANES), 0)
    meta_ref[0] = jnp.where(mrow == 0, run8, jnp.where(mrow == 1, jnp.broadcast_to(start, (SUBLANES, LANES)),
                                                       off_scr[...]))
    off_scr[...] = off_scr[...] + run
    cnt_ref[...] = off_scr[...]


def _route_call(logits):
    t = logits.shape[0]
    r = ROW_TILE
    row = pl.BlockSpec((r, LANES), lambda i: (i, 0))
    return pl.pallas_call(
        _route_kernel,
        grid=(t // r,),
        in_specs=[row],
        out_specs=[row, row, pl.BlockSpec((1, SUBLANES, r), lambda i: (i, 0, 0)),
                   pl.BlockSpec((1, SUBLANES, LANES), lambda i: (i, 0, 0)),
                   pl.BlockSpec((SUBLANES, LANES), lambda i: (0, 0))],
        out_shape=[jax.ShapeDtypeStruct((t, LANES), jnp.int32), jax.ShapeDtypeStruct((t, LANES), F32),
                   jax.ShapeDtypeStruct((t // r, SUBLANES, r), F32),
                   jax.ShapeDtypeStruct((t // r, SUBLANES, LANES), F32),
                   jax.ShapeDtypeStruct((SUBLANES, LANES), F32)],
        scratch_shapes=[pltpu.VMEM((SUBLANES, LANES), F32)],
        compiler_params=_params(("arbitrary",)),
        name="route",
    )(logits)


def _pack_pairs(x):
    m = x.shape[1] // 2
    bits = pltpu.bitcast(x, jnp.uint32)
    return bits[:, :m] | (bits[:, m:] >> 16)


def _unpack_pairs(w):
    hi = pltpu.bitcast(w & jnp.uint32(0xFFFF0000), F32).astype(BF16)
    lo = pltpu.bitcast(w << 16, F32).astype(BF16)
    return hi, lo


def _local_rows():
    rows = ROW_TILE * TOP_K + N_EXPERTS * (SUBLANES - 1)
    return -(-rows // ROW_TILE) * ROW_TILE


RUN_PIECE = 4 * SUBLANES


def _max_pieces():
    return _local_rows() // RUN_PIECE, N_EXPERTS * (RUN_PIECE // SUBLANES - 1)


def _meta_width():
    n_big, n_small = _max_pieces()
    return -(-(2 + 2 * n_big + 2 * n_small) // LANES) * LANES


def _piece_tables(first_slot, first_row, pieces):
    per_big = RUN_PIECE // SUBLANES
    max_big, max_small = _max_pieces()

    def flat(count, limit, offset_rows, step):
        ends = jnp.cumsum(count, axis=1)
        p = jnp.arange(limit, dtype=jnp.int32)[None, :, None]
        mine = ((ends - count)[:, None, :] <= p) & (p < ends[:, None, :])
        take = lambda a: jnp.sum(jnp.where(mine, a[:, None, :], 0), axis=2)
        j = p[:, :, 0] - take(ends - count)
        off = take(offset_rows) + j * step
        return take(first_row) + off, take(first_slot) + off, ends[:, -1:]

    n_big = pieces // per_big
    big_row, big_slot, tot_big = flat(n_big, max_big, jnp.zeros_like(pieces), RUN_PIECE)
    small_row, small_slot, tot_small = flat(pieces - n_big * per_big, max_small, n_big * RUN_PIECE, SUBLANES)
    table = jnp.concatenate([tot_big, tot_small, big_row, big_slot, small_row, small_slot], axis=1)
    return jnp.pad(table, ((0, 0), (0, _meta_width() - table.shape[1])))[:, None, :]


def _run_copies(meta_ref, make_copy, do):
    max_big, max_small = _max_pieces()
    big0 = 2
    small0 = 2 + 2 * max_big

    def big(p, _):
        do(make_copy(pl.multiple_of(meta_ref[0, 0, big0 + p], SUBLANES),
                     pl.multiple_of(meta_ref[0, 0, big0 + max_big + p], SUBLANES), RUN_PIECE))
        return 0

    def small(p, _):
        do(make_copy(pl.multiple_of(meta_ref[0, 0, small0 + p], SUBLANES),
                     pl.multiple_of(meta_ref[0, 0, small0 + max_small + p], SUBLANES), SUBLANES))
        return 0

    lax.fori_loop(0, meta_ref[0, 0, 0], big, 0)
    lax.fori_loop(0, meta_ref[0, 0, 1], small, 0)


def _dispatch_kernel(meta_ref, prev_meta_ref, lst_ref, h_ref, xs_in_ref, xs_ref, perm_scr, sem):
    del xs_in_ref
    i = pl.program_id(0)
    cur = i % 2
    r = h_ref.shape[0]
    ls = lst_ref[0].astype(jnp.int32)
    hb = h_ref[...]
    for s in range(perm_scr.shape[1] // r):
        rows = s * r + lax.broadcasted_iota(jnp.int32, (r, r), 0)
        sel = rows == ls[0:1, :]
        for k in range(1, TOP_K):
            sel = jnp.logical_or(sel, rows == ls[k:k + 1, :])
        perm_scr[cur, pl.ds(s * r, r), :] = _pack_pairs(_dot(jnp.where(sel, 1.0, 0.0).astype(BF16), hb))

    def copies_from(buf):
        def make_copy(row, slot, n):
            return pltpu.make_async_copy(perm_scr.at[buf, pl.ds(row, n)], xs_ref.at[pl.ds(slot, n)], sem.at[buf])
        return make_copy

    _run_copies(meta_ref, copies_from(cur), lambda c: c.start())

    @pl.when(i > 0)
    def _():
        _run_copies(prev_meta_ref, copies_from(1 - cur), lambda c: c.wait())

    @pl.when(i == pl.num_programs(0) - 1)
    def _():
        _run_copies(meta_ref, copies_from(cur), lambda c: c.wait())


def _dispatch_call(h2, meta, lst, slots):
    t = h2.shape[0]
    r = ROW_TILE
    return pl.pallas_call(
        _dispatch_kernel,
        grid=(t // r,),
        in_specs=[pl.BlockSpec((1, 1, _meta_width()), lambda i: (i, 0, 0), memory_space=pltpu.SMEM),
                  pl.BlockSpec((1, 1, _meta_width()), lambda i: (jnp.maximum(i - 1, 0), 0, 0), memory_space=pltpu.SMEM),
                  pl.BlockSpec((1, SUBLANES, r), lambda i: (i, 0, 0)),
                  pl.BlockSpec((r, D_MODEL), lambda i: (i, 0)),
                  pl.BlockSpec(memory_space=pl.ANY)],
        out_specs=pl.BlockSpec(memory_space=pl.ANY),
        out_shape=jax.ShapeDtypeStruct(slots.shape, jnp.uint32),
        scratch_shapes=[pltpu.VMEM((2, _local_rows(), D_MODEL // 2), jnp.uint32), pltpu.SemaphoreType.DMA((2,))],
        input_output_aliases={4: 0},
        compiler_params=_params(("arbitrary",)),
        name="dispatch",
    )(meta, meta, lst, h2, slots)


def _expert_kernel(bexp_ref, nv_ref, next_ref, par_ref, x_ref, wgu_hbm, bgu_ref, wd_hbm, bd_ref, y_ref,
                   wgu_scr, wd_scr, gu_stage, d_stage, sem, *, layer):
    i = pl.program_id(0)
    f = D_MODEL
    prev = bexp_ref[jnp.maximum(i - 1, 0)]
    new_expert = jnp.logical_or(i == 0, bexp_ref[i] != prev)

    def weight_copies(e, b):
        return (pltpu.make_async_copy(wgu_hbm.at[layer, e], gu_stage.at[b], sem.at[b, 0]),
                pltpu.make_async_copy(wd_hbm.at[layer, e], d_stage.at[b], sem.at[b, 1]))

    @pl.when(jnp.logical_and(i < nv_ref[0], new_expert))
    def _():
        b = par_ref[i]

        @pl.when(i == 0)
        def _():
            for c in weight_copies(bexp_ref[i], b):
                c.start()

        for c in weight_copies(bexp_ref[i], b):
            c.wait()

        @pl.when(next_ref[i] >= 0)
        def _():
            for c in weight_copies(next_ref[i], 1 - b):
                c.start()

        slab = 256
        for s in range(D_MODEL // slab):
            wgu_scr[pl.ds(s * slab, slab), :] = gu_stage[b, pl.ds(s * slab, slab), :].astype(BF16)
            wd_scr[pl.ds(s * slab, slab), :] = d_stage[b, pl.ds(s * slab, slab), :].astype(BF16)

    @pl.when(i < nv_ref[0])
    def _():
        x_hi, x_lo = _unpack_pairs(x_ref[...])
        half = D_MODEL // 2
        gu = _dot(x_hi, wgu_scr[:half, :]) + _dot(x_lo, wgu_scr[half:, :]) + bgu_ref[0, 0]
        gate = jnp.minimum(gu[:, :f], SWIGLU_LIMIT)
        up = jnp.clip(gu[:, f:], -SWIGLU_LIMIT, SWIGLU_LIMIT)
        glu = gate * _sigmoid(SWIGLU_ALPHA * gate)
        y = _dot(((up + 1.0) * glu).astype(BF16), wd_scr[...]) + bd_ref[0, 0]
        y_ref[...] = _pack_pairs(y.astype(BF16).astype(F32))

    @pl.when(i >= nv_ref[0])
    def _():
        y_ref[...] = jnp.zeros(y_ref.shape, jnp.uint32)


def _expert_call(slots, bexp, nvalid, next_expert, stage, w_gu, b_gu, w_down, b_down, layer):
    s = slots.shape[0]
    nb = s // MOE_BLOCK
    f = D_MODEL
    blk = lambda i, be, nv, nx, pr: (jnp.minimum(i, nv[0] - 1), 0)
    exp3 = lambda i, be, nv, nx, pr: (layer, be[jnp.minimum(i, nv[0] - 1)], 0, 0)
    grid_spec = pltpu.PrefetchScalarGridSpec(
        num_scalar_prefetch=4,
        grid=(nb,),
        in_specs=[pl.BlockSpec((MOE_BLOCK, D_MODEL // 2), blk),
                  pl.BlockSpec(memory_space=pl.ANY),
                  pl.BlockSpec((1, 1, 1, 2 * f), exp3),
                  pl.BlockSpec(memory_space=pl.ANY),
                  pl.BlockSpec((1, 1, 1, D_MODEL), exp3)],
        out_specs=pl.BlockSpec((MOE_BLOCK, D_MODEL // 2), lambda i, be, nv, nx, pr: (i, 0)),
        scratch_shapes=[pltpu.VMEM((D_MODEL, 2 * f), BF16), pltpu.VMEM((f, D_MODEL), BF16),
                        pltpu.VMEM((2, D_MODEL, 2 * f), F32), pltpu.VMEM((2, f, D_MODEL), F32),
                        pltpu.SemaphoreType.DMA((2, 2))],
    )
    return pl.pallas_call(
        functools.partial(_expert_kernel, layer=layer),
        grid_spec=grid_spec,
        out_shape=jax.ShapeDtypeStruct((s, D_MODEL // 2), jnp.uint32),
        compiler_params=_params(("arbitrary",)),
        name="experts",
    )(bexp, nvalid, next_expert, stage, slots, w_gu, b_gu.reshape(DEPTH, N_EXPERTS, 1, 2 * f), w_down,
      b_down.reshape(DEPTH, N_EXPERTS, 1, D_MODEL))


def _combine_kernel(meta_ref, next_meta_ref, x_ref, ml_ref, mc_ref, oi_ref, w_ref, fg_ref, y_ref, o_ref, buf, sem,
                    *, last, n_t):
    r = x_ref.shape[0]
    i = pl.program_id(0)
    cur = i % 2

    def copies_into(b):
        def make_copy(row, slot, n):
            return pltpu.make_async_copy(y_ref.at[pl.ds(slot, n)], buf.at[b, pl.ds(row, n)], sem.at[b])
        return make_copy

    @pl.when(i == 0)
    def _():
        buf[...] = jnp.zeros(buf.shape, jnp.uint32)
        _run_copies(meta_ref, copies_into(0), lambda c: c.start())

    @pl.when(i + 1 < pl.num_programs(0))
    def _():
        _run_copies(next_meta_ref, copies_into(1 - cur), lambda c: c.start())

    _run_copies(meta_ref, copies_into(cur), lambda c: c.wait())

    if last:
        is_ctx = False
    else:
        is_ctx = (i % n_t) >= SEQ // ROW_TILE
    w = w_ref[...]
    oi = oi_ref[...]
    ff_hi = jnp.zeros((r, D_MODEL // 2), F32)
    ff_lo = jnp.zeros((r, D_MODEL // 2), F32)
    for s in range(buf.shape[1] // r):
        cols = s * r + lax.broadcasted_iota(jnp.int32, (r, r), 1)
        wm = jnp.zeros((r, r), F32)
        for k in range(TOP_K):
            wm = jnp.where(cols == oi[:, k:k + 1], w[:, k:k + 1], wm)
        y_hi, y_lo = _unpack_pairs(buf[cur, pl.ds(s * r, r), :])
        ff_hi = ff_hi + _dot(wm.astype(BF16), y_hi)
        ff_lo = ff_lo + _dot(wm.astype(BF16), y_lo)
    ff = jnp.concatenate([ff_hi, ff_lo], axis=1)
    g2 = jnp.where(is_ctx, mc_ref[0][5:6, :], ml_ref[0][5:6, :])
    xn = x_ref[...] + g2 * ff
    if last:
        ms = jnp.mean(xn * xn, axis=-1, keepdims=True)
        xn = xn * lax.rsqrt(ms + NORM_EPS) * fg_ref[...]
    o_ref[...] = xn


def _combine_call(xs, mod, meta, rows, wts, y_slots, final_g, last):
    t = xs.shape[0]
    r = ROW_TILE
    tiles_per_b = _seg() // r
    n_t = SEQ // r if last else tiles_per_b
    n_steps = BATCH * n_t
    tile = lambda i: (i // n_t) * tiles_per_b + i % n_t
    out_rows = BATCH * SEQ if last else t
    return pl.pallas_call(
        functools.partial(_combine_kernel, last=last, n_t=n_t),
        grid=(n_steps,),
        in_specs=[pl.BlockSpec((1, 1, _meta_width()), lambda i: (tile(i), 0, 0), memory_space=pltpu.SMEM),
                  pl.BlockSpec((1, 1, _meta_width()), lambda i: (tile(jnp.minimum(i + 1, n_steps - 1)), 0, 0),
                               memory_space=pltpu.SMEM),
                  pl.BlockSpec((r, D_MODEL), lambda i: (tile(i), 0)),
                  pl.BlockSpec((1, N_MOD, D_MODEL), lambda i: (i // n_t, 0, 0)),
                  pl.BlockSpec((1, N_MOD, D_MODEL), lambda i: (BATCH, 0, 0)),
                  pl.BlockSpec((r, LANES), lambda i: (tile(i), 0)),
                  pl.BlockSpec((r, LANES), lambda i: (tile(i), 0)),
                  pl.BlockSpec((1, D_MODEL), lambda i: (0, 0)),
                  pl.BlockSpec(memory_space=pl.ANY)],
        out_specs=pl.BlockSpec((r, D_MODEL), lambda i: (i, 0)),
        out_shape=jax.ShapeDtypeStruct((out_rows, D_MODEL), F32),
        scratch_shapes=[pltpu.VMEM((2, _local_rows(), D_MODEL // 2), jnp.uint32), pltpu.SemaphoreType.DMA((2,))],
        compiler_params=_params(("arbitrary",)),
        name="combine",
    )(meta, meta, xs, mod, mod, rows, wts, final_g, y_slots)


def _rotary_tables():
    rows = SEQ // GRID_W
    row = jnp.repeat(jnp.arange(rows, dtype=F32), GRID_W)
    colp = jnp.tile(jnp.arange(GRID_W, dtype=F32), rows)
    n_freq = RET_DK // 4
    inv = ROPE_BASE ** (-jnp.arange(n_freq, dtype=F32) / n_freq)
    ang = jnp.concatenate([row[:, None] * inv, colp[:, None] * inv], axis=-1)
    cos, sin = jnp.cos(ang), jnp.sin(ang)
    reps = LANES // (RET_DK // 2)
    cos_t = jnp.tile(cos, (1, reps))
    sign = jnp.where((jnp.arange(LANES) % RET_DK) < RET_DK // 2, -1.0, 1.0).astype(F32)
    sin_t = jnp.tile(sin, (1, reps)) * sign
    return cos_t, sin_t


def _lru_gate_weights(wa, ba, wx, bx):
    nb = LRU_W // LANES
    per = LANES // LRU_BD
    eye = jnp.eye(per, dtype=F32)

    def embed(w):
        w = w.reshape(nb, per, LRU_BD, LRU_BD)
        return jnp.einsum('cpij,pq->cpiqj', w, eye).reshape(nb, LANES, LANES)

    wg = jnp.concatenate([embed(wa[0]), embed(wx[0]), embed(wa[1]), embed(wx[1])], axis=-1)
    bias = jnp.concatenate([ba[0].reshape(nb, 1, LANES), bx[0].reshape(nb, 1, LANES),
                            ba[1].reshape(nb, 1, LANES), bx[1].reshape(nb, 1, LANES)], axis=-1)
    return wg.astype(BF16), bias


def kernel(x, c, ctx, c_ctx, mod_w, mod_b, norm1_g, norm2_g, final_g, w_in, ret_decay, ret_gn_g, w_ret_o,
           hgrn_lb_logits, hgrn_gn_g, w_hgrn_o, lru_conv_w, lru_conv_b, lru_wa, lru_ba, lru_wx, lru_bx,
           lru_lambda, w_lru_o, w_out, router_w, router_b, exp_w_gu, exp_b_gu, exp_w_down, exp_b_down):
    assert CTX_LEN == RET_CHUNK and SEQ % RET_CHUNK == 0
    sb = _seg()
    t = BATCH * sb
    xs = jnp.concatenate([x, ctx], axis=1).reshape(t, D_MODEL)

    cvec = jnp.concatenate([c, c_ctx[None, :], jnp.zeros((SUBLANES - BATCH - 1, D_MODEL), F32)], axis=0)
    mod_all = _mod_call(cvec, mod_w, mod_b).reshape(DEPTH, SUBLANES, N_MOD, D_MODEL)

    cos_t, sin_t = _rotary_tables()
    lb_logits = jnp.transpose(hgrn_lb_logits, (1, 0, 2))
    n_slots = t * TOP_K + (t // ROW_TILE) * N_EXPERTS * (SUBLANES - 1) + N_EXPERTS * (MOE_BLOCK - 1)
    n_blocks = -(-n_slots // MOE_BLOCK)
    n_slots = n_blocks * MOE_BLOCK
    slots = jnp.zeros((n_slots, D_MODEL // 2), jnp.uint32)
    rw_pad = jnp.pad(router_w, ((0, 0), (0, 0), (0, LANES - N_EXPERTS)))
    rw_hi = rw_pad.astype(BF16)
    rw_mid = (rw_pad - rw_hi.astype(F32)).astype(BF16)
    rb_pad = jnp.pad(router_b, ((0, 0), (0, LANES - N_EXPERTS)))
    w_ret_b, w_hg_b, w_lru_b, w_out_b = (w.astype(BF16) for w in (w_ret_o, w_hgrn_o, w_lru_o, w_out))
    block_first_slot = jnp.arange(n_blocks, dtype=jnp.int32) * MOE_BLOCK
    expert_ids = jnp.arange(N_EXPERTS, dtype=jnp.int32)

    out = None
    for l in range(DEPTH):
        last = l == DEPTH - 1
        mod = mod_all[l]
        u, uz = _inproj_call(xs, mod, norm1_g[l][None, :], w_in, l)

        dec = jnp.broadcast_to(ret_decay[l].reshape(2, RET_HEADS // 2, 2, 1).transpose(1, 0, 2, 3),
                               (RET_HEADS // 2, 2, 2, LANES)).reshape(RET_HEADS // 2, 4, LANES)
        o_ret = _ret_call(u, dec, cos_t, sin_t)
        o_hg = _hg_call(u, uz, lb_logits, l)
        wg, bg = _lru_gate_weights(lru_wa[l], lru_ba[l], lru_wx[l], lru_bx[l])
        lam = lru_lambda[l].reshape(2, LRU_W // LANES, LANES).transpose(1, 0, 2)
        y_lru = _lru_call(u, wg, bg, lam, lru_conv_w[l], lru_conv_b[l][None, :])

        xs, h2, logits = _merge_call(
            xs, mod, o_ret, o_hg, y_lru, u, ret_gn_g[l][None, :], hgrn_gn_g[l][None, :],
            w_ret_b, w_hg_b, w_lru_b, w_out_b, norm2_g[l][None, :], rw_hi[l], rw_mid[l], rb_pad[l][None, :], l)

        rows, wts, lst, tile_meta, cnt = _route_call(logits)
        run = tile_meta[:, 0, :N_EXPERTS].astype(jnp.int32)
        first_row = tile_meta[:, 1, :N_EXPERTS].astype(jnp.int32)
        earlier = tile_meta[:, 2, :N_EXPERTS].astype(jnp.int32)
        counts = cnt[0, :N_EXPERTS].astype(jnp.int32)
        padded = (counts + MOE_BLOCK - 1) // MOE_BLOCK * MOE_BLOCK
        pad_end = jnp.cumsum(padded)
        first_slot = (pad_end - padded)[None, :] + earlier
        meta = _piece_tables(first_slot, first_row, run // SUBLANES)
        bexp = jnp.minimum(jnp.sum((pad_end[None, :] <= block_first_slot[:, None]).astype(jnp.int32), axis=1),
                           N_EXPERTS - 1)
        nvalid = (pad_end[-1:] // MOE_BLOCK).astype(jnp.int32)
        present = padded > 0
        later = jnp.where(present[None, :] & (expert_ids[None, :] > expert_ids[:, None]), expert_ids[None, :],
                          N_EXPERTS)
        next_of = jnp.min(later, axis=1)
        next_of = jnp.where(next_of >= N_EXPERTS, -1, next_of)
        stage_of = (jnp.cumsum(present.astype(jnp.int32)) - present.astype(jnp.int32)) % 2
        is_e = bexp[:, None] == expert_ids[None, :]
        next_expert = jnp.sum(jnp.where(is_e, next_of[None, :], 0), axis=1)
        stage = jnp.sum(jnp.where(is_e, stage_of[None, :], 0), axis=1)

        slots = _dispatch_call(h2, meta, lst, slots)
        y_slots = _expert_call(slots, bexp, nvalid, next_expert, stage, exp_w_gu, exp_b_gu, exp_w_down,
                               exp_b_down, l)
        res = _combine_call(xs, mod, meta, rows, wts, y_slots, final_g[None, :], last)
        if last:
            out = res.reshape(BATCH, SEQ, D_MODEL)
        else:
            xs = res
    return out
```

```python
import functools

import jax
import jax.numpy as jnp
from jax import lax
from jax.experimental import pallas as pl
from jax.experimental.pallas import tpu as pltpu

F32 = jnp.float32
BF16 = jnp.bfloat16

D_MODEL = 1024
BATCH = 4
SEQ = 4096
DEPTH = 4
GRID_W = 64
CTX_LEN = 256
N_MOD = 6
NORM_EPS = 1e-6
RET_HEADS = 8
RET_DK = 64
RET_W = 512
ROPE_BASE = 10000.0
HG_HEADS = 4
HG_DK = 128
HG_W = 512
LRU_W = 512
LRU_BLOCKS = 8
LRU_BD = 64
LRU_C = 8.0
CONV_W = 4
N_EXPERTS = 32
TOP_K = 4
SWIGLU_ALPHA = 1.702
SWIGLU_LIMIT = 7.0

LANES = 128
SUBLANES = 8
RET_CHUNK = 256
RET_PAIRS_PER_STEP = 2
HG_CHUNK = 256
HG_BLOCK = 128
HG_SUB = 4
HG_HEADS_PER_STEP = 2
LOG2_E = 1.4426950408889634
LRU_CHUNK = 256
MOE_BLOCK = 512
ROW_TILE = 256
MOD_TN = 1536
VMEM_LIMIT = 56 * 1024 * 1024

_C_RQ, _C_RK, _C_RV, _C_RG = 0, 4, 8, 12
_C_HQ, _C_HFF, _C_HFB, _C_HI, _C_HGT = 16, 20, 24, 28, 32
_C_LX, _C_LGT = 36, 40
_C_GATES = 44


def _seg():
    return SEQ + CTX_LEN


def _in_cols():
    return _C_GATES * LANES + 3 * D_MODEL


def _dot(a, b):
    return jnp.dot(a, b, preferred_element_type=F32)


def _dot_nt(a, b):
    return lax.dot_general(a, b, (((1,), (1,)), ((), ())), preferred_element_type=F32)


def _dot_tn(a, b):
    return lax.dot_general(a, b, (((0,), (0,)), ((), ())), preferred_element_type=F32)


def _split3(x):
    hi = x.astype(BF16)
    r = x - hi.astype(F32)
    mid = r.astype(BF16)
    lo = (r - mid.astype(F32)).astype(BF16)
    return hi, mid, lo


def _dot_exact_lhs(a, x):
    hi, mid, lo = _split3(x)
    return _dot(a, hi) + _dot(a, mid) + _dot(a, lo)


def _dot3(a, b):
    ah, am, _ = _split3(a)
    bh, bm, _ = _split3(b)
    return _dot(ah, bh) + (_dot(ah, bm) + _dot(am, bh))


def _sigmoid(x):
    return 0.5 + 0.5 * jnp.tanh(0.5 * x)


def _log_sigmoid(x):
    return jnp.minimum(x, 0.0) - jnp.log1p(jnp.exp(-jnp.abs(x)))


def _params(sem):
    return pltpu.CompilerParams(dimension_semantics=sem, vmem_limit_bytes=VMEM_LIMIT)


def _mod_kernel(c_ref, w_ref, b_ref, o_ref):
    c = c_ref[...]
    s = c * _sigmoid(c)
    o_ref[0] = _dot3(s, w_ref[0]) + b_ref[0]


def _mod_call(cvec, mod_w, mod_b):
    n6 = N_MOD * D_MODEL
    tn = MOD_TN
    return pl.pallas_call(
        _mod_kernel,
        grid=(DEPTH, n6 // tn),
        in_specs=[pl.BlockSpec((SUBLANES, D_MODEL), lambda l, j: (0, 0)),
                  pl.BlockSpec((1, D_MODEL, tn), lambda l, j: (l, 0, j)),
                  pl.BlockSpec((1, 1, tn), lambda l, j: (l, 0, j))],
        out_specs=pl.BlockSpec((1, SUBLANES, tn), lambda l, j: (l, 0, j)),
        out_shape=jax.ShapeDtypeStruct((DEPTH, SUBLANES, n6), F32),
        compiler_params=_params(("arbitrary", "arbitrary")),
        name="modulation",
    )(cvec, mod_w, mod_b.reshape(DEPTH, 1, n6))


def _norm_modulate(x, g, mod_l, mod_c, is_ctx, first):
    ms = jnp.mean(x * x, axis=-1, keepdims=True)
    y = x * lax.rsqrt(ms + NORM_EPS) * g
    sh = jnp.where(is_ctx, mod_c[first:first + 1, :], mod_l[first:first + 1, :])
    sc = jnp.where(is_ctx, mod_c[first + 1:first + 2, :], mod_l[first + 1:first + 2, :])
    return y * (1.0 + sc) + sh


def _inproj_kernel(x_ref, ml_ref, mc_ref, g_ref, w_ref, o_ref, oz_ref, h_scr, *, tm, z_first, z_count):
    j = pl.program_id(1)

    @pl.when(j == 0)
    def _():
        tiles_per_b = _seg() // tm
        row0 = (pl.program_id(0) % tiles_per_b) * tm
        rows = row0 + lax.broadcasted_iota(jnp.int32, (tm, 1), 0)
        h = _norm_modulate(x_ref[...], g_ref[...], ml_ref[0], mc_ref[0], rows >= SEQ, 0)
        h_scr[...] = h.astype(BF16)

    acc = _dot(h_scr[...], w_ref[0].astype(BF16))
    o_ref[...] = acc.astype(BF16)

    @pl.when(jnp.logical_and(j >= z_first, j < z_first + z_count))
    def _():
        oz_ref[...] = acc


def _inproj_call(xs, mod, g, w, layer):
    t = xs.shape[0]
    tm = _seg() // 2
    ncol = _in_cols()
    tn = HG_W
    tiles_per_b = _seg() // tm
    z_first = _C_HFF * LANES // tn
    z_count = 2
    return pl.pallas_call(
        functools.partial(_inproj_kernel, tm=tm, z_first=z_first, z_count=z_count),
        grid=(t // tm, ncol // tn),
        in_specs=[pl.BlockSpec((tm, D_MODEL), lambda i, j: (i, 0)),
                  pl.BlockSpec((1, N_MOD, D_MODEL), lambda i, j: (i // tiles_per_b, 0, 0)),
                  pl.BlockSpec((1, N_MOD, D_MODEL), lambda i, j: (BATCH, 0, 0)),
                  pl.BlockSpec((1, D_MODEL), lambda i, j: (0, 0)),
                  pl.BlockSpec((1, D_MODEL, tn), lambda i, j: (layer, 0, j))],
        out_specs=[pl.BlockSpec((tm, tn), lambda i, j: (i, j)),
                   pl.BlockSpec((tm, tn), lambda i, j: (i, jnp.clip(j - z_first, 0, z_count - 1)))],
        out_shape=[jax.ShapeDtypeStruct((t, ncol), BF16), jax.ShapeDtypeStruct((t, z_count * tn), F32)],
        scratch_shapes=[pltpu.VMEM((tm, D_MODEL), BF16)],
        compiler_params=_params(("arbitrary", "arbitrary")),
        name="in_projection",
    )(xs, mod, mod, g, w)


def _ret_kernel(dec_ref, cos_ref, sin_ref, q_ref, k_ref, v_ref, o_ref, mask_scr, pw_scr):
    c_len = RET_CHUNK
    n_lat = SEQ // c_len
    n_all = _seg() // c_len
    lane = lax.broadcasted_iota(jnp.int32, (1, LANES), 1)
    head0 = lane < RET_DK
    first_half = (lane % RET_DK) < (RET_DK // 2)

    pairs = o_ref.shape[1] // LANES
    r = lax.broadcasted_iota(jnp.int32, (c_len, 1), 0).astype(F32)
    ri = lax.broadcasted_iota(jnp.int32, (c_len, c_len), 0)
    ci = lax.broadcasted_iota(jnp.int32, (c_len, c_len), 1)
    dm = (ri - ci).astype(F32)
    dcf, dcb = [], []
    for pp in range(pairs):
        lg = _log_sigmoid(dec_ref[pp])
        dvf = jnp.where(head0, lg[0:1], lg[1:2])
        dvb = jnp.where(head0, lg[2:3], lg[3:4])
        pw_scr[4 * pp + 0] = jnp.exp((r + 1.0) * dvf)
        pw_scr[4 * pp + 1] = jnp.exp((c_len - 1.0 - r) * dvf)
        pw_scr[4 * pp + 2] = jnp.exp((c_len - r) * dvb)
        pw_scr[4 * pp + 3] = jnp.exp(r * dvb)
        dcf.append(jnp.exp(c_len * dvf))
        dcb.append(jnp.exp(c_len * dvb))
        for h in range(2):
            lgf = jnp.concatenate([lg[h:h + 1]] * (c_len // LANES), axis=1)
            lgb = jnp.concatenate([lg[2 + h:3 + h]] * (c_len // LANES), axis=1)
            mask_scr[2 * pp + h] = jnp.exp(jnp.where(dm >= 0, dm * lgf, -dm * lgb))
    rb = lax.broadcasted_iota(jnp.int32, (LANES, LANES), 0) < RET_DK
    cb = lax.broadcasted_iota(jnp.int32, (LANES, LANES), 1) < RET_DK
    same_head = (rb == cb).astype(F32)

    def load(c, pp):
        rows = pl.ds(pl.multiple_of(c * c_len, c_len), c_len)
        ln = pl.ds(pp * LANES, LANES)
        q = q_ref[rows, ln].astype(F32)
        k = k_ref[rows, ln].astype(F32) * (RET_DK ** -0.5)
        v = v_ref[rows, ln].astype(F32)
        return rows, ln, q, k, v

    def rotary(t, c):
        rows = pl.ds(pl.multiple_of(c * c_len, c_len), c_len)
        partner = jnp.where(first_half, pltpu.roll(t, LANES - RET_DK // 2, 1), pltpu.roll(t, RET_DK // 2, 1))
        return t * cos_ref[rows, :] + partner * sin_ref[rows, :]

    def load_rot(c, pp):
        rows, ln, q, k, v = load(c, pp)
        is_lat = c < n_lat
        cc = jnp.minimum(c, n_lat - 1)
        q = jnp.where(is_lat, rotary(q, cc), q)
        k = jnp.where(is_lat, rotary(k, cc), k)
        return rows, ln, q, k, v

    def fwd_step(s, st, pp):
        c = jnp.where(s < n_all - n_lat, n_lat + s, s - (n_all - n_lat))
        rows, ln, q, k, v = load_rot(c, pp)
        kb = k.astype(BF16)
        vb = v.astype(BF16)
        o0 = _dot((_dot_nt(jnp.where(head0, q, 0.0).astype(BF16), kb) * mask_scr[2 * pp]).astype(BF16), vb)
        o1 = _dot((_dot_nt(jnp.where(head0, 0.0, q).astype(BF16), kb) * mask_scr[2 * pp + 1]).astype(BF16), vb)
        o = jnp.where(head0, o0, o1)
        o = o + _dot_nt((q * pw_scr[4 * pp]).astype(BF16), st.astype(BF16))
        o_ref[rows, ln] += o
        return st * dcf[pp] + _dot_tn(vb, (k * pw_scr[4 * pp + 1]).astype(BF16)) * same_head

    def bwd_step(s, st, pp):
        c = n_all - 1 - s
        rows, ln, q, k, v = load_rot(c, pp)
        o_ref[rows, ln] += _dot_nt((q * pw_scr[4 * pp + 2]).astype(BF16), st.astype(BF16))
        return st * dcb[pp] + _dot_tn(v.astype(BF16), (k * pw_scr[4 * pp + 3]).astype(BF16)) * same_head

    def step(s, sts):
        out = []
        for pp in range(pairs):
            out += [fwd_step(s, sts[2 * pp], pp), bwd_step(s, sts[2 * pp + 1], pp)]
        return tuple(out)

    o_ref[...] = jnp.zeros(o_ref.shape, F32)
    zero = jnp.zeros((LANES, LANES), F32)
    lax.fori_loop(0, n_all, step, (zero,) * (2 * pairs))


def _ret_call(u, dec, cos_t, sin_t):
    t = u.shape[0]
    sb = _seg()
    pp = RET_PAIRS_PER_STEP
    w = pp * LANES
    col = lambda off: pl.BlockSpec((sb, w), lambda b, p: (b, off // pp + p))
    return pl.pallas_call(
        _ret_kernel,
        grid=(BATCH, RET_HEADS // 2 // pp),
        in_specs=[pl.BlockSpec((pp, 4, LANES), lambda b, p: (p, 0, 0)),
                  pl.BlockSpec((SEQ, LANES), lambda b, p: (0, 0)),
                  pl.BlockSpec((SEQ, LANES), lambda b, p: (0, 0)),
                  col(_C_RQ), col(_C_RK), col(_C_RV)],
        out_specs=pl.BlockSpec((sb, w), lambda b, p: (b, p)),
        out_shape=jax.ShapeDtypeStruct((t, RET_W), F32),
        scratch_shapes=[pltpu.VMEM((2 * pp, RET_CHUNK, RET_CHUNK), F32),
                        pltpu.VMEM((4 * pp, RET_CHUNK, LANES), F32)],
        compiler_params=_params(("arbitrary", "arbitrary")),
        name="retention",
    )(dec, cos_t, sin_t, u, u, u)


def _hg_codes(fwd):
    c_len = HG_BLOCK
    ri = lax.broadcasted_iota(jnp.int32, (c_len, c_len), 0)
    ci = lax.broadcasted_iota(jnp.int32, (c_len, c_len), 1)
    dist = (ri - ci) if fwd else (ci - ri)
    code = jnp.full((c_len, c_len), -1, jnp.int32)
    half, level = c_len // 2, 0
    while half >= HG_SUB:
        sh = half.bit_length() - 1
        same_seg = jnp.right_shift(ri, sh + 1) == jnp.right_shift(ci, sh + 1)
        other_half = jnp.right_shift(ri, sh) != jnp.right_shift(ci, sh)
        code = jnp.where(same_seg & other_half & (dist > 0), level, code)
        half, level = half // 2, level + 1
    sub_sh = HG_SUB.bit_length() - 1
    same_sub = jnp.right_shift(ri, sub_sh) == jnp.right_shift(ci, sub_sh)
    for dl in range(HG_SUB):
        code = jnp.where(same_sub & (dist == dl), 100 + dl, code)
    return code


def _hg_chunk(fwd, q, z, v, lb, st, code, tri):
    c_len = HG_CHUNK

    a = jnp.log(lb)
    b = jnp.log1p(-lb) + _log_sigmoid(z)
    lf = jnp.maximum(a, b) + jnp.log1p(jnp.exp(-jnp.abs(a - b)))
    kin = (1.0 - lb) * _sigmoid(-z)

    g = _dot_exact_lhs(tri, lf) * LOG2_E
    g_end = g[c_len - 1:c_len, :] if fwd else g[0:1, :]
    vb = v.astype(BF16)

    o = _dot_nt((q * jnp.exp2(g)).astype(BF16), st.astype(BF16))

    blk = HG_BLOCK
    lo, hi = slice(0, blk), slice(blk, 2 * blk)
    qrows, krows = (hi, lo) if fwd else (lo, hi)
    bnd = g[blk - 1:blk, :] if fwd else g[blk:blk + 1, :]
    qs = (q[qrows] * jnp.exp2(g[qrows] - bnd)).astype(BF16)
    ks = (kin[krows] * jnp.exp2(bnd - g[krows])).astype(BF16)
    cross = _dot(_dot_nt(qs, ks).astype(BF16), vb[krows])

    parts = []
    for rows in (lo, hi):
        parts.append(_hg_block(fwd, q[rows], kin[rows], g[rows], vb[rows], code))
    parts[1 if fwd else 0] = parts[1 if fwd else 0] + cross
    o = o + jnp.concatenate(parts, axis=0)
    st_new = st * jnp.exp2(g_end) + _dot_tn(vb, (kin * jnp.exp2(g_end - g)).astype(BF16))
    return o, st_new


def _hg_block(fwd, q, kin, g, vb, code):
    n = HG_BLOCK

    scores = jnp.zeros((n, n), F32)
    half, level = n // 2, 0
    while half >= HG_SUB:
        seg = 2 * half
        g3 = g.reshape(n // seg, seg, LANES)
        e = half - 1 if fwd else half
        bsel = jnp.broadcast_to(g3[:, e:e + 1, :], g3.shape).reshape(n, LANES)
        qs = (q * jnp.exp2(g - bsel)).astype(BF16)
        ks = (kin * jnp.exp2(bsel - g)).astype(BF16)
        scores = jnp.where(code == level, _dot_nt(qs, ks), scores)
        half, level = half // 2, level + 1

    def shift(x, dl):
        x3 = x.reshape(n // SUBLANES, SUBLANES, LANES)
        return pltpu.roll(x3, dl if fwd else SUBLANES - dl, 1).reshape(n, LANES)

    for dl in range(HG_SUB):
        w = q * kin if dl == 0 else q * shift(kin, dl) * jnp.exp2(g - shift(g, dl))
        scores = jnp.where(code == 100 + dl, jnp.sum(w, axis=-1, keepdims=True), scores)

    return _dot(scores.astype(BF16), vb)


def _hg_kernel(lbl_ref, q_ref, zf_ref, zb_ref, v_ref, o_ref, *, layer):
    c_len = HG_CHUNK
    n_lat = SEQ // c_len
    n_all = _seg() // c_len
    heads = o_ref.shape[1] // LANES

    lbs = []
    for d in range(2):
        x = lbl_ref[d]
        e = jnp.exp(x - jnp.max(x, axis=0, keepdims=True))
        tot = jnp.sum(e, axis=0, keepdims=True)
        part = jnp.zeros_like(tot)
        for i in range(1, layer + 1):
            part = part + e[i:i + 1, :]
        lbs.append(part / tot)

    ri = lax.broadcasted_iota(jnp.int32, (c_len, c_len), 0)
    ci = lax.broadcasted_iota(jnp.int32, (c_len, c_len), 1)
    tri_f = (ri >= ci).astype(BF16)
    tri_b = (ri <= ci).astype(BF16)
    code_f = _hg_codes(True)
    code_b = _hg_codes(False)

    o_ref[...] = jnp.zeros(o_ref.shape, F32)

    def step(s, carry):
        cf = jnp.where(s < n_all - n_lat, n_lat + s, s - (n_all - n_lat))
        cbk = n_all - 1 - s
        rf = pl.ds(pl.multiple_of(cf * c_len, c_len), c_len)
        rk = pl.ds(pl.multiple_of(cbk * c_len, c_len), c_len)
        out = []
        for hh in range(heads):
            ln = pl.ds(hh * LANES, LANES)
            qf = q_ref[rf, ln].astype(F32)
            of, st_f = _hg_chunk(True, qf * _sigmoid(qf), zf_ref[rf, ln], v_ref[rf, ln].astype(F32),
                                 lbs[0][:, hh * LANES:(hh + 1) * LANES], carry[2 * hh], code_f, tri_f)
            o_ref[rf, ln] += of
            qb = q_ref[rk, ln].astype(F32)
            ob, st_b = _hg_chunk(False, qb * _sigmoid(qb), zb_ref[rk, ln], v_ref[rk, ln].astype(F32),
                                 lbs[1][:, hh * LANES:(hh + 1) * LANES], carry[2 * hh + 1], code_b, tri_b)
            o_ref[rk, ln] += ob
            out += [st_f, st_b]
        return tuple(out)

    zero = jnp.zeros((LANES, LANES), F32)
    lax.fori_loop(0, n_all, step, (zero,) * (2 * heads))


def _hg_call(u, uz, lb_logits, layer):
    t = u.shape[0]
    sb = _seg()
    hp = HG_HEADS_PER_STEP
    w = hp * LANES
    col = lambda off: pl.BlockSpec((sb, w), lambda b, h: (b, off // hp + h))
    return pl.pallas_call(
        functools.partial(_hg_kernel, layer=layer),
        grid=(BATCH, HG_HEADS // hp),
        in_specs=[pl.BlockSpec((2, DEPTH, w), lambda b, h: (0, 0, h)),
                  col(_C_HQ), col(0), col(HG_HEADS), col(_C_HI)],
        out_specs=pl.BlockSpec((sb, w), lambda b, h: (b, h)),
        out_shape=jax.ShapeDtypeStruct((t, HG_W), F32),
        compiler_params=_params(("arbitrary", "arbitrary")),
        name="hgrn2",
    )(lb_logits, u, uz, uz, u)


def _lru_kernel(x_ref, wg_ref, bg_ref, lam_ref, cw_ref, cb_ref, y_ref,
                xp_scr, af_scr, bf_scr, ab_scr, bb_scr, hb_scr):
    sb = _seg()
    c_len = LRU_CHUNK
    halo = SUBLANES
    n_tiles = sb // SUBLANES
    n_lat_t = SEQ // SUBLANES

    xp_scr[pl.ds(0, halo), :] = jnp.zeros((halo, LANES), F32)
    xp_scr[pl.ds(halo + sb, halo), :] = jnp.zeros((halo, LANES), F32)
    xp_scr[pl.ds(halo, sb), :] = x_ref[...].astype(F32)

    lam = lam_ref[0]
    sp = jnp.maximum(-lam, 0.0) + jnp.log1p(jnp.exp(-jnp.abs(lam)))
    cw = cw_ref[...]
    wg = wg_ref[0]
    bg = bg_ref[0]
    r8 = lax.broadcasted_iota(jnp.int32, (c_len, 1), 0)
    rm8 = r8 % SUBLANES

    def tile_shift(x, s):
        x3 = x.reshape(c_len // SUBLANES, SUBLANES, LANES)
        return pltpu.roll(x3, s, 1).reshape(c_len, LANES)

    def coeff_chunk(c, _):
        base = pl.multiple_of(c * c_len, c_len)
        w = xp_scr[pl.ds(base, c_len + 2 * halo), :]
        grow = base + r8
        in_lat = grow < SEQ
        pos = jnp.where(in_lat, grow, grow - SEQ)
        seglen = jnp.where(in_lat, SEQ, CTX_LEN)
        wlen = c_len + 2 * halo
        xm2 = jnp.where(pos >= 2, pltpu.roll(w, 2, 0)[halo:halo + c_len], 0.0)
        xm1 = jnp.where(pos >= 1, pltpu.roll(w, 1, 0)[halo:halo + c_len], 0.0)
        x0 = w[halo:halo + c_len]
        xp1 = jnp.where(pos + 1 < seglen, pltpu.roll(w, wlen - 1, 0)[halo:halo + c_len], 0.0)
        xc = xm2 * cw[0:1] + xm1 * cw[1:2] + x0 * cw[2:3] + xp1 * cw[3:4] + cb_ref[...]
        gates = _sigmoid(_dot(xc.astype(BF16), wg) + bg)
        rows = pl.ds(base, c_len)
        for d, (a_scr, b_scr) in enumerate(((af_scr, bf_scr), (ab_scr, bb_scr))):
            rg = gates[:, (2 * d) * LANES:(2 * d + 1) * LANES]
            ig = gates[:, (2 * d + 1) * LANES:(2 * d + 2) * LANES]
            log_a = -LRU_C * rg * sp[d:d + 1]
            a = jnp.exp(log_a)
            th = jnp.tanh(log_a)
            bv = jnp.sqrt(-2.0 * th / (1.0 - th)) * (ig * xc)
            for sft in (1, 2, 4):
                if d == 0:
                    a_s = tile_shift(a, sft)
                    b_s = tile_shift(bv, sft)
                    ok = rm8 >= sft
                else:
                    a_s = tile_shift(a, SUBLANES - sft)
                    b_s = tile_shift(bv, SUBLANES - sft)
                    ok = rm8 < SUBLANES - sft
                bv = jnp.where(ok, a * b_s + bv, bv)
                a = jnp.where(ok, a * a_s, a)
            a_scr[rows, :] = a
            b_scr[rows, :] = bv
        return 0

    lax.fori_loop(0, sb // c_len, coeff_chunk, 0)

    def carry_step(s, carry):
        hf, hb = carry
        tf = jnp.where(s < n_tiles - n_lat_t, n_lat_t + s, s - (n_tiles - n_lat_t))
        tb = n_tiles - 1 - s
        rf = pl.ds(pl.multiple_of(tf * SUBLANES, SUBLANES), SUBLANES)
        rk = pl.ds(pl.multiple_of(tb * SUBLANES, SUBLANES), SUBLANES)
        h1 = af_scr[rf, :] * hf + bf_scr[rf, :]
        y_ref[rf, :] = h1
        h2 = ab_scr[rk, :] * hb + bb_scr[rk, :]
        hb_scr[rk, :] = h2
        return (jnp.broadcast_to(h1[SUBLANES - 1:SUBLANES, :], (SUBLANES, LANES)),
                jnp.broadcast_to(h2[0:1, :], (SUBLANES, LANES)))

    zero = jnp.zeros((SUBLANES, LANES), F32)
    lax.fori_loop(0, n_tiles, carry_step, (zero, zero), unroll=8)
    y_ref[...] += hb_scr[...]


def _lru_call(u, wg, bg, lam, cw, cb):
    t = u.shape[0]
    sb = _seg()
    nb = LRU_W // LANES
    return pl.pallas_call(
        _lru_kernel,
        grid=(BATCH, nb),
        in_specs=[pl.BlockSpec((sb, LANES), lambda b, c: (b, _C_LX + c)),
                  pl.BlockSpec((1, LANES, 4 * LANES), lambda b, c: (c, 0, 0)),
                  pl.BlockSpec((1, 1, 4 * LANES), lambda b, c: (c, 0, 0)),
                  pl.BlockSpec((1, 2, LANES), lambda b, c: (c, 0, 0)),
                  pl.BlockSpec((CONV_W, LANES), lambda b, c: (0, c)),
                  pl.BlockSpec((1, LANES), lambda b, c: (0, c))],
        out_specs=pl.BlockSpec((sb, LANES), lambda b, c: (b, c)),
        out_shape=jax.ShapeDtypeStruct((t, LRU_W), F32),
        scratch_shapes=[pltpu.VMEM((sb + 2 * SUBLANES, LANES), F32)] + [pltpu.VMEM((sb, LANES), F32)] * 5,
        compiler_params=_params(("arbitrary", "arbitrary")),
        name="rglru",
    )(u, wg, bg, lam, cw, cb)


def _group_rms(o, group):
    ri = lax.broadcasted_iota(jnp.int32, (LANES, LANES), 0) // group
    ci = lax.broadcasted_iota(jnp.int32, (LANES, LANES), 1) // group
    ones_bd = (ri == ci).astype(BF16)
    outs = []
    for tix in range(o.shape[1] // LANES):
        x = o[:, tix * LANES:(tix + 1) * LANES]
        sq = x * x
        hi = sq.astype(BF16)
        lo = (sq - hi.astype(F32)).astype(BF16)
        ss = _dot(hi, ones_bd) + _dot(lo, ones_bd)
        outs.append(x * lax.rsqrt(ss * (1.0 / group) + NORM_EPS))
    return jnp.concatenate(outs, axis=1)


def _merge_kernel(*refs, n_gate):
    (x_ref, ml_ref, mc_ref, oret_ref, ohg_ref, ylru_ref, rg_ref, hgt_ref, lgt_ref) = refs[:9]
    gate_refs = refs[9:9 + 3 * n_gate]
    (gng_ref, ghg_ref, wr_ref, wh_ref, wl_ref, wo_ref, n2g_ref, rwh_ref, rwm_ref, rb_ref,
     xo_ref, h2_ref, lg_ref) = refs[9 + 3 * n_gate:]
    tiles_per_b = _seg() // ROW_TILE
    is_ctx = (pl.program_id(0) % tiles_per_b) >= SEQ // ROW_TILE
    ml = ml_ref[0]
    mc = mc_ref[0]

    def gate(i):
        parts = [gate_refs[i * n_gate + j][...].astype(F32) for j in range(n_gate)]
        return _sigmoid(parts[0] if n_gate == 1 else jnp.concatenate(parts, axis=1))

    rg = rg_ref[...].astype(F32)
    ret_in = _group_rms(oret_ref[...], RET_DK) * gng_ref[...] * (rg * _sigmoid(rg))
    ret = _dot(ret_in.astype(BF16), wr_ref[0])
    hg_in = _group_rms(ohg_ref[...], HG_DK) * ghg_ref[...] * _sigmoid(hgt_ref[...].astype(F32))
    hgr = _dot(hg_in.astype(BF16), wh_ref[0])
    lgt = lgt_ref[...].astype(F32)
    gelu = 0.5 * lgt * (1.0 + jnp.tanh(0.7978845608028654 * (lgt + 0.044715 * lgt * lgt * lgt)))
    lru = _dot((ylru_ref[...] * gelu).astype(BF16), wl_ref[0])
    merged = gate(0) * ret + gate(1) * hgr + gate(2) * lru
    mix = _dot(merged.astype(BF16), wo_ref[0])
    g1 = jnp.where(is_ctx, mc[2:3, :], ml[2:3, :])
    xn = x_ref[...] + g1 * mix
    xo_ref[...] = xn
    h2 = _norm_modulate(xn, n2g_ref[...], ml, mc, is_ctx, 3)
    h2_ref[...] = h2.astype(BF16)
    hh, hm, _ = _split3(h2)
    lg_ref[...] = _dot(hh, rwh_ref[...]) + (_dot(hh, rwm_ref[...]) + _dot(hm, rwh_ref[...])) + rb_ref[...]


def _merge_call(xs, mod, o_ret, o_hg, y_lru, u, gn_g, hgn_g, wr, wh, wl, wo, n2g, rw_hi, rw_mid, rb, layer):
    t = xs.shape[0]
    r = ROW_TILE
    tiles_per_b = _seg() // r
    n_gate = D_MODEL // 512
    row = lambda w: pl.BlockSpec((r, w), lambda i: (i, 0))
    ucol = lambda blk: pl.BlockSpec((r, 512), lambda i: (i, blk))

    def full(a):
        if a.ndim == 3:
            return pl.BlockSpec((1,) + a.shape[1:], lambda i: (layer, 0, 0))
        return pl.BlockSpec(a.shape, lambda i: (0,) * a.ndim)
    gate_base = _C_GATES * LANES // 512
    in_specs = [row(D_MODEL),
                pl.BlockSpec((1, N_MOD, D_MODEL), lambda i: (i // tiles_per_b, 0, 0)),
                pl.BlockSpec((1, N_MOD, D_MODEL), lambda i: (BATCH, 0, 0)),
                row(RET_W), row(HG_W), row(LRU_W),
                ucol(_C_RG * LANES // 512), ucol(_C_HGT * LANES // 512), ucol(_C_LGT * LANES // 512)]
    in_specs += [ucol(gate_base + j) for j in range(3 * n_gate)]
    consts = [gn_g, hgn_g, wr, wh, wl, wo, n2g, rw_hi, rw_mid, rb]
    in_specs += [full(a) for a in consts]
    return pl.pallas_call(
        functools.partial(_merge_kernel, n_gate=n_gate),
        grid=(t // r,),
        in_specs=in_specs,
        out_specs=[row(D_MODEL), row(D_MODEL), row(LANES)],
        out_shape=[jax.ShapeDtypeStruct((t, D_MODEL), F32), jax.ShapeDtypeStruct((t, D_MODEL), BF16),
                   jax.ShapeDtypeStruct((t, LANES), F32)],
        compiler_params=_params(("arbitrary",)),
        name="merge",
    )(xs, mod, mod, o_ret, o_hg, y_lru, *([u] * (3 + 3 * n_gate)), *consts)


def _route_kernel(lg_ref, oi_ref, ow_ref, lst_ref, meta_ref, cnt_ref, off_scr):
    @pl.when(pl.program_id(0) == 0)
    def _():
        off_scr[...] = jnp.zeros(off_scr.shape, F32)

    r = lg_ref.shape[0]
    lane = lax.broadcasted_iota(jnp.int32, (r, LANES), 1)
    neg = -jnp.inf
    l = jnp.where(lane < N_EXPERTS, lg_ref[...], neg)
    vals, hots = [], []
    lanef = lane.astype(F32)
    for _ in range(TOP_K):
        m = jnp.max(l, axis=-1, keepdims=True)
        idxf = jnp.min(jnp.where(l == m, lanef, float(LANES)), axis=-1, keepdims=True)
        hot = lanef == idxf
        l = jnp.where(hot, neg, l)
        vals.append(m)
        hots.append(hot)
    es = [jnp.exp(v - vals[0]) for v in vals]
    den = es[0]
    for e in es[1:]:
        den = den + e
    chosen = hots[0].astype(F32)
    for hot in hots[1:]:
        chosen = chosen + hot.astype(F32)
    ri = lax.broadcasted_iota(jnp.int32, (r, r), 0)
    ci = lax.broadcasted_iota(jnp.int32, (r, r), 1)
    before = _dot((ri > ci).astype(BF16), chosen.astype(BF16))
    count = jnp.sum(chosen, axis=0, keepdims=True)
    run = jnp.floor((count + (SUBLANES - 1.0)) * (1.0 / SUBLANES)) * SUBLANES
    li = lax.broadcasted_iota(jnp.int32, (LANES, LANES), 0)
    lj = lax.broadcasted_iota(jnp.int32, (LANES, LANES), 1)
    run8 = jnp.broadcast_to(run, (SUBLANES, LANES))
    start = _dot(run8.astype(BF16), (li < lj).astype(BF16))[0:1, :]
    pos = start + before
    oi = jnp.zeros((r, LANES), jnp.int32)
    ow = jnp.zeros((r, LANES), F32)
    lsf = jnp.zeros((r, LANES), F32)
    for k in range(TOP_K):
        row_k = jnp.sum(jnp.where(hots[k], pos, 0.0), axis=-1, keepdims=True)
        oi = jnp.where(lane == k, row_k.astype(jnp.int32), oi)
        lsf = jnp.where(lane == k, row_k, lsf)
        ow = jnp.where(lane == k, es[k] / den, ow)
    oi_ref[...] = oi
    ow_ref[...] = ow
    lst_ref[0] = jnp.transpose(lsf)[0:SUBLANES, :]
    mrow = lax.broadcasted_iota(jnp.int32, (SUBLANES, LANES), 0)
    meta_ref[0] = jnp.where(mrow == 0, run8, jnp.where(mrow == 1, jnp.broadcast_to(start, (SUBLANES, LANES)),
                                                       off_scr[...]))
    off_scr[...] = off_scr[...] + run
    cnt_ref[...] = off_scr[...]


def _route_call(logits):
    t = logits.shape[0]
    r = ROW_TILE
    row = pl.BlockSpec((r, LANES), lambda i: (i, 0))
    return pl.pallas_call(
        _route_kernel,
        grid=(t // r,),
        in_specs=[row],
        out_specs=[row, row, pl.BlockSpec((1, SUBLANES, r), lambda i: (i, 0, 0)),
                   pl.BlockSpec((1, SUBLANES, LANES), lambda i: (i, 0, 0)),
                   pl.BlockSpec((SUBLANES, LANES), lambda i: (0, 0))],
        out_shape=[jax.ShapeDtypeStruct((t, LANES), jnp.int32), jax.ShapeDtypeStruct((t, LANES), F32),
                   jax.ShapeDtypeStruct((t // r, SUBLANES, r), F32),
                   jax.ShapeDtypeStruct((t // r, SUBLANES, LANES), F32),
                   jax.ShapeDtypeStruct((SUBLANES, LANES), F32)],
        scratch_shapes=[pltpu.VMEM((SUBLANES, LANES), F32)],
        compiler_params=_params(("arbitrary",)),
        name="route",
    )(logits)


def _pack_pairs(x):
    m = x.shape[1] // 2
    bits = pltpu.bitcast(x, jnp.uint32)
    return bits[:, :m] | (bits[:, m:] >> 16)


def _unpack_pairs(w):
    hi = pltpu.bitcast(w & jnp.uint32(0xFFFF0000), F32).astype(BF16)
    lo = pltpu.bitcast(w << 16, F32).astype(BF16)
    return hi, lo


def _local_rows():
    rows = ROW_TILE * TOP_K + N_EXPERTS * (SUBLANES - 1)
    return -(-rows // ROW_TILE) * ROW_TILE


RUN_PIECE = 4 * SUBLANES


def _max_pieces():
    return _local_rows() // RUN_PIECE, N_EXPERTS * (RUN_PIECE // SUBLANES - 1)


def _meta_width():
    n_big, n_small = _max_pieces()
    return -(-(2 + 2 * n_big + 2 * n_small) // LANES) * LANES


def _piece_tables(first_slot, first_row, pieces):
    per_big = RUN_PIECE // SUBLANES
    max_big, max_small = _max_pieces()

    def flat(count, limit, offset_rows, step):
        ends = jnp.cumsum(count, axis=1)
        p = jnp.arange(limit, dtype=jnp.int32)[None, :, None]
        mine = ((ends - count)[:, None, :] <= p) & (p < ends[:, None, :])
        take = lambda a: jnp.sum(jnp.where(mine, a[:, None, :], 0), axis=2)
        j = p[:, :, 0] - take(ends - count)
        off = take(offset_rows) + j * step
        return take(first_row) + off, take(first_slot) + off, ends[:, -1:]

    n_big = pieces // per_big
    big_row, big_slot, tot_big = flat(n_big, max_big, jnp.zeros_like(pieces), RUN_PIECE)
    small_row, small_slot, tot_small = flat(pieces - n_big * per_big, max_small, n_big * RUN_PIECE, SUBLANES)
    table = jnp.concatenate([tot_big, tot_small, big_row, big_slot, small_row, small_slot], axis=1)
    return jnp.pad(table, ((0, 0), (0, _meta_width() - table.shape[1])))[:, None, :]


def _run_copies(meta_ref, make_copy, do):
    max_big, max_small = _max_pieces()
    big0 = 2
    small0 = 2 + 2 * max_big

    def big(p, _):
        do(make_copy(pl.multiple_of(meta_ref[0, 0, big0 + p], SUBLANES),
                     pl.multiple_of(meta_ref[0, 0, big0 + max_big + p], SUBLANES), RUN_PIECE))
        return 0

    def small(p, _):
        do(make_copy(pl.multiple_of(meta_ref[0, 0, small0 + p], SUBLANES),
                     pl.multiple_of(meta_ref[0, 0, small0 + max_small + p], SUBLANES), SUBLANES))
        return 0

    lax.fori_loop(0, meta_ref[0, 0, 0], big, 0)
    lax.fori_loop(0, meta_ref[0, 0, 1], small, 0)


def _dispatch_kernel(meta_ref, prev_meta_ref, lst_ref, h_ref, xs_in_ref, xs_ref, perm_scr, sem):
    del xs_in_ref
    i = pl.program_id(0)
    cur = i % 2
    r = h_ref.shape[0]
    ls = lst_ref[0].astype(jnp.int32)
    hb = h_ref[...]
    for s in range(perm_scr.shape[1] // r):
        rows = s * r + lax.broadcasted_iota(jnp.int32, (r, r), 0)
        sel = rows == ls[0:1, :]
        for k in range(1, TOP_K):
            sel = jnp.logical_or(sel, rows == ls[k:k + 1, :])
        perm_scr[cur, pl.ds(s * r, r), :] = _pack_pairs(_dot(jnp.where(sel, 1.0, 0.0).astype(BF16), hb))

    def copies_from(buf):
        def make_copy(row, slot, n):
            return pltpu.make_async_copy(perm_scr.at[buf, pl.ds(row, n)], xs_ref.at[pl.ds(slot, n)], sem.at[buf])
        return make_copy

    _run_copies(meta_ref, copies_from(cur), lambda c: c.start())

    @pl.when(i > 0)
    def _():
        _run_copies(prev_meta_ref, copies_from(1 - cur), lambda c: c.wait())

    @pl.when(i == pl.num_programs(0) - 1)
    def _():
        _run_copies(meta_ref, copies_from(cur), lambda c: c.wait())


def _dispatch_call(h2, meta, lst, slots):
    t = h2.shape[0]
    r = ROW_TILE
    return pl.pallas_call(
        _dispatch_kernel,
        grid=(t // r,),
        in_specs=[pl.BlockSpec((1, 1, _meta_width()), lambda i: (i, 0, 0), memory_space=pltpu.SMEM),
                  pl.BlockSpec((1, 1, _meta_width()), lambda i: (jnp.maximum(i - 1, 0), 0, 0), memory_space=pltpu.SMEM),
                  pl.BlockSpec((1, SUBLANES, r), lambda i: (i, 0, 0)),
                  pl.BlockSpec((r, D_MODEL), lambda i: (i, 0)),
                  pl.BlockSpec(memory_space=pl.ANY)],
        out_specs=pl.BlockSpec(memory_space=pl.ANY),
        out_shape=jax.ShapeDtypeStruct(slots.shape, jnp.uint32),
        scratch_shapes=[pltpu.VMEM((2, _local_rows(), D_MODEL // 2), jnp.uint32), pltpu.SemaphoreType.DMA((2,))],
        input_output_aliases={4: 0},
        compiler_params=_params(("arbitrary",)),
        name="dispatch",
    )(meta, meta, lst, h2, slots)


def _expert_kernel(bexp_ref, nv_ref, next_ref, par_ref, x_ref, wgu_hbm, bgu_ref, wd_hbm, bd_ref, y_ref,
                   wgu_scr, wd_scr, gu_stage, d_stage, sem, *, layer):
    i = pl.program_id(0)
    f = D_MODEL
    prev = bexp_ref[jnp.maximum(i - 1, 0)]
    new_expert = jnp.logical_or(i == 0, bexp_ref[i] != prev)

    def weight_copies(e, b):
        return (pltpu.make_async_copy(wgu_hbm.at[layer, e], gu_stage.at[b], sem.at[b, 0]),
                pltpu.make_async_copy(wd_hbm.at[layer, e], d_stage.at[b], sem.at[b, 1]))

    @pl.when(jnp.logical_and(i < nv_ref[0], new_expert))
    def _():
        b = par_ref[i]

        @pl.when(i == 0)
        def _():
            for c in weight_copies(bexp_ref[i], b):
                c.start()

        for c in weight_copies(bexp_ref[i], b):
            c.wait()

        @pl.when(next_ref[i] >= 0)
        def _():
            for c in weight_copies(next_ref[i], 1 - b):
                c.start()

        slab = 256
        for s in range(D_MODEL // slab):
            wgu_scr[pl.ds(s * slab, slab), :] = gu_stage[b, pl.ds(s * slab, slab), :].astype(BF16)
            wd_scr[pl.ds(s * slab, slab), :] = d_stage[b, pl.ds(s * slab, slab), :].astype(BF16)

    @pl.when(i < nv_ref[0])
    def _():
        x_hi, x_lo = _unpack_pairs(x_ref[...])
        half = D_MODEL // 2
        gu = _dot(x_hi, wgu_scr[:half, :]) + _dot(x_lo, wgu_scr[half:, :]) + bgu_ref[0, 0]
        gate = jnp.minimum(gu[:, :f], SWIGLU_LIMIT)
        up = jnp.clip(gu[:, f:], -SWIGLU_LIMIT, SWIGLU_LIMIT)
        glu = gate * _sigmoid(SWIGLU_ALPHA * gate)
        y = _dot(((up + 1.0) * glu).astype(BF16), wd_scr[...]) + bd_ref[0, 0]
        y_ref[...] = _pack_pairs(y.astype(BF16).astype(F32))

    @pl.when(i >= nv_ref[0])
    def _():
        y_ref[...] = jnp.zeros(y_ref.shape, jnp.uint32)


def _expert_call(slots, bexp, nvalid, next_expert, stage, w_gu, b_gu, w_down, b_down, layer):
    s = slots.shape[0]
    nb = s // MOE_BLOCK
    f = D_MODEL
    blk = lambda i, be, nv, nx, pr: (jnp.minimum(i, nv[0] - 1), 0)
    exp3 = lambda i, be, nv, nx, pr: (layer, be[jnp.minimum(i, nv[0] - 1)], 0, 0)
    grid_spec = pltpu.PrefetchScalarGridSpec(
        num_scalar_prefetch=4,
        grid=(nb,),
        in_specs=[pl.BlockSpec((MOE_BLOCK, D_MODEL // 2), blk),
                  pl.BlockSpec(memory_space=pl.ANY),
                  pl.BlockSpec((1, 1, 1, 2 * f), exp3),
                  pl.BlockSpec(memory_space=pl.ANY),
                  pl.BlockSpec((1, 1, 1, D_MODEL), exp3)],
        out_specs=pl.BlockSpec((MOE_BLOCK, D_MODEL // 2), lambda i, be, nv, nx, pr: (i, 0)),
        scratch_shapes=[pltpu.VMEM((D_MODEL, 2 * f), BF16), pltpu.VMEM((f, D_MODEL), BF16),
                        pltpu.VMEM((2, D_MODEL, 2 * f), F32), pltpu.VMEM((2, f, D_MODEL), F32),
                        pltpu.SemaphoreType.DMA((2, 2))],
    )
    return pl.pallas_call(
        functools.partial(_expert_kernel, layer=layer),
        grid_spec=grid_spec,
        out_shape=jax.ShapeDtypeStruct((s, D_MODEL // 2), jnp.uint32),
        compiler_params=_params(("arbitrary",)),
        name="experts",
    )(bexp, nvalid, next_expert, stage, slots, w_gu, b_gu.reshape(DEPTH, N_EXPERTS, 1, 2 * f), w_down,
      b_down.reshape(DEPTH, N_EXPERTS, 1, D_MODEL))


def _combine_kernel(meta_ref, next_meta_ref, x_ref, ml_ref, mc_ref, oi_ref, w_ref, fg_ref, y_ref, o_ref, buf, sem,
                    *, last, n_t):
    r = x_ref.shape[0]
    i = pl.program_id(0)
    cur = i % 2

    def copies_into(b):
        def make_copy(row, slot, n):
            return pltpu.make_async_copy(y_ref.at[pl.ds(slot, n)], buf.at[b, pl.ds(row, n)], sem.at[b])
        return make_copy

    @pl.when(i == 0)
    def _():
        buf[...] = jnp.zeros(buf.shape, jnp.uint32)
        _run_copies(meta_ref, copies_into(0), lambda c: c.start())

    @pl.when(i + 1 < pl.num_programs(0))
    def _():
        _run_copies(next_meta_ref, copies_into(1 - cur), lambda c: c.start())

    _run_copies(meta_ref, copies_into(cur), lambda c: c.wait())

    if last:
        is_ctx = False
    else:
        is_ctx = (i % n_t) >= SEQ // ROW_TILE
    w = w_ref[...]
    oi = oi_ref[...]
    ff_hi = jnp.zeros((r, D_MODEL // 2), F32)
    ff_lo = jnp.zeros((r, D_MODEL // 2), F32)
    for s in range(buf.shape[1] // r):
        cols = s * r + lax.broadcasted_iota(jnp.int32, (r, r), 1)
        wm = jnp.zeros((r, r), F32)
        for k in range(TOP_K):
            wm = jnp.where(cols == oi[:, k:k + 1], w[:, k:k + 1], wm)
        y_hi, y_lo = _unpack_pairs(buf[cur, pl.ds(s * r, r), :])
        ff_hi = ff_hi + _dot(wm.astype(BF16), y_hi)
        ff_lo = ff_lo + _dot(wm.astype(BF16), y_lo)
    ff = jnp.concatenate([ff_hi, ff_lo], axis=1)
    g2 = jnp.where(is_ctx, mc_ref[0][5:6, :], ml_ref[0][5:6, :])
    xn = x_ref[...] + g2 * ff
    if last:
        ms = jnp.mean(xn * xn, axis=-1, keepdims=True)
        xn = xn * lax.rsqrt(ms + NORM_EPS) * fg_ref[...]
    o_ref[...] = xn


def _combine_call(xs, mod, meta, rows, wts, y_slots, final_g, last):
    t = xs.shape[0]
    r = ROW_TILE
    tiles_per_b = _seg() // r
    n_t = SEQ // r if last else tiles_per_b
    n_steps = BATCH * n_t
    tile = lambda i: (i // n_t) * tiles_per_b + i % n_t
    out_rows = BATCH * SEQ if last else t
    return pl.pallas_call(
        functools.partial(_combine_kernel, last=last, n_t=n_t),
        grid=(n_steps,),
        in_specs=[pl.BlockSpec((1, 1, _meta_width()), lambda i: (tile(i), 0, 0), memory_space=pltpu.SMEM),
                  pl.BlockSpec((1, 1, _meta_width()), lambda i: (tile(jnp.minimum(i + 1, n_steps - 1)), 0, 0),
                               memory_space=pltpu.SMEM),
                  pl.BlockSpec((r, D_MODEL), lambda i: (tile(i), 0)),
                  pl.BlockSpec((1, N_MOD, D_MODEL), lambda i: (i // n_t, 0, 0)),
                  pl.BlockSpec((1, N_MOD, D_MODEL), lambda i: (BATCH, 0, 0)),
                  pl.BlockSpec((r, LANES), lambda i: (tile(i), 0)),
                  pl.BlockSpec((r, LANES), lambda i: (tile(i), 0)),
                  pl.BlockSpec((1, D_MODEL), lambda i: (0, 0)),
                  pl.BlockSpec(memory_space=pl.ANY)],
        out_specs=pl.BlockSpec((r, D_MODEL), lambda i: (i, 0)),
        out_shape=jax.ShapeDtypeStruct((out_rows, D_MODEL), F32),
        scratch_shapes=[pltpu.VMEM((2, _local_rows(), D_MODEL // 2), jnp.uint32), pltpu.SemaphoreType.DMA((2,))],
        compiler_params=_params(("arbitrary",)),
        name="combine",
    )(meta, meta, xs, mod, mod, rows, wts, final_g, y_slots)


def _rotary_tables():
    rows = SEQ // GRID_W
    row = jnp.repeat(jnp.arange(rows, dtype=F32), GRID_W)
    colp = jnp.tile(jnp.arange(GRID_W, dtype=F32), rows)
    n_freq = RET_DK // 4
    inv = ROPE_BASE ** (-jnp.arange(n_freq, dtype=F32) / n_freq)
    ang = jnp.concatenate([row[:, None] * inv, colp[:, None] * inv], axis=-1)
    cos, sin = jnp.cos(ang), jnp.sin(ang)
    reps = LANES // (RET_DK // 2)
    cos_t = jnp.tile(cos, (1, reps))
    sign = jnp.where((jnp.arange(LANES) % RET_DK) < RET_DK // 2, -1.0, 1.0).astype(F32)
    sin_t = jnp.tile(sin, (1, reps)) * sign
    return cos_t, sin_t


def _lru_gate_weights(wa, ba, wx, bx):
    nb = LRU_W // LANES
    per = LANES // LRU_BD
    eye = jnp.eye(per, dtype=F32)

    def embed(w):
        w = w.reshape(nb, per, LRU_BD, LRU_BD)
        return jnp.einsum('cpij,pq->cpiqj', w, eye).reshape(nb, LANES, LANES)

    wg = jnp.concatenate([embed(wa[0]), embed(wx[0]), embed(wa[1]), embed(wx[1])], axis=-1)
    bias = jnp.concatenate([ba[0].reshape(nb, 1, LANES), bx[0].reshape(nb, 1, LANES),
                            ba[1].reshape(nb, 1, LANES), bx[1].reshape(nb, 1, LANES)], axis=-1)
    return wg.astype(BF16), bias


def kernel(x, c, ctx, c_ctx, mod_w, mod_b, norm1_g, norm2_g, final_g, w_in, ret_decay, ret_gn_g, w_ret_o,
           hgrn_lb_logits, hgrn_gn_g, w_hgrn_o, lru_conv_w, lru_conv_b, lru_wa, lru_ba, lru_wx, lru_bx,
           lru_lambda, w_lru_o, w_out, router_w, router_b, exp_w_gu, exp_b_gu, exp_w_down, exp_b_down):
    assert CTX_LEN == RET_CHUNK and SEQ % RET_CHUNK == 0
    sb = _seg()
    t = BATCH * sb
    xs = jnp.concatenate([x, ctx], axis=1).reshape(t, D_MODEL)

    cvec = jnp.concatenate([c, c_ctx[None, :], jnp.zeros((SUBLANES - BATCH - 1, D_MODEL), F32)], axis=0)
    mod_all = _mod_call(cvec, mod_w, mod_b).reshape(DEPTH, SUBLANES, N_MOD, D_MODEL)

    cos_t, sin_t = _rotary_tables()
    lb_logits = jnp.transpose(hgrn_lb_logits, (1, 0, 2))
    n_slots = t * TOP_K + (t // ROW_TILE) * N_EXPERTS * (SUBLANES - 1) + N_EXPERTS * (MOE_BLOCK - 1)
    n_blocks = -(-n_slots // MOE_BLOCK)
    n_slots = n_blocks * MOE_BLOCK
    slots = jnp.zeros((n_slots, D_MODEL // 2), jnp.uint32)
    rw_pad = jnp.pad(router_w, ((0, 0), (0, 0), (0, LANES - N_EXPERTS)))
    rw_hi = rw_pad.astype(BF16)
    rw_mid = (rw_pad - rw_hi.astype(F32)).astype(BF16)
    rb_pad = jnp.pad(router_b, ((0, 0), (0, LANES - N_EXPERTS)))
    w_ret_b, w_hg_b, w_lru_b, w_out_b = (w.astype(BF16) for w in (w_ret_o, w_hgrn_o, w_lru_o, w_out))
    block_first_slot = jnp.arange(n_blocks, dtype=jnp.int32) * MOE_BLOCK
    expert_ids = jnp.arange(N_EXPERTS, dtype=jnp.int32)

    out = None
    for l in range(DEPTH):
        last = l == DEPTH - 1
        mod = mod_all[l]
        u, uz = _inproj_call(xs, mod, norm1_g[l][None, :], w_in, l)

        dec = jnp.broadcast_to(ret_decay[l].reshape(2, RET_HEADS // 2, 2, 1).transpose(1, 0, 2, 3),
                               (RET_HEADS // 2, 2, 2, LANES)).reshape(RET_HEADS // 2, 4, LANES)
        o_ret = _ret_call(u, dec, cos_t, sin_t)
        o_hg = _hg_call(u, uz, lb_logits, l)
        wg, bg = _lru_gate_weights(lru_wa[l], lru_ba[l], lru_wx[l], lru_bx[l])
        lam = lru_lambda[l].reshape(2, LRU_W // LANES, LANES).transpose(1, 0, 2)
        y_lru = _lru_call(u, wg, bg, lam, lru_conv_w[l], lru_conv_b[l][None, :])

        xs, h2, logits = _merge_call(
            xs, mod, o_ret, o_hg, y_lru, u, ret_gn_g[l][None, :], hgrn_gn_g[l][None, :],
            w_ret_b, w_hg_b, w_lru_b, w_out_b, norm2_g[l][None, :], rw_hi[l], rw_mid[l], rb_pad[l][None, :], l)

        rows, wts, lst, tile_meta, cnt = _route_call(logits)
        run = tile_meta[:, 0, :N_EXPERTS].astype(jnp.int32)
        first_row = tile_meta[:, 1, :N_EXPERTS].astype(jnp.int32)
        earlier = tile_meta[:, 2, :N_EXPERTS].astype(jnp.int32)
        counts = cnt[0, :N_EXPERTS].astype(jnp.int32)
        padded = (counts + MOE_BLOCK - 1) // MOE_BLOCK * MOE_BLOCK
        pad_end = jnp.cumsum(padded)
        first_slot = (pad_end - padded)[None, :] + earlier
        meta = _piece_tables(first_slot, first_row, run // SUBLANES)
        bexp = jnp.minimum(jnp.sum((pad_end[None, :] <= block_first_slot[:, None]).astype(jnp.int32), axis=1),
                           N_EXPERTS - 1)
        nvalid = (pad_end[-1:] // MOE_BLOCK).astype(jnp.int32)
        present = padded > 0
        later = jnp.where(present[None, :] & (expert_ids[None, :] > expert_ids[:, None]), expert_ids[None, :],
                          N_EXPERTS)
        next_of = jnp.min(later, axis=1)
        next_of = jnp.where(next_of >= N_EXPERTS, -1, next_of)
        stage_of = (jnp.cumsum(present.astype(jnp.int32)) - present.astype(jnp.int32)) % 2
        is_e = bexp[:, None] == expert_ids[None, :]
        next_expert = jnp.sum(jnp.where(is_e, next_of[None, :], 0), axis=1)
        stage = jnp.sum(jnp.where(is_e, stage_of[None, :], 0), axis=1)

        slots = _dispatch_call(h2, meta, lst, slots)
        y_slots = _expert_call(slots, bexp, nvalid, next_expert, stage, exp_w_gu, exp_b_gu, exp_w_down,
                               exp_b_down, l)
        res = _combine_call(xs, mod, meta, rows, wts, y_slots, final_g[None, :], last)
        if last:
            out = res.reshape(BATCH, SEQ, D_MODEL)
        else:
            xs = res
    return out
```
